```python
import math
import jax
import jax.numpy as jnp
from jax import lax
import numpy as np

D_MODEL = 2048
BATCH = 2
SEQ = 4096
DEPTH = 1

GRID_W = 64
CTX_LEN = 256
EPS = 1e-6

ATTN_HEADS = 8
ATTN_KV_HEADS = 2
HEAD_DIM = 128
ROPE_THETA = 10000.0
Q_BLOCK = 128

GLA_HEADS = 4
GLA_DK = 128
GLA_DV = 256
GLA_GATE_RANK = 16
GLA_GATE_NORMALIZER = 16.0
GLA_CHUNK = 64

ATTN_Q_W = ATTN_HEADS * HEAD_DIM
ATTN_KV_W = ATTN_KV_HEADS * HEAD_DIM
GLA_K_W = GLA_HEADS * GLA_DK
GLA_V_W = GLA_HEADS * GLA_DV
MIX_WIDTH = ATTN_Q_W + GLA_V_W
IN_SPLITS = (ATTN_Q_W, ATTN_KV_W, ATTN_KV_W, GLA_K_W, GLA_K_W, GLA_V_W, 2 * GLA_GATE_RANK, GLA_V_W)
IN_OFFSETS = tuple(int(o) for o in np.cumsum(IN_SPLITS)[:-1])
IN_WIDTH = int(sum(IN_SPLITS))

N_GROUPS = 8
EXPERTS_PER_GROUP = 8
N_EXPERTS = N_GROUPS * EXPERTS_PER_GROUP
TOP_K = 2
EXPERT_FF = 512
MOE_BLOCK = 128

kernel_name = 'hybrid_gqa_gla_hmoe_dit_block'


def _rms_norm(x, g):
    xf = x.astype(jnp.float32)
    xf = xf * lax.rsqrt(jnp.mean(xf * xf, axis=-1, keepdims=True) + EPS)
    return xf.astype(x.dtype) * g


def _axial_angles(T):
    rows = T // GRID_W
    r, col = jnp.meshgrid(jnp.arange(rows, dtype=jnp.float32), jnp.arange(GRID_W, dtype=jnp.float32), indexing='ij')
    r = r.reshape(-1)
    col = col.reshape(-1)
    n_freq = HEAD_DIM // 4
    inv = ROPE_THETA ** (-jnp.arange(n_freq, dtype=jnp.float32) / n_freq)
    ang = jnp.stack([r[:, None] * inv, col[:, None] * inv], axis=1)
    return jnp.cos(ang), jnp.sin(ang)


def _rope2d(x, cos, sin):
    B, T, H, D = x.shape
    xr = x.reshape(B, T, H, 2, 2, D // 4)
    x1 = xr[..., 0, :]
    x2 = xr[..., 1, :]
    c = cos.astype(x.dtype)[None, :, None]
    s = sin.astype(x.dtype)[None, :, None]
    out = jnp.stack([x1 * c - x2 * s, x2 * c + x1 * s], axis=-2)
    return out.reshape(B, T, H, D)


def _attend(q, k, v):
    B, Tq, Hq, D = q.shape
    Hkv = k.shape[2]
    G = Hq // Hkv
    nb = Tq // Q_BLOCK
    qb = q.reshape(B, nb, Q_BLOCK, Hkv, G, D).transpose(1, 0, 2, 3, 4, 5)
    scale = D ** -0.5

    def one_block(qi):
        s = jnp.einsum('bqhgd,bkhd->bhgqk', qi, k).astype(jnp.float32) * scale
        p = jax.nn.softmax(s, axis=-1).astype(v.dtype)
        return jnp.einsum('bhgqk,bkhd->bqhgd', p, v)

    o = lax.map(one_block, qb)
    return o.transpose(1, 0, 2, 3, 4, 5).reshape(B, Tq, Hq * D)


def _gla_inputs(gq, gk, gv, glr, w2, b):
    B, T, _ = gq.shape
    q = gq.reshape(B, T, GLA_HEADS, GLA_DK).astype(jnp.float32) * (GLA_DK ** -0.5)
    k = gk.reshape(B, T, GLA_HEADS, GLA_DK).astype(jnp.float32)
    v = gv.reshape(B, T, GLA_HEADS, GLA_DV).astype(jnp.float32)
    lr = glr.reshape(B, T, 2, GLA_GATE_RANK)
    logits = jnp.einsum('btnr,nrk->btnk', lr, w2) + b
    g = jax.nn.log_sigmoid(logits.astype(jnp.float32)) / GLA_GATE_NORMALIZER
    g = g.reshape(B, T, 2, GLA_HEADS, GLA_DK)
    return q, k, v, g[:, :, 0], g[:, :, 1]


def _gla_scan(q, k, v, g, s0):
    B, T, H, dk = q.shape
    dv = v.shape[-1]
    C = GLA_CHUNK
    nc = T // C

    def chunks(a):
        return a.reshape(B, nc, C, H, a.shape[-1]).transpose(1, 0, 3, 2, 4)

    qc, kc, vc, gc = chunks(q), chunks(k), chunks(v), chunks(g)
    bc = jnp.cumsum(gc, axis=3)
    mask = jnp.tril(jnp.ones((C, C), dtype=bool))

    def step(S, inp):
        qi, ki, vi, bi = inp
        b_last = bi[:, :, -1:, :]
        qe = qi * jnp.exp(bi)
        ke = ki * jnp.exp(-bi)
        kend = ki * jnp.exp(b_last - bi)
        A = jnp.where(mask, jnp.einsum('bhid,bhjd->bhij', qe, ke), 0.0)
        o = jnp.einsum('bhij,bhje->bhie', A, vi) + jnp.einsum('bhid,bhde->bhie', qe, S)
        S = jnp.exp(b_last[:, :, 0, :])[..., None] * S + jnp.einsum('bhjd,bhje->bhde', kend, vi)
        return S, o

    S, o = lax.scan(step, s0, (qc, kc, vc, bc))
    o = o.transpose(1, 0, 3, 2, 4).reshape(B, T, H, dv)
    return o, S


def _gla_bidir(q, k, v, g_f, g_b, s0_f, s0_b):
    o_f, S_f = _gla_scan(q, k, v, g_f, s0_f)
    o_b, S_b = _gla_scan(q[:, ::-1], k[:, ::-1], v[:, ::-1], g_b[:, ::-1], s0_b)
    return o_f + o_b[:, ::-1], S_f, S_b


def _gla_output(o, og, norm_g):
    B, T = o.shape[:2]
    on = _rms_norm(o, norm_g.astype(jnp.float32)).astype(og.dtype)
    return (on * jax.nn.silu(og.reshape(B, T, GLA_HEADS, GLA_DV))).reshape(B, T, GLA_V_W)


def _token_mixers(h, hc, w_in, qn_g, kn_g, gla_w2, gla_b, gla_norm_g, cos, sin, ctx_out):
    B, T, _ = h.shape
    Tc = hc.shape[1]
    aq, ak, av, gq, gk, gv, glr, gog = jnp.split(h @ w_in, IN_OFFSETS, axis=-1)
    aqc, akc, avc, gqc, gkc, gvc, glrc, gogc = jnp.split(hc @ w_in, IN_OFFSETS, axis=-1)

    q = _rope2d(_rms_norm(aq.reshape(B, T, ATTN_HEADS, HEAD_DIM), qn_g), cos, sin)
    k = _rope2d(_rms_norm(ak.reshape(B, T, ATTN_KV_HEADS, HEAD_DIM), kn_g), cos, sin)
    v = av.reshape(B, T, ATTN_KV_HEADS, HEAD_DIM)
    k_c = _rms_norm(akc.reshape(B, Tc, ATTN_KV_HEADS, HEAD_DIM), kn_g)
    v_c = avc.reshape(B, Tc, ATTN_KV_HEADS, HEAD_DIM)
    attn = _attend(q, jnp.concatenate([k_c, k], axis=1), jnp.concatenate([v_c, v], axis=1))

    q_l, k_l, v_l, gf_l, gb_l = _gla_inputs(gq, gk, gv, glr, gla_w2, gla_b)
    q_x, k_x, v_x, gf_x, gb_x = _gla_inputs(gqc, gkc, gvc, glrc, gla_w2, gla_b)
    zeros = jnp.zeros((B, GLA_HEADS, GLA_DK, GLA_DV), jnp.float32)
    o_x, S_cf, S_cb = _gla_bidir(q_x, k_x, v_x, gf_x, gb_x, zeros, zeros)
    o_l, _, _ = _gla_bidir(q_l, k_l, v_l, gf_l, gb_l, S_cf, S_cb)
    gla = _gla_output(o_l, gog, gla_norm_g)

    mix = jnp.concatenate([attn, gla], axis=-1)
    if not ctx_out:
        return mix, None
    q_cx = _rms_norm(aqc.reshape(B, Tc, ATTN_HEADS, HEAD_DIM), qn_g)
    attn_c = _attend(q_cx, k_c, v_c)
    gla_c = _gla_output(o_x, gogc, gla_norm_g)
    return mix, jnp.concatenate([attn_c, gla_c], axis=-1)


def _moe_ffn(xt, wg, bg, we, be, w1, w3, w2):
    M, D = xt.shape
    logits_g = (xt @ wg).astype(jnp.float32) + bg
    p_g = jax.nn.softmax(logits_g, axis=-1)
    grp = jnp.argmax(logits_g, axis=-1)
    pg_sel = jnp.take_along_axis(p_g, grp[:, None], axis=-1)
    logits_e = ((xt @ we).astype(jnp.float32) + be).reshape(M, N_GROUPS, EXPERTS_PER_GROUP)
    le = jnp.take_along_axis(logits_e, grp[:, None, None], axis=1)[:, 0]
    top_v, top_i = lax.top_k(le, TOP_K)
    wts = jax.nn.softmax(top_v, axis=-1) * pg_sel
    eid = grp[:, None] * EXPERTS_PER_GROUP + top_i

    A = M * TOP_K
    a_exp = eid.reshape(A)
    a_tok = jnp.repeat(jnp.arange(M, dtype=jnp.int32), TOP_K)
    a_w = wts.reshape(A)
    order = jnp.argsort(a_exp)
    s_exp = a_exp[order]
    counts = jax.ops.segment_sum(jnp.ones((A,), jnp.int32), a_exp, num_segments=N_EXPERTS)
    starts = jnp.cumsum(counts) - counts
    pcounts = (counts + MOE_BLOCK - 1) // MOE_BLOCK * MOE_BLOCK
    pends = jnp.cumsum(pcounts)
    pstarts = pends - pcounts
    dest = pstarts[s_exp] + (jnp.arange(A, dtype=jnp.int32) - starts[s_exp])
    nb = -(-(A + N_EXPERTS * (MOE_BLOCK - 1)) // MOE_BLOCK)
    P = nb * MOE_BLOCK
    buf_tok = jnp.full((P,), M, jnp.int32).at[dest].set(a_tok[order])
    buf_w = jnp.zeros((P,), jnp.float32).at[dest].set(a_w[order])
    blk_exp = jnp.clip(jnp.searchsorted(pends, jnp.arange(nb, dtype=jnp.int32) * MOE_BLOCK, side='right'), 0, N_EXPERTS - 1)
    x_pad = jnp.concatenate([xt, jnp.zeros((1, D), xt.dtype)], axis=0)
    xb = x_pad[buf_tok].reshape(nb, MOE_BLOCK, D)

    def expert_block(args):
        xblk, e = args
        hid = jax.nn.silu(xblk @ w1[e]) * (xblk @ w3[e])
        return hid @ w2[e]

    yb = lax.map(expert_block, (xb, blk_exp)).reshape(P, D)
    y = jax.ops.segment_sum(yb * buf_w[:, None].astype(yb.dtype), buf_tok, num_segments=M + 1)
    return y[:M]


def _ada(c, w, b):
    return jnp.split(jax.nn.silu(c) @ w + b, 6, axis=-1)


def setup_inputs(seed: int = 0) -> dict:
    key = jax.random.key(seed)
    ks = jax.random.split(key, 24)
    D = D_MODEL

    def nrm(k, shape, scale):
        return jax.random.normal(k, shape, jnp.float32) * scale

    return {
        'x': nrm(ks[0], (BATCH, SEQ, D), 1.0),
        'c': nrm(ks[1], (BATCH, D), 1.0),
        'ctx': nrm(ks[2], (BATCH, CTX_LEN, D), 1.0),
        'c_ctx': nrm(ks[3], (D,), 1.0),
        'w_ada': nrm(ks[4], (DEPTH, D, 6 * D), 0.5 * D ** -0.5),
        'b_ada': nrm(ks[5], (DEPTH, 6 * D), 0.02),
        'norm1_g': 1.0 + nrm(ks[6], (DEPTH, D), 0.1),
        'w_in': nrm(ks[7], (DEPTH, D, IN_WIDTH), D ** -0.5),
        'q_norm_g': 1.0 + nrm(ks[8], (DEPTH, HEAD_DIM), 0.1),
        'k_norm_g': 1.0 + nrm(ks[9], (DEPTH, HEAD_DIM), 0.1),
        'gla_gate_w2': nrm(ks[10], (DEPTH, 2, GLA_GATE_RANK, GLA_K_W), GLA_GATE_RANK ** -0.5),
        'gla_gate_b': nrm(ks[11], (DEPTH, 2, GLA_K_W), 0.1),
        'gla_norm_g': 1.0 + nrm(ks[12], (DEPTH, GLA_DV), 0.1),
        'w_out': nrm(ks[13], (DEPTH, MIX_WIDTH, D), MIX_WIDTH ** -0.5),
        'norm2_g': 1.0 + nrm(ks[14], (DEPTH, D), 0.1),
        'router_grp_w': nrm(ks[15], (DEPTH, D, N_GROUPS), D ** -0.5),
        'router_grp_b': nrm(ks[16], (DEPTH, N_GROUPS), 0.01),
        'router_exp_w': nrm(ks[17], (DEPTH, D, N_EXPERTS), D ** -0.5),
        'router_exp_b': nrm(ks[18], (DEPTH, N_EXPERTS), 0.01),
        'moe_w1': nrm(ks[19], (DEPTH, N_EXPERTS, D, EXPERT_FF), D ** -0.5),
        'moe_w3': nrm(ks[20], (DEPTH, N_EXPERTS, D, EXPERT_FF), D ** -0.5),
        'moe_w2': nrm(ks[21], (DEPTH, N_EXPERTS, EXPERT_FF, D), EXPERT_FF ** -0.5),
    }


def reference(x, c, ctx, c_ctx, w_ada, b_ada, norm1_g, w_in, q_norm_g, k_norm_g, gla_gate_w2, gla_gate_b, gla_norm_g, w_out, norm2_g, router_grp_w, router_grp_b, router_exp_w, router_exp_b, moe_w1, moe_w3, moe_w2):
    B, T, D = x.shape
    Tc = ctx.shape[1]
    cos, sin = _axial_angles(T)
    xc = ctx
    for layer in range(DEPTH):
        last = layer == DEPTH - 1
        sh1, sc1, gt1, sh2, sc2, gt2 = [m[:, None, :] for m in _ada(c, w_ada[layer], b_ada[layer])]
        sh1c, sc1c, gt1c, sh2c, sc2c, gt2c = _ada(c_ctx, w_ada[layer], b_ada[layer])

        h = _rms_norm(x, norm1_g[layer]) * (1.0 + sc1) + sh1
        hc = _rms_norm(xc, norm1_g[layer]) * (1.0 + sc1c) + sh1c
        mix, mix_c = _token_mixers(h, hc, w_in[layer], q_norm_g[layer], k_norm_g[layer], gla_gate_w2[layer], gla_gate_b[layer], gla_norm_g[layer], cos, sin, not last)
        x = x + gt1 * (mix @ w_out[layer])
        h2 = _rms_norm(x, norm2_g[layer]) * (1.0 + sc2) + sh2
        moe_args = (router_grp_w[layer], router_grp_b[layer], router_exp_w[layer], router_exp_b[layer], moe_w1[layer], moe_w3[layer], moe_w2[layer])
        if last:
            ff = _moe_ffn(h2.reshape(B * T, D), *moe_args).reshape(B, T, D)
            x = x + gt2 * ff
        else:
            xc = xc + gt1c * (mix_c @ w_out[layer])
            h2c = _rms_norm(xc, norm2_g[layer]) * (1.0 + sc2c) + sh2c
            tokens = jnp.concatenate([h2.reshape(B * T, D), h2c.reshape(B * Tc, D)], axis=0)
            ff = _moe_ffn(tokens, *moe_args)
            x = x + gt2 * ff[:B * T].reshape(B, T, D)
            xc = xc + gt2c * ff[B * T:].reshape(B, Tc, D)
    return x
```

```python
import functools

import jax
import jax.numpy as jnp
from jax import lax
from jax.experimental import pallas as pl
from jax.experimental.pallas import tpu as pltpu

EPS = 1e-6
GRID_W = 64
ROPE_THETA = 10000.0

ATTN_HEADS = 8
ATTN_KV_HEADS = 2
HEAD_DIM = 128
GQA_GROUP = ATTN_HEADS // ATTN_KV_HEADS

GLA_HEADS = 4
GLA_DK = 128
GLA_DV = 256
GLA_GATE_RANK = 16
GLA_GATE_NORMALIZER = 16.0
GLA_CHUNK = 64

N_GROUPS = 8
EXPERTS_PER_GROUP = 8
N_EXPERTS = N_GROUPS * EXPERTS_PER_GROUP
TOP_K = 2
MOE_BLOCK = 128

LANES = 128
VMEM_LIMIT = 56 * 1024 * 1024

ATTN_Q_W = ATTN_HEADS * HEAD_DIM
ATTN_KV_W = ATTN_KV_HEADS * HEAD_DIM
GLA_K_W = GLA_HEADS * GLA_DK
GLA_V_W = GLA_HEADS * GLA_DV

BF16 = jnp.bfloat16
F32 = jnp.float32


def _cparams(*sem):
    return pltpu.CompilerParams(dimension_semantics=sem, vmem_limit_bytes=VMEM_LIMIT)


def _ada_kernel(c_ref, w_ref, b_ref, o_ref):
    c = c_ref[...]
    s = c * jax.nn.sigmoid(c)
    o_ref[...] = jnp.dot(s.astype(BF16), w_ref[...].astype(BF16), preferred_element_type=F32) + b_ref[...]


def _ada(c8, w, b):
    d, n = w.shape
    tn = 1024
    return pl.pallas_call(
        _ada_kernel,
        grid=(n // tn,),
        in_specs=[
            pl.BlockSpec((8, d), lambda j: (0, 0)),
            pl.BlockSpec((d, tn), lambda j: (0, j)),
            pl.BlockSpec((1, tn), lambda j: (0, j)),
        ],
        out_specs=pl.BlockSpec((8, tn), lambda j: (0, j)),
        out_shape=jax.ShapeDtypeStruct((8, n), F32),
        compiler_params=_cparams("arbitrary"),
        name="ada",
    )(c8, w, b.reshape(1, n))


def _swap32(y):
    lane = lax.broadcasted_iota(jnp.int32, y.shape, 1)
    return jnp.where((lane & 63) < 32, pltpu.roll(y, 96, 1), pltpu.roll(y, 32, 1))


def _head_norm_rope(a, g, cos, sin):
    y = a * lax.rsqrt(jnp.mean(a * a, axis=-1, keepdims=True) + EPS) * g
    return y * cos + _swap32(y) * sin


def _in_proj_kernel(x_ref, g1_ref, sc_ref, sh_ref, w_ref, qg_ref, kg_ref, cos_ref, sin_ref,
                    q_ref, k_ref, v_ref, gq_ref, gk_ref, gv_ref, og_ref, lr_ref):
    x = x_ref[0]
    xn = x * lax.rsqrt(jnp.mean(x * x, axis=-1, keepdims=True) + EPS)
    h = (xn * g1_ref[...] * (1.0 + sc_ref[0]) + sh_ref[0]).astype(BF16)
    cos = cos_ref[...]
    sin = sin_ref[...]

    def proj(off, width):
        return jnp.dot(h, w_ref[:, off:off + width], preferred_element_type=F32)

    off = 0
    for hd in range(ATTN_HEADS):
        a = proj(off, HEAD_DIM)
        q_ref[0, :, hd * HEAD_DIM:(hd + 1) * HEAD_DIM] = _head_norm_rope(a, qg_ref[...], cos, sin).astype(BF16)
        off += HEAD_DIM
    for hd in range(ATTN_KV_HEADS):
        a = proj(off, HEAD_DIM)
        k_ref[0, :, hd * HEAD_DIM:(hd + 1) * HEAD_DIM] = _head_norm_rope(a, kg_ref[...], cos, sin).astype(BF16)
        off += HEAD_DIM
    v_ref[0] = proj(off, ATTN_KV_W).astype(BF16)
    off += ATTN_KV_W
    gq_ref[0] = proj(off, GLA_K_W)
    off += GLA_K_W
    gk_ref[0] = proj(off, GLA_K_W)
    off += GLA_K_W
    gv_ref[0] = proj(off, GLA_V_W).astype(BF16)
    off += GLA_V_W
    og_ref[0] = proj(off, GLA_V_W)
    off += GLA_V_W
    lr_ref[0] = proj(off, LANES)


def _in_proj(x, g1, sc, sh, w_p, qg, kg, cos_t, sin_t, tm):
    b, t, d = x.shape
    n = w_p.shape[1]
    row = lambda bi, i: (bi, i, 0)
    vec = lambda bi, i: (bi, 0, 0)
    const = lambda bi, i: (0, 0)
    tab = lambda bi, i: (i, 0)
    widths = [(ATTN_Q_W, BF16), (ATTN_KV_W, BF16), (ATTN_KV_W, BF16), (GLA_K_W, F32), (GLA_K_W, F32),
              (GLA_V_W, BF16), (GLA_V_W, F32), (LANES, F32)]
    return pl.pallas_call(
        _in_proj_kernel,
        grid=(b, t // tm),
        in_specs=[
            pl.BlockSpec((1, tm, d), row),
            pl.BlockSpec((1, d), const),
            pl.BlockSpec((1, 1, d), vec),
            pl.BlockSpec((1, 1, d), vec),
            pl.BlockSpec((d, n), const, pipeline_mode=pl.Buffered(1)),
            pl.BlockSpec((1, HEAD_DIM), const),
            pl.BlockSpec((1, HEAD_DIM), const),
            pl.BlockSpec((tm, HEAD_DIM), tab),
            pl.BlockSpec((tm, HEAD_DIM), tab),
        ],
        out_specs=[pl.BlockSpec((1, tm, wd), row) for wd, _ in widths],
        out_shape=[jax.ShapeDtypeStruct((b, t, wd), dt) for wd, dt in widths],
        compiler_params=_cparams("arbitrary", "arbitrary"),
        name="in_proj",
    )(x, g1, sc, sh, w_p, qg, kg, cos_t, sin_t)


def _attn_kernel(q_ref, k_ref, v_ref, o_ref):
    k = k_ref[0]
    v = v_ref[0]
    scale = HEAD_DIM ** -0.5
    for g in range(GQA_GROUP):
        q = q_ref[0, :, g * HEAD_DIM:(g + 1) * HEAD_DIM]
        s = lax.dot_general(q, k, (((1,), (1,)), ((), ())), preferred_element_type=F32) * scale
        m = jnp.max(s, axis=-1, keepdims=True)
        p = jnp.exp(s - m)
        l = jnp.sum(p, axis=-1, keepdims=True)
        pn = (p / l).astype(BF16)
        o = jnp.dot(pn, v, preferred_element_type=F32)
        o_ref[0, :, g * HEAD_DIM:(g + 1) * HEAD_DIM] = o.astype(BF16)


def _attn(q, k_all, v_all, tq):
    b, t, _ = q.shape
    tk = k_all.shape[1]
    gw = GQA_GROUP * HEAD_DIM
    return pl.pallas_call(
        _attn_kernel,
        grid=(b, ATTN_KV_HEADS, t // tq),
        in_specs=[
            pl.BlockSpec((1, tq, gw), lambda bi, hk, i: (bi, i, hk)),
            pl.BlockSpec((1, tk, HEAD_DIM), lambda bi, hk, i: (bi, 0, hk)),
            pl.BlockSpec((1, tk, HEAD_DIM), lambda bi, hk, i: (bi, 0, hk)),
        ],
        out_specs=pl.BlockSpec((1, tq, gw), lambda bi, hk, i: (bi, i, hk)),
        out_shape=jax.ShapeDtypeStruct((b, t, ATTN_Q_W), BF16),
        compiler_params=_cparams("arbitrary", "arbitrary", "arbitrary"),
        name="attn",
    )(q, k_all, v_all)


def _log_sigmoid(x):
    return jnp.minimum(x, 0.0) - jnp.log(1.0 + jnp.exp(-jnp.abs(x)))


def _gla_chunk(q, k, v, lr, w2, bias, s_t, reverse):
    c = q.shape[0]
    logits = jnp.dot(lr, w2, preferred_element_type=F32, precision=lax.Precision.HIGHEST) + bias
    g = _log_sigmoid(logits) * (1.0 / GLA_GATE_NORMALIZER)
    ri = lax.broadcasted_iota(jnp.int32, (c, c), 0)
    ci = lax.broadcasted_iota(jnp.int32, (c, c), 1)
    tri = (ci >= ri) if reverse else (ci <= ri)
    bc = jnp.dot(tri.astype(F32), g, preferred_element_type=F32, precision=lax.Precision.HIGHEST)
    b_last = jnp.sum(g, axis=0, keepdims=True)
    qe = (q * (GLA_DK ** -0.5) * jnp.exp(bc)).astype(BF16)
    ke = (k * jnp.exp(-bc)).astype(BF16)
    kend = (k * jnp.exp(b_last - bc)).astype(BF16)
    a = lax.dot_general(qe, ke, (((1,), (1,)), ((), ())), preferred_element_type=F32)
    a = jnp.where(tri, a, 0.0).astype(BF16)
    o = jnp.dot(a, v, preferred_element_type=F32)
    o = o + lax.dot_general(qe, s_t.astype(BF16), (((1,), (1,)), ((), ())), preferred_element_type=F32)
    upd = lax.dot_general(v, kend, (((0,), (0,)), ((), ())), preferred_element_type=F32)
    s_new = s_t * jnp.exp(b_last) + upd
    return o, s_new


def _gla_kernel(q_ref, k_ref, v_ref, lr_ref, og_ref, qc_ref, kc_ref, vc_ref, lrc_ref,
                w2_ref, b_ref, ng_ref, o_ref, sf_ref, sb_ref, of_ref, ob_ref):
    c = GLA_CHUNK
    t = q_ref.shape[1]
    tc = qc_ref.shape[1]
    w2f, w2b = w2_ref[0], w2_ref[1]
    bf, bb = b_ref[0:1, :], b_ref[1:2, :]

    sf_ref[...] = jnp.zeros_like(sf_ref)
    sb_ref[...] = jnp.zeros_like(sb_ref)

    for j in range(tc // c):
        lo = j * c
        _, s = _gla_chunk(qc_ref[0, lo:lo + c, :], kc_ref[0, lo:lo + c, :], vc_ref[0, lo:lo + c, :],
                          lrc_ref[0, lo:lo + c, :], w2f, bf, sf_ref[...], False)
        sf_ref[...] = s
        lo = tc - (j + 1) * c
        _, s = _gla_chunk(qc_ref[0, lo:lo + c, :], kc_ref[0, lo:lo + c, :], vc_ref[0, lo:lo + c, :],
                          lrc_ref[0, lo:lo + c, :], w2b, bb, sb_ref[...], True)
        sb_ref[...] = s

    nc = t // c

    def body(j, carry):
        lo = pl.multiple_of(j * c, c)
        o, s = _gla_chunk(q_ref[0, pl.ds(lo, c), :], k_ref[0, pl.ds(lo, c), :], v_ref[0, pl.ds(lo, c), :],
                          lr_ref[0, pl.ds(lo, c), :], w2f, bf, sf_ref[...], False)
        sf_ref[...] = s
        of_ref[pl.ds(lo, c), :] = o
        lo = pl.multiple_of((nc - 1 - j) * c, c)
        o, s = _gla_chunk(q_ref[0, pl.ds(lo, c), :], k_ref[0, pl.ds(lo, c), :], v_ref[0, pl.ds(lo, c), :],
                          lr_ref[0, pl.ds(lo, c), :], w2b, bb, sb_ref[...], True)
        sb_ref[...] = s
        ob_ref[pl.ds(lo, c), :] = o
        return carry

    lax.fori_loop(0, nc, body, 0, unroll=2)

    rt = 512
    for i in range(t // rt):
        o = of_ref[i * rt:(i + 1) * rt, :] + ob_ref[i * rt:(i + 1) * rt, :]
        on = o * lax.rsqrt(jnp.mean(o * o, axis=-1, keepdims=True) + EPS) * ng_ref[...]
        og = og_ref[0, i * rt:(i + 1) * rt, :]
        o_ref[0, i * rt:(i + 1) * rt, :] = (on * (og * jax.nn.sigmoid(og))).astype(BF16)


def _gla(gq, gk, gv, lr, og, gqc, gkc, gvc, lrc, w2, bias, ng):
    b, t, _ = gq.shape
    tc = gqc.shape[1]
    hk = lambda bi, h: (bi, 0, h)
    h0 = lambda bi, h: (bi, 0, 0)
    return pl.pallas_call(
        _gla_kernel,
        grid=(b, GLA_HEADS),
        in_specs=[
            pl.BlockSpec((1, t, GLA_DK), hk),
            pl.BlockSpec((1, t, GLA_DK), hk),
            pl.BlockSpec((1, t, GLA_DV), hk),
            pl.BlockSpec((1, t, LANES), h0),
            pl.BlockSpec((1, t, GLA_DV), hk),
            pl.BlockSpec((1, tc, GLA_DK), hk),
            pl.BlockSpec((1, tc, GLA_DK), hk),
            pl.BlockSpec((1, tc, GLA_DV), hk),
            pl.BlockSpec((1, tc, LANES), h0),
            pl.BlockSpec((2, LANES, GLA_DK), lambda bi, h: (0, 0, h)),
            pl.BlockSpec((2, GLA_DK), lambda bi, h: (0, h)),
            pl.BlockSpec((1, GLA_DV), lambda bi, h: (0, 0)),
        ],
        out_specs=pl.BlockSpec((1, t, GLA_DV), hk),
        out_shape=jax.ShapeDtypeStruct((b, t, GLA_V_W), BF16),
        scratch_shapes=[
            pltpu.VMEM((GLA_DV, GLA_DK), F32),
            pltpu.VMEM((GLA_DV, GLA_DK), F32),
            pltpu.VMEM((t, GLA_DV), F32),
            pltpu.VMEM((t, GLA_DV), F32),
        ],
        compiler_params=_cparams("arbitrary", "arbitrary"),
        name="gla",
    )(gq, gk, gv, lr, og, gqc, gkc, gvc, lrc, w2, bias, ng)


def _out_proj_kernel(attn_ref, gla_ref, wo_ref, x_ref, gt_ref, g2_ref, sc_ref, sh_ref, wr_ref, br_ref,
                     x1_ref, h2_ref, ids_ref, wts_ref):
    y = jnp.dot(attn_ref[0], wo_ref[0:ATTN_Q_W, :], preferred_element_type=F32)
    y = y + jnp.dot(gla_ref[0], wo_ref[ATTN_Q_W:, :], preferred_element_type=F32)
    x1 = x_ref[0] + gt_ref[0] * y
    x1_ref[0] = x1
    xn = x1 * lax.rsqrt(jnp.mean(x1 * x1, axis=-1, keepdims=True) + EPS)
    h2 = xn * g2_ref[...] * (1.0 + sc_ref[0]) + sh_ref[0]
    h2_ref[0] = h2

    logits = jnp.dot(h2, wr_ref[...], preferred_element_type=F32, precision=lax.Precision.HIGHEST) + br_ref[...]
    lane = lax.broadcasted_iota(jnp.int32, logits.shape, 1)
    lane_f = lane.astype(F32)
    neg = jnp.float32(-jnp.inf)

    def first_argmax(vals):
        m = jnp.max(vals, axis=-1, keepdims=True)
        idx = jnp.min(jnp.where(vals == m, lane_f, float(LANES)), axis=-1, keepdims=True)
        return m, idx

    lg = jnp.where(lane < N_GROUPS, logits, neg)
    mg, grp = first_argmax(lg)
    pg_sel = 1.0 / jnp.sum(jnp.exp(lg - mg), axis=-1, keepdims=True)
    lo = N_GROUPS + grp * EXPERTS_PER_GROUP
    in_grp = (lane_f >= lo) & (lane_f < lo + EXPERTS_PER_GROUP)
    le = jnp.where(in_grp, logits, neg)
    v1, i1 = first_argmax(le)
    v2, i2 = first_argmax(jnp.where(lane_f == i1, neg, le))
    e2 = jnp.exp(v2 - v1)
    w1 = pg_sel / (1.0 + e2)
    w2 = pg_sel * e2 / (1.0 + e2)
    ids = jnp.where(lane == 0, i1 - N_GROUPS, jnp.where(lane == 1, i2 - N_GROUPS, 0.0))
    ids_ref[0] = ids.astype(jnp.int32)
    wts_ref[0] = jnp.where(lane == 0, w1, jnp.where(lane == 1, w2, 0.0))


def _out_proj(attn, gla, wo, x, gt1, g2, sc2, sh2, wr, br, tm):
    b, t, d = x.shape
    row = lambda bi, i: (bi, i, 0)
    vec = lambda bi, i: (bi, 0, 0)
    const = lambda bi, i: (0, 0)
    return pl.pallas_call(
        _out_proj_kernel,
        grid=(b, t // tm),
        in_specs=[
            pl.BlockSpec((1, tm, ATTN_Q_W), row),
            pl.BlockSpec((1, tm, GLA_V_W), row),
            pl.BlockSpec(wo.shape, const, pipeline_mode=pl.Buffered(1)),
            pl.BlockSpec((1, tm, d), row),
            pl.BlockSpec((1, 1, d), vec),
            pl.BlockSpec((1, d), const),
            pl.BlockSpec((1, 1, d), vec),
            pl.BlockSpec((1, 1, d), vec),
            pl.BlockSpec((d, LANES), const),
            pl.BlockSpec((1, LANES), const),
        ],
        out_specs=[
            pl.BlockSpec((1, tm, d), row),
            pl.BlockSpec((1, tm, d), row),
            pl.BlockSpec((1, tm, LANES), row),
            pl.BlockSpec((1, tm, LANES), row),
        ],
        out_shape=[
            jax.ShapeDtypeStruct((b, t, d), F32),
            jax.ShapeDtypeStruct((b, t, d), F32),
            jax.ShapeDtypeStruct((b, t, LANES), jnp.int32),
            jax.ShapeDtypeStruct((b, t, LANES), F32),
        ],
        compiler_params=_cparams("arbitrary", "arbitrary"),
        name="out_proj",
    )(attn, gla, wo, x, gt1, g2, sc2, sh2, wr, br)


def _moe_kernel(bexp_ref, bs0_ref, bn_ref, bfirst_ref, nblk_ref, order_ref,
                h2_hbm, w1_ref, w3_ref, w2_ref, y_hbm,
                xbuf, ybuf, w1b, w3b, w2b, gsem, ssem):
    i = pl.program_id(0)
    nblk = nblk_ref[0]
    slot = i % 2

    def gather_copy(blk, sl, j):
        tok = order_ref[bs0_ref[blk] + j] // TOP_K
        return pltpu.make_async_copy(h2_hbm.at[pl.ds(tok, 1)], xbuf.at[sl, pl.ds(j, 1)], gsem.at[sl])

    def scatter_copy(blk, sl, j):
        dst = order_ref[bs0_ref[blk] + j]
        return pltpu.make_async_copy(ybuf.at[sl, pl.ds(j, 1)], y_hbm.at[pl.ds(dst, 1)], ssem.at[sl])

    def start_gather(blk, sl):
        lax.fori_loop(0, bn_ref[blk], lambda j, c: (gather_copy(blk, sl, j).start(), c)[1], 0)

    def wait_gather(blk, sl):
        lax.fori_loop(0, bn_ref[blk], lambda j, c: (gather_copy(blk, sl, j).wait(), c)[1], 0)

    def start_scatter(blk, sl):
        lax.fori_loop(0, bn_ref[blk], lambda j, c: (scatter_copy(blk, sl, j).start(), c)[1], 0)

    def wait_scatter(blk, sl):
        lax.fori_loop(0, bn_ref[blk], lambda j, c: (scatter_copy(blk, sl, j).wait(), c)[1], 0)

    @pl.when(i == 0)
    def _():
        xbuf[...] = jnp.zeros_like(xbuf)
        start_gather(0, 0)

    @pl.when(i + 1 < nblk)
    def _():
        start_gather(i + 1, 1 - slot)

    @pl.when(i < nblk)
    def _():
        @pl.when(bfirst_ref[i] == 1)
        def _():
            w1b[...] = w1_ref[...].astype(BF16)
            w3b[...] = w3_ref[...].astype(BF16)
            w2b[...] = w2_ref[...].astype(BF16)

        wait_gather(i, slot)
        x = xbuf[slot].astype(BF16)
        a = jnp.dot(x, w1b[...], preferred_element_type=F32)
        g = jnp.dot(x, w3b[...], preferred_element_type=F32)
        hid = (a * jax.nn.sigmoid(a) * g).astype(BF16)
        y = jnp.dot(hid, w2b[...], preferred_element_type=F32)

        @pl.when(i >= 2)
        def _():
            wait_scatter(i - 2, slot)

        ybuf[slot] = y
        start_scatter(i, slot)

    @pl.when(i == nblk - 1)
    def _():
        @pl.when(i >= 1)
        def _():
            wait_scatter(i - 1, 1 - slot)

        wait_scatter(i, slot)


def _moe(h2, w1, w3, w2, bexp, bs0, bn, bfirst, nblk, order):
    m, d = h2.shape
    ne, _, ff = w1.shape
    nb = bexp.shape[0]
    a = order.shape[0]
    wmap = lambda i, bexp, *_: (bexp[i], 0, 0)
    grid_spec = pltpu.PrefetchScalarGridSpec(
        num_scalar_prefetch=6,
        grid=(nb,),
        in_specs=[
            pl.BlockSpec(memory_space=pl.ANY),
            pl.BlockSpec((None, d, ff), wmap),
            pl.BlockSpec((None, d, ff), wmap),
            pl.BlockSpec((None, ff, d), wmap),
        ],
        out_specs=pl.BlockSpec(memory_space=pl.ANY),
        scratch_shapes=[
            pltpu.VMEM((2, MOE_BLOCK, d), F32),
            pltpu.VMEM((2, MOE_BLOCK, d), F32),
            pltpu.VMEM((d, ff), BF16),
            pltpu.VMEM((d, ff), BF16),
            pltpu.VMEM((ff, d), BF16),
            pltpu.SemaphoreType.DMA((2,)),
            pltpu.SemaphoreType.DMA((2,)),
        ],
    )
    return pl.pallas_call(
        _moe_kernel,
        grid_spec=grid_spec,
        out_shape=jax.ShapeDtypeStruct((a, d), F32),
        compiler_params=_cparams("arbitrary"),
        name="moe",
    )(bexp, bs0, bn, bfirst, nblk, order, h2, w1, w3, w2)


def _combine_kernel(x1_ref, y_ref, wts_ref, gt_ref, o_ref):
    d = x1_ref.shape[2]
    w = wts_ref[0]
    ff = w[:, 0:1] * y_ref[0, :, 0:d] + w[:, 1:2] * y_ref[0, :, d:2 * d]
    o_ref[0] = x1_ref[0] + gt_ref[0] * ff


def _combine(x1, y2, wts, gt2, tm):
    b, t, d = x1.shape
    row = lambda bi, i: (bi, i, 0)
    vec = lambda bi, i: (bi, 0, 0)
    return pl.pallas_call(
        _combine_kernel,
        grid=(b, t // tm),
        in_specs=[
            pl.BlockSpec((1, tm, d), row),
            pl.BlockSpec((1, tm, TOP_K * d), row),
            pl.BlockSpec((1, tm, LANES), row),
            pl.BlockSpec((1, 1, d), vec),
        ],
        out_specs=pl.BlockSpec((1, tm, d), row),
        out_shape=jax.ShapeDtypeStruct((b, t, d), F32),
        compiler_params=_cparams("arbitrary", "arbitrary"),
        name="combine",
    )(x1, y2, wts, gt2)


def _rope_tables(t):
    pos = jnp.arange(t, dtype=jnp.int32)
    r = (pos // GRID_W).astype(F32)
    col = (pos % GRID_W).astype(F32)
    n_freq = HEAD_DIM // 4
    inv = ROPE_THETA ** (-jnp.arange(n_freq, dtype=F32) / n_freq)
    ar = r[:, None] * inv
    ac = col[:, None] * inv
    cos_t = jnp.concatenate([jnp.cos(ar), jnp.cos(ar), jnp.cos(ac), jnp.cos(ac)], axis=1)
    sin_t = jnp.concatenate([-jnp.sin(ar), jnp.sin(ar), -jnp.sin(ac), jnp.sin(ac)], axis=1)
    return cos_t, sin_t


def _block_plan(eid_flat, nb):
    a = eid_flat.shape[0]
    order = jnp.argsort(eid_flat).astype(jnp.int32)
    counts = jnp.zeros((N_EXPERTS,), jnp.int32).at[eid_flat].add(1)
    starts = jnp.cumsum(counts) - counts
    nblk_e = (counts + MOE_BLOCK - 1) // MOE_BLOCK
    bends = jnp.cumsum(nblk_e)
    bstarts = bends - nblk_e
    nblk = bends[-1]
    bi = jnp.arange(nb, dtype=jnp.int32)
    last = jnp.maximum(nblk - 1, 0)
    bic = jnp.minimum(bi, last)
    bexp = jnp.clip(jnp.searchsorted(bends, bic, side='right'), 0, N_EXPERTS - 1).astype(jnp.int32)
    r0 = (bic - bstarts[bexp]) * MOE_BLOCK
    bs0 = (starts[bexp] + r0).astype(jnp.int32)
    bn = jnp.where(bi < nblk, jnp.minimum(counts[bexp] - r0, MOE_BLOCK), 0).astype(jnp.int32)
    bfirst = ((r0 == 0) & (bi < nblk)).astype(jnp.int32)
    return bexp, bs0, bn, bfirst, nblk.reshape(1).astype(jnp.int32), order


def kernel(x, c, ctx, c_ctx, w_ada, b_ada, norm1_g, w_in, q_norm_g, k_norm_g, gla_gate_w2, gla_gate_b, gla_norm_g, w_out, norm2_g, router_grp_w, router_grp_b, router_exp_w, router_exp_b, moe_w1, moe_w3, moe_w2):
    b, t, d = x.shape
    tc = ctx.shape[1]
    depth = w_ada.shape[0]
    assert depth == 1, "single-layer stack: the context stream only feeds keys/values and GLA states"
    layer = 0

    c8 = jnp.zeros((8, d), F32).at[0:b].set(c).at[b].set(c_ctx)
    mod = _ada(c8, w_ada[layer], b_ada[layer])
    sh1, sc1, gt1, sh2, sc2, gt2 = [mod[0:b, i * d:(i + 1) * d].reshape(b, 1, d) for i in range(6)]
    sh1c, sc1c = [jnp.broadcast_to(mod[b, i * d:(i + 1) * d].reshape(1, 1, d), (b, 1, d)) for i in range(2)]

    wi = w_in[layer]
    o_aq, o_ak, o_av, o_gq, o_gk, o_gv, o_lr, o_og = 0, 1024, 1280, 1536, 2048, 2560, 3584, 3616
    w_p = jnp.concatenate([
        wi[:, o_aq:o_lr], wi[:, o_og:o_og + GLA_V_W], wi[:, o_lr:o_og],
        jnp.zeros((d, LANES - 2 * GLA_GATE_RANK), F32)], axis=1).astype(BF16)

    cos_t, sin_t = _rope_tables(t)
    g1 = norm1_g[layer].reshape(1, d)
    qg = q_norm_g[layer].reshape(1, HEAD_DIM)
    kg = k_norm_g[layer].reshape(1, HEAD_DIM)
    q, k, v, gq, gk, gv, og, lr = _in_proj(x, g1, sc1, sh1, w_p, qg, kg, cos_t, sin_t, 256)
    ones_t = jnp.ones((tc, HEAD_DIM), F32)
    _, kc, vc, gqc, gkc, gvc, _, lrc = _in_proj(ctx, g1, sc1c, sh1c, w_p, qg, kg, ones_t, jnp.zeros_like(ones_t), tc)

    attn = _attn(q, jnp.concatenate([kc, k], axis=1), jnp.concatenate([vc, v], axis=1), 256)
    r = GLA_GATE_RANK
    w2p = jnp.zeros((2, LANES, GLA_K_W), F32)
    w2p = w2p.at[0, 0:r].set(gla_gate_w2[layer, 0]).at[1, r:2 * r].set(gla_gate_w2[layer, 1])
    gla = _gla(gq, gk, gv, lr, og, gqc, gkc, gvc, lrc, w2p, gla_gate_b[layer],
               gla_norm_g[layer].reshape(1, GLA_DV))

    wr = jnp.concatenate([router_grp_w[layer], router_exp_w[layer],
                          jnp.zeros((d, LANES - N_GROUPS - N_EXPERTS), F32)], axis=1)
    br = jnp.concatenate([router_grp_b[layer], router_exp_b[layer],
                          jnp.zeros((LANES - N_GROUPS - N_EXPERTS,), F32)]).reshape(1, LANES)
    x1, h2, ids, wts = _out_proj(attn, gla, w_out[layer].astype(BF16), x, gt1, norm2_g[layer].reshape(1, d),
                                 sc2, sh2, wr, br, 256)

    m = b * t
    n_assign = m * TOP_K
    nb = -(-(n_assign + N_EXPERTS * (MOE_BLOCK - 1)) // MOE_BLOCK)
    eid_flat = ids[:, :, 0:TOP_K].reshape(n_assign)
    plan = _block_plan(eid_flat, nb)
    y2 = _moe(h2.reshape(m, d), moe_w1[layer], moe_w3[layer], moe_w2[layer], *plan)
    return _combine(x1, y2.reshape(b, t, TOP_K * d), wts, gt2, 256)
```

```python
import functools

import jax
import jax.numpy as jnp
from jax import lax
from jax.experimental import pallas as pl
from jax.experimental.pallas import tpu as pltpu

EPS = 1e-6
GRID_W = 64
ROPE_THETA = 10000.0

ATTN_HEADS = 8
ATTN_KV_HEADS = 2
HEAD_DIM = 128
GQA_GROUP = ATTN_HEADS // ATTN_KV_HEADS

GLA_HEADS = 4
GLA_DK = 128
GLA_DV = 256
GLA_GATE_RANK = 16
GLA_GATE_NORMALIZER = 16.0
GLA_CHUNK = 64

N_GROUPS = 8
EXPERTS_PER_GROUP = 8
N_EXPERTS = N_GROUPS * EXPERTS_PER_GROUP
TOP_K = 2
MOE_BLOCK = 128
ROW_GROUP_LOG2 = 3
ROW_GROUP = 1 << ROW_GROUP_LOG2

LOG2_E = 1.4426950408889634
LANES = 128
VMEM_LIMIT = 56 * 1024 * 1024

ATTN_Q_W = ATTN_HEADS * HEAD_DIM
ATTN_KV_W = ATTN_KV_HEADS * HEAD_DIM
GLA_K_W = GLA_HEADS * GLA_DK
GLA_V_W = GLA_HEADS * GLA_DV

BF16 = jnp.bfloat16
F32 = jnp.float32


def _cparams(*sem):
    return pltpu.CompilerParams(dimension_semantics=sem, vmem_limit_bytes=VMEM_LIMIT)


def _ada_kernel(c_ref, w_ref, b_ref, o_ref):
    c = c_ref[...]
    s = c * jax.nn.sigmoid(c)
    o_ref[...] = jnp.dot(s.astype(BF16), w_ref[...].astype(BF16), preferred_element_type=F32) + b_ref[...]


def _ada(c8, w, b):
    d, n = w.shape
    tn = 1024
    return pl.pallas_call(
        _ada_kernel,
        grid=(n // tn,),
        in_specs=[
            pl.BlockSpec((8, d), lambda j: (0, 0)),
            pl.BlockSpec((d, tn), lambda j: (0, j)),
            pl.BlockSpec((1, tn), lambda j: (0, j)),
        ],
        out_specs=pl.BlockSpec((8, tn), lambda j: (0, j)),
        out_shape=jax.ShapeDtypeStruct((8, n), F32),
        compiler_params=_cparams("arbitrary"),
        name="ada",
    )(c8, w, b.reshape(1, n))


def _swap32(y):
    lane = lax.broadcasted_iota(jnp.int32, y.shape, 1)
    return jnp.where((lane & 63) < 32, pltpu.roll(y, 96, 1), pltpu.roll(y, 32, 1))


def _head_norm_rope(a, g, cos, sin):
    y = a * lax.rsqrt(jnp.mean(a * a, axis=-1, keepdims=True) + EPS) * g
    return y * cos + _swap32(y) * sin


def _in_proj_kernel(x_ref, g1_ref, sc_ref, sh_ref, w_ref, qg_ref, kg_ref, cos_ref, sin_ref,
                    q_ref, k_ref, v_ref, gq_ref, gk_ref, gv_ref, og_ref, lr_ref):
    x = x_ref[0]
    xn = x * lax.rsqrt(jnp.mean(x * x, axis=-1, keepdims=True) + EPS)
    h = (xn * g1_ref[...] * (1.0 + sc_ref[0]) + sh_ref[0]).astype(BF16)
    cos = cos_ref[...]
    sin = sin_ref[...]

    def proj(off, width):
        return jnp.dot(h, w_ref[:, off:off + width], preferred_element_type=F32)

    qg = qg_ref[...] * (HEAD_DIM ** -0.5 * LOG2_E)
    off = 0
    for hd in range(ATTN_HEADS):
        a = proj(off, HEAD_DIM)
        q_ref[0, :, hd * HEAD_DIM:(hd + 1) * HEAD_DIM] = _head_norm_rope(a, qg, cos, sin).astype(BF16)
        off += HEAD_DIM
    for hd in range(ATTN_KV_HEADS):
        a = proj(off, HEAD_DIM)
        k_ref[0, :, hd * HEAD_DIM:(hd + 1) * HEAD_DIM] = _head_norm_rope(a, kg_ref[...], cos, sin).astype(BF16)
        off += HEAD_DIM
    v_ref[0] = proj(off, ATTN_KV_W).astype(BF16)
    off += ATTN_KV_W
    gq_ref[0] = proj(off, GLA_K_W)
    off += GLA_K_W
    gk_ref[0] = proj(off, GLA_K_W)
    off += GLA_K_W
    gv_ref[0] = proj(off, GLA_V_W).astype(BF16)
    off += GLA_V_W
    og_ref[0] = proj(off, GLA_V_W)
    off += GLA_V_W
    lr_ref[0] = proj(off, LANES)


def _in_proj(x, g1, sc, sh, w_p, qg, kg, cos_t, sin_t, tm):
    b, t, d = x.shape
    n = w_p.shape[1]
    row = lambda bi, i: (bi, i, 0)
    vec = lambda bi, i: (bi, 0, 0)
    const = lambda bi, i: (0, 0)
    tab = lambda bi, i: (i, 0)
    widths = [(ATTN_Q_W, BF16), (ATTN_KV_W, BF16), (ATTN_KV_W, BF16), (GLA_K_W, F32), (GLA_K_W, F32),
              (GLA_V_W, BF16), (GLA_V_W, F32), (LANES, F32)]
    return pl.pallas_call(
        _in_proj_kernel,
        grid=(b, t // tm),
        in_specs=[
            pl.BlockSpec((1, tm, d), row),
            pl.BlockSpec((1, d), const),
            pl.BlockSpec((1, 1, d), vec),
            pl.BlockSpec((1, 1, d), vec),
            pl.BlockSpec((d, n), const, pipeline_mode=pl.Buffered(1)),
            pl.BlockSpec((1, HEAD_DIM), const),
            pl.BlockSpec((1, HEAD_DIM), const),
            pl.BlockSpec((tm, HEAD_DIM), tab),
            pl.BlockSpec((tm, HEAD_DIM), tab),
        ],
        out_specs=[pl.BlockSpec((1, tm, wd), row) for wd, _ in widths],
        out_shape=[jax.ShapeDtypeStruct((b, t, wd), dt) for wd, dt in widths],
        compiler_params=_cparams("arbitrary", "arbitrary"),
        name="in_proj",
    )(x, g1, sc, sh, w_p, qg, kg, cos_t, sin_t)


def _attn_kernel(q_ref, k_ref, v_ref, o_ref, *, key_tiles):
    del key_tiles
    k = k_ref[0]
    v = v_ref[0]
    for g in range(GQA_GROUP):
        q = q_ref[0, :, g * HEAD_DIM:(g + 1) * HEAD_DIM]
        s = lax.dot_general(q, k, (((1,), (1,)), ((), ())), preferred_element_type=F32)
        p = jnp.exp2(s - jnp.max(s, axis=-1, keepdims=True))
        l = jnp.sum(p, axis=-1, keepdims=True)
        o = jnp.dot(p.astype(BF16), v, preferred_element_type=F32)
        o_ref[0, :, g * HEAD_DIM:(g + 1) * HEAD_DIM] = (o / l).astype(BF16)


def _attn(q, k_all, v_all, tq):
    b, t, _ = q.shape
    tk = k_all.shape[1]
    gw = GQA_GROUP * HEAD_DIM
    key_tile = 512
    n_full = tk // key_tile
    key_tiles = [(i * key_tile, key_tile) for i in range(n_full)]
    if tk % key_tile:
        key_tiles.append((n_full * key_tile, tk % key_tile))
    return pl.pallas_call(
        functools.partial(_attn_kernel, key_tiles=tuple(key_tiles)),
        grid=(b, ATTN_KV_HEADS, t // tq),
        in_specs=[
            pl.BlockSpec((1, tq, gw), lambda bi, hk, i: (bi, i, hk)),
            pl.BlockSpec((1, tk, HEAD_DIM), lambda bi, hk, i: (bi, 0, hk)),
            pl.BlockSpec((1, tk, HEAD_DIM), lambda bi, hk, i: (bi, 0, hk)),
        ],
        out_specs=pl.BlockSpec((1, tq, gw), lambda bi, hk, i: (bi, i, hk)),
        out_shape=jax.ShapeDtypeStruct((b, t, ATTN_Q_W), BF16),
        compiler_params=_cparams("arbitrary", "arbitrary", "arbitrary"),
        name="attn",
    )(q, k_all, v_all)


def _log_sigmoid(x):
    return jnp.minimum(x, 0.0) - jnp.log(1.0 + jnp.exp(-jnp.abs(x)))


def _gla_chunk(q, k, v, lr, w2, bias, s_t, reverse):
    c = q.shape[0]
    logits = jnp.dot(lr, w2, preferred_element_type=F32, precision=lax.Precision.HIGHEST) + bias
    g = _log_sigmoid(logits) * (1.0 / GLA_GATE_NORMALIZER)
    ri = lax.broadcasted_iota(jnp.int32, (c, c), 0)
    ci = lax.broadcasted_iota(jnp.int32, (c, c), 1)
    tri = (ci >= ri) if reverse else (ci <= ri)
    bc = jnp.dot(tri.astype(F32), g, preferred_element_type=F32, precision=lax.Precision.HIGHEST)
    b_last = jnp.sum(g, axis=0, keepdims=True)
    qe = (q * (GLA_DK ** -0.5) * jnp.exp(bc)).astype(BF16)
    ke = (k * jnp.exp(-bc)).astype(BF16)
    kend = (k * jnp.exp(b_last - bc)).astype(BF16)
    a = lax.dot_general(qe, ke, (((1,), (1,)), ((), ())), preferred_element_type=F32)
    a = jnp.where(tri, a, 0.0).astype(BF16)
    o = jnp.dot(a, v, preferred_element_type=F32)
    o = o + lax.dot_general(qe, s_t.astype(BF16), (((1,), (1,)), ((), ())), preferred_element_type=F32)
    upd = lax.dot_general(v, kend, (((0,), (0,)), ((), ())), preferred_element_type=F32)
    s_new = s_t * jnp.exp(b_last) + upd
    return o, s_new


def _gla_kernel(q_ref, k_ref, v_ref, lr_ref, og_ref, qc_ref, kc_ref, vc_ref, lrc_ref,
                w2_ref, b_ref, ng_ref, o_ref, sf_ref, sb_ref, of_ref, ob_ref):
    c = GLA_CHUNK
    t = q_ref.shape[1]
    tc = qc_ref.shape[1]
    w2f, w2b = w2_ref[0], w2_ref[1]
    bf, bb = b_ref[0:1, :], b_ref[1:2, :]

    sf_ref[...] = jnp.zeros_like(sf_ref)
    sb_ref[...] = jnp.zeros_like(sb_ref)

    for j in range(tc // c):
        lo = j * c
        _, s = _gla_chunk(qc_ref[0, lo:lo + c, :], kc_ref[0, lo:lo + c, :], vc_ref[0, lo:lo + c, :],
                          lrc_ref[0, lo:lo + c, :], w2f, bf, sf_ref[...], False)
        sf_ref[...] = s
        lo = tc - (j + 1) * c
        _, s = _gla_chunk(qc_ref[0, lo:lo + c, :], kc_ref[0, lo:lo + c, :], vc_ref[0, lo:lo + c, :],
                          lrc_ref[0, lo:lo + c, :], w2b, bb, sb_ref[...], True)
        sb_ref[...] = s

    nc = t // c

    def body(j, carry):
        lo = pl.multiple_of(j * c, c)
        o, s = _gla_chunk(q_ref[0, pl.ds(lo, c), :], k_ref[0, pl.ds(lo, c), :], v_ref[0, pl.ds(lo, c), :],
                          lr_ref[0, pl.ds(lo, c), :], w2f, bf, sf_ref[...], False)
        sf_ref[...] = s
        of_ref[pl.ds(lo, c), :] = o
        lo = pl.multiple_of((nc - 1 - j) * c, c)
        o, s = _gla_chunk(q_ref[0, pl.ds(lo, c), :], k_ref[0, pl.ds(lo, c), :], v_ref[0, pl.ds(lo, c), :],
                          lr_ref[0, pl.ds(lo, c), :], w2b, bb, sb_ref[...], True)
        sb_ref[...] = s
        ob_ref[pl.ds(lo, c), :] = o
        return carry

    lax.fori_loop(0, nc, body, 0, unroll=2)

    rt = 512
    for i in range(t // rt):
        o = of_ref[i * rt:(i + 1) * rt, :] + ob_ref[i * rt:(i + 1) * rt, :]
        on = o * lax.rsqrt(jnp.mean(o * o, axis=-1, keepdims=True) + EPS) * ng_ref[...]
        og = og_ref[0, i * rt:(i + 1) * rt, :]
        o_ref[0, i * rt:(i + 1) * rt, :] = (on * (og * jax.nn.sigmoid(og))).astype(BF16)


def _gla(gq, gk, gv, lr, og, gqc, gkc, gvc, lrc, w2, bias, ng):
    b, t, _ = gq.shape
    tc = gqc.shape[1]
    hk = lambda bi, h: (bi, 0, h)
    h0 = lambda bi, h: (bi, 0, 0)
    return pl.pallas_call(
        _gla_kernel,
        grid=(b, GLA_HEADS),
        in_specs=[
            pl.BlockSpec((1, t, GLA_DK), hk),
            pl.BlockSpec((1, t, GLA_DK), hk),
            pl.BlockSpec((1, t, GLA_DV), hk),
            pl.BlockSpec((1, t, LANES), h0),
            pl.BlockSpec((1, t, GLA_DV), hk),
            pl.BlockSpec((1, tc, GLA_DK), hk),
            pl.BlockSpec((1, tc, GLA_DK), hk),
            pl.BlockSpec((1, tc, GLA_DV), hk),
            pl.BlockSpec((1, tc, LANES), h0),
            pl.BlockSpec((2, LANES, GLA_DK), lambda bi, h: (0, 0, h)),
            pl.BlockSpec((2, GLA_DK), lambda bi, h: (0, h)),
            pl.BlockSpec((1, GLA_DV), lambda bi, h: (0, 0)),
        ],
        out_specs=pl.BlockSpec((1, t, GLA_DV), hk),
        out_shape=jax.ShapeDtypeStruct((b, t, GLA_V_W), BF16),
        scratch_shapes=[
            pltpu.VMEM((GLA_DV, GLA_DK), F32),
            pltpu.VMEM((GLA_DV, GLA_DK), F32),
            pltpu.VMEM((t, GLA_DV), F32),
            pltpu.VMEM((t, GLA_DV), F32),
        ],
        compiler_params=_cparams("arbitrary", "arbitrary"),
        name="gla",
    )(gq, gk, gv, lr, og, gqc, gkc, gvc, lrc, w2, bias, ng)


def _out_proj_kernel(attn_ref, gla_ref, wo_ref, x_ref, gt_ref, g2_ref, sc_ref, sh_ref, wr_ref, br_ref,
                     x1_ref, h2_ref, ids_ref, wts_ref):
    y = jnp.dot(attn_ref[0], wo_ref[0:ATTN_Q_W, :], preferred_element_type=F32)
    y = y + jnp.dot(gla_ref[0], wo_ref[ATTN_Q_W:, :], preferred_element_type=F32)
    x1 = x_ref[0] + gt_ref[0] * y
    x1_ref[0] = x1
    xn = x1 * lax.rsqrt(jnp.mean(x1 * x1, axis=-1, keepdims=True) + EPS)
    h2 = xn * g2_ref[...] * (1.0 + sc_ref[0]) + sh_ref[0]
    h2_ref[0] = h2

    logits = jnp.dot(h2, wr_ref[...], preferred_element_type=F32, precision=lax.Precision.HIGHEST) + br_ref[...]
    lane = lax.broadcasted_iota(jnp.int32, logits.shape, 1)
    lane_f = lane.astype(F32)
    neg = jnp.float32(-jnp.inf)

    def first_argmax(vals):
        m = jnp.max(vals, axis=-1, keepdims=True)
        idx = jnp.min(jnp.where(vals == m, lane_f, float(LANES)), axis=-1, keepdims=True)
        return m, idx

    lg = jnp.where(lane < N_GROUPS, logits, neg)
    mg, grp = first_argmax(lg)
    pg_sel = 1.0 / jnp.sum(jnp.exp(lg - mg), axis=-1, keepdims=True)
    lo = N_GROUPS + grp * EXPERTS_PER_GROUP
    in_grp = (lane_f >= lo) & (lane_f < lo + EXPERTS_PER_GROUP)
    le = jnp.where(in_grp, logits, neg)
    v1, i1 = first_argmax(le)
    v2, i2 = first_argmax(jnp.where(lane_f == i1, neg, le))
    e2 = jnp.exp(v2 - v1)
    w1 = pg_sel / (1.0 + e2)
    w2 = pg_sel * e2 / (1.0 + e2)
    ids = jnp.where(lane == 0, i1 - N_GROUPS, jnp.where(lane == 1, i2 - N_GROUPS, 0.0))
    ids_ref[0] = ids.astype(jnp.int32)
    wts_ref[0] = jnp.where(lane == 0, w1, jnp.where(lane == 1, w2, 0.0))


def _out_proj(attn, gla, wo, x, gt1, g2, sc2, sh2, wr, br, tm):
    b, t, d = x.shape
    row = lambda bi, i: (bi, i, 0)
    vec = lambda bi, i: (bi, 0, 0)
    const = lambda bi, i: (0, 0)
    return pl.pallas_call(
        _out_proj_kernel,
        grid=(b, t // tm),
        in_specs=[
            pl.BlockSpec((1, tm, ATTN_Q_W), row),
            pl.BlockSpec((1, tm, GLA_V_W), row),
            pl.BlockSpec(wo.shape, const, pipeline_mode=pl.Buffered(1)),
            pl.BlockSpec((1, tm, d), row),
            pl.BlockSpec((1, 1, d), vec),
            pl.BlockSpec((1, d), const),
            pl.BlockSpec((1, 1, d), vec),
            pl.BlockSpec((1, 1, d), vec),
            pl.BlockSpec((d, LANES), const),
            pl.BlockSpec((1, LANES), const),
        ],
        out_specs=[
            pl.BlockSpec((1, tm, d), row),
            pl.BlockSpec((1, tm, d), row),
            pl.BlockSpec((1, tm, LANES), row),
            pl.BlockSpec((1, tm, LANES), row),
        ],
        out_shape=[
            jax.ShapeDtypeStruct((b, t, d), F32),
            jax.ShapeDtypeStruct((b, t, d), F32),
            jax.ShapeDtypeStruct((b, t, LANES), jnp.int32),
            jax.ShapeDtypeStruct((b, t, LANES), F32),
        ],
        compiler_params=_cparams("arbitrary", "arbitrary"),
        name="out_proj",
    )(attn, gla, wo, x, gt1, g2, sc2, sh2, wr, br)


def _moe_kernel(bexp_ref, bs0_ref, bn_ref, bfirst_ref, nblk_ref, order_ref,
                h2_hbm, w1_ref, w3_ref, w2_ref, y_hbm,
                xbuf, ybuf, w1b, w3b, w2b, gsem, ssem):
    i = pl.program_id(0)
    nblk = nblk_ref[0]
    slot = i % 2

    def gather_copy(sl, hbm_row, j, rows=1):
        return pltpu.make_async_copy(h2_hbm.at[pl.ds(hbm_row, rows)], xbuf.at[sl, pl.ds(j, rows)], gsem.at[sl])

    def scatter_copy(sl, hbm_row, j, rows=1):
        return pltpu.make_async_copy(ybuf.at[sl, pl.ds(j, rows)], y_hbm.at[pl.ds(hbm_row, rows)], ssem.at[sl])

    def gather_row(s, j):
        return lax.shift_right_logical(order_ref[s + j], 1)

    def scatter_row(s, j):
        a = order_ref[s + j]
        return (a & 1) * (y_hbm.shape[0] // TOP_K) + lax.shift_right_logical(a, 1)

    def for_rows(n, per_group, per_row):
        ng = lax.shift_right_logical(n, ROW_GROUP_LOG2)
        lax.fori_loop(0, ng, lambda g, c: (per_group(g * ROW_GROUP), c)[1], 0)
        lax.fori_loop(ng * ROW_GROUP, n, lambda j, c: (per_row(j), c)[1], 0)

    def start_rows(copy, hbm_row, blk, sl):
        s = bs0_ref[blk]

        def group(j0):
            for u in range(ROW_GROUP):
                copy(sl, hbm_row(s, j0 + u), j0 + u).start()

        for_rows(bn_ref[blk], group, lambda j: copy(sl, hbm_row(s, j), j).start())

    def wait_rows(copy, blk, sl):
        for_rows(bn_ref[blk], lambda j0: copy(sl, 0, 0, ROW_GROUP).wait(), lambda j: copy(sl, 0, 0).wait())

    start_gather = functools.partial(start_rows, gather_copy, gather_row)
    wait_gather = functools.partial(wait_rows, gather_copy)
    start_scatter = functools.partial(start_rows, scatter_copy, scatter_row)
    wait_scatter = functools.partial(wait_rows, scatter_copy)

    @pl.when(i == 0)
    def _():
        xbuf[...] = jnp.zeros_like(xbuf)
        start_gather(0, 0)

    @pl.when(i + 1 < nblk)
    def _():
        start_gather(i + 1, 1 - slot)

    @pl.when(i < nblk)
    def _():
        @pl.when(bfirst_ref[i] == 1)
        def _():
            w1b[...] = w1_ref[...].astype(BF16)
            w3b[...] = w3_ref[...].astype(BF16)
            w2b[...] = w2_ref[...].astype(BF16)

        wait_gather(i, slot)
        x = xbuf[slot].astype(BF16)
        a = jnp.dot(x, w1b[...], preferred_element_type=F32)
        g = jnp.dot(x, w3b[...], preferred_element_type=F32)
        hid = (a * jax.nn.sigmoid(a) * g).astype(BF16)
        y = jnp.dot(hid, w2b[...], preferred_element_type=F32)

        @pl.when(i >= 2)
        def _():
            wait_scatter(i - 2, slot)

        ybuf[slot] = y
        start_scatter(i, slot)

    @pl.when(i == nblk - 1)
    def _():
        @pl.when(i >= 1)
        def _():
            wait_scatter(i - 1, 1 - slot)

        wait_scatter(i, slot)


def _moe(h2, w1, w3, w2, bexp, bs0, bn, bfirst, nblk, order):
    m, d = h2.shape
    ne, _, ff = w1.shape
    nb = bexp.shape[0]
    a = order.shape[0]
    wmap = lambda i, bexp, *_: (bexp[i], 0, 0)
    grid_spec = pltpu.PrefetchScalarGridSpec(
        num_scalar_prefetch=6,
        grid=(nb,),
        in_specs=[
            pl.BlockSpec(memory_space=pl.ANY),
            pl.BlockSpec((None, d, ff), wmap),
            pl.BlockSpec((None, d, ff), wmap),
            pl.BlockSpec((None, ff, d), wmap),
        ],
        out_specs=pl.BlockSpec(memory_space=pl.ANY),
        scratch_shapes=[
            pltpu.VMEM((2, MOE_BLOCK, d), F32),
            pltpu.VMEM((2, MOE_BLOCK, d), F32),
            pltpu.VMEM((d, ff), BF16),
            pltpu.VMEM((d, ff), BF16),
            pltpu.VMEM((ff, d), BF16),
            pltpu.SemaphoreType.DMA((2,)),
            pltpu.SemaphoreType.DMA((2,)),
        ],
    )
    return pl.pallas_call(
        _moe_kernel,
        grid_spec=grid_spec,
        out_shape=jax.ShapeDtypeStruct((a, d), F32),
        compiler_params=_cparams("arbitrary"),
        name="moe",
    )(bexp, bs0, bn, bfirst, nblk, order, h2, w1, w3, w2)


def _combine_kernel(x1_ref, y0_ref, y1_ref, wts_ref, gt_ref, o_ref):
    w = wts_ref[0]
    ff = w[:, 0:1] * y0_ref[0] + w[:, 1:2] * y1_ref[0]
    o_ref[0] = x1_ref[0] + gt_ref[0] * ff


def _combine(x1, y2, wts, gt2, tm):
    b, t, d = x1.shape
    row = lambda bi, i: (bi, i, 0)
    vec = lambda bi, i: (bi, 0, 0)
    return pl.pallas_call(
        _combine_kernel,
        grid=(b, t // tm),
        in_specs=[
            pl.BlockSpec((1, tm, d), row),
            pl.BlockSpec((None, 1, tm, d), lambda bi, i: (0, bi, i, 0)),
            pl.BlockSpec((None, 1, tm, d), lambda bi, i: (1, bi, i, 0)),
            pl.BlockSpec((1, tm, LANES), row),
            pl.BlockSpec((1, 1, d), vec),
        ],
        out_specs=pl.BlockSpec((1, tm, d), row),
        out_shape=jax.ShapeDtypeStruct((b, t, d), F32),
        compiler_params=_cparams("arbitrary", "arbitrary"),
        name="combine",
    )(x1, y2, y2, wts, gt2)


def _rope_tables(t):
    pos = jnp.arange(t, dtype=jnp.int32)
    r = (pos // GRID_W).astype(F32)
    col = (pos % GRID_W).astype(F32)
    n_freq = HEAD_DIM // 4
    inv = ROPE_THETA ** (-jnp.arange(n_freq, dtype=F32) / n_freq)
    ar = r[:, None] * inv
    ac = col[:, None] * inv
    cos_t = jnp.concatenate([jnp.cos(ar), jnp.cos(ar), jnp.cos(ac), jnp.cos(ac)], axis=1)
    sin_t = jnp.concatenate([-jnp.sin(ar), jnp.sin(ar), -jnp.sin(ac), jnp.sin(ac)], axis=1)
    return cos_t, sin_t


def _block_plan(eid_flat, nb):
    a = eid_flat.shape[0]
    order = jnp.argsort(eid_flat).astype(jnp.int32)
    counts = jnp.zeros((N_EXPERTS,), jnp.int32).at[eid_flat].add(1)
    starts = jnp.cumsum(counts) - counts
    nblk_e = (counts + MOE_BLOCK - 1) // MOE_BLOCK
    bends = jnp.cumsum(nblk_e)
    bstarts = bends - nblk_e
    nblk = bends[-1]
    bi = jnp.arange(nb, dtype=jnp.int32)
    last = jnp.maximum(nblk - 1, 0)
    bic = jnp.minimum(bi, last)
    bexp = jnp.clip(jnp.searchsorted(bends, bic, side='right'), 0, N_EXPERTS - 1).astype(jnp.int32)
    r0 = (bic - bstarts[bexp]) * MOE_BLOCK
    bs0 = (starts[bexp] + r0).astype(jnp.int32)
    bn = jnp.where(bi < nblk, jnp.minimum(counts[bexp] - r0, MOE_BLOCK), 0).astype(jnp.int32)
    bfirst = ((r0 == 0) & (bi < nblk)).astype(jnp.int32)
    return bexp, bs0, bn, bfirst, nblk.reshape(1).astype(jnp.int32), order


def kernel(x, c, ctx, c_ctx, w_ada, b_ada, norm1_g, w_in, q_norm_g, k_norm_g, gla_gate_w2, gla_gate_b, gla_norm_g, w_out, norm2_g, router_grp_w, router_grp_b, router_exp_w, router_exp_b, moe_w1, moe_w3, moe_w2):
    b, t, d = x.shape
    tc = ctx.shape[1]
    depth = w_ada.shape[0]
    assert depth == 1, "single-layer stack: the context stream only feeds keys/values and GLA states"
    layer = 0

    c8 = jnp.zeros((8, d), F32).at[0:b].set(c).at[b].set(c_ctx)
    mod = _ada(c8, w_ada[layer], b_ada[layer])
    sh1, sc1, gt1, sh2, sc2, gt2 = [mod[0:b, i * d:(i + 1) * d].reshape(b, 1, d) for i in range(6)]
    sh1c, sc1c = [jnp.broadcast_to(mod[b, i * d:(i + 1) * d].reshape(1, 1, d), (b, 1, d)) for i in range(2)]

    wi = w_in[layer]
    o_aq, o_ak, o_av, o_gq, o_gk, o_gv, o_lr, o_og = 0, 1024, 1280, 1536, 2048, 2560, 3584, 3616
    w_p = jnp.concatenate([
        wi[:, o_aq:o_lr], wi[:, o_og:o_og + GLA_V_W], wi[:, o_lr:o_og],
        jnp.zeros((d, LANES - 2 * GLA_GATE_RANK), F32)], axis=1).astype(BF16)

    cos_t, sin_t = _rope_tables(t)
    g1 = norm1_g[layer].reshape(1, d)
    qg = q_norm_g[layer].reshape(1, HEAD_DIM)
    kg = k_norm_g[layer].reshape(1, HEAD_DIM)
    q, k, v, gq, gk, gv, og, lr = _in_proj(x, g1, sc1, sh1, w_p, qg, kg, cos_t, sin_t, 256)
    ones_t = jnp.ones((tc, HEAD_DIM), F32)
    _, kc, vc, gqc, gkc, gvc, _, lrc = _in_proj(ctx, g1, sc1c, sh1c, w_p, qg, kg, ones_t, jnp.zeros_like(ones_t), tc)

    attn = _attn(q, jnp.concatenate([kc, k], axis=1), jnp.concatenate([vc, v], axis=1), 256)
    r = GLA_GATE_RANK
    w2p = jnp.zeros((2, LANES, GLA_K_W), F32)
    w2p = w2p.at[0, 0:r].set(gla_gate_w2[layer, 0]).at[1, r:2 * r].set(gla_gate_w2[layer, 1])
    gla = _gla(gq, gk, gv, lr, og, gqc, gkc, gvc, lrc, w2p, gla_gate_b[layer],
               gla_norm_g[layer].reshape(1, GLA_DV))

    wr = jnp.concatenate([router_grp_w[layer], router_exp_w[layer],
                          jnp.zeros((d, LANES - N_GROUPS - N_EXPERTS), F32)], axis=1)
    br = jnp.concatenate([router_grp_b[layer], router_exp_b[layer],
                          jnp.zeros((LANES - N_GROUPS - N_EXPERTS,), F32)]).reshape(1, LANES)
    x1, h2, ids, wts = _out_proj(attn, gla, w_out[layer].astype(BF16), x, gt1, norm2_g[layer].reshape(1, d),
                                 sc2, sh2, wr, br, 256)

    m = b * t
    n_assign = m * TOP_K
    nb = -(-(n_assign + N_EXPERTS * (MOE_BLOCK - 1)) // MOE_BLOCK)
    eid_flat = ids[:, :, 0:TOP_K].reshape(n_assign)
    plan = _block_plan(eid_flat, nb)
    y2 = _moe(h2.reshape(m, d), moe_w1[layer], moe_w3[layer], moe_w2[layer], *plan)
    return _combine(x1, y2.reshape(TOP_K, b, t, d), wts, gt2, 256)
```

```python
import functools

import jax
import jax.numpy as jnp
from jax import lax
from jax.experimental import pallas as pl
from jax.experimental.pallas import tpu as pltpu

EPS = 1e-6
GRID_W = 64
ROPE_THETA = 10000.0

ATTN_HEADS = 8
ATTN_KV_HEADS = 2
HEAD_DIM = 128
GQA_GROUP = ATTN_HEADS // ATTN_KV_HEADS

GLA_HEADS = 4
GLA_DK = 128
GLA_DV = 256
GLA_GATE_RANK = 16
GLA_GATE_NORMALIZER = 16.0
GLA_CHUNK = 64
GLA_CHUNK_LOG2 = 6
GLA_PREP_TILE = 256
GLA_OUT_GROUP = 4

N_GROUPS = 8
EXPERTS_PER_GROUP = 8
N_EXPERTS = N_GROUPS * EXPERTS_PER_GROUP
TOP_K = 2
MOE_BLOCK = 128
ROW_GROUP_LOG2 = 3
ROW_GROUP = 1 << ROW_GROUP_LOG2

LOG2_E = 1.4426950408889634
LANES = 128
VMEM_LIMIT = 56 * 1024 * 1024

ATTN_Q_W = ATTN_HEADS * HEAD_DIM
ATTN_KV_W = ATTN_KV_HEADS * HEAD_DIM
GLA_K_W = GLA_HEADS * GLA_DK
GLA_V_W = GLA_HEADS * GLA_DV

BF16 = jnp.bfloat16
F32 = jnp.float32


def _cparams(*sem):
    return pltpu.CompilerParams(dimension_semantics=sem, vmem_limit_bytes=VMEM_LIMIT)


def _ada_kernel(c_ref, w_ref, b_ref, o_ref):
    c = c_ref[...]
    s = c * jax.nn.sigmoid(c)
    o_ref[...] = jnp.dot(s.astype(BF16), w_ref[...].astype(BF16), preferred_element_type=F32) + b_ref[...]


def _ada(c8, w, b):
    d, n = w.shape
    tn = 1024
    return pl.pallas_call(
        _ada_kernel,
        grid=(n // tn,),
        in_specs=[
            pl.BlockSpec((8, d), lambda j: (0, 0)),
            pl.BlockSpec((d, tn), lambda j: (0, j)),
            pl.BlockSpec((1, tn), lambda j: (0, j)),
        ],
        out_specs=pl.BlockSpec((8, tn), lambda j: (0, j)),
        out_shape=jax.ShapeDtypeStruct((8, n), F32),
        compiler_params=_cparams("arbitrary"),
        name="ada",
    )(c8, w, b.reshape(1, n))


def _swap32(y):
    lane = lax.broadcasted_iota(jnp.int32, y.shape, 1)
    return jnp.where((lane & 63) < 32, pltpu.roll(y, 96, 1), pltpu.roll(y, 32, 1))


def _head_norm_rope(a, g, cos, sin):
    y = a * lax.rsqrt(jnp.mean(a * a, axis=-1, keepdims=True) + EPS) * g
    return y * cos + _swap32(y) * sin


def _in_proj_kernel(x_ref, g1_ref, sc_ref, sh_ref, w_ref, qg_ref, kg_ref, cos_ref, sin_ref,
                    q_ref, k_ref, v_ref, gq_ref, gk_ref, gv_ref, og_ref, lr_ref):
    x = x_ref[0]
    xn = x * lax.rsqrt(jnp.mean(x * x, axis=-1, keepdims=True) + EPS)
    h = (xn * g1_ref[...] * (1.0 + sc_ref[0]) + sh_ref[0]).astype(BF16)
    cos = cos_ref[...]
    sin = sin_ref[...]

    def proj(off, width):
        return jnp.dot(h, w_ref[:, off:off + width], preferred_element_type=F32)

    qg = qg_ref[...] * (HEAD_DIM ** -0.5 * LOG2_E)
    off = 0
    for hd in range(ATTN_HEADS):
        a = proj(off, HEAD_DIM)
        q_ref[0, :, hd * HEAD_DIM:(hd + 1) * HEAD_DIM] = _head_norm_rope(a, qg, cos, sin).astype(BF16)
        off += HEAD_DIM
    for hd in range(ATTN_KV_HEADS):
        a = proj(off, HEAD_DIM)
        k_ref[0, :, hd * HEAD_DIM:(hd + 1) * HEAD_DIM] = _head_norm_rope(a, kg_ref[...], cos, sin).astype(BF16)
        off += HEAD_DIM
    v_ref[0] = proj(off, ATTN_KV_W).astype(BF16)
    off += ATTN_KV_W
    gq_ref[0] = proj(off, GLA_K_W)
    off += GLA_K_W
    gk_ref[0] = proj(off, GLA_K_W)
    off += GLA_K_W
    gv_ref[0] = proj(off, GLA_V_W).astype(BF16)
    off += GLA_V_W
    og_ref[0] = proj(off, GLA_V_W)
    off += GLA_V_W
    lr_ref[0] = proj(off, LANES)


def _in_proj(x, g1, sc, sh, w_p, qg, kg, cos_t, sin_t, tm):
    b, t, d = x.shape
    n = w_p.shape[1]
    row = lambda bi, i: (bi, i, 0)
    vec = lambda bi, i: (bi, 0, 0)
    const = lambda bi, i: (0, 0)
    tab = lambda bi, i: (i, 0)
    widths = [(ATTN_Q_W, BF16), (ATTN_KV_W, BF16), (ATTN_KV_W, BF16), (GLA_K_W, F32), (GLA_K_W, F32),
              (GLA_V_W, BF16), (GLA_V_W, F32), (LANES, F32)]
    return pl.pallas_call(
        _in_proj_kernel,
        grid=(b, t // tm),
        in_specs=[
            pl.BlockSpec((1, tm, d), row),
            pl.BlockSpec((1, d), const),
            pl.BlockSpec((1, 1, d), vec),
            pl.BlockSpec((1, 1, d), vec),
            pl.BlockSpec((d, n), const, pipeline_mode=pl.Buffered(1)),
            pl.BlockSpec((1, HEAD_DIM), const),
            pl.BlockSpec((1, HEAD_DIM), const),
            pl.BlockSpec((tm, HEAD_DIM), tab),
            pl.BlockSpec((tm, HEAD_DIM), tab),
        ],
        out_specs=[pl.BlockSpec((1, tm, wd), row) for wd, _ in widths],
        out_shape=[jax.ShapeDtypeStruct((b, t, wd), dt) for wd, dt in widths],
        compiler_params=_cparams("arbitrary", "arbitrary"),
        name="in_proj",
    )(x, g1, sc, sh, w_p, qg, kg, cos_t, sin_t)


def _attn_kernel(q_ref, k_ref, v_ref, o_ref, *, key_tiles):
    del key_tiles
    k = k_ref[0]
    v = v_ref[0]
    for g in range(GQA_GROUP):
        q = q_ref[0, :, g * HEAD_DIM:(g + 1) * HEAD_DIM]
        s = lax.dot_general(q, k, (((1,), (1,)), ((), ())), preferred_element_type=F32)
        p = jnp.exp2(s - jnp.max(s, axis=-1, keepdims=True))
        l = jnp.sum(p, axis=-1, keepdims=True)
        o = jnp.dot(p.astype(BF16), v, preferred_element_type=F32)
        o_ref[0, :, g * HEAD_DIM:(g + 1) * HEAD_DIM] = (o / l).astype(BF16)


def _attn(q, k_all, v_all, tq):
    b, t, _ = q.shape
    tk = k_all.shape[1]
    gw = GQA_GROUP * HEAD_DIM
    key_tile = 512
    n_full = tk // key_tile
    key_tiles = [(i * key_tile, key_tile) for i in range(n_full)]
    if tk % key_tile:
        key_tiles.append((n_full * key_tile, tk % key_tile))
    return pl.pallas_call(
        functools.partial(_attn_kernel, key_tiles=tuple(key_tiles)),
        grid=(b, ATTN_KV_HEADS, t // tq),
        in_specs=[
            pl.BlockSpec((1, tq, gw), lambda bi, hk, i: (bi, i, hk)),
            pl.BlockSpec((1, tk, HEAD_DIM), lambda bi, hk, i: (bi, 0, hk)),
            pl.BlockSpec((1, tk, HEAD_DIM), lambda bi, hk, i: (bi, 0, hk)),
        ],
        out_specs=pl.BlockSpec((1, tq, gw), lambda bi, hk, i: (bi, i, hk)),
        out_shape=jax.ShapeDtypeStruct((b, t, ATTN_Q_W), BF16),
        compiler_params=_cparams("arbitrary", "arbitrary", "arbitrary"),
        name="attn",
    )(q, k_all, v_all)


def _log_sigmoid(x):
    return jnp.minimum(x, 0.0) - jnp.log(1.0 + jnp.exp(-jnp.abs(x)))


def _split3(x):
    hi = x.astype(BF16)
    r = x - hi.astype(F32)
    mid = r.astype(BF16)
    lo = (r - mid.astype(F32)).astype(BF16)
    return hi, mid, lo


def _dot_exact_lhs(m, x):
    return sum(jnp.dot(m, part, preferred_element_type=F32) for part in _split3(x))


def _prefix_operator(n):
    ri = lax.broadcasted_iota(jnp.int32, (n, n), 0)
    ci = lax.broadcasted_iota(jnp.int32, (n, n), 1)
    same = lax.shift_right_logical(ri, GLA_CHUNK_LOG2) == lax.shift_right_logical(ci, GLA_CHUNK_LOG2)
    return (same & (ci <= ri)).astype(BF16)


def _gate_prep(q, k, lr, w2s, bias2, prefix):
    c = GLA_CHUNK
    hi = lr.astype(BF16).astype(F32)
    mid = (lr - hi).astype(BF16).astype(F32)
    lhs = (hi + pltpu.roll(mid, 2 * GLA_GATE_RANK, 1) + pltpu.roll(hi, 4 * GLA_GATE_RANK, 1)).astype(BF16)
    logits = jnp.dot(lhs, w2s, preferred_element_type=F32) + bias2
    g = _log_sigmoid(logits) * (1.0 / GLA_GATE_NORMALIZER)
    pre = _dot_exact_lhs(prefix, g)
    tot = jnp.concatenate([jnp.broadcast_to(pre[lo + c - 1:lo + c, :], (c, pre.shape[1]))
                           for lo in range(0, pre.shape[0], c)], axis=0)
    dk = GLA_DK
    bcs = (pre[:, :dk], tot[:, dk:] - pre[:, dk:] + g[:, dk:])
    out = []
    for d, bc in enumerate(bcs):
        b_end = tot[:, d * dk:(d + 1) * dk]
        qe = (q * (GLA_DK ** -0.5) * jnp.exp(bc)).astype(BF16)
        ke = (k * jnp.exp(-bc)).astype(BF16)
        kend = (k * jnp.exp(b_end - bc)).astype(BF16)
        out.append((qe, ke, kend, b_end))
    return out


def _state_step(s_ref, v, kend, dec):
    upd = lax.dot_general(v, kend, (((0,), (0,)), ((), ())), preferred_element_type=F32)
    s_ref[...] = s_ref[...] * dec + upd


def _gla_kernel(q_ref, k_ref, v_ref, lr_ref, og_ref, qc_ref, kc_ref, vc_ref, lrc_ref,
                w2_ref, b_ref, ng_ref, o_ref,
                s_ref, qe_ref, ke_ref, kend_ref, dec_ref, kendc_ref, decc_ref, sbf_ref):
    c = GLA_CHUNK
    t = q_ref.shape[1]
    tc = qc_ref.shape[1]
    nc = t // c
    tile = GLA_PREP_TILE
    cpt = tile // c
    dirs = (0, 1)

    def store_dec(ref, d, base, b_end):
        for ch in range(b_end.shape[0] // c):
            ref[d, pl.ds(base + ch, 1), :] = jnp.exp(b_end[ch * c:ch * c + 1, :])

    prefix = _prefix_operator(tile)

    for i in range(tc // tile):
        rows = slice(i * tile, (i + 1) * tile)
        prep_c = _gate_prep(qc_ref[0, rows, :], kc_ref[0, rows, :], lrc_ref[0, rows, :],
                            w2_ref[...], b_ref[...], prefix)
        for d, (_, _, kend, b_end) in enumerate(prep_c):
            kendc_ref[d, rows, :] = kend
            store_dec(decc_ref, d, i * cpt, b_end)
    s_ref[...] = jnp.zeros_like(s_ref)
    for j in range(tc // c):
        for d in dirs:
            ch = j if d == 0 else tc // c - 1 - j
            _state_step(s_ref.at[d], vc_ref[0, ch * c:(ch + 1) * c, :], kendc_ref[d, ch * c:(ch + 1) * c, :],
                        decc_ref[d, ch:ch + 1, :])

    def prep(i, carry):
        lo = pl.multiple_of(i * tile, tile)
        rows = pl.ds(lo, tile)
        prep_l = _gate_prep(q_ref[0, rows, :], k_ref[0, rows, :], lr_ref[0, rows, :],
                            w2_ref[...], b_ref[...], prefix)
        for d, (qe, ke, kend, b_end) in enumerate(prep_l):
            qe_ref[d, rows, :] = qe
            ke_ref[d, rows, :] = ke
            kend_ref[d, rows, :] = kend
            store_dec(dec_ref, d, i * cpt, b_end)
        return carry

    lax.fori_loop(0, t // tile, prep, 0)

    def scan(j, carry):
        for d in dirs:
            ch = j if d == 0 else nc - 1 - j
            rows = pl.ds(pl.multiple_of(ch * c, c), c)
            sbf_ref[d, ch] = s_ref[d].astype(BF16)
            _state_step(s_ref.at[d], v_ref[0, rows, :], kend_ref[d, rows, :], dec_ref[d, pl.ds(ch, 1), :])
        return carry

    lax.fori_loop(0, nc, scan, 0, unroll=4)

    ri = lax.broadcasted_iota(jnp.int32, (c, c), 0)
    ci = lax.broadcasted_iota(jnp.int32, (c, c), 1)
    masks = (ci <= ri, ci >= ri)

    group = GLA_OUT_GROUP

    def out(i, carry):
        chunks = [i * group + u for u in range(group)]
        rows = [pl.ds(pl.multiple_of(ch * c, c), c) for ch in chunks]
        qes = [[qe_ref[d, r, :] for d in dirs] for r in rows]
        scores = [[lax.dot_general(qes[u][d], ke_ref[d, rows[u], :], (((1,), (1,)), ((), ())),
                                   preferred_element_type=F32) for d in dirs] for u in range(group)]
        inter = [[lax.dot_general(qes[u][d], sbf_ref[d, chunks[u]], (((1,), (1,)), ((), ())),
                                  preferred_element_type=F32) for d in dirs] for u in range(group)]
        for u in range(group):
            v = v_ref[0, rows[u], :]
            o = inter[u][0] + inter[u][1]
            for d in dirs:
                a = jnp.where(masks[d], scores[u][d], 0.0).astype(BF16)
                o = o + jnp.dot(a, v, preferred_element_type=F32)
            on = o * lax.rsqrt(jnp.mean(o * o, axis=-1, keepdims=True) + EPS) * ng_ref[...]
            og = og_ref[0, rows[u], :]
            o_ref[0, rows[u], :] = (on * (og * jax.nn.sigmoid(og))).astype(BF16)
        return carry

    lax.fori_loop(0, nc // group, out, 0)


def _gla(gq, gk, gv, lr, og, gqc, gkc, gvc, lrc, w2, bias, ng):
    b, t, _ = gq.shape
    tc = gqc.shape[1]
    hk = lambda bi, h: (bi, 0, h)
    h0 = lambda bi, h: (bi, 0, 0)
    return pl.pallas_call(
        _gla_kernel,
        grid=(b, GLA_HEADS),
        in_specs=[
            pl.BlockSpec((1, t, GLA_DK), hk),
            pl.BlockSpec((1, t, GLA_DK), hk),
            pl.BlockSpec((1, t, GLA_DV), hk),
            pl.BlockSpec((1, t, LANES), h0),
            pl.BlockSpec((1, t, GLA_DV), hk),
            pl.BlockSpec((1, tc, GLA_DK), hk),
            pl.BlockSpec((1, tc, GLA_DK), hk),
            pl.BlockSpec((1, tc, GLA_DV), hk),
            pl.BlockSpec((1, tc, LANES), h0),
            pl.BlockSpec((None, LANES, 2 * GLA_DK), lambda bi, h: (h, 0, 0)),
            pl.BlockSpec((None, 1, 2 * GLA_DK), lambda bi, h: (h, 0, 0)),
            pl.BlockSpec((1, GLA_DV), lambda bi, h: (0, 0)),
        ],
        out_specs=pl.BlockSpec((1, t, GLA_DV), hk),
        out_shape=jax.ShapeDtypeStruct((b, t, GLA_V_W), BF16),
        scratch_shapes=[
            pltpu.VMEM((2, GLA_DV, GLA_DK), F32),
            pltpu.VMEM((2, t, GLA_DK), BF16),
            pltpu.VMEM((2, t, GLA_DK), BF16),
            pltpu.VMEM((2, t, GLA_DK), BF16),
            pltpu.VMEM((2, t // GLA_CHUNK, GLA_DK), F32),
            pltpu.VMEM((2, tc, GLA_DK), BF16),
            pltpu.VMEM((2, tc // GLA_CHUNK, GLA_DK), F32),
            pltpu.VMEM((2, t // GLA_CHUNK, GLA_DV, GLA_DK), BF16),
        ],
        compiler_params=_cparams("arbitrary", "arbitrary"),
        name="gla",
    )(gq, gk, gv, lr, og, gqc, gkc, gvc, lrc, w2, bias, ng)


def _out_proj_kernel(attn_ref, gla_ref, wo_ref, x_ref, gt_ref, g2_ref, sc_ref, sh_ref, wr_ref, br_ref,
                     x1_ref, h2_ref, ids_ref, wts_ref):
    y = jnp.dot(attn_ref[0], wo_ref[0:ATTN_Q_W, :], preferred_element_type=F32)
    y = y + jnp.dot(gla_ref[0], wo_ref[ATTN_Q_W:, :], preferred_element_type=F32)
    x1 = x_ref[0] + gt_ref[0] * y
    x1_ref[0] = x1
    xn = x1 * lax.rsqrt(jnp.mean(x1 * x1, axis=-1, keepdims=True) + EPS)
    h2 = xn * g2_ref[...] * (1.0 + sc_ref[0]) + sh_ref[0]
    hi = h2.astype(BF16)
    hi_f = hi.astype(F32)

    half = h2.shape[1] // 2
    bits = lax.bitcast_convert_type(hi_f, jnp.uint32)
    h2_ref[0] = lax.shift_right_logical(bits[:, :half], jnp.uint32(16)) | (bits[:, half:] & jnp.uint32(0xFFFF0000))

    mid = (h2 - hi_f).astype(BF16)
    logits = (jnp.dot(hi, wr_ref[0], preferred_element_type=F32)
              + jnp.dot(mid, wr_ref[0], preferred_element_type=F32)
              + jnp.dot(hi, wr_ref[1], preferred_element_type=F32)) + br_ref[...]
    lane = lax.broadcasted_iota(jnp.int32, logits.shape, 1)
    lane_f = lane.astype(F32)
    neg = jnp.float32(-jnp.inf)

    def first_argmax(vals):
        m = jnp.max(vals, axis=-1, keepdims=True)
        idx = jnp.min(jnp.where(vals == m, lane_f, float(LANES)), axis=-1, keepdims=True)
        return m, idx

    lg = jnp.where(lane < N_GROUPS, logits, neg)
    mg, grp = first_argmax(lg)
    pg_sel = 1.0 / jnp.sum(jnp.exp(lg - mg), axis=-1, keepdims=True)
    lo = N_GROUPS + grp * EXPERTS_PER_GROUP
    in_grp = (lane_f >= lo) & (lane_f < lo + EXPERTS_PER_GROUP)
    le = jnp.where(in_grp, logits, neg)
    v1, i1 = first_argmax(le)
    v2, i2 = first_argmax(jnp.where(lane_f == i1, neg, le))
    e2 = jnp.exp(v2 - v1)
    w1 = pg_sel / (1.0 + e2)
    w2 = pg_sel * e2 / (1.0 + e2)
    ids = jnp.where(lane == 0, i1 - N_GROUPS, jnp.where(lane == 1, i2 - N_GROUPS, 0.0))
    ids_ref[0] = ids.astype(jnp.int32)
    wts_ref[0] = jnp.where(lane == 0, w1, jnp.where(lane == 1, w2, 0.0))


def _out_proj(attn, gla, wo, x, gt1, g2, sc2, sh2, wr, br, tm):
    b, t, d = x.shape
    row = lambda bi, i: (bi, i, 0)
    vec = lambda bi, i: (bi, 0, 0)
    const = lambda bi, i: (0, 0)
    return pl.pallas_call(
        _out_proj_kernel,
        grid=(b, t // tm),
        in_specs=[
            pl.BlockSpec((1, tm, ATTN_Q_W), row),
            pl.BlockSpec((1, tm, GLA_V_W), row),
            pl.BlockSpec(wo.shape, const, pipeline_mode=pl.Buffered(1)),
            pl.BlockSpec((1, tm, d), row),
            pl.BlockSpec((1, 1, d), vec),
            pl.BlockSpec((1, d), const),
            pl.BlockSpec((1, 1, d), vec),
            pl.BlockSpec((1, 1, d), vec),
            pl.BlockSpec((2, d, LANES), lambda bi, i: (0, 0, 0)),
            pl.BlockSpec((1, LANES), const),
        ],
        out_specs=[
            pl.BlockSpec((1, tm, d), row),
            pl.BlockSpec((1, tm, d // 2), row),
            pl.BlockSpec((1, tm, LANES), row),
            pl.BlockSpec((1, tm, LANES), row),
        ],
        out_shape=[
            jax.ShapeDtypeStruct((b, t, d), F32),
            jax.ShapeDtypeStruct((b, t, d // 2), jnp.uint32),
            jax.ShapeDtypeStruct((b, t, LANES), jnp.int32),
            jax.ShapeDtypeStruct((b, t, LANES), F32),
        ],
        compiler_params=_cparams("arbitrary", "arbitrary"),
        name="out_proj",
    )(attn, gla, wo, x, gt1, g2, sc2, sh2, wr, br)


def _moe_kernel(bexp_ref, bs0_ref, bn_ref, bfirst_ref, nblk_ref, order_ref,
                h2_hbm, w1_ref, w3_ref, w2_ref, y_hbm,
                xbuf, ybuf, w1b, w3b, w2b, gsem, ssem):
    i = pl.program_id(0)
    nblk = nblk_ref[0]
    slot = i % 2

    def gather_copy(sl, hbm_row, j, rows=1):
        return pltpu.make_async_copy(h2_hbm.at[pl.ds(hbm_row, rows)], xbuf.at[sl, pl.ds(j, rows)], gsem.at[sl])

    def scatter_copy(sl, hbm_row, j, rows=1):
        return pltpu.make_async_copy(ybuf.at[sl, pl.ds(j, rows)], y_hbm.at[pl.ds(hbm_row, rows)], ssem.at[sl])

    def gather_row(s, j):
        return lax.shift_right_logical(order_ref[s + j], 1)

    def scatter_row(s, j):
        a = order_ref[s + j]
        return (a & 1) * (y_hbm.shape[0] // TOP_K) + lax.shift_right_logical(a, 1)

    def for_rows(n, per_group, per_row):
        ng = lax.shift_right_logical(n, ROW_GROUP_LOG2)
        lax.fori_loop(0, ng, lambda g, c: (per_group(g * ROW_GROUP), c)[1], 0)
        lax.fori_loop(ng * ROW_GROUP, n, lambda j, c: (per_row(j), c)[1], 0)

    def start_rows(copy, hbm_row, blk, sl):
        s = bs0_ref[blk]

        def group(j0):
            for u in range(ROW_GROUP):
                copy(sl, hbm_row(s, j0 + u), j0 + u).start()

        for_rows(bn_ref[blk], group, lambda j: copy(sl, hbm_row(s, j), j).start())

    def wait_rows(copy, blk, sl):
        for_rows(bn_ref[blk], lambda j0: copy(sl, 0, 0, ROW_GROUP).wait(), lambda j: copy(sl, 0, 0).wait())

    start_gather = functools.partial(start_rows, gather_copy, gather_row)
    wait_gather = functools.partial(wait_rows, gather_copy)
    start_scatter = functools.partial(start_rows, scatter_copy, scatter_row)
    wait_scatter = functools.partial(wait_rows, scatter_copy)

    @pl.when(i == 0)
    def _():
        xbuf[...] = jnp.zeros_like(xbuf)
        start_gather(0, 0)

    @pl.when(i + 1 < nblk)
    def _():
        start_gather(i + 1, 1 - slot)

    @pl.when(i < nblk)
    def _():
        @pl.when(bfirst_ref[i] == 1)
        def _():
            w1b[...] = w1_ref[...].astype(BF16)
            w3b[...] = w3_ref[...].astype(BF16)
            w2b[...] = w2_ref[...].astype(BF16)

        wait_gather(i, slot)
        words = xbuf[slot]
        x_lo = lax.bitcast_convert_type(lax.shift_left(words, jnp.uint32(16)), F32)
        x_hi = lax.bitcast_convert_type(words & jnp.uint32(0xFFFF0000), F32)
        x = jnp.concatenate([x_lo, x_hi], axis=1).astype(BF16)
        a = jnp.dot(x, w1b[...], preferred_element_type=F32)
        g = jnp.dot(x, w3b[...], preferred_element_type=F32)
        hid = (a * jax.nn.sigmoid(a) * g).astype(BF16)
        y = jnp.dot(hid, w2b[...], preferred_element_type=F32)

        @pl.when(i >= 2)
        def _():
            wait_scatter(i - 2, slot)

        ybuf[slot] = y
        start_scatter(i, slot)

    @pl.when(i == nblk - 1)
    def _():
        @pl.when(i >= 1)
        def _():
            wait_scatter(i - 1, 1 - slot)

        wait_scatter(i, slot)


def _moe(h2p, w1, w3, w2, bexp, bs0, bn, bfirst, nblk, order):
    ne, d, ff = w1.shape
    nb = bexp.shape[0]
    a = order.shape[0]
    wmap = lambda i, bexp, *_: (bexp[i], 0, 0)
    grid_spec = pltpu.PrefetchScalarGridSpec(
        num_scalar_prefetch=6,
        grid=(nb,),
        in_specs=[
            pl.BlockSpec(memory_space=pl.ANY),
            pl.BlockSpec((None, d, ff), wmap),
            pl.BlockSpec((None, d, ff), wmap),
            pl.BlockSpec((None, ff, d), wmap),
        ],
        out_specs=pl.BlockSpec(memory_space=pl.ANY),
        scratch_shapes=[
            pltpu.VMEM((2, MOE_BLOCK, d // 2), jnp.uint32),
            pltpu.VMEM((2, MOE_BLOCK, d), F32),
            pltpu.VMEM((d, ff), BF16),
            pltpu.VMEM((d, ff), BF16),
            pltpu.VMEM((ff, d), BF16),
            pltpu.SemaphoreType.DMA((2,)),
            pltpu.SemaphoreType.DMA((2,)),
        ],
    )
    return pl.pallas_call(
        _moe_kernel,
        grid_spec=grid_spec,
        out_shape=jax.ShapeDtypeStruct((a, d), F32),
        compiler_params=_cparams("arbitrary"),
        name="moe",
    )(bexp, bs0, bn, bfirst, nblk, order, h2p, w1, w3, w2)


def _combine_kernel(x1_ref, y0_ref, y1_ref, wts_ref, gt_ref, o_ref):
    w = wts_ref[0]
    ff = w[:, 0:1] * y0_ref[0] + w[:, 1:2] * y1_ref[0]
    o_ref[0] = x1_ref[0] + gt_ref[0] * ff


def _combine(x1, y2, wts, gt2, tm):
    b, t, d = x1.shape
    row = lambda bi, i: (bi, i, 0)
    vec = lambda bi, i: (bi, 0, 0)
    return pl.pallas_call(
        _combine_kernel,
        grid=(b, t // tm),
        in_specs=[
            pl.BlockSpec((1, tm, d), row),
            pl.BlockSpec((None, 1, tm, d), lambda bi, i: (0, bi, i, 0)),
            pl.BlockSpec((None, 1, tm, d), lambda bi, i: (1, bi, i, 0)),
            pl.BlockSpec((1, tm, LANES), row),
            pl.BlockSpec((1, 1, d), vec),
        ],
        out_specs=pl.BlockSpec((1, tm, d), row),
        out_shape=jax.ShapeDtypeStruct((b, t, d), F32),
        compiler_params=_cparams("arbitrary", "arbitrary"),
        name="combine",
    )(x1, y2, y2, wts, gt2)


def _rope_tables(t):
    pos = jnp.arange(t, dtype=jnp.int32)
    r = (pos // GRID_W).astype(F32)
    col = (pos % GRID_W).astype(F32)
    n_freq = HEAD_DIM // 4
    inv = ROPE_THETA ** (-jnp.arange(n_freq, dtype=F32) / n_freq)
    ar = r[:, None] * inv
    ac = col[:, None] * inv
    cos_t = jnp.concatenate([jnp.cos(ar), jnp.cos(ar), jnp.cos(ac), jnp.cos(ac)], axis=1)
    sin_t = jnp.concatenate([-jnp.sin(ar), jnp.sin(ar), -jnp.sin(ac), jnp.sin(ac)], axis=1)
    return cos_t, sin_t


def _gate_weights(w2, bias):
    r = GLA_GATE_RANK
    wh = w2.reshape(2, r, GLA_HEADS, GLA_DK).transpose(2, 0, 1, 3)
    w = jnp.zeros((GLA_HEADS, 2 * r, 2 * GLA_DK), F32)
    w = w.at[:, 0:r, 0:GLA_DK].set(wh[:, 0]).at[:, r:2 * r, GLA_DK:].set(wh[:, 1])
    hi = w.astype(BF16)
    mid = (w - hi.astype(F32)).astype(BF16)
    pad = jnp.zeros((GLA_HEADS, LANES - 6 * r, 2 * GLA_DK), BF16)
    w2s = jnp.concatenate([hi, hi, mid, pad], axis=1)
    bias2 = bias.reshape(2, GLA_HEADS, GLA_DK).transpose(1, 0, 2).reshape(GLA_HEADS, 1, 2 * GLA_DK)
    return w2s, bias2


def _block_plan(eid_flat, nb):
    a = eid_flat.shape[0]
    order = jnp.argsort(eid_flat).astype(jnp.int32)
    counts = jnp.zeros((N_EXPERTS,), jnp.int32).at[eid_flat].add(1)
    starts = jnp.cumsum(counts) - counts
    nblk_e = (counts + MOE_BLOCK - 1) // MOE_BLOCK
    bends = jnp.cumsum(nblk_e)
    bstarts = bends - nblk_e
    nblk = bends[-1]
    bi = jnp.arange(nb, dtype=jnp.int32)
    last = jnp.maximum(nblk - 1, 0)
    bic = jnp.minimum(bi, last)
    bexp = jnp.clip(jnp.searchsorted(bends, bic, side='right'), 0, N_EXPERTS - 1).astype(jnp.int32)
    r0 = (bic - bstarts[bexp]) * MOE_BLOCK
    bs0 = (starts[bexp] + r0).astype(jnp.int32)
    bn = jnp.where(bi < nblk, jnp.minimum(counts[bexp] - r0, MOE_BLOCK), 0).astype(jnp.int32)
    bfirst = ((r0 == 0) & (bi < nblk)).astype(jnp.int32)
    return bexp, bs0, bn, bfirst, nblk.reshape(1).astype(jnp.int32), order


def kernel(x, c, ctx, c_ctx, w_ada, b_ada, norm1_g, w_in, q_norm_g, k_norm_g, gla_gate_w2, gla_gate_b, gla_norm_g, w_out, norm2_g, router_grp_w, router_grp_b, router_exp_w, router_exp_b, moe_w1, moe_w3, moe_w2):
    b, t, d = x.shape
    tc = ctx.shape[1]
    depth = w_ada.shape[0]
    assert depth == 1, "single-layer stack: the context stream only feeds keys/values and GLA states"
    layer = 0

    c8 = jnp.zeros((8, d), F32).at[0:b].set(c).at[b].set(c_ctx)
    mod = _ada(c8, w_ada[layer], b_ada[layer])
    sh1, sc1, gt1, sh2, sc2, gt2 = [mod[0:b, i * d:(i + 1) * d].reshape(b, 1, d) for i in range(6)]
    sh1c, sc1c = [jnp.broadcast_to(mod[b, i * d:(i + 1) * d].reshape(1, 1, d), (b, 1, d)) for i in range(2)]

    wi = w_in[layer]
    o_aq, o_ak, o_av, o_gq, o_gk, o_gv, o_lr, o_og = 0, 1024, 1280, 1536, 2048, 2560, 3584, 3616
    w_p = jnp.concatenate([
        wi[:, o_aq:o_lr], wi[:, o_og:o_og + GLA_V_W], wi[:, o_lr:o_og],
        jnp.zeros((d, LANES - 2 * GLA_GATE_RANK), F32)], axis=1).astype(BF16)

    cos_t, sin_t = _rope_tables(t)
    g1 = norm1_g[layer].reshape(1, d)
    qg = q_norm_g[layer].reshape(1, HEAD_DIM)
    kg = k_norm_g[layer].reshape(1, HEAD_DIM)
    q, k, v, gq, gk, gv, og, lr = _in_proj(x, g1, sc1, sh1, w_p, qg, kg, cos_t, sin_t, 512)
    ones_t = jnp.ones((tc, HEAD_DIM), F32)
    _, kc, vc, gqc, gkc, gvc, _, lrc = _in_proj(ctx, g1, sc1c, sh1c, w_p, qg, kg, ones_t, jnp.zeros_like(ones_t), tc)

    attn = _attn(q, jnp.concatenate([kc, k], axis=1), jnp.concatenate([vc, v], axis=1), 256)
    w2s, bias2 = _gate_weights(gla_gate_w2[layer], gla_gate_b[layer])
    gla = _gla(gq, gk, gv, lr, og, gqc, gkc, gvc, lrc, w2s, bias2, gla_norm_g[layer].reshape(1, GLA_DV))

    wr = jnp.concatenate([router_grp_w[layer], router_exp_w[layer],
                          jnp.zeros((d, LANES - N_GROUPS - N_EXPERTS), F32)], axis=1)
    br = jnp.concatenate([router_grp_b[layer], router_exp_b[layer],
                          jnp.zeros((LANES - N_GROUPS - N_EXPERTS,), F32)]).reshape(1, LANES)
    wr_hi = wr.astype(BF16)
    wr_parts = jnp.stack([wr_hi, (wr - wr_hi.astype(F32)).astype(BF16)])
    x1, h2p, ids, wts = _out_proj(attn, gla, w_out[layer].astype(BF16), x, gt1, norm2_g[layer].reshape(1, d),
                                  sc2, sh2, wr_parts, br, 512)

    m = b * t
    n_assign = m * TOP_K
    nb = -(-(n_assign + N_EXPERTS * (MOE_BLOCK - 1)) // MOE_BLOCK)
    eid_flat = ids[:, :, 0:TOP_K].reshape(n_assign)
    plan = _block_plan(eid_flat, nb)
    y2 = _moe(h2p.reshape(m, d // 2), moe_w1[layer], moe_w3[layer], moe_w2[layer], *plan)
    return _combine(x1, y2.reshape(TOP_K, b, t, d), wts, gt2, 256)
```

```python
import functools

import jax
import jax.numpy as jnp
from jax import lax
from jax.experimental import pallas as pl
from jax.experimental.pallas import tpu as pltpu

EPS = 1e-6
GRID_W = 64
ROPE_THETA = 10000.0

ATTN_HEADS = 8
ATTN_KV_HEADS = 2
HEAD_DIM = 128
GQA_GROUP = ATTN_HEADS // ATTN_KV_HEADS

GLA_HEADS = 4
GLA_DK = 128
GLA_DV = 256
GLA_GATE_RANK = 16
GLA_GATE_NORMALIZER = 16.0
GLA_CHUNK = 64
GLA_CHUNK_LOG2 = 6
GLA_PREP_TILE = 256
GLA_OUT_GROUP = 4

N_GROUPS = 8
EXPERTS_PER_GROUP = 8
N_EXPERTS = N_GROUPS * EXPERTS_PER_GROUP
TOP_K = 2
MOE_BLOCK = 128
ROW_GROUP_LOG2 = 3
ROW_GROUP = 1 << ROW_GROUP_LOG2

LOG2_E = 1.4426950408889634
LANES = 128
VMEM_LIMIT = 56 * 1024 * 1024

ATTN_Q_W = ATTN_HEADS * HEAD_DIM
ATTN_KV_W = ATTN_KV_HEADS * HEAD_DIM
GLA_K_W = GLA_HEADS * GLA_DK
GLA_V_W = GLA_HEADS * GLA_DV

BF16 = jnp.bfloat16
F32 = jnp.float32


def _cparams(*sem):
    return pltpu.CompilerParams(dimension_semantics=sem, vmem_limit_bytes=VMEM_LIMIT)


def _ada_kernel(c_ref, w_ref, b_ref, o_ref):
    c = c_ref[...]
    s = c * jax.nn.sigmoid(c)
    o_ref[...] = jnp.dot(s.astype(BF16), w_ref[...].astype(BF16), preferred_element_type=F32) + b_ref[...]


def _ada(c8, w, b):
    d, n = w.shape
    tn = 1024
    return pl.pallas_call(
        _ada_kernel,
        grid=(n // tn,),
        in_specs=[
            pl.BlockSpec((8, d), lambda j: (0, 0)),
            pl.BlockSpec((d, tn), lambda j: (0, j)),
            pl.BlockSpec((1, tn), lambda j: (0, j)),
        ],
        out_specs=pl.BlockSpec((8, tn), lambda j: (0, j)),
        out_shape=jax.ShapeDtypeStruct((8, n), F32),
        compiler_params=_cparams("arbitrary"),
        name="ada",
    )(c8, w, b.reshape(1, n))


def _swap32(y):
    lane = lax.broadcasted_iota(jnp.int32, y.shape, 1)
    return jnp.where((lane & 63) < 32, pltpu.roll(y, 96, 1), pltpu.roll(y, 32, 1))


def _head_norm_rope(a, g, cos, sin):
    y = a * lax.rsqrt(jnp.mean(a * a, axis=-1, keepdims=True) + EPS) * g
    return y * cos + _swap32(y) * sin


def _in_proj_kernel(x_ref, g1_ref, sc_ref, sh_ref, w_ref, qg_ref, kg_ref, cos_ref, sin_ref,
                    q_ref, k_ref, v_ref, gq_ref, gk_ref, gv_ref, og_ref, lr_ref):
    x = x_ref[0]
    xn = x * lax.rsqrt(jnp.mean(x * x, axis=-1, keepdims=True) + EPS)
    h = (xn * g1_ref[...] * (1.0 + sc_ref[0]) + sh_ref[0]).astype(BF16)
    cos = cos_ref[...]
    sin = sin_ref[...]

    def proj(off, width):
        return jnp.dot(h, w_ref[:, off:off + width], preferred_element_type=F32)

    qg = qg_ref[...] * (HEAD_DIM ** -0.5 * LOG2_E)
    off = 0
    for hd in range(ATTN_HEADS):
        a = proj(off, HEAD_DIM)
        q_ref[0, :, hd * HEAD_DIM:(hd + 1) * HEAD_DIM] = _head_norm_rope(a, qg, cos, sin).astype(BF16)
        off += HEAD_DIM
    for hd in range(ATTN_KV_HEADS):
        a = proj(off, HEAD_DIM)
        k_ref[0, :, hd * HEAD_DIM:(hd + 1) * HEAD_DIM] = _head_norm_rope(a, kg_ref[...], cos, sin).astype(BF16)
        off += HEAD_DIM
    v_ref[0] = proj(off, ATTN_KV_W).astype(BF16)
    off += ATTN_KV_W
    gq_ref[0] = proj(off, GLA_K_W)
    off += GLA_K_W
    gk_ref[0] = proj(off, GLA_K_W)
    off += GLA_K_W
    gv_ref[0] = proj(off, GLA_V_W).astype(BF16)
    off += GLA_V_W
    og_ref[0] = proj(off, GLA_V_W)
    off += GLA_V_W
    lr_ref[0] = proj(off, LANES)


def _in_proj(x, g1, sc, sh, w_p, qg, kg, cos_t, sin_t, tm):
    b, t, d = x.shape
    n = w_p.shape[1]
    row = lambda bi, i: (bi, i, 0)
    vec = lambda bi, i: (bi, 0, 0)
    const = lambda bi, i: (0, 0)
    tab = lambda bi, i: (i, 0)
    widths = [(ATTN_Q_W, BF16), (ATTN_KV_W, BF16), (ATTN_KV_W, BF16), (GLA_K_W, F32), (GLA_K_W, F32),
              (GLA_V_W, BF16), (GLA_V_W, F32), (LANES, F32)]
    return pl.pallas_call(
        _in_proj_kernel,
        grid=(b, t // tm),
        in_specs=[
            pl.BlockSpec((1, tm, d), row),
            pl.BlockSpec((1, d), const),
            pl.BlockSpec((1, 1, d), vec),
            pl.BlockSpec((1, 1, d), vec),
            pl.BlockSpec((d, n), const, pipeline_mode=pl.Buffered(1)),
            pl.BlockSpec((1, HEAD_DIM), const),
            pl.BlockSpec((1, HEAD_DIM), const),
            pl.BlockSpec((tm, HEAD_DIM), tab),
            pl.BlockSpec((tm, HEAD_DIM), tab),
        ],
        out_specs=[pl.BlockSpec((1, tm, wd), row) for wd, _ in widths],
        out_shape=[jax.ShapeDtypeStruct((b, t, wd), dt) for wd, dt in widths],
        compiler_params=_cparams("arbitrary", "arbitrary"),
        name="in_proj",
    )(x, g1, sc, sh, w_p, qg, kg, cos_t, sin_t)


def _attn_kernel(q_ref, k_ref, v_ref, o_ref, *, key_tiles):
    del key_tiles
    k = k_ref[0]
    v = v_ref[0]
    for g in range(GQA_GROUP):
        q = q_ref[0, :, g * HEAD_DIM:(g + 1) * HEAD_DIM]
        s = lax.dot_general(q, k, (((1,), (1,)), ((), ())), preferred_element_type=F32)
        p = jnp.exp2(s - jnp.max(s, axis=-1, keepdims=True))
        l = jnp.sum(p, axis=-1, keepdims=True)
        o = jnp.dot(p.astype(BF16), v, preferred_element_type=F32)
        o_ref[0, :, g * HEAD_DIM:(g + 1) * HEAD_DIM] = (o / l).astype(BF16)


def _attn(q, k_all, v_all, tq):
    b, t, _ = q.shape
    tk = k_all.shape[1]
    gw = GQA_GROUP * HEAD_DIM
    key_tile = 512
    n_full = tk // key_tile
    key_tiles = [(i * key_tile, key_tile) for i in range(n_full)]
    if tk % key_tile:
        key_tiles.append((n_full * key_tile, tk % key_tile))
    return pl.pallas_call(
        functools.partial(_attn_kernel, key_tiles=tuple(key_tiles)),
        grid=(b, ATTN_KV_HEADS, t // tq),
        in_specs=[
            pl.BlockSpec((1, tq, gw), lambda bi, hk, i: (bi, i, hk)),
            pl.BlockSpec((1, tk, HEAD_DIM), lambda bi, hk, i: (bi, 0, hk)),
            pl.BlockSpec((1, tk, HEAD_DIM), lambda bi, hk, i: (bi, 0, hk)),
        ],
        out_specs=pl.BlockSpec((1, tq, gw), lambda bi, hk, i: (bi, i, hk)),
        out_shape=jax.ShapeDtypeStruct((b, t, ATTN_Q_W), BF16),
        compiler_params=_cparams("arbitrary", "arbitrary", "arbitrary"),
        name="attn",
    )(q, k_all, v_all)


def _log_sigmoid(x):
    return jnp.minimum(x, 0.0) - jnp.log(1.0 + jnp.exp(-jnp.abs(x)))


def _split3(x):
    hi = x.astype(BF16)
    r = x - hi.astype(F32)
    mid = r.astype(BF16)
    lo = (r - mid.astype(F32)).astype(BF16)
    return hi, mid, lo


def _dot_exact_lhs(m, x):
    return sum(jnp.dot(m, part, preferred_element_type=F32) for part in _split3(x))


def _prefix_operator(n):
    ri = lax.broadcasted_iota(jnp.int32, (n, n), 0)
    ci = lax.broadcasted_iota(jnp.int32, (n, n), 1)
    same = lax.shift_right_logical(ri, GLA_CHUNK_LOG2) == lax.shift_right_logical(ci, GLA_CHUNK_LOG2)
    return (same & (ci <= ri)).astype(BF16)


def _gate_prep(q, k, lr, w2s, bias2, prefix):
    c = GLA_CHUNK
    hi = lr.astype(BF16).astype(F32)
    mid = (lr - hi).astype(BF16).astype(F32)
    lhs = (hi + pltpu.roll(mid, 2 * GLA_GATE_RANK, 1) + pltpu.roll(hi, 4 * GLA_GATE_RANK, 1)).astype(BF16)
    logits = jnp.dot(lhs, w2s, preferred_element_type=F32) + bias2
    g = _log_sigmoid(logits) * (1.0 / GLA_GATE_NORMALIZER)
    pre = _dot_exact_lhs(prefix, g)
    tot = jnp.concatenate([jnp.broadcast_to(pre[lo + c - 1:lo + c, :], (c, pre.shape[1]))
                           for lo in range(0, pre.shape[0], c)], axis=0)
    dk = GLA_DK
    bcs = (pre[:, :dk], tot[:, dk:] - pre[:, dk:] + g[:, dk:])
    out = []
    for d, bc in enumerate(bcs):
        b_end = tot[:, d * dk:(d + 1) * dk]
        qe = (q * (GLA_DK ** -0.5) * jnp.exp(bc)).astype(BF16)
        ke = (k * jnp.exp(-bc)).astype(BF16)
        kend = (k * jnp.exp(b_end - bc)).astype(BF16)
        out.append((qe, ke, kend, b_end))
    return out


def _state_step(s_ref, v, kend, dec):
    upd = lax.dot_general(v, kend, (((0,), (0,)), ((), ())), preferred_element_type=F32)
    s_ref[...] = s_ref[...] * dec + upd


def _gla_kernel(q_ref, k_ref, v_ref, lr_ref, og_ref, qc_ref, kc_ref, vc_ref, lrc_ref,
                w2_ref, b_ref, ng_ref, o_ref,
                s_ref, qe_ref, ke_ref, kend_ref, dec_ref, kendc_ref, decc_ref, sbf_ref):
    c = GLA_CHUNK
    t = q_ref.shape[1]
    tc = qc_ref.shape[1]
    nc = t // c
    tile = GLA_PREP_TILE
    cpt = tile // c
    dirs = (0, 1)

    def store_dec(ref, d, base, b_end):
        for ch in range(b_end.shape[0] // c):
            ref[d, pl.ds(base + ch, 1), :] = jnp.exp(b_end[ch * c:ch * c + 1, :])

    prefix = _prefix_operator(tile)

    for i in range(tc // tile):
        rows = slice(i * tile, (i + 1) * tile)
        prep_c = _gate_prep(qc_ref[0, rows, :], kc_ref[0, rows, :], lrc_ref[0, rows, :],
                            w2_ref[...], b_ref[...], prefix)
        for d, (_, _, kend, b_end) in enumerate(prep_c):
            kendc_ref[d, rows, :] = kend
            store_dec(decc_ref, d, i * cpt, b_end)
    s_ref[...] = jnp.zeros_like(s_ref)
    for j in range(tc // c):
        for d in dirs:
            ch = j if d == 0 else tc // c - 1 - j
            _state_step(s_ref.at[d], vc_ref[0, ch * c:(ch + 1) * c, :], kendc_ref[d, ch * c:(ch + 1) * c, :],
                        decc_ref[d, ch:ch + 1, :])

    def prep(i, carry):
        lo = pl.multiple_of(i * tile, tile)
        rows = pl.ds(lo, tile)
        prep_l = _gate_prep(q_ref[0, rows, :], k_ref[0, rows, :], lr_ref[0, rows, :],
                            w2_ref[...], b_ref[...], prefix)
        for d, (qe, ke, kend, b_end) in enumerate(prep_l):
            qe_ref[d, rows, :] = qe
            ke_ref[d, rows, :] = ke
            kend_ref[d, rows, :] = kend
            store_dec(dec_ref, d, i * cpt, b_end)
        return carry

    lax.fori_loop(0, t // tile, prep, 0)

    def scan(j, carry):
        for d in dirs:
            ch = j if d == 0 else nc - 1 - j
            rows = pl.ds(pl.multiple_of(ch * c, c), c)
            sbf_ref[d, ch] = s_ref[d].astype(BF16)
            _state_step(s_ref.at[d], v_ref[0, rows, :], kend_ref[d, rows, :], dec_ref[d, pl.ds(ch, 1), :])
        return carry

    lax.fori_loop(0, nc, scan, 0, unroll=4)

    ri = lax.broadcasted_iota(jnp.int32, (c, c), 0)
    ci = lax.broadcasted_iota(jnp.int32, (c, c), 1)
    masks = (ci <= ri, ci >= ri)

    group = GLA_OUT_GROUP

    def out(i, carry):
        chunks = [i * group + u for u in range(group)]
        rows = [pl.ds(pl.multiple_of(ch * c, c), c) for ch in chunks]
        qes = [[qe_ref[d, r, :] for d in dirs] for r in rows]
        scores = [[lax.dot_general(qes[u][d], ke_ref[d, rows[u], :], (((1,), (1,)), ((), ())),
                                   preferred_element_type=F32) for d in dirs] for u in range(group)]
        inter = [[lax.dot_general(qes[u][d], sbf_ref[d, chunks[u]], (((1,), (1,)), ((), ())),
                                  preferred_element_type=F32) for d in dirs] for u in range(group)]
        for u in range(group):
            v = v_ref[0, rows[u], :]
            o = inter[u][0] + inter[u][1]
            for d in dirs:
                a = jnp.where(masks[d], scores[u][d], 0.0).astype(BF16)
                o = o + jnp.dot(a, v, preferred_element_type=F32)
            on = o * lax.rsqrt(jnp.mean(o * o, axis=-1, keepdims=True) + EPS) * ng_ref[...]
            og = og_ref[0, rows[u], :]
            o_ref[0, rows[u], :] = (on * (og * jax.nn.sigmoid(og))).astype(BF16)
        return carry

    lax.fori_loop(0, nc // group, out, 0)


def _gla(gq, gk, gv, lr, og, gqc, gkc, gvc, lrc, w2, bias, ng):
    b, t, _ = gq.shape
    tc = gqc.shape[1]
    hk = lambda bi, h: (bi, 0, h)
    h0 = lambda bi, h: (bi, 0, 0)
    return pl.pallas_call(
        _gla_kernel,
        grid=(b, GLA_HEADS),
        in_specs=[
            pl.BlockSpec((1, t, GLA_DK), hk),
            pl.BlockSpec((1, t, GLA_DK), hk),
            pl.BlockSpec((1, t, GLA_DV), hk),
            pl.BlockSpec((1, t, LANES), h0),
            pl.BlockSpec((1, t, GLA_DV), hk),
            pl.BlockSpec((1, tc, GLA_DK), hk),
            pl.BlockSpec((1, tc, GLA_DK), hk),
            pl.BlockSpec((1, tc, GLA_DV), hk),
            pl.BlockSpec((1, tc, LANES), h0),
            pl.BlockSpec((None, LANES, 2 * GLA_DK), lambda bi, h: (h, 0, 0)),
            pl.BlockSpec((None, 1, 2 * GLA_DK), lambda bi, h: (h, 0, 0)),
            pl.BlockSpec((1, GLA_DV), lambda bi, h: (0, 0)),
        ],
        out_specs=pl.BlockSpec((1, t, GLA_DV), hk),
        out_shape=jax.ShapeDtypeStruct((b, t, GLA_V_W), BF16),
        scratch_shapes=[
            pltpu.VMEM((2, GLA_DV, GLA_DK), F32),
            pltpu.VMEM((2, t, GLA_DK), BF16),
            pltpu.VMEM((2, t, GLA_DK), BF16),
            pltpu.VMEM((2, t, GLA_DK), BF16),
            pltpu.VMEM((2, t // GLA_CHUNK, GLA_DK), F32),
            pltpu.VMEM((2, tc, GLA_DK), BF16),
            pltpu.VMEM((2, tc // GLA_CHUNK, GLA_DK), F32),
            pltpu.VMEM((2, t // GLA_CHUNK, GLA_DV, GLA_DK), BF16),
        ],
        compiler_params=_cparams("arbitrary", "arbitrary"),
        name="gla",
    )(gq, gk, gv, lr, og, gqc, gkc, gvc, lrc, w2, bias, ng)


def _store_token_tiles(ref, x):
    n = x.shape[0]
    k = x.shape[1] // LANES
    for s in range(k):
        ref[pl.ds(s, n, stride=k), :] = x[:, s * LANES:(s + 1) * LANES]


def _load_token_tiles(ref, n):
    k = ref.shape[0] // n
    return jnp.concatenate([ref[pl.ds(s, n, stride=k), :] for s in range(k)], axis=1)


def _out_proj_kernel(attn_ref, gla_ref, wo_ref, x_ref, gt_ref, g2_ref, sc_ref, sh_ref, wr_ref, br_ref,
                     x1_ref, h2_ref, ids_ref, wts_ref):
    y = jnp.dot(attn_ref[0], wo_ref[0:ATTN_Q_W, :], preferred_element_type=F32)
    y = y + jnp.dot(gla_ref[0], wo_ref[ATTN_Q_W:, :], preferred_element_type=F32)
    x1 = x_ref[0] + gt_ref[0] * y
    x1_ref[0] = x1
    xn = x1 * lax.rsqrt(jnp.mean(x1 * x1, axis=-1, keepdims=True) + EPS)
    h2 = xn * g2_ref[...] * (1.0 + sc_ref[0]) + sh_ref[0]
    hi = h2.astype(BF16)
    hi_f = hi.astype(F32)

    half = h2.shape[1] // 2
    bits = lax.bitcast_convert_type(hi_f, jnp.uint32)
    words = lax.shift_right_logical(bits[:, :half], jnp.uint32(16)) | (bits[:, half:] & jnp.uint32(0xFFFF0000))
    _store_token_tiles(h2_ref.at[0], words)

    mid = (h2 - hi_f).astype(BF16)
    logits = (jnp.dot(hi, wr_ref[0], preferred_element_type=F32)
              + jnp.dot(mid, wr_ref[0], preferred_element_type=F32)
              + jnp.dot(hi, wr_ref[1], preferred_element_type=F32)) + br_ref[...]
    lane = lax.broadcasted_iota(jnp.int32, logits.shape, 1)
    lane_f = lane.astype(F32)
    neg = jnp.float32(-jnp.inf)

    def first_argmax(vals):
        m = jnp.max(vals, axis=-1, keepdims=True)
        idx = jnp.min(jnp.where(vals == m, lane_f, float(LANES)), axis=-1, keepdims=True)
        return m, idx

    lg = jnp.where(lane < N_GROUPS, logits, neg)
    mg, grp = first_argmax(lg)
    pg_sel = 1.0 / jnp.sum(jnp.exp(lg - mg), axis=-1, keepdims=True)
    lo = N_GROUPS + grp * EXPERTS_PER_GROUP
    in_grp = (lane_f >= lo) & (lane_f < lo + EXPERTS_PER_GROUP)
    le = jnp.where(in_grp, logits, neg)
    v1, i1 = first_argmax(le)
    v2, i2 = first_argmax(jnp.where(lane_f == i1, neg, le))
    e2 = jnp.exp(v2 - v1)
    w1 = pg_sel / (1.0 + e2)
    w2 = pg_sel * e2 / (1.0 + e2)
    ids = jnp.where(lane == 0, i1 - N_GROUPS, jnp.where(lane == 1, i2 - N_GROUPS, 0.0))
    ids_ref[0] = ids.astype(jnp.int32)
    wts_ref[0] = jnp.where(lane == 0, w1, jnp.where(lane == 1, w2, 0.0))


def _out_proj(attn, gla, wo, x, gt1, g2, sc2, sh2, wr, br, tm):
    b, t, d = x.shape
    row = lambda bi, i: (bi, i, 0)
    vec = lambda bi, i: (bi, 0, 0)
    const = lambda bi, i: (0, 0)
    return pl.pallas_call(
        _out_proj_kernel,
        grid=(b, t // tm),
        in_specs=[
            pl.BlockSpec((1, tm, ATTN_Q_W), row),
            pl.BlockSpec((1, tm, GLA_V_W), row),
            pl.BlockSpec(wo.shape, const, pipeline_mode=pl.Buffered(1)),
            pl.BlockSpec((1, tm, d), row),
            pl.BlockSpec((1, 1, d), vec),
            pl.BlockSpec((1, d), const),
            pl.BlockSpec((1, 1, d), vec),
            pl.BlockSpec((1, 1, d), vec),
            pl.BlockSpec((2, d, LANES), lambda bi, i: (0, 0, 0)),
            pl.BlockSpec((1, LANES), const),
        ],
        out_specs=[
            pl.BlockSpec((1, tm, d), row),
            pl.BlockSpec((1, tm * (d // 2 // LANES), LANES), row),
            pl.BlockSpec((1, tm, LANES), row),
            pl.BlockSpec((1, tm, LANES), row),
        ],
        out_shape=[
            jax.ShapeDtypeStruct((b, t, d), F32),
            jax.ShapeDtypeStruct((b, t * (d // 2 // LANES), LANES), jnp.uint32),
            jax.ShapeDtypeStruct((b, t, LANES), jnp.int32),
            jax.ShapeDtypeStruct((b, t, LANES), F32),
        ],
        compiler_params=_cparams("arbitrary", "arbitrary"),
        name="out_proj",
    )(attn, gla, wo, x, gt1, g2, sc2, sh2, wr, br)


def _moe_kernel(eidx_ref, eblk0_ref, enb_ref, bs0_ref, bn_ref, nblk_ref, order_ref,
                h2_hbm, w1_ref, w3_ref, w2_ref, y_hbm,
                xbuf, ybuf, w1b, w3b, w2b, gsem, ssem):
    del eidx_ref
    e = pl.program_id(0)
    nblk = nblk_ref[0]
    kx = xbuf.shape[1] // MOE_BLOCK
    ky = ybuf.shape[1] // MOE_BLOCK
    tokens = y_hbm.shape[0] // (TOP_K * ky)

    def gather_copy(sl, hbm_row, j, rows=1):
        return pltpu.make_async_copy(h2_hbm.at[pl.ds(pl.multiple_of(hbm_row * kx, kx), rows * kx)],
                                     xbuf.at[sl, pl.ds(pl.multiple_of(j * kx, kx), rows * kx)], gsem.at[sl])

    def scatter_copy(sl, hbm_row, j, rows=1):
        return pltpu.make_async_copy(ybuf.at[sl, pl.ds(pl.multiple_of(j * ky, ky), rows * ky)],
                                     y_hbm.at[pl.ds(pl.multiple_of(hbm_row * ky, ky), rows * ky)], ssem.at[sl])

    def gather_row(s, j):
        return lax.shift_right_logical(order_ref[s + j], 1)

    def scatter_row(s, j):
        a = order_ref[s + j]
        return (a & 1) * tokens + lax.shift_right_logical(a, 1)

    def for_rows(n, per_group, per_row):
        ng = lax.shift_right_logical(n, ROW_GROUP_LOG2)
        lax.fori_loop(0, ng, lambda g, c: (per_group(g * ROW_GROUP), c)[1], 0)
        lax.fori_loop(ng * ROW_GROUP, n, lambda j, c: (per_row(j), c)[1], 0)

    def start_rows(copy, hbm_row, blk, sl):
        s = bs0_ref[blk]

        def group(j0):
            for u in range(ROW_GROUP):
                copy(sl, hbm_row(s, j0 + u), j0 + u).start()

        for_rows(bn_ref[blk], group, lambda j: copy(sl, hbm_row(s, j), j).start())

    def wait_rows(copy, blk, sl):
        for_rows(bn_ref[blk], lambda j0: copy(sl, 0, 0, ROW_GROUP).wait(), lambda j: copy(sl, 0, 0).wait())

    start_gather = functools.partial(start_rows, gather_copy, gather_row)
    wait_gather = functools.partial(wait_rows, gather_copy)
    start_scatter = functools.partial(start_rows, scatter_copy, scatter_row)
    wait_scatter = functools.partial(wait_rows, scatter_copy)

    @pl.when(e == 0)
    def _():
        xbuf[...] = jnp.zeros_like(xbuf)
        start_gather(0, 0)

    def block(g, carry):
        slot = g & 1

        @pl.when(g + 1 < nblk)
        def _():
            start_gather(g + 1, 1 - slot)

        wait_gather(g, slot)
        words = _load_token_tiles(xbuf.at[slot], MOE_BLOCK)
        x_lo = lax.bitcast_convert_type(lax.shift_left(words, jnp.uint32(16)), F32)
        x_hi = lax.bitcast_convert_type(words & jnp.uint32(0xFFFF0000), F32)
        x = jnp.concatenate([x_lo, x_hi], axis=1).astype(BF16)
        a = jnp.dot(x, w1b[...], preferred_element_type=F32)
        gate = jnp.dot(x, w3b[...], preferred_element_type=F32)
        hid = (a * jax.nn.sigmoid(a) * gate).astype(BF16)
        y = jnp.dot(hid, w2b[...], preferred_element_type=F32)

        @pl.when(g >= 2)
        def _():
            wait_scatter(g - 2, slot)

        _store_token_tiles(ybuf.at[slot], y)
        start_scatter(g, slot)
        return carry

    @pl.when(enb_ref[e] > 0)
    def _():
        w1b[...] = w1_ref[...].astype(BF16)
        w3b[...] = w3_ref[...].astype(BF16)
        w2b[...] = w2_ref[...].astype(BF16)
        lax.fori_loop(eblk0_ref[e], eblk0_ref[e] + enb_ref[e], block, 0)

    @pl.when(e == pl.num_programs(0) - 1)
    def _():
        @pl.when(nblk >= 2)
        def _():
            wait_scatter(nblk - 2, nblk & 1)

        wait_scatter(nblk - 1, (nblk - 1) & 1)


def _moe(h2t, w1, w3, w2, eidx, eblk0, enb, bs0, bn, nblk, order):
    ne, d, ff = w1.shape
    kx = d // 2 // LANES
    ky = d // LANES
    a = order.shape[0]
    wmap = lambda e, eidx, *_: (eidx[e], 0, 0)
    grid_spec = pltpu.PrefetchScalarGridSpec(
        num_scalar_prefetch=7,
        grid=(ne,),
        in_specs=[
            pl.BlockSpec(memory_space=pl.ANY),
            pl.BlockSpec((None, d, ff), wmap),
            pl.BlockSpec((None, d, ff), wmap),
            pl.BlockSpec((None, ff, d), wmap),
        ],
        out_specs=pl.BlockSpec(memory_space=pl.ANY),
        scratch_shapes=[
            pltpu.VMEM((2, MOE_BLOCK * kx, LANES), jnp.uint32),
            pltpu.VMEM((2, MOE_BLOCK * ky, LANES), F32),
            pltpu.VMEM((d, ff), BF16),
            pltpu.VMEM((d, ff), BF16),
            pltpu.VMEM((ff, d), BF16),
            pltpu.SemaphoreType.DMA((2,)),
            pltpu.SemaphoreType.DMA((2,)),
        ],
    )
    return pl.pallas_call(
        _moe_kernel,
        grid_spec=grid_spec,
        out_shape=jax.ShapeDtypeStruct((a * ky, LANES), F32),
        compiler_params=_cparams("arbitrary"),
        name="moe",
    )(eidx, eblk0, enb, bs0, bn, nblk, order, h2t, w1, w3, w2)


def _combine_kernel(x1_ref, y0_ref, y1_ref, wts_ref, gt_ref, o_ref):
    tm = x1_ref.shape[1]
    w = wts_ref[0]
    ff = w[:, 0:1] * _load_token_tiles(y0_ref.at[0], tm) + w[:, 1:2] * _load_token_tiles(y1_ref.at[0], tm)
    o_ref[0] = x1_ref[0] + gt_ref[0] * ff


def _combine(x1, y2, wts, gt2, tm):
    b, t, d = x1.shape
    ky = d // LANES
    row = lambda bi, i: (bi, i, 0)
    vec = lambda bi, i: (bi, 0, 0)
    return pl.pallas_call(
        _combine_kernel,
        grid=(b, t // tm),
        in_specs=[
            pl.BlockSpec((1, tm, d), row),
            pl.BlockSpec((None, 1, tm * ky, LANES), lambda bi, i: (0, bi, i, 0)),
            pl.BlockSpec((None, 1, tm * ky, LANES), lambda bi, i: (1, bi, i, 0)),
            pl.BlockSpec((1, tm, LANES), row),
            pl.BlockSpec((1, 1, d), vec),
        ],
        out_specs=pl.BlockSpec((1, tm, d), row),
        out_shape=jax.ShapeDtypeStruct((b, t, d), F32),
        compiler_params=_cparams("arbitrary", "arbitrary"),
        name="combine",
    )(x1, y2, y2, wts, gt2)


def _rope_tables(t):
    pos = jnp.arange(t, dtype=jnp.int32)
    r = (pos // GRID_W).astype(F32)
    col = (pos % GRID_W).astype(F32)
    n_freq = HEAD_DIM // 4
    inv = ROPE_THETA ** (-jnp.arange(n_freq, dtype=F32) / n_freq)
    ar = r[:, None] * inv
    ac = col[:, None] * inv
    cos_t = jnp.concatenate([jnp.cos(ar), jnp.cos(ar), jnp.cos(ac), jnp.cos(ac)], axis=1)
    sin_t = jnp.concatenate([-jnp.sin(ar), jnp.sin(ar), -jnp.sin(ac), jnp.sin(ac)], axis=1)
    return cos_t, sin_t


def _gate_weights(w2, bias):
    r = GLA_GATE_RANK
    wh = w2.reshape(2, r, GLA_HEADS, GLA_DK).transpose(2, 0, 1, 3)
    w = jnp.zeros((GLA_HEADS, 2 * r, 2 * GLA_DK), F32)
    w = w.at[:, 0:r, 0:GLA_DK].set(wh[:, 0]).at[:, r:2 * r, GLA_DK:].set(wh[:, 1])
    hi = w.astype(BF16)
    mid = (w - hi.astype(F32)).astype(BF16)
    pad = jnp.zeros((GLA_HEADS, LANES - 6 * r, 2 * GLA_DK), BF16)
    w2s = jnp.concatenate([hi, hi, mid, pad], axis=1)
    bias2 = bias.reshape(2, GLA_HEADS, GLA_DK).transpose(1, 0, 2).reshape(GLA_HEADS, 1, 2 * GLA_DK)
    return w2s, bias2


def _block_plan(eid_flat, nb):
    order = jnp.argsort(eid_flat).astype(jnp.int32)
    counts = jnp.zeros((N_EXPERTS,), jnp.int32).at[eid_flat].add(1)
    starts = jnp.cumsum(counts) - counts
    nblk_e = (counts + MOE_BLOCK - 1) // MOE_BLOCK
    bends = jnp.cumsum(nblk_e)
    bstarts = bends - nblk_e
    nblk = bends[-1]
    eids = jnp.arange(N_EXPERTS, dtype=jnp.int32)
    prev_used = lax.cummax(jnp.where(counts > 0, eids, -1))
    eidx = jnp.where(prev_used >= 0, prev_used, jnp.argmax(counts > 0)).astype(jnp.int32)
    bi = jnp.arange(nb, dtype=jnp.int32)
    bic = jnp.minimum(bi, jnp.maximum(nblk - 1, 0))
    bexp = jnp.clip(jnp.searchsorted(bends, bic, side='right'), 0, N_EXPERTS - 1)
    r0 = (bic - bstarts[bexp]) * MOE_BLOCK
    bs0 = (starts[bexp] + r0).astype(jnp.int32)
    bn = jnp.where(bi < nblk, jnp.minimum(counts[bexp] - r0, MOE_BLOCK), 0).astype(jnp.int32)
    return (eidx, bstarts.astype(jnp.int32), nblk_e.astype(jnp.int32), bs0, bn,
            nblk.reshape(1).astype(jnp.int32), order)


def kernel(x, c, ctx, c_ctx, w_ada, b_ada, norm1_g, w_in, q_norm_g, k_norm_g, gla_gate_w2, gla_gate_b, gla_norm_g, w_out, norm2_g, router_grp_w, router_grp_b, router_exp_w, router_exp_b, moe_w1, moe_w3, moe_w2):
    b, t, d = x.shape
    tc = ctx.shape[1]
    depth = w_ada.shape[0]
    assert depth == 1, "single-layer stack: the context stream only feeds keys/values and GLA states"
    layer = 0

    c8 = jnp.zeros((8, d), F32).at[0:b].set(c).at[b].set(c_ctx)
    mod = _ada(c8, w_ada[layer], b_ada[layer])
    sh1, sc1, gt1, sh2, sc2, gt2 = [mod[0:b, i * d:(i + 1) * d].reshape(b, 1, d) for i in range(6)]
    sh1c, sc1c = [jnp.broadcast_to(mod[b, i * d:(i + 1) * d].reshape(1, 1, d), (b, 1, d)) for i in range(2)]

    wi = w_in[layer]
    o_aq, o_ak, o_av, o_gq, o_gk, o_gv, o_lr, o_og = 0, 1024, 1280, 1536, 2048, 2560, 3584, 3616
    w_p = jnp.concatenate([
        wi[:, o_aq:o_lr], wi[:, o_og:o_og + GLA_V_W], wi[:, o_lr:o_og],
        jnp.zeros((d, LANES - 2 * GLA_GATE_RANK), F32)], axis=1).astype(BF16)

    cos_t, sin_t = _rope_tables(t)
    g1 = norm1_g[layer].reshape(1, d)
    qg = q_norm_g[layer].reshape(1, HEAD_DIM)
    kg = k_norm_g[layer].reshape(1, HEAD_DIM)
    q, k, v, gq, gk, gv, og, lr = _in_proj(x, g1, sc1, sh1, w_p, qg, kg, cos_t, sin_t, 512)
    ones_t = jnp.ones((tc, HEAD_DIM), F32)
    _, kc, vc, gqc, gkc, gvc, _, lrc = _in_proj(ctx, g1, sc1c, sh1c, w_p, qg, kg, ones_t, jnp.zeros_like(ones_t), tc)

    attn = _attn(q, jnp.concatenate([kc, k], axis=1), jnp.concatenate([vc, v], axis=1), 256)
    w2s, bias2 = _gate_weights(gla_gate_w2[layer], gla_gate_b[layer])
    gla = _gla(gq, gk, gv, lr, og, gqc, gkc, gvc, lrc, w2s, bias2, gla_norm_g[layer].reshape(1, GLA_DV))

    wr = jnp.concatenate([router_grp_w[layer], router_exp_w[layer],
                          jnp.zeros((d, LANES - N_GROUPS - N_EXPERTS), F32)], axis=1)
    br = jnp.concatenate([router_grp_b[layer], router_exp_b[layer],
                          jnp.zeros((LANES - N_GROUPS - N_EXPERTS,), F32)]).reshape(1, LANES)
    wr_hi = wr.astype(BF16)
    wr_parts = jnp.stack([wr_hi, (wr - wr_hi.astype(F32)).astype(BF16)])
    x1, h2p, ids, wts = _out_proj(attn, gla, w_out[layer].astype(BF16), x, gt1, norm2_g[layer].reshape(1, d),
                                  sc2, sh2, wr_parts, br, 512)

    m = b * t
    n_assign = m * TOP_K
    nb = -(-(n_assign + N_EXPERTS * (MOE_BLOCK - 1)) // MOE_BLOCK)
    eid_flat = ids[:, :, 0:TOP_K].reshape(n_assign)
    plan = _block_plan(eid_flat, nb)
    y2 = _moe(h2p.reshape(-1, LANES), moe_w1[layer], moe_w3[layer], moe_w2[layer], *plan)
    return _combine(x1, y2.reshape(TOP_K, b, t * (d // LANES), LANES), wts, gt2, 256)
```

```python
import functools

import jax
import jax.numpy as jnp
from jax import lax
from jax.experimental import pallas as pl
from jax.experimental.pallas import tpu as pltpu

EPS = 1e-6
GRID_W = 64
ROPE_THETA = 10000.0

ATTN_HEADS = 8
ATTN_KV_HEADS = 2
HEAD_DIM = 128
GQA_GROUP = ATTN_HEADS // ATTN_KV_HEADS

GLA_HEADS = 4
GLA_DK = 128
GLA_DV = 256
GLA_GATE_RANK = 16
GLA_GATE_NORMALIZER = 16.0
GLA_CHUNK = 64
GLA_CHUNK_LOG2 = 6
GLA_PREP_TILE = 256
GLA_OUT_GROUP = 4

N_GROUPS = 8
EXPERTS_PER_GROUP = 8
N_EXPERTS = N_GROUPS * EXPERTS_PER_GROUP
TOP_K = 2
MOE_BLOCK = 128
ROW_GROUP_LOG2 = 3
ROW_GROUP = 1 << ROW_GROUP_LOG2
ROW_DMA_PRIORITY = 1

LOG2_E = 1.4426950408889634
LANES = 128
VMEM_LIMIT = 56 * 1024 * 1024

ATTN_Q_W = ATTN_HEADS * HEAD_DIM
ATTN_KV_W = ATTN_KV_HEADS * HEAD_DIM
GLA_K_W = GLA_HEADS * GLA_DK
GLA_V_W = GLA_HEADS * GLA_DV

BF16 = jnp.bfloat16
F32 = jnp.float32


def _cparams(*sem):
    return pltpu.CompilerParams(dimension_semantics=sem, vmem_limit_bytes=VMEM_LIMIT)


def _ada_kernel(c_ref, w_ref, b_ref, o_ref):
    c = c_ref[...]
    s = c * jax.nn.sigmoid(c)
    o_ref[...] = jnp.dot(s.astype(BF16), w_ref[...].astype(BF16), preferred_element_type=F32) + b_ref[...]


def _ada(c8, w, b):
    d, n = w.shape
    tn = 1024
    return pl.pallas_call(
        _ada_kernel,
        grid=(n // tn,),
        in_specs=[
            pl.BlockSpec((8, d), lambda j: (0, 0)),
            pl.BlockSpec((d, tn), lambda j: (0, j)),
            pl.BlockSpec((1, tn), lambda j: (0, j)),
        ],
        out_specs=pl.BlockSpec((8, tn), lambda j: (0, j)),
        out_shape=jax.ShapeDtypeStruct((8, n), F32),
        compiler_params=_cparams("arbitrary"),
        name="ada",
    )(c8, w, b.reshape(1, n))


def _swap32(y):
    lane = lax.broadcasted_iota(jnp.int32, y.shape, 1)
    return jnp.where((lane & 63) < 32, pltpu.roll(y, 96, 1), pltpu.roll(y, 32, 1))


def _head_norm_rope(a, g, cos, sin):
    y = a * lax.rsqrt(jnp.mean(a * a, axis=-1, keepdims=True) + EPS) * g
    return y * cos + _swap32(y) * sin


def _in_proj_kernel(x_ref, g1_ref, sc_ref, sh_ref, w_ref, qg_ref, kg_ref, cos_ref, sin_ref,
                    q_ref, k_ref, v_ref, gq_ref, gk_ref, gv_ref, og_ref, lr_ref):
    x = x_ref[0]
    xn = x * lax.rsqrt(jnp.mean(x * x, axis=-1, keepdims=True) + EPS)
    h = (xn * g1_ref[...] * (1.0 + sc_ref[0]) + sh_ref[0]).astype(BF16)
    cos = cos_ref[...]
    sin = sin_ref[...]

    def proj(off, width):
        return jnp.dot(h, w_ref[:, off:off + width], preferred_element_type=F32)

    qg = qg_ref[...] * (HEAD_DIM ** -0.5 * LOG2_E)
    off = 0
    for out_ref, gain, heads in ((q_ref, qg, ATTN_HEADS), (k_ref, kg_ref[...], ATTN_KV_HEADS)):
        for pair in range(heads // 2):
            a2 = proj(off, 2 * HEAD_DIM)
            for u in range(2):
                lo = (2 * pair + u) * HEAD_DIM
                a = a2[:, u * HEAD_DIM:(u + 1) * HEAD_DIM]
                out_ref[0, :, lo:lo + HEAD_DIM] = _head_norm_rope(a, gain, cos, sin).astype(BF16)
            off += 2 * HEAD_DIM
    v_ref[0] = proj(off, ATTN_KV_W).astype(BF16)
    off += ATTN_KV_W
    gq_ref[0] = proj(off, GLA_K_W)
    off += GLA_K_W
    gk_ref[0] = proj(off, GLA_K_W)
    off += GLA_K_W
    gv_ref[0] = proj(off, GLA_V_W).astype(BF16)
    off += GLA_V_W
    og_ref[0] = proj(off, GLA_V_W)
    off += GLA_V_W
    lr_ref[0] = proj(off, LANES)


def _in_proj(x, g1, sc, sh, w_p, qg, kg, cos_t, sin_t, tm):
    b, t, d = x.shape
    n = w_p.shape[1]
    row = lambda bi, i: (bi, i, 0)
    vec = lambda bi, i: (bi, 0, 0)
    const = lambda bi, i: (0, 0)
    tab = lambda bi, i: (i, 0)
    widths = [(ATTN_Q_W, BF16), (ATTN_KV_W, BF16), (ATTN_KV_W, BF16), (GLA_K_W, F32), (GLA_K_W, F32),
              (GLA_V_W, BF16), (GLA_V_W, F32), (LANES, F32)]
    return pl.pallas_call(
        _in_proj_kernel,
        grid=(b, t // tm),
        in_specs=[
            pl.BlockSpec((1, tm, d), row),
            pl.BlockSpec((1, d), const),
            pl.BlockSpec((1, 1, d), vec),
            pl.BlockSpec((1, 1, d), vec),
            pl.BlockSpec((d, n), const, pipeline_mode=pl.Buffered(1)),
            pl.BlockSpec((1, HEAD_DIM), const),
            pl.BlockSpec((1, HEAD_DIM), const),
            pl.BlockSpec((tm, HEAD_DIM), tab),
            pl.BlockSpec((tm, HEAD_DIM), tab),
        ],
        out_specs=[pl.BlockSpec((1, tm, wd), row) for wd, _ in widths],
        out_shape=[jax.ShapeDtypeStruct((b, t, wd), dt) for wd, dt in widths],
        compiler_params=_cparams("arbitrary", "arbitrary"),
        name="in_proj",
    )(x, g1, sc, sh, w_p, qg, kg, cos_t, sin_t)


def _attn_kernel(q_ref, k_ref, v_ref, o_ref):
    k = k_ref[0]
    v = v_ref[0]

    def scores(g):
        q = q_ref[0, :, g * HEAD_DIM:(g + 1) * HEAD_DIM]
        return lax.dot_general(q, k, (((1,), (1,)), ((), ())), preferred_element_type=F32)

    s_next = scores(0)
    for g in range(GQA_GROUP):
        s = s_next
        if g + 1 < GQA_GROUP:
            s_next = scores(g + 1)
        p = jnp.exp2(s - jnp.max(s, axis=-1, keepdims=True))
        l = jnp.sum(p, axis=-1, keepdims=True)
        o = jnp.dot(p.astype(BF16), v, preferred_element_type=F32)
        o_ref[0, :, g * HEAD_DIM:(g + 1) * HEAD_DIM] = (o / l).astype(BF16)


def _attn(q, k_all, v_all, tq):
    b, t, _ = q.shape
    tk = k_all.shape[1]
    gw = GQA_GROUP * HEAD_DIM
    return pl.pallas_call(
        _attn_kernel,
        grid=(b, ATTN_KV_HEADS, t // tq),
        in_specs=[
            pl.BlockSpec((1, tq, gw), lambda bi, hk, i: (bi, i, hk)),
            pl.BlockSpec((1, tk, HEAD_DIM), lambda bi, hk, i: (bi, 0, hk)),
            pl.BlockSpec((1, tk, HEAD_DIM), lambda bi, hk, i: (bi, 0, hk)),
        ],
        out_specs=pl.BlockSpec((1, tq, gw), lambda bi, hk, i: (bi, i, hk)),
        out_shape=jax.ShapeDtypeStruct((b, t, ATTN_Q_W), BF16),
        compiler_params=_cparams("arbitrary", "arbitrary", "arbitrary"),
        name="attn",
    )(q, k_all, v_all)


def _log_sigmoid(x):
    return jnp.minimum(x, 0.0) - jnp.log(1.0 + jnp.exp(-jnp.abs(x)))


def _split3(x):
    hi = x.astype(BF16)
    r = x - hi.astype(F32)
    mid = r.astype(BF16)
    lo = (r - mid.astype(F32)).astype(BF16)
    return hi, mid, lo


def _dot_exact_lhs(m, x):
    return sum(jnp.dot(m, part, preferred_element_type=F32) for part in _split3(x))


def _prefix_operator(n):
    ri = lax.broadcasted_iota(jnp.int32, (n, n), 0)
    ci = lax.broadcasted_iota(jnp.int32, (n, n), 1)
    same = lax.shift_right_logical(ri, GLA_CHUNK_LOG2) == lax.shift_right_logical(ci, GLA_CHUNK_LOG2)
    return (same & (ci <= ri)).astype(BF16)


def _gate_prep(q, k, lr, w2s, bias2, prefix):
    c = GLA_CHUNK
    hi = lr.astype(BF16).astype(F32)
    mid = (lr - hi).astype(BF16).astype(F32)
    lhs = (hi + pltpu.roll(mid, 2 * GLA_GATE_RANK, 1) + pltpu.roll(hi, 4 * GLA_GATE_RANK, 1)).astype(BF16)
    logits = jnp.dot(lhs, w2s, preferred_element_type=F32) + bias2
    g = _log_sigmoid(logits) * (1.0 / GLA_GATE_NORMALIZER)
    pre = _dot_exact_lhs(prefix, g)
    tot = jnp.concatenate([jnp.broadcast_to(pre[lo + c - 1:lo + c, :], (c, pre.shape[1]))
                           for lo in range(0, pre.shape[0], c)], axis=0)
    dk = GLA_DK
    bcs = (pre[:, :dk], tot[:, dk:] - pre[:, dk:] + g[:, dk:])
    out = []
    for d, bc in enumerate(bcs):
        b_end = tot[:, d * dk:(d + 1) * dk]
        qe = (q * (GLA_DK ** -0.5) * jnp.exp(bc)).astype(BF16)
        ke = (k * jnp.exp(-bc)).astype(BF16)
        kend = (k * jnp.exp(b_end - bc)).astype(BF16)
        out.append((qe, ke, kend, b_end))
    return out


def _state_step(s_ref, v, kend, dec):
    upd = lax.dot_general(v, kend, (((0,), (0,)), ((), ())), preferred_element_type=F32)
    s_ref[...] = s_ref[...] * dec + upd


def _gla_kernel(q_ref, k_ref, v_ref, lr_ref, og_ref, qc_ref, kc_ref, vc_ref, lrc_ref,
                w2_ref, b_ref, ng_ref, o_ref,
                s_ref, qe_ref, ke_ref, kend_ref, dec_ref, kendc_ref, decc_ref, sbf_ref):
    c = GLA_CHUNK
    t = q_ref.shape[1]
    tc = qc_ref.shape[1]
    nc = t // c
    tile = GLA_PREP_TILE
    cpt = tile // c
    dirs = (0, 1)

    def store_dec(ref, d, base, b_end):
        for ch in range(b_end.shape[0] // c):
            ref[d, pl.ds(base + ch, 1), :] = jnp.exp(b_end[ch * c:ch * c + 1, :])

    prefix = _prefix_operator(tile)

    for i in range(tc // tile):
        rows = slice(i * tile, (i + 1) * tile)
        prep_c = _gate_prep(qc_ref[0, rows, :], kc_ref[0, rows, :], lrc_ref[0, rows, :],
                            w2_ref[...], b_ref[...], prefix)
        for d, (_, _, kend, b_end) in enumerate(prep_c):
            kendc_ref[d, rows, :] = kend
            store_dec(decc_ref, d, i * cpt, b_end)
    s_ref[...] = jnp.zeros_like(s_ref)
    for j in range(tc // c):
        for d in dirs:
            ch = j if d == 0 else tc // c - 1 - j
            _state_step(s_ref.at[d], vc_ref[0, ch * c:(ch + 1) * c, :], kendc_ref[d, ch * c:(ch + 1) * c, :],
                        decc_ref[d, ch:ch + 1, :])

    def prep(i, carry):
        lo = pl.multiple_of(i * tile, tile)
        rows = pl.ds(lo, tile)
        prep_l = _gate_prep(q_ref[0, rows, :], k_ref[0, rows, :], lr_ref[0, rows, :],
                            w2_ref[...], b_ref[...], prefix)
        for d, (qe, ke, kend, b_end) in enumerate(prep_l):
            qe_ref[d, rows, :] = qe
            ke_ref[d, rows, :] = ke
            kend_ref[d, rows, :] = kend
            store_dec(dec_ref, d, i * cpt, b_end)
        return carry

    lax.fori_loop(0, t // tile, prep, 0)

    def scan(j, carry):
        for d in dirs:
            ch = j if d == 0 else nc - 1 - j
            rows = pl.ds(pl.multiple_of(ch * c, c), c)
            sbf_ref[d, ch] = s_ref[d].astype(BF16)
            _state_step(s_ref.at[d], v_ref[0, rows, :], kend_ref[d, rows, :], dec_ref[d, pl.ds(ch, 1), :])
        return carry

    lax.fori_loop(0, nc, scan, 0, unroll=4)

    ri = lax.broadcasted_iota(jnp.int32, (c, c), 0)
    ci = lax.broadcasted_iota(jnp.int32, (c, c), 1)
    masks = (ci <= ri, ci >= ri)

    group = GLA_OUT_GROUP

    def out(i, carry):
        chunks = [i * group + u for u in range(group)]
        rows = [pl.ds(pl.multiple_of(ch * c, c), c) for ch in chunks]
        qes = [[qe_ref[d, r, :] for d in dirs] for r in rows]
        scores = [[lax.dot_general(qes[u][d], ke_ref[d, rows[u], :], (((1,), (1,)), ((), ())),
                                   preferred_element_type=F32) for d in dirs] for u in range(group)]
        inter = [[lax.dot_general(qes[u][d], sbf_ref[d, chunks[u]], (((1,), (1,)), ((), ())),
                                  preferred_element_type=F32) for d in dirs] for u in range(group)]
        for u in range(group):
            v = v_ref[0, rows[u], :]
            o = inter[u][0] + inter[u][1]
            for d in dirs:
                a = jnp.where(masks[d], scores[u][d], 0.0).astype(BF16)
                o = o + jnp.dot(a, v, preferred_element_type=F32)
            on = o * lax.rsqrt(jnp.mean(o * o, axis=-1, keepdims=True) + EPS) * ng_ref[...]
            og = og_ref[0, rows[u], :]
            o_ref[0, rows[u], :] = (on * (og * jax.nn.sigmoid(og))).astype(BF16)
        return carry

    lax.fori_loop(0, nc // group, out, 0)


def _gla(gq, gk, gv, lr, og, gqc, gkc, gvc, lrc, w2, bias, ng):
    b, t, _ = gq.shape
    tc = gqc.shape[1]
    hk = lambda bi, h: (bi, 0, h)
    h0 = lambda bi, h: (bi, 0, 0)
    return pl.pallas_call(
        _gla_kernel,
        grid=(b, GLA_HEADS),
        in_specs=[
            pl.BlockSpec((1, t, GLA_DK), hk),
            pl.BlockSpec((1, t, GLA_DK), hk),
            pl.BlockSpec((1, t, GLA_DV), hk),
            pl.BlockSpec((1, t, LANES), h0),
            pl.BlockSpec((1, t, GLA_DV), hk),
            pl.BlockSpec((1, tc, GLA_DK), hk),
            pl.BlockSpec((1, tc, GLA_DK), hk),
            pl.BlockSpec((1, tc, GLA_DV), hk),
            pl.BlockSpec((1, tc, LANES), h0),
            pl.BlockSpec((None, LANES, 2 * GLA_DK), lambda bi, h: (h, 0, 0)),
            pl.BlockSpec((None, 1, 2 * GLA_DK), lambda bi, h: (h, 0, 0)),
            pl.BlockSpec((1, GLA_DV), lambda bi, h: (0, 0)),
        ],
        out_specs=pl.BlockSpec((1, t, GLA_DV), hk),
        out_shape=jax.ShapeDtypeStruct((b, t, GLA_V_W), BF16),
        scratch_shapes=[
            pltpu.VMEM((2, GLA_DV, GLA_DK), F32),
            pltpu.VMEM((2, t, GLA_DK), BF16),
            pltpu.VMEM((2, t, GLA_DK), BF16),
            pltpu.VMEM((2, t, GLA_DK), BF16),
            pltpu.VMEM((2, t // GLA_CHUNK, GLA_DK), F32),
            pltpu.VMEM((2, tc, GLA_DK), BF16),
            pltpu.VMEM((2, tc // GLA_CHUNK, GLA_DK), F32),
            pltpu.VMEM((2, t // GLA_CHUNK, GLA_DV, GLA_DK), BF16),
        ],
        compiler_params=_cparams("arbitrary", "arbitrary"),
        name="gla",
    )(gq, gk, gv, lr, og, gqc, gkc, gvc, lrc, w2, bias, ng)


def _store_token_tiles(ref, x):
    n = x.shape[0]
    k = x.shape[1] // LANES
    for s in range(k):
        ref[pl.ds(s, n, stride=k), :] = x[:, s * LANES:(s + 1) * LANES]


def _load_token_tiles(ref, n):
    k = ref.shape[0] // n
    return jnp.concatenate([ref[pl.ds(s, n, stride=k), :] for s in range(k)], axis=1)


def _out_proj_kernel(attn_ref, gla_ref, wo_ref, x_ref, gt_ref, g2_ref, sc_ref, sh_ref, wr_ref, br_ref,
                     x1_ref, h2_ref, ids_ref, wts_ref):
    y = jnp.dot(attn_ref[0], wo_ref[0:ATTN_Q_W, :], preferred_element_type=F32)
    y = y + jnp.dot(gla_ref[0], wo_ref[ATTN_Q_W:, :], preferred_element_type=F32)
    x1 = x_ref[0] + gt_ref[0] * y
    x1_ref[0] = x1
    xn = x1 * lax.rsqrt(jnp.mean(x1 * x1, axis=-1, keepdims=True) + EPS)
    h2 = xn * g2_ref[...] * (1.0 + sc_ref[0]) + sh_ref[0]
    hi = h2.astype(BF16)
    hi_f = hi.astype(F32)

    half = h2.shape[1] // 2
    bits = lax.bitcast_convert_type(hi_f, jnp.uint32)
    words = lax.shift_right_logical(bits[:, :half], jnp.uint32(16)) | (bits[:, half:] & jnp.uint32(0xFFFF0000))
    _store_token_tiles(h2_ref.at[0], words)

    mid = (h2 - hi_f).astype(BF16)
    both = jnp.dot(hi, wr_ref[...], preferred_element_type=F32)
    logits = (both[:, :LANES] + both[:, LANES:]
              + jnp.dot(mid, wr_ref[:, :LANES], preferred_element_type=F32)) + br_ref[...]
    lane = lax.broadcasted_iota(jnp.int32, logits.shape, 1)
    lane_f = lane.astype(F32)
    neg = jnp.float32(-jnp.inf)

    def first_argmax(vals):
        m = jnp.max(vals, axis=-1, keepdims=True)
        idx = jnp.min(jnp.where(vals == m, lane_f, float(LANES)), axis=-1, keepdims=True)
        return m, idx

    lg = jnp.where(lane < N_GROUPS, logits, neg)
    mg, grp = first_argmax(lg)
    pg_sel = 1.0 / jnp.sum(jnp.exp(lg - mg), axis=-1, keepdims=True)
    lo = N_GROUPS + grp * EXPERTS_PER_GROUP
    in_grp = (lane_f >= lo) & (lane_f < lo + EXPERTS_PER_GROUP)
    le = jnp.where(in_grp, logits, neg)
    v1, i1 = first_argmax(le)
    v2, i2 = first_argmax(jnp.where(lane_f == i1, neg, le))
    e2 = jnp.exp(v2 - v1)
    w1 = pg_sel / (1.0 + e2)
    w2 = pg_sel * e2 / (1.0 + e2)
    ids = jnp.where(lane == 0, i1 - N_GROUPS, jnp.where(lane == 1, i2 - N_GROUPS, 0.0))
    ids_ref[0] = ids.astype(jnp.int32)
    wts_ref[0] = jnp.where(lane == 0, w1, jnp.where(lane == 1, w2, 0.0))


def _out_proj(attn, gla, wo, x, gt1, g2, sc2, sh2, wr, br, tm):
    b, t, d = x.shape
    row = lambda bi, i: (bi, i, 0)
    vec = lambda bi, i: (bi, 0, 0)
    const = lambda bi, i: (0, 0)
    return pl.pallas_call(
        _out_proj_kernel,
        grid=(b, t // tm),
        in_specs=[
            pl.BlockSpec((1, tm, ATTN_Q_W), row),
            pl.BlockSpec((1, tm, GLA_V_W), row),
            pl.BlockSpec(wo.shape, const, pipeline_mode=pl.Buffered(1)),
            pl.BlockSpec((1, tm, d), row),
            pl.BlockSpec((1, 1, d), vec),
            pl.BlockSpec((1, d), const),
            pl.BlockSpec((1, 1, d), vec),
            pl.BlockSpec((1, 1, d), vec),
            pl.BlockSpec((d, 2 * LANES), const),
            pl.BlockSpec((1, LANES), const),
        ],
        out_specs=[
            pl.BlockSpec((1, tm, d), row),
            pl.BlockSpec((1, tm * (d // 2 // LANES), LANES), row),
            pl.BlockSpec((1, tm, LANES), row),
            pl.BlockSpec((1, tm, LANES), row),
        ],
        out_shape=[
            jax.ShapeDtypeStruct((b, t, d), F32),
            jax.ShapeDtypeStruct((b, t * (d // 2 // LANES), LANES), jnp.uint32),
            jax.ShapeDtypeStruct((b, t, LANES), jnp.int32),
            jax.ShapeDtypeStruct((b, t, LANES), F32),
        ],
        compiler_params=_cparams("arbitrary", "arbitrary"),
        name="out_proj",
    )(attn, gla, wo, x, gt1, g2, sc2, sh2, wr, br)


def _moe_kernel(eidx_ref, eblk0_ref, enb_ref, bs0_ref, bn_ref, nblk_ref, order_ref,
                h2_hbm, w1_ref, w3_ref, w2_ref, y_hbm,
                xbuf, ybuf, w1b, w3b, w2b, gsem, ssem):
    del eidx_ref
    e = pl.program_id(0)
    nblk = nblk_ref[0]
    kx = xbuf.shape[1] // MOE_BLOCK
    ky = ybuf.shape[1] // MOE_BLOCK
    tokens = y_hbm.shape[0] // (TOP_K * ky)

    def gather_copy(sl, hbm_row, j, rows=1):
        return pltpu.make_async_copy(h2_hbm.at[pl.ds(pl.multiple_of(hbm_row * kx, kx), rows * kx)],
                                     xbuf.at[sl, pl.ds(pl.multiple_of(j * kx, kx), rows * kx)], gsem.at[sl])

    def scatter_copy(sl, hbm_row, j, rows=1):
        return pltpu.make_async_copy(ybuf.at[sl, pl.ds(pl.multiple_of(j * ky, ky), rows * ky)],
                                     y_hbm.at[pl.ds(pl.multiple_of(hbm_row * ky, ky), rows * ky)], ssem.at[sl])

    def gather_row(s, j):
        return lax.shift_right_logical(order_ref[s + j], 1)

    def scatter_row(s, j):
        a = order_ref[s + j]
        return (a & 1) * tokens + lax.shift_right_logical(a, 1)

    def for_rows(n, per_group, per_row):
        ng = lax.shift_right_logical(n, ROW_GROUP_LOG2)
        lax.fori_loop(0, ng, lambda g, c: (per_group(g * ROW_GROUP), c)[1], 0)
        lax.fori_loop(ng * ROW_GROUP, n, lambda j, c: (per_row(j), c)[1], 0)

    def start_rows(copy, hbm_row, blk, sl):
        s = bs0_ref[blk]

        def group(j0):
            for u in range(ROW_GROUP):
                copy(sl, hbm_row(s, j0 + u), j0 + u).start(priority=ROW_DMA_PRIORITY)

        for_rows(bn_ref[blk], group, lambda j: copy(sl, hbm_row(s, j), j).start(priority=ROW_DMA_PRIORITY))

    def wait_rows(copy, blk, sl):
        for_rows(bn_ref[blk], lambda j0: copy(sl, 0, 0, ROW_GROUP).wait(), lambda j: copy(sl, 0, 0).wait())

    start_gather = functools.partial(start_rows, gather_copy, gather_row)
    wait_gather = functools.partial(wait_rows, gather_copy)
    start_scatter = functools.partial(start_rows, scatter_copy, scatter_row)
    wait_scatter = functools.partial(wait_rows, scatter_copy)

    @pl.when(e == 0)
    def _():
        xbuf[...] = jnp.zeros_like(xbuf)
        start_gather(0, 0)

    def block(g, carry):
        slot = g & 1

        @pl.when(g + 1 < nblk)
        def _():
            start_gather(g + 1, 1 - slot)

        wait_gather(g, slot)
        words = _load_token_tiles(xbuf.at[slot], MOE_BLOCK)
        x_lo = lax.bitcast_convert_type(lax.shift_left(words, jnp.uint32(16)), F32)
        x_hi = lax.bitcast_convert_type(words & jnp.uint32(0xFFFF0000), F32)
        x = jnp.concatenate([x_lo, x_hi], axis=1).astype(BF16)
        a = jnp.dot(x, w1b[...], preferred_element_type=F32)
        gate = jnp.dot(x, w3b[...], preferred_element_type=F32)
        hid = (a * jax.nn.sigmoid(a) * gate).astype(BF16)
        y = jnp.dot(hid, w2b[...], preferred_element_type=F32)

        @pl.when(g >= 2)
        def _():
            wait_scatter(g - 2, slot)

        _store_token_tiles(ybuf.at[slot], y)
        start_scatter(g, slot)
        return carry

    @pl.when(enb_ref[e] > 0)
    def _():
        w1b[...] = w1_ref[...].astype(BF16)
        w3b[...] = w3_ref[...].astype(BF16)
        w2b[...] = w2_ref[...].astype(BF16)
        lax.fori_loop(eblk0_ref[e], eblk0_ref[e] + enb_ref[e], block, 0)

    @pl.when(e == pl.num_programs(0) - 1)
    def _():
        @pl.when(nblk >= 2)
        def _():
            wait_scatter(nblk - 2, nblk & 1)

        wait_scatter(nblk - 1, (nblk - 1) & 1)


def _moe(h2t, w1, w3, w2, eidx, eblk0, enb, bs0, bn, nblk, order):
    ne, d, ff = w1.shape
    kx = d // 2 // LANES
    ky = d // LANES
    a = order.shape[0]
    wmap = lambda e, eidx, *_: (eidx[e], 0, 0)
    grid_spec = pltpu.PrefetchScalarGridSpec(
        num_scalar_prefetch=7,
        grid=(ne,),
        in_specs=[
            pl.BlockSpec(memory_space=pl.ANY),
            pl.BlockSpec((None, d, ff), wmap),
            pl.BlockSpec((None, d, ff), wmap),
            pl.BlockSpec((None, ff, d), wmap),
        ],
        out_specs=pl.BlockSpec(memory_space=pl.ANY),
        scratch_shapes=[
            pltpu.VMEM((2, MOE_BLOCK * kx, LANES), jnp.uint32),
            pltpu.VMEM((2, MOE_BLOCK * ky, LANES), F32),
            pltpu.VMEM((d, ff), BF16),
            pltpu.VMEM((d, ff), BF16),
            pltpu.VMEM((ff, d), BF16),
            pltpu.SemaphoreType.DMA((2,)),
            pltpu.SemaphoreType.DMA((2,)),
        ],
    )
    return pl.pallas_call(
        _moe_kernel,
        grid_spec=grid_spec,
        out_shape=jax.ShapeDtypeStruct((a * ky, LANES), F32),
        compiler_params=_cparams("arbitrary"),
        name="moe",
    )(eidx, eblk0, enb, bs0, bn, nblk, order, h2t, w1, w3, w2)


def _combine_kernel(x1_ref, y0_ref, y1_ref, wts_ref, gt_ref, o_ref):
    tm = x1_ref.shape[1]
    w = wts_ref[0]
    ff = w[:, 0:1] * _load_token_tiles(y0_ref.at[0], tm) + w[:, 1:2] * _load_token_tiles(y1_ref.at[0], tm)
    o_ref[0] = x1_ref[0] + gt_ref[0] * ff


def _combine(x1, y2, wts, gt2, tm):
    b, t, d = x1.shape
    ky = d // LANES
    row = lambda bi, i: (bi, i, 0)
    vec = lambda bi, i: (bi, 0, 0)
    return pl.pallas_call(
        _combine_kernel,
        grid=(b, t // tm),
        in_specs=[
            pl.BlockSpec((1, tm, d), row),
            pl.BlockSpec((None, 1, tm * ky, LANES), lambda bi, i: (0, bi, i, 0)),
            pl.BlockSpec((None, 1, tm * ky, LANES), lambda bi, i: (1, bi, i, 0)),
            pl.BlockSpec((1, tm, LANES), row),
            pl.BlockSpec((1, 1, d), vec),
        ],
        out_specs=pl.BlockSpec((1, tm, d), row),
        out_shape=jax.ShapeDtypeStruct((b, t, d), F32),
        compiler_params=_cparams("arbitrary", "arbitrary"),
        name="combine",
    )(x1, y2, y2, wts, gt2)


def _rope_tables(t):
    pos = jnp.arange(t, dtype=jnp.int32)
    r = (pos // GRID_W).astype(F32)
    col = (pos % GRID_W).astype(F32)
    n_freq = HEAD_DIM // 4
    inv = ROPE_THETA ** (-jnp.arange(n_freq, dtype=F32) / n_freq)
    ar = r[:, None] * inv
    ac = col[:, None] * inv
    cos_t = jnp.concatenate([jnp.cos(ar), jnp.cos(ar), jnp.cos(ac), jnp.cos(ac)], axis=1)
    sin_t = jnp.concatenate([-jnp.sin(ar), jnp.sin(ar), -jnp.sin(ac), jnp.sin(ac)], axis=1)
    return cos_t, sin_t


def _gate_weights(w2, bias):
    r = GLA_GATE_RANK
    wh = w2.reshape(2, r, GLA_HEADS, GLA_DK).transpose(2, 0, 1, 3)
    w = jnp.zeros((GLA_HEADS, 2 * r, 2 * GLA_DK), F32)
    w = w.at[:, 0:r, 0:GLA_DK].set(wh[:, 0]).at[:, r:2 * r, GLA_DK:].set(wh[:, 1])
    hi = w.astype(BF16)
    mid = (w - hi.astype(F32)).astype(BF16)
    pad = jnp.zeros((GLA_HEADS, LANES - 6 * r, 2 * GLA_DK), BF16)
    w2s = jnp.concatenate([hi, hi, mid, pad], axis=1)
    bias2 = bias.reshape(2, GLA_HEADS, GLA_DK).transpose(1, 0, 2).reshape(GLA_HEADS, 1, 2 * GLA_DK)
    return w2s, bias2


def _block_plan(eid_flat, nb):
    order = jnp.argsort(eid_flat).astype(jnp.int32)
    counts = jnp.zeros((N_EXPERTS,), jnp.int32).at[eid_flat].add(1)
    starts = jnp.cumsum(counts) - counts
    nblk_e = (counts + MOE_BLOCK - 1) // MOE_BLOCK
    bends = jnp.cumsum(nblk_e)
    bstarts = bends - nblk_e
    nblk = bends[-1]
    eids = jnp.arange(N_EXPERTS, dtype=jnp.int32)
    prev_used = lax.cummax(jnp.where(counts > 0, eids, -1))
    eidx = jnp.where(prev_used >= 0, prev_used, jnp.argmax(counts > 0)).astype(jnp.int32)
    bi = jnp.arange(nb, dtype=jnp.int32)
    bic = jnp.minimum(bi, jnp.maximum(nblk - 1, 0))
    bexp = jnp.clip(jnp.searchsorted(bends, bic, side='right'), 0, N_EXPERTS - 1)
    r0 = (bic - bstarts[bexp]) * MOE_BLOCK
    bs0 = (starts[bexp] + r0).astype(jnp.int32)
    bn = jnp.where(bi < nblk, jnp.minimum(counts[bexp] - r0, MOE_BLOCK), 0).astype(jnp.int32)
    return (eidx, bstarts.astype(jnp.int32), nblk_e.astype(jnp.int32), bs0, bn,
            nblk.reshape(1).astype(jnp.int32), order)


def kernel(x, c, ctx, c_ctx, w_ada, b_ada, norm1_g, w_in, q_norm_g, k_norm_g, gla_gate_w2, gla_gate_b, gla_norm_g, w_out, norm2_g, router_grp_w, router_grp_b, router_exp_w, router_exp_b, moe_w1, moe_w3, moe_w2):
    b, t, d = x.shape
    tc = ctx.shape[1]
    depth = w_ada.shape[0]
    assert depth == 1, "single-layer stack: the context stream only feeds keys/values and GLA states"
    layer = 0

    c8 = jnp.zeros((8, d), F32).at[0:b].set(c).at[b].set(c_ctx)
    mod = _ada(c8, w_ada[layer], b_ada[layer])
    sh1, sc1, gt1, sh2, sc2, gt2 = [mod[0:b, i * d:(i + 1) * d].reshape(b, 1, d) for i in range(6)]
    sh1c, sc1c = [jnp.broadcast_to(mod[b, i * d:(i + 1) * d].reshape(1, 1, d), (b, 1, d)) for i in range(2)]

    wi = w_in[layer]
    o_aq, o_ak, o_av, o_gq, o_gk, o_gv, o_lr, o_og = 0, 1024, 1280, 1536, 2048, 2560, 3584, 3616
    w_p = jnp.concatenate([
        wi[:, o_aq:o_lr], wi[:, o_og:o_og + GLA_V_W], wi[:, o_lr:o_og],
        jnp.zeros((d, LANES - 2 * GLA_GATE_RANK), F32)], axis=1).astype(BF16)

    cos_t, sin_t = _rope_tables(t)
    g1 = norm1_g[layer].reshape(1, d)
    qg = q_norm_g[layer].reshape(1, HEAD_DIM)
    kg = k_norm_g[layer].reshape(1, HEAD_DIM)
    q, k, v, gq, gk, gv, og, lr = _in_proj(x, g1, sc1, sh1, w_p, qg, kg, cos_t, sin_t, 512)
    ones_t = jnp.ones((tc, HEAD_DIM), F32)
    _, kc, vc, gqc, gkc, gvc, _, lrc = _in_proj(ctx, g1, sc1c, sh1c, w_p, qg, kg, ones_t, jnp.zeros_like(ones_t), tc)

    attn = _attn(q, jnp.concatenate([kc, k], axis=1), jnp.concatenate([vc, v], axis=1), 256)
    w2s, bias2 = _gate_weights(gla_gate_w2[layer], gla_gate_b[layer])
    gla = _gla(gq, gk, gv, lr, og, gqc, gkc, gvc, lrc, w2s, bias2, gla_norm_g[layer].reshape(1, GLA_DV))

    wr = jnp.concatenate([router_grp_w[layer], router_exp_w[layer],
                          jnp.zeros((d, LANES - N_GROUPS - N_EXPERTS), F32)], axis=1)
    br = jnp.concatenate([router_grp_b[layer], router_exp_b[layer],
                          jnp.zeros((LANES - N_GROUPS - N_EXPERTS,), F32)]).reshape(1, LANES)
    wr_hi = wr.astype(BF16)
    wr_parts = jnp.concatenate([wr_hi, (wr - wr_hi.astype(F32)).astype(BF16)], axis=1)
    x1, h2p, ids, wts = _out_proj(attn, gla, w_out[layer].astype(BF16), x, gt1, norm2_g[layer].reshape(1, d),
                                  sc2, sh2, wr_parts, br, 512)

    m = b * t
    n_assign = m * TOP_K
    nb = -(-(n_assign + N_EXPERTS * (MOE_BLOCK - 1)) // MOE_BLOCK)
    eid_flat = ids[:, :, 0:TOP_K].reshape(n_assign)
    plan = _block_plan(eid_flat, nb)
    y2 = _moe(h2p.reshape(-1, LANES), moe_w1[layer], moe_w3[layer], moe_w2[layer], *plan)
    return _combine(x1, y2.reshape(TOP_K, b, t * (d // LANES), LANES), wts, gt2, 256)
```

```python
import functools

import jax
import jax.numpy as jnp
from jax import lax
from jax.experimental import pallas as pl
from jax.experimental.pallas import tpu as pltpu

EPS = 1e-6
GRID_W = 64
ROPE_THETA = 10000.0

ATTN_HEADS = 8
ATTN_KV_HEADS = 2
HEAD_DIM = 128
GQA_GROUP = ATTN_HEADS // ATTN_KV_HEADS

GLA_HEADS = 4
GLA_DK = 128
GLA_DV = 256
GLA_GATE_RANK = 16
GLA_GATE_NORMALIZER = 16.0
GLA_CHUNK = 64
GLA_CHUNK_LOG2 = 6
GLA_PREP_TILE = 256
GLA_OUT_GROUP = 4

N_GROUPS = 8
EXPERTS_PER_GROUP = 8
N_EXPERTS = N_GROUPS * EXPERTS_PER_GROUP
TOP_K = 2
MOE_BLOCK = 128
ROW_GROUP_LOG2 = 3
ROW_GROUP = 1 << ROW_GROUP_LOG2
GATHER_AHEAD = 2

LOG2_E = 1.4426950408889634
LANES = 128
VMEM_LIMIT = 56 * 1024 * 1024

ATTN_Q_W = ATTN_HEADS * HEAD_DIM
ATTN_KV_W = ATTN_KV_HEADS * HEAD_DIM
GLA_K_W = GLA_HEADS * GLA_DK
GLA_V_W = GLA_HEADS * GLA_DV

BF16 = jnp.bfloat16
F32 = jnp.float32


def _cparams(*sem):
    return pltpu.CompilerParams(dimension_semantics=sem, vmem_limit_bytes=VMEM_LIMIT)


def _ada_kernel(c_ref, w_ref, b_ref, o_ref):
    c = c_ref[...]
    s = c * jax.nn.sigmoid(c)
    o_ref[...] = jnp.dot(s.astype(BF16), w_ref[...].astype(BF16), preferred_element_type=F32) + b_ref[...]


def _ada(c8, w, b):
    d, n = w.shape
    tn = 1024
    return pl.pallas_call(
        _ada_kernel,
        grid=(n // tn,),
        in_specs=[
            pl.BlockSpec((8, d), lambda j: (0, 0)),
            pl.BlockSpec((d, tn), lambda j: (0, j)),
            pl.BlockSpec((1, tn), lambda j: (0, j)),
        ],
        out_specs=pl.BlockSpec((8, tn), lambda j: (0, j)),
        out_shape=jax.ShapeDtypeStruct((8, n), F32),
        compiler_params=_cparams("arbitrary"),
        name="ada",
    )(c8, w, b.reshape(1, n))


def _swap32(y):
    lane = lax.broadcasted_iota(jnp.int32, y.shape, 1)
    return jnp.where((lane & 63) < 32, pltpu.roll(y, 96, 1), pltpu.roll(y, 32, 1))


def _head_norm_rope(a, g, cos, sin):
    y = a * lax.rsqrt(jnp.mean(a * a, axis=-1, keepdims=True) + EPS) * g
    return y * cos + _swap32(y) * sin


def _in_proj_kernel(x_ref, g1_ref, sc_ref, sh_ref, w_ref, wt_ref, qg_ref, kg_ref, cos_ref, sin_ref,
                    q_ref, k_ref, v_ref, gq_ref, gk_ref, gv_ref, og_ref, lr_ref):
    x = x_ref[0]
    xn = x * lax.rsqrt(jnp.mean(x * x, axis=-1, keepdims=True) + EPS)
    h = (xn * g1_ref[...] * (1.0 + sc_ref[0]) + sh_ref[0]).astype(BF16)
    cos = cos_ref[...]
    sin = sin_ref[...]

    def proj(off, width, w=w_ref):
        return jnp.dot(h, w[:, off:off + width], preferred_element_type=F32)

    qg = qg_ref[...] * (HEAD_DIM ** -0.5 * LOG2_E)
    off = 0
    for out_ref, gain, heads in ((q_ref, qg, ATTN_HEADS), (k_ref, kg_ref[...], ATTN_KV_HEADS)):
        for pair in range(heads // 2):
            a2 = proj(off, 2 * HEAD_DIM)
            for u in range(2):
                lo = (2 * pair + u) * HEAD_DIM
                a = a2[:, u * HEAD_DIM:(u + 1) * HEAD_DIM]
                out_ref[0, :, lo:lo + HEAD_DIM] = _head_norm_rope(a, gain, cos, sin).astype(BF16)
            off += 2 * HEAD_DIM
    v_ref[0] = proj(off, ATTN_KV_W).astype(BF16)
    off += ATTN_KV_W
    gq_ref[0] = proj(off, GLA_K_W)
    off += GLA_K_W
    gk_ref[0] = proj(off, GLA_K_W)
    off += GLA_K_W
    gv_ref[0] = proj(off, GLA_V_W).astype(BF16)
    og_ref[0] = proj(0, GLA_V_W, wt_ref)
    lr_ref[0] = proj(GLA_V_W, LANES, wt_ref)


def _in_proj(x, g1, sc, sh, w_main, w_tail, qg, kg, cos_t, sin_t, tm):
    b, t, d = x.shape
    row = lambda bi, i: (bi, i, 0)
    vec = lambda bi, i: (bi, 0, 0)
    const = lambda bi, i: (0, 0)
    tab = lambda bi, i: (i, 0)
    widths = [(ATTN_Q_W, BF16), (ATTN_KV_W, BF16), (ATTN_KV_W, BF16), (GLA_K_W, F32), (GLA_K_W, F32),
              (GLA_V_W, BF16), (GLA_V_W, F32), (LANES, F32)]
    return pl.pallas_call(
        _in_proj_kernel,
        grid=(b, t // tm),
        in_specs=[
            pl.BlockSpec((1, tm, d), row),
            pl.BlockSpec((1, d), const),
            pl.BlockSpec((1, 1, d), vec),
            pl.BlockSpec((1, 1, d), vec),
            pl.BlockSpec(w_main.shape, const, pipeline_mode=pl.Buffered(1)),
            pl.BlockSpec(w_tail.shape, const, pipeline_mode=pl.Buffered(1)),
            pl.BlockSpec((1, HEAD_DIM), const),
            pl.BlockSpec((1, HEAD_DIM), const),
            pl.BlockSpec((tm, HEAD_DIM), tab),
            pl.BlockSpec((tm, HEAD_DIM), tab),
        ],
        out_specs=[pl.BlockSpec((1, tm, wd), row) for wd, _ in widths],
        out_shape=[jax.ShapeDtypeStruct((b, t, wd), dt) for wd, dt in widths],
        compiler_params=_cparams("arbitrary", "arbitrary"),
        name="in_proj",
    )(x, g1, sc, sh, w_main, w_tail, qg, kg, cos_t, sin_t)


def _attn_kernel(q_ref, k_ref, v_ref, o_ref):
    k = k_ref[0]
    v = v_ref[0]

    def scores(g):
        q = q_ref[0, :, g * HEAD_DIM:(g + 1) * HEAD_DIM]
        return lax.dot_general(q, k, (((1,), (1,)), ((), ())), preferred_element_type=F32)

    s_next = scores(0)
    for g in range(GQA_GROUP):
        s = s_next
        if g + 1 < GQA_GROUP:
            s_next = scores(g + 1)
        p = jnp.exp2(s - jnp.max(s, axis=-1, keepdims=True))
        l = jnp.sum(p, axis=-1, keepdims=True)
        o = jnp.dot(p.astype(BF16), v, preferred_element_type=F32)
        o_ref[0, :, g * HEAD_DIM:(g + 1) * HEAD_DIM] = (o / l).astype(BF16)


def _attn(q, k_all, v_all, tq):
    b, t, _ = q.shape
    tk = k_all.shape[1]
    gw = GQA_GROUP * HEAD_DIM
    return pl.pallas_call(
        _attn_kernel,
        grid=(b, ATTN_KV_HEADS, t // tq),
        in_specs=[
            pl.BlockSpec((1, tq, gw), lambda bi, hk, i: (bi, i, hk)),
            pl.BlockSpec((1, tk, HEAD_DIM), lambda bi, hk, i: (bi, 0, hk)),
            pl.BlockSpec((1, tk, HEAD_DIM), lambda bi, hk, i: (bi, 0, hk)),
        ],
        out_specs=pl.BlockSpec((1, tq, gw), lambda bi, hk, i: (bi, i, hk)),
        out_shape=jax.ShapeDtypeStruct((b, t, ATTN_Q_W), BF16),
        compiler_params=_cparams("arbitrary", "arbitrary", "arbitrary"),
        name="attn",
    )(q, k_all, v_all)


def _log_sigmoid(x):
    return jnp.minimum(x, 0.0) - jnp.log(1.0 + jnp.exp(-jnp.abs(x)))


def _split3(x):
    hi = x.astype(BF16)
    r = x - hi.astype(F32)
    mid = r.astype(BF16)
    lo = (r - mid.astype(F32)).astype(BF16)
    return hi, mid, lo


def _dot_exact_lhs(m, x):
    return sum(jnp.dot(m, part, preferred_element_type=F32) for part in _split3(x))


def _prefix_operator(n):
    ri = lax.broadcasted_iota(jnp.int32, (n, n), 0)
    ci = lax.broadcasted_iota(jnp.int32, (n, n), 1)
    same = lax.shift_right_logical(ri, GLA_CHUNK_LOG2) == lax.shift_right_logical(ci, GLA_CHUNK_LOG2)
    return (same & (ci <= ri)).astype(BF16)


def _gate_prep(q, k, lr, w2s, bias2, prefix):
    c = GLA_CHUNK
    hi = lr.astype(BF16).astype(F32)
    mid = (lr - hi).astype(BF16).astype(F32)
    lhs = (hi + pltpu.roll(mid, 2 * GLA_GATE_RANK, 1) + pltpu.roll(hi, 4 * GLA_GATE_RANK, 1)).astype(BF16)
    logits = jnp.dot(lhs, w2s, preferred_element_type=F32) + bias2
    g = _log_sigmoid(logits) * (1.0 / GLA_GATE_NORMALIZER)
    pre = _dot_exact_lhs(prefix, g)
    tot = jnp.concatenate([jnp.broadcast_to(pre[lo + c - 1:lo + c, :], (c, pre.shape[1]))
                           for lo in range(0, pre.shape[0], c)], axis=0)
    dk = GLA_DK
    bcs = (pre[:, :dk], tot[:, dk:] - pre[:, dk:] + g[:, dk:])
    out = []
    for d, bc in enumerate(bcs):
        b_end = tot[:, d * dk:(d + 1) * dk]
        qe = (q * (GLA_DK ** -0.5) * jnp.exp(bc)).astype(BF16)
        ke = (k * jnp.exp(-bc)).astype(BF16)
        kend = (k * jnp.exp(b_end - bc)).astype(BF16)
        out.append((qe, ke, kend, b_end))
    return out


def _state_step(s_ref, v, kend, dec):
    upd = lax.dot_general(v, kend, (((0,), (0,)), ((), ())), preferred_element_type=F32)
    s_ref[...] = s_ref[...] * dec + upd


def _gla_kernel(q_ref, k_ref, v_ref, lr_ref, og_ref, qc_ref, kc_ref, vc_ref, lrc_ref,
                w2_ref, b_ref, ng_ref, o_ref,
                s_ref, qe_ref, ke_ref, kend_ref, dec_ref, kendc_ref, decc_ref, sbf_ref):
    c = GLA_CHUNK
    t = q_ref.shape[1]
    tc = qc_ref.shape[1]
    nc = t // c
    tile = GLA_PREP_TILE
    cpt = tile // c
    dirs = (0, 1)

    def store_dec(ref, d, base, b_end):
        for ch in range(b_end.shape[0] // c):
            ref[d, pl.ds(base + ch, 1), :] = jnp.exp(b_end[ch * c:ch * c + 1, :])

    prefix = _prefix_operator(tile)

    for i in range(tc // tile):
        rows = slice(i * tile, (i + 1) * tile)
        prep_c = _gate_prep(qc_ref[0, rows, :], kc_ref[0, rows, :], lrc_ref[0, rows, :],
                            w2_ref[...], b_ref[...], prefix)
        for d, (_, _, kend, b_end) in enumerate(prep_c):
            kendc_ref[d, rows, :] = kend
            store_dec(decc_ref, d, i * cpt, b_end)
    s_ref[...] = jnp.zeros_like(s_ref)
    for j in range(tc // c):
        for d in dirs:
            ch = j if d == 0 else tc // c - 1 - j
            _state_step(s_ref.at[d], vc_ref[0, ch * c:(ch + 1) * c, :], kendc_ref[d, ch * c:(ch + 1) * c, :],
                        decc_ref[d, ch:ch + 1, :])

    def prep(i, carry):
        lo = pl.multiple_of(i * tile, tile)
        rows = pl.ds(lo, tile)
        prep_l = _gate_prep(q_ref[0, rows, :], k_ref[0, rows, :], lr_ref[0, rows, :],
                            w2_ref[...], b_ref[...], prefix)
        for d, (qe, ke, kend, b_end) in enumerate(prep_l):
            qe_ref[d, rows, :] = qe
            ke_ref[d, rows, :] = ke
            kend_ref[d, rows, :] = kend
            store_dec(dec_ref, d, i * cpt, b_end)
        return carry

    lax.fori_loop(0, t // tile, prep, 0)

    def scan(j, carry):
        for d in dirs:
            ch = j if d == 0 else nc - 1 - j
            rows = pl.ds(pl.multiple_of(ch * c, c), c)
            sbf_ref[d, ch] = s_ref[d].astype(BF16)
            _state_step(s_ref.at[d], v_ref[0, rows, :], kend_ref[d, rows, :], dec_ref[d, pl.ds(ch, 1), :])
        return carry

    lax.fori_loop(0, nc, scan, 0, unroll=4)

    ri = lax.broadcasted_iota(jnp.int32, (c, c), 0)
    ci = lax.broadcasted_iota(jnp.int32, (c, c), 1)
    masks = (ci <= ri, ci >= ri)

    group = GLA_OUT_GROUP

    def out(i, carry):
        chunks = [i * group + u for u in range(group)]
        rows = [pl.ds(pl.multiple_of(ch * c, c), c) for ch in chunks]
        qes = [[qe_ref[d, r, :] for d in dirs] for r in rows]
        scores = [[lax.dot_general(qes[u][d], ke_ref[d, rows[u], :], (((1,), (1,)), ((), ())),
                                   preferred_element_type=F32) for d in dirs] for u in range(group)]
        inter = [[lax.dot_general(qes[u][d], sbf_ref[d, chunks[u]], (((1,), (1,)), ((), ())),
                                  preferred_element_type=F32) for d in dirs] for u in range(group)]
        for u in range(group):
            v = v_ref[0, rows[u], :]
            o = inter[u][0] + inter[u][1]
            for d in dirs:
                a = jnp.where(masks[d], scores[u][d], 0.0).astype(BF16)
                o = o + jnp.dot(a, v, preferred_element_type=F32)
            on = o * lax.rsqrt(jnp.mean(o * o, axis=-1, keepdims=True) + EPS) * ng_ref[...]
            og = og_ref[0, rows[u], :]
            o_ref[0, rows[u], :] = (on * (og * jax.nn.sigmoid(og))).astype(BF16)
        return carry

    lax.fori_loop(0, nc // group, out, 0)


def _gla(gq, gk, gv, lr, og, gqc, gkc, gvc, lrc, w2, bias, ng):
    b, t, _ = gq.shape
    tc = gqc.shape[1]
    hk = lambda bi, h: (bi, 0, h)
    h0 = lambda bi, h: (bi, 0, 0)
    return pl.pallas_call(
        _gla_kernel,
        grid=(b, GLA_HEADS),
        in_specs=[
            pl.BlockSpec((1, t, GLA_DK), hk),
            pl.BlockSpec((1, t, GLA_DK), hk),
            pl.BlockSpec((1, t, GLA_DV), hk),
            pl.BlockSpec((1, t, LANES), h0),
            pl.BlockSpec((1, t, GLA_DV), hk),
            pl.BlockSpec((1, tc, GLA_DK), hk),
            pl.BlockSpec((1, tc, GLA_DK), hk),
            pl.BlockSpec((1, tc, GLA_DV), hk),
            pl.BlockSpec((1, tc, LANES), h0),
            pl.BlockSpec((None, LANES, 2 * GLA_DK), lambda bi, h: (h, 0, 0)),
            pl.BlockSpec((None, 1, 2 * GLA_DK), lambda bi, h: (h, 0, 0)),
            pl.BlockSpec((1, GLA_DV), lambda bi, h: (0, 0)),
        ],
        out_specs=pl.BlockSpec((1, t, GLA_DV), hk),
        out_shape=jax.ShapeDtypeStruct((b, t, GLA_V_W), BF16),
        scratch_shapes=[
            pltpu.VMEM((2, GLA_DV, GLA_DK), F32),
            pltpu.VMEM((2, t, GLA_DK), BF16),
            pltpu.VMEM((2, t, GLA_DK), BF16),
            pltpu.VMEM((2, t, GLA_DK), BF16),
            pltpu.VMEM((2, t // GLA_CHUNK, GLA_DK), F32),
            pltpu.VMEM((2, tc, GLA_DK), BF16),
            pltpu.VMEM((2, tc // GLA_CHUNK, GLA_DK), F32),
            pltpu.VMEM((2, t // GLA_CHUNK, GLA_DV, GLA_DK), BF16),
        ],
        compiler_params=_cparams("arbitrary", "arbitrary"),
        name="gla",
    )(gq, gk, gv, lr, og, gqc, gkc, gvc, lrc, w2, bias, ng)


def _store_token_tiles(ref, x):
    n = x.shape[0]
    k = x.shape[1] // LANES
    for s in range(k):
        ref[pl.ds(s, n, stride=k), :] = x[:, s * LANES:(s + 1) * LANES]


def _load_token_tiles(ref, n):
    k = ref.shape[0] // n
    return jnp.concatenate([ref[pl.ds(s, n, stride=k), :] for s in range(k)], axis=1)


def _out_proj_kernel(attn_ref, gla_ref, wo_ref, x_ref, gt_ref, g2_ref, sc_ref, sh_ref, wr_ref, br_ref,
                     x1_ref, h2_ref, ids_ref, wts_ref):
    y = jnp.dot(attn_ref[0], wo_ref[0:ATTN_Q_W, :], preferred_element_type=F32)
    y = y + jnp.dot(gla_ref[0], wo_ref[ATTN_Q_W:, :], preferred_element_type=F32)
    x1 = x_ref[0] + gt_ref[0] * y
    x1_ref[0] = x1
    xn = x1 * lax.rsqrt(jnp.mean(x1 * x1, axis=-1, keepdims=True) + EPS)
    h2 = xn * g2_ref[...] * (1.0 + sc_ref[0]) + sh_ref[0]
    hi = h2.astype(BF16)
    hi_f = hi.astype(F32)

    _store_token_tiles(h2_ref.at[0], h2)

    mid = (h2 - hi_f).astype(BF16)
    both = jnp.dot(hi, wr_ref[...], preferred_element_type=F32)
    logits = (both[:, :LANES] + both[:, LANES:]
              + jnp.dot(mid, wr_ref[:, :LANES], preferred_element_type=F32)) + br_ref[...]
    lane = lax.broadcasted_iota(jnp.int32, logits.shape, 1)
    lane_f = lane.astype(F32)
    neg = jnp.float32(-jnp.inf)

    def first_argmax(vals):
        m = jnp.max(vals, axis=-1, keepdims=True)
        idx = jnp.min(jnp.where(vals == m, lane_f, float(LANES)), axis=-1, keepdims=True)
        return m, idx

    lg = jnp.where(lane < N_GROUPS, logits, neg)
    mg, grp = first_argmax(lg)
    pg_sel = 1.0 / jnp.sum(jnp.exp(lg - mg), axis=-1, keepdims=True)
    lo = N_GROUPS + grp * EXPERTS_PER_GROUP
    in_grp = (lane_f >= lo) & (lane_f < lo + EXPERTS_PER_GROUP)
    le = jnp.where(in_grp, logits, neg)
    v1, i1 = first_argmax(le)
    v2, i2 = first_argmax(jnp.where(lane_f == i1, neg, le))
    e2 = jnp.exp(v2 - v1)
    w1 = pg_sel / (1.0 + e2)
    w2 = pg_sel * e2 / (1.0 + e2)
    ids = jnp.where(lane == 0, i1 - N_GROUPS, jnp.where(lane == 1, i2 - N_GROUPS, 0.0))
    ids_ref[0] = ids.astype(jnp.int32)
    wts_ref[0] = jnp.where(lane == 0, w1, jnp.where(lane == 1, w2, 0.0))


def _out_proj(attn, gla, wo, x, gt1, g2, sc2, sh2, wr, br, tm):
    b, t, d = x.shape
    row = lambda bi, i: (bi, i, 0)
    vec = lambda bi, i: (bi, 0, 0)
    const = lambda bi, i: (0, 0)
    return pl.pallas_call(
        _out_proj_kernel,
        grid=(b, t // tm),
        in_specs=[
            pl.BlockSpec((1, tm, ATTN_Q_W), row),
            pl.BlockSpec((1, tm, GLA_V_W), row),
            pl.BlockSpec(wo.shape, const, pipeline_mode=pl.Buffered(1)),
            pl.BlockSpec((1, tm, d), row),
            pl.BlockSpec((1, 1, d), vec),
            pl.BlockSpec((1, d), const),
            pl.BlockSpec((1, 1, d), vec),
            pl.BlockSpec((1, 1, d), vec),
            pl.BlockSpec((d, 2 * LANES), const),
            pl.BlockSpec((1, LANES), const),
        ],
        out_specs=[
            pl.BlockSpec((1, tm, d), row),
            pl.BlockSpec((1, tm * (d // LANES), LANES), row),
            pl.BlockSpec((1, tm, LANES), row),
            pl.BlockSpec((1, tm, LANES), row),
        ],
        out_shape=[
            jax.ShapeDtypeStruct((b, t, d), F32),
            jax.ShapeDtypeStruct((b, t * (d // LANES), LANES), F32),
            jax.ShapeDtypeStruct((b, t, LANES), jnp.int32),
            jax.ShapeDtypeStruct((b, t, LANES), F32),
        ],
        compiler_params=_cparams("arbitrary", "arbitrary"),
        name="out_proj",
    )(attn, gla, wo, x, gt1, g2, sc2, sh2, wr, br)


def _moe_kernel(eidx_ref, eblk0_ref, enb_ref, bs0_ref, bn_ref, nblk_ref, order_ref,
                h2_hbm, w1_ref, w3_ref, w2_ref, y_hbm,
                xbuf, ybuf, w1b, w3b, w2b, gsem, ssem):
    del eidx_ref
    e = pl.program_id(0)
    nblk = nblk_ref[0]
    kx = xbuf.shape[1] // MOE_BLOCK
    ky = ybuf.shape[1] // MOE_BLOCK
    tokens = y_hbm.shape[0] // (TOP_K * ky)

    def gather_copy(sl, hbm_row, j, rows=1):
        return pltpu.make_async_copy(h2_hbm.at[pl.ds(pl.multiple_of(hbm_row * kx, kx), rows * kx)],
                                     xbuf.at[sl, pl.ds(pl.multiple_of(j * kx, kx), rows * kx)], gsem.at[sl])

    def scatter_copy(sl, hbm_row, j, rows=1):
        return pltpu.make_async_copy(ybuf.at[sl, pl.ds(pl.multiple_of(j * ky, ky), rows * ky)],
                                     y_hbm.at[pl.ds(pl.multiple_of(hbm_row * ky, ky), rows * ky)], ssem.at[sl])

    def gather_row(s, j):
        return lax.shift_right_logical(order_ref[s + j], 1)

    def scatter_row(s, j):
        a = order_ref[s + j]
        return (a & 1) * tokens + lax.shift_right_logical(a, 1)

    def for_rows(n, per_group, per_row):
        ng = lax.shift_right_logical(n, ROW_GROUP_LOG2)
        lax.fori_loop(0, ng, lambda g, c: (per_group(g * ROW_GROUP), c)[1], 0)
        lax.fori_loop(ng * ROW_GROUP, n, lambda j, c: (per_row(j), c)[1], 0)

    def start_rows(copy, hbm_row, blk, sl):
        s = bs0_ref[blk]

        def group(j0):
            for u in range(ROW_GROUP):
                copy(sl, hbm_row(s, j0 + u), j0 + u).start()

        for_rows(bn_ref[blk], group, lambda j: copy(sl, hbm_row(s, j), j).start())

    def wait_rows(copy, blk, sl):
        for_rows(bn_ref[blk], lambda j0: copy(sl, 0, 0, ROW_GROUP).wait(), lambda j: copy(sl, 0, 0).wait())

    start_gather = functools.partial(start_rows, gather_copy, gather_row)
    wait_gather = functools.partial(wait_rows, gather_copy)
    start_scatter = functools.partial(start_rows, scatter_copy, scatter_row)
    wait_scatter = functools.partial(wait_rows, scatter_copy)

    @pl.when(e == 0)
    def _():
        xbuf[...] = jnp.zeros_like(xbuf)
        for g0 in range(GATHER_AHEAD):
            @pl.when(g0 < nblk)
            def _():
                start_gather(g0, g0)

    def block(g, carry):
        slot = g & 1
        xslot = lax.rem(g, GATHER_AHEAD + 1)

        @pl.when(g + GATHER_AHEAD < nblk)
        def _():
            start_gather(g + GATHER_AHEAD, lax.rem(g + GATHER_AHEAD, GATHER_AHEAD + 1))

        wait_gather(g, xslot)
        x = _load_token_tiles(xbuf.at[xslot], MOE_BLOCK).astype(BF16)
        a = jnp.dot(x, w1b[...], preferred_element_type=F32)
        gate = jnp.dot(x, w3b[...], preferred_element_type=F32)
        hid = (a * jax.nn.sigmoid(a) * gate).astype(BF16)
        y = jnp.dot(hid, w2b[...], preferred_element_type=F32)

        @pl.when(g >= 2)
        def _():
            wait_scatter(g - 2, slot)

        _store_token_tiles(ybuf.at[slot], y)
        start_scatter(g, slot)
        return carry

    @pl.when(enb_ref[e] > 0)
    def _():
        w1b[...] = w1_ref[...].astype(BF16)
        w3b[...] = w3_ref[...].astype(BF16)
        w2b[...] = w2_ref[...].astype(BF16)
        lax.fori_loop(eblk0_ref[e], eblk0_ref[e] + enb_ref[e], block, 0)

    @pl.when(e == pl.num_programs(0) - 1)
    def _():
        @pl.when(nblk >= 2)
        def _():
            wait_scatter(nblk - 2, nblk & 1)

        wait_scatter(nblk - 1, (nblk - 1) & 1)


def _moe(h2t, w1, w3, w2, eidx, eblk0, enb, bs0, bn, nblk, order):
    ne, d, ff = w1.shape
    kx = d // LANES
    ky = d // LANES
    a = order.shape[0]
    wmap = lambda e, eidx, *_: (eidx[e], 0, 0)
    grid_spec = pltpu.PrefetchScalarGridSpec(
        num_scalar_prefetch=7,
        grid=(ne,),
        in_specs=[
            pl.BlockSpec(memory_space=pl.ANY),
            pl.BlockSpec((None, d, ff), wmap),
            pl.BlockSpec((None, d, ff), wmap),
            pl.BlockSpec((None, ff, d), wmap),
        ],
        out_specs=pl.BlockSpec(memory_space=pl.ANY),
        scratch_shapes=[
            pltpu.VMEM((GATHER_AHEAD + 1, MOE_BLOCK * kx, LANES), F32),
            pltpu.VMEM((2, MOE_BLOCK * ky, LANES), F32),
            pltpu.VMEM((d, ff), BF16),
            pltpu.VMEM((d, ff), BF16),
            pltpu.VMEM((ff, d), BF16),
            pltpu.SemaphoreType.DMA((GATHER_AHEAD + 1,)),
            pltpu.SemaphoreType.DMA((2,)),
        ],
    )
    return pl.pallas_call(
        _moe_kernel,
        grid_spec=grid_spec,
        out_shape=jax.ShapeDtypeStruct((a * ky, LANES), F32),
        compiler_params=_cparams("arbitrary"),
        name="moe",
    )(eidx, eblk0, enb, bs0, bn, nblk, order, h2t, w1, w3, w2)


def _combine_kernel(x1_ref, y0_ref, y1_ref, wts_ref, gt_ref, o_ref):
    tm = x1_ref.shape[1]
    w = wts_ref[0]
    ff = w[:, 0:1] * _load_token_tiles(y0_ref.at[0], tm) + w[:, 1:2] * _load_token_tiles(y1_ref.at[0], tm)
    o_ref[0] = x1_ref[0] + gt_ref[0] * ff


def _combine(x1, y2, wts, gt2, tm):
    b, t, d = x1.shape
    ky = d // LANES
    row = lambda bi, i: (bi, i, 0)
    vec = lambda bi, i: (bi, 0, 0)
    return pl.pallas_call(
        _combine_kernel,
        grid=(b, t // tm),
        in_specs=[
            pl.BlockSpec((1, tm, d), row),
            pl.BlockSpec((None, 1, tm * ky, LANES), lambda bi, i: (0, bi, i, 0)),
            pl.BlockSpec((None, 1, tm * ky, LANES), lambda bi, i: (1, bi, i, 0)),
            pl.BlockSpec((1, tm, LANES), row),
            pl.BlockSpec((1, 1, d), vec),
        ],
        out_specs=pl.BlockSpec((1, tm, d), row),
        out_shape=jax.ShapeDtypeStruct((b, t, d), F32),
        compiler_params=_cparams("arbitrary", "arbitrary"),
        name="combine",
    )(x1, y2, y2, wts, gt2)


def _rope_tables(t):
    rows = t // GRID_W
    n_freq = HEAD_DIM // 4
    inv = ROPE_THETA ** (-jnp.arange(n_freq, dtype=F32) / n_freq)
    ar = jnp.arange(rows, dtype=F32)[:, None] * inv
    ac = jnp.arange(GRID_W, dtype=F32)[:, None] * inv

    def expand(fr, fc, sign):
        by_row = jnp.broadcast_to(fr[:, None, :], (rows, GRID_W, n_freq))
        by_col = jnp.broadcast_to(fc[None, :, :], (rows, GRID_W, n_freq))
        return jnp.concatenate([sign * by_row, by_row, sign * by_col, by_col], axis=2).reshape(t, HEAD_DIM)

    return expand(jnp.cos(ar), jnp.cos(ac), 1.0), expand(jnp.sin(ar), jnp.sin(ac), -1.0)


def _gate_weights(w2, bias):
    r = GLA_GATE_RANK
    wh = w2.reshape(2, r, GLA_HEADS, GLA_DK).transpose(2, 0, 1, 3)
    w = jnp.zeros((GLA_HEADS, 2 * r, 2 * GLA_DK), F32)
    w = w.at[:, 0:r, 0:GLA_DK].set(wh[:, 0]).at[:, r:2 * r, GLA_DK:].set(wh[:, 1])
    hi = w.astype(BF16)
    mid = (w - hi.astype(F32)).astype(BF16)
    pad = jnp.zeros((GLA_HEADS, LANES - 6 * r, 2 * GLA_DK), BF16)
    w2s = jnp.concatenate([hi, hi, mid, pad], axis=1)
    bias2 = bias.reshape(2, GLA_HEADS, GLA_DK).transpose(1, 0, 2).reshape(GLA_HEADS, 1, 2 * GLA_DK)
    return w2s, bias2


def _block_plan(eid_flat, nb):
    order = jnp.argsort(eid_flat).astype(jnp.int32)
    eids = jnp.arange(N_EXPERTS, dtype=jnp.int32)
    counts = jnp.sum((eid_flat[:, None] == eids[None, :]).astype(jnp.int32), axis=0)
    starts = jnp.cumsum(counts) - counts
    nblk_e = (counts + MOE_BLOCK - 1) // MOE_BLOCK
    bends = jnp.cumsum(nblk_e)
    bstarts = bends - nblk_e
    nblk = bends[-1]
    prev_used = lax.cummax(jnp.where(counts > 0, eids, -1))
    eidx = jnp.where(prev_used >= 0, prev_used, jnp.argmax(counts > 0)).astype(jnp.int32)
    bi = jnp.arange(nb, dtype=jnp.int32)
    bic = jnp.minimum(bi, jnp.maximum(nblk - 1, 0))
    bexp = jnp.minimum(jnp.sum(bends[None, :] <= bic[:, None], axis=1), N_EXPERTS - 1)
    of_block = (bexp[:, None] == eids[None, :]).astype(jnp.int32)

    def lookup(per_expert):
        return jnp.sum(of_block * per_expert[None, :], axis=1)

    r0 = (bic - lookup(bstarts)) * MOE_BLOCK
    bs0 = (lookup(starts) + r0).astype(jnp.int32)
    bn = jnp.where(bi < nblk, jnp.minimum(lookup(counts) - r0, MOE_BLOCK), 0).astype(jnp.int32)
    return (eidx, bstarts.astype(jnp.int32), nblk_e.astype(jnp.int32), bs0, bn,
            nblk.reshape(1).astype(jnp.int32), order)


def kernel(x, c, ctx, c_ctx, w_ada, b_ada, norm1_g, w_in, q_norm_g, k_norm_g, gla_gate_w2, gla_gate_b, gla_norm_g, w_out, norm2_g, router_grp_w, router_grp_b, router_exp_w, router_exp_b, moe_w1, moe_w3, moe_w2):
    b, t, d = x.shape
    tc = ctx.shape[1]
    depth = w_ada.shape[0]
    assert depth == 1, "single-layer stack: the context stream only feeds keys/values and GLA states"
    layer = 0

    c8 = jnp.zeros((8, d), F32).at[0:b].set(c).at[b].set(c_ctx)
    mod = _ada(c8, w_ada[layer], b_ada[layer])
    sh1, sc1, gt1, sh2, sc2, gt2 = [mod[0:b, i * d:(i + 1) * d].reshape(b, 1, d) for i in range(6)]
    sh1c, sc1c = [jnp.broadcast_to(mod[b, i * d:(i + 1) * d].reshape(1, 1, d), (b, 1, d)) for i in range(2)]

    wi = w_in[layer]
    o_lr = ATTN_Q_W + 2 * ATTN_KV_W + 2 * GLA_K_W + GLA_V_W
    o_og = o_lr + 2 * GLA_GATE_RANK
    w_main = wi[:, :o_lr].astype(BF16)
    w_tail = jnp.concatenate([wi[:, o_og:], wi[:, o_lr:o_og],
                              jnp.zeros((d, LANES - 2 * GLA_GATE_RANK), F32)], axis=1).astype(BF16)

    cos_t, sin_t = _rope_tables(t)
    g1 = norm1_g[layer].reshape(1, d)
    qg = q_norm_g[layer].reshape(1, HEAD_DIM)
    kg = k_norm_g[layer].reshape(1, HEAD_DIM)
    q, k, v, gq, gk, gv, og, lr = _in_proj(x, g1, sc1, sh1, w_main, w_tail, qg, kg, cos_t, sin_t, 512)
    ones_t = jnp.ones((tc, HEAD_DIM), F32)
    _, kc, vc, gqc, gkc, gvc, _, lrc = _in_proj(ctx, g1, sc1c, sh1c, w_main, w_tail, qg, kg,
                                                ones_t, jnp.zeros_like(ones_t), tc)

    attn = _attn(q, jnp.concatenate([kc, k], axis=1), jnp.concatenate([vc, v], axis=1), 256)
    w2s, bias2 = _gate_weights(gla_gate_w2[layer], gla_gate_b[layer])
    gla = _gla(gq, gk, gv, lr, og, gqc, gkc, gvc, lrc, w2s, bias2, gla_norm_g[layer].reshape(1, GLA_DV))

    wr = jnp.concatenate([router_grp_w[layer], router_exp_w[layer],
                          jnp.zeros((d, LANES - N_GROUPS - N_EXPERTS), F32)], axis=1)
    br = jnp.concatenate([router_grp_b[layer], router_exp_b[layer],
                          jnp.zeros((LANES - N_GROUPS - N_EXPERTS,), F32)]).reshape(1, LANES)
    wr_hi = wr.astype(BF16)
    wr_parts = jnp.concatenate([wr_hi, (wr - wr_hi.astype(F32)).astype(BF16)], axis=1)
    x1, h2t, ids, wts = _out_proj(attn, gla, w_out[layer].astype(BF16), x, gt1, norm2_g[layer].reshape(1, d),
                                  sc2, sh2, wr_parts, br, 512)

    m = b * t
    n_assign = m * TOP_K
    nb = -(-(n_assign + N_EXPERTS * (MOE_BLOCK - 1)) // MOE_BLOCK)
    eid_flat = ids[:, :, 0:TOP_K].reshape(n_assign)
    plan = _block_plan(eid_flat, nb)
    y2 = _moe(h2t.reshape(-1, LANES), moe_w1[layer], moe_w3[layer], moe_w2[layer], *plan)
    return _combine(x1, y2.reshape(TOP_K, b, t * (d // LANES), LANES), wts, gt2, 256)
```

```python
import functools

import jax
import jax.numpy as jnp
from jax import lax
from jax.experimental import pallas as pl
from jax.experimental.pallas import tpu as pltpu

EPS = 1e-6
GRID_W = 64
ROPE_THETA = 10000.0

ATTN_HEADS = 8
ATTN_KV_HEADS = 2
HEAD_DIM = 128
GQA_GROUP = ATTN_HEADS // ATTN_KV_HEADS

GLA_HEADS = 4
GLA_DK = 128
GLA_DV = 256
GLA_GATE_RANK = 16
GLA_GATE_NORMALIZER = 16.0
GLA_CHUNK = 64
GLA_CHUNK_LOG2 = 6
GLA_PREP_TILE = 256
GLA_OUT_GROUP = 8

N_GROUPS = 8
EXPERTS_PER_GROUP = 8
N_EXPERTS = N_GROUPS * EXPERTS_PER_GROUP
TOP_K = 2
MOE_BLOCK = 256
ROW_GROUP_LOG2 = 3
ROW_GROUP = 1 << ROW_GROUP_LOG2
GATHER_AHEAD = 2

LOG2_E = 1.4426950408889634
LANES = 128
VMEM_LIMIT = 56 * 1024 * 1024

ATTN_Q_W = ATTN_HEADS * HEAD_DIM
ATTN_KV_W = ATTN_KV_HEADS * HEAD_DIM
GLA_K_W = GLA_HEADS * GLA_DK
GLA_V_W = GLA_HEADS * GLA_DV

BF16 = jnp.bfloat16
F32 = jnp.float32


def _cparams(*sem):
    return pltpu.CompilerParams(dimension_semantics=sem, vmem_limit_bytes=VMEM_LIMIT)


def _ada_kernel(c_ref, w_ref, b_ref, o_ref):
    c = c_ref[...]
    s = c * jax.nn.sigmoid(c)
    o_ref[...] = jnp.dot(s.astype(BF16), w_ref[...].astype(BF16), preferred_element_type=F32) + b_ref[...]


def _ada(c8, w, b):
    d, n = w.shape
    tn = 1024
    return pl.pallas_call(
        _ada_kernel,
        grid=(n // tn,),
        in_specs=[
            pl.BlockSpec((8, d), lambda j: (0, 0)),
            pl.BlockSpec((d, tn), lambda j: (0, j)),
            pl.BlockSpec((1, tn), lambda j: (0, j)),
        ],
        out_specs=pl.BlockSpec((8, tn), lambda j: (0, j)),
        out_shape=jax.ShapeDtypeStruct((8, n), F32),
        compiler_params=_cparams("arbitrary"),
        name="ada",
    )(c8, w, b.reshape(1, n))


def _w_in_kernel(w_ref, main_ref, tail_ref):
    n_main = main_ref.shape[1]
    r = 2 * GLA_GATE_RANK
    main_ref[...] = w_ref[:, :n_main].astype(BF16)
    tail_ref[:, :GLA_V_W] = w_ref[:, n_main + r:].astype(BF16)
    lowrank = w_ref[:, n_main:n_main + r].astype(BF16)
    tail_ref[:, GLA_V_W:] = jnp.concatenate(
        [lowrank, jnp.zeros((lowrank.shape[0], LANES - r), BF16)], axis=1)


def _w_in_parts(w):
    d, n = w.shape
    n_main = n - GLA_V_W - 2 * GLA_GATE_RANK
    tk = 256
    return pl.pallas_call(
        _w_in_kernel,
        grid=(d // tk,),
        in_specs=[pl.BlockSpec((tk, n), lambda i: (i, 0))],
        out_specs=[pl.BlockSpec((tk, n_main), lambda i: (i, 0)),
                   pl.BlockSpec((tk, GLA_V_W + LANES), lambda i: (i, 0))],
        out_shape=[jax.ShapeDtypeStruct((d, n_main), BF16),
                   jax.ShapeDtypeStruct((d, GLA_V_W + LANES), BF16)],
        compiler_params=_cparams("arbitrary"),
        name="w_in_parts",
    )(w)


def _swap32(y):
    lane = lax.broadcasted_iota(jnp.int32, y.shape, 1)
    return jnp.where((lane & 63) < 32, pltpu.roll(y, 96, 1), pltpu.roll(y, 32, 1))


def _head_norm_rope(a, g, cos, sin):
    y = a * lax.rsqrt(jnp.mean(a * a, axis=-1, keepdims=True) + EPS) * g
    return y * cos + _swap32(y) * sin


def _in_proj_kernel(x_ref, g1_ref, sc_ref, sh_ref, w_ref, wt_ref, qg_ref, kg_ref, cos_ref, sin_ref,
                    q_ref, k_ref, v_ref, gq_ref, gk_ref, gv_ref, og_ref, lr_ref):
    x = x_ref[0]
    xn = x * lax.rsqrt(jnp.mean(x * x, axis=-1, keepdims=True) + EPS)
    h = (xn * g1_ref[...] * (1.0 + sc_ref[0]) + sh_ref[0]).astype(BF16)
    cos = cos_ref[...]
    sin = sin_ref[...]

    def proj(off, width, w=w_ref):
        return jnp.dot(h, w[:, off:off + width], preferred_element_type=F32)

    qg = qg_ref[...] * (HEAD_DIM ** -0.5 * LOG2_E)
    off = 0
    for out_ref, gain, heads in ((q_ref, qg, ATTN_HEADS), (k_ref, kg_ref[...], ATTN_KV_HEADS)):
        for pair in range(heads // 2):
            a2 = proj(off, 2 * HEAD_DIM)
            for u in range(2):
                lo = (2 * pair + u) * HEAD_DIM
                a = a2[:, u * HEAD_DIM:(u + 1) * HEAD_DIM]
                out_ref[0, :, lo:lo + HEAD_DIM] = _head_norm_rope(a, gain, cos, sin).astype(BF16)
            off += 2 * HEAD_DIM
    v_ref[0] = proj(off, ATTN_KV_W).astype(BF16)
    off += ATTN_KV_W
    gq_ref[0] = proj(off, GLA_K_W)
    off += GLA_K_W
    gk_ref[0] = proj(off, GLA_K_W)
    off += GLA_K_W
    gv_ref[0] = proj(off, GLA_V_W).astype(BF16)
    og_ref[0] = proj(0, GLA_V_W, wt_ref)
    lr_ref[0] = proj(GLA_V_W, LANES, wt_ref)


def _in_proj(x, g1, sc, sh, w_main, w_tail, qg, kg, cos_t, sin_t, tm):
    b, t, d = x.shape
    row = lambda bi, i: (bi, i, 0)
    vec = lambda bi, i: (bi, 0, 0)
    const = lambda bi, i: (0, 0)
    tab = lambda bi, i: (i, 0)
    widths = [(ATTN_Q_W, BF16), (ATTN_KV_W, BF16), (ATTN_KV_W, BF16), (GLA_K_W, F32), (GLA_K_W, F32),
              (GLA_V_W, BF16), (GLA_V_W, F32), (LANES, F32)]
    return pl.pallas_call(
        _in_proj_kernel,
        grid=(b, t // tm),
        in_specs=[
            pl.BlockSpec((1, tm, d), row),
            pl.BlockSpec((1, d), const),
            pl.BlockSpec((1, 1, d), vec),
            pl.BlockSpec((1, 1, d), vec),
            pl.BlockSpec(w_main.shape, const, pipeline_mode=pl.Buffered(1)),
            pl.BlockSpec(w_tail.shape, const, pipeline_mode=pl.Buffered(1)),
            pl.BlockSpec((1, HEAD_DIM), const),
            pl.BlockSpec((1, HEAD_DIM), const),
            pl.BlockSpec((tm, HEAD_DIM), tab),
            pl.BlockSpec((tm, HEAD_DIM), tab),
        ],
        out_specs=[pl.BlockSpec((1, tm, wd), row) for wd, _ in widths],
        out_shape=[jax.ShapeDtypeStruct((b, t, wd), dt) for wd, dt in widths],
        compiler_params=_cparams("arbitrary", "arbitrary"),
        name="in_proj",
    )(x, g1, sc, sh, w_main, w_tail, qg, kg, cos_t, sin_t)


def _attn_kernel(q_ref, k_ref, v_ref, o_ref):
    k = k_ref[0]
    v = v_ref[0]

    def scores(g):
        q = q_ref[0, :, g * HEAD_DIM:(g + 1) * HEAD_DIM]
        return lax.dot_general(q, k, (((1,), (1,)), ((), ())), preferred_element_type=F32)

    s_next = scores(0)
    for g in range(GQA_GROUP):
        s = s_next
        if g + 1 < GQA_GROUP:
            s_next = scores(g + 1)
        p = jnp.exp2(s - jnp.max(s, axis=-1, keepdims=True))
        l = jnp.sum(p, axis=-1, keepdims=True)
        o = jnp.dot(p.astype(BF16), v, preferred_element_type=F32)
        o_ref[0, :, g * HEAD_DIM:(g + 1) * HEAD_DIM] = (o / l).astype(BF16)


def _attn(q, k_all, v_all, tq):
    b, t, _ = q.shape
    tk = k_all.shape[1]
    gw = GQA_GROUP * HEAD_DIM
    return pl.pallas_call(
        _attn_kernel,
        grid=(b, ATTN_KV_HEADS, t // tq),
        in_specs=[
            pl.BlockSpec((1, tq, gw), lambda bi, hk, i: (bi, i, hk)),
            pl.BlockSpec((1, tk, HEAD_DIM), lambda bi, hk, i: (bi, 0, hk)),
            pl.BlockSpec((1, tk, HEAD_DIM), lambda bi, hk, i: (bi, 0, hk)),
        ],
        out_specs=pl.BlockSpec((1, tq, gw), lambda bi, hk, i: (bi, i, hk)),
        out_shape=jax.ShapeDtypeStruct((b, t, ATTN_Q_W), BF16),
        compiler_params=_cparams("arbitrary", "arbitrary", "arbitrary"),
        name="attn",
    )(q, k_all, v_all)


def _log_sigmoid(x):
    return jnp.minimum(x, 0.0) - jnp.log(1.0 + jnp.exp(-jnp.abs(x)))


def _split3(x):
    hi = x.astype(BF16)
    r = x - hi.astype(F32)
    mid = r.astype(BF16)
    lo = (r - mid.astype(F32)).astype(BF16)
    return hi, mid, lo


def _dot_exact_lhs(m, x):
    return sum(jnp.dot(m, part, preferred_element_type=F32) for part in _split3(x))


def _prefix_operator(n):
    ri = lax.broadcasted_iota(jnp.int32, (n, n), 0)
    ci = lax.broadcasted_iota(jnp.int32, (n, n), 1)
    same = lax.shift_right_logical(ri, GLA_CHUNK_LOG2) == lax.shift_right_logical(ci, GLA_CHUNK_LOG2)
    return (same & (ci <= ri)).astype(BF16)


def _gate_prep(q, k, lr, w2s, bias2, prefix):
    c = GLA_CHUNK
    hi = lr.astype(BF16).astype(F32)
    mid = (lr - hi).astype(BF16).astype(F32)
    lhs = (hi + pltpu.roll(mid, 2 * GLA_GATE_RANK, 1) + pltpu.roll(hi, 4 * GLA_GATE_RANK, 1)).astype(BF16)
    logits = jnp.dot(lhs, w2s, preferred_element_type=F32) + bias2
    g = _log_sigmoid(logits) * (1.0 / GLA_GATE_NORMALIZER)
    pre = _dot_exact_lhs(prefix, g)
    tot = jnp.concatenate([jnp.broadcast_to(pre[lo + c - 1:lo + c, :], (c, pre.shape[1]))
                           for lo in range(0, pre.shape[0], c)], axis=0)
    dk = GLA_DK
    bcs = (pre[:, :dk], tot[:, dk:] - pre[:, dk:] + g[:, dk:])
    out = []
    for d, bc in enumerate(bcs):
        b_end = tot[:, d * dk:(d + 1) * dk]
        qe = (q * (GLA_DK ** -0.5) * jnp.exp(bc)).astype(BF16)
        ke = (k * jnp.exp(-bc)).astype(BF16)
        kend = (k * jnp.exp(b_end - bc)).astype(BF16)
        out.append((qe, ke, kend, b_end))
    return out


def _state_step(s_ref, v, kend, dec):
    upd = lax.dot_general(v, kend, (((0,), (0,)), ((), ())), preferred_element_type=F32)
    s_ref[...] = s_ref[...] * dec + upd


def _gla_kernel(q_ref, k_ref, v_ref, lr_ref, og_ref, qc_ref, kc_ref, vc_ref, lrc_ref,
                w2_ref, b_ref, ng_ref, o_ref,
                s_ref, qe_ref, ke_ref, kend_ref, dec_ref, kendc_ref, decc_ref, sbf_ref):
    c = GLA_CHUNK
    t = q_ref.shape[1]
    tc = qc_ref.shape[1]
    nc = t // c
    tile = GLA_PREP_TILE
    cpt = tile // c
    dirs = (0, 1)

    def store_dec(ref, d, base, b_end):
        for ch in range(b_end.shape[0] // c):
            ref[d, pl.ds(base + ch, 1), :] = jnp.exp(b_end[ch * c:ch * c + 1, :])

    prefix = _prefix_operator(tile)

    for i in range(tc // tile):
        rows = slice(i * tile, (i + 1) * tile)
        prep_c = _gate_prep(qc_ref[0, rows, :], kc_ref[0, rows, :], lrc_ref[0, rows, :],
                            w2_ref[...], b_ref[...], prefix)
        for d, (_, _, kend, b_end) in enumerate(prep_c):
            kendc_ref[d, rows, :] = kend
            store_dec(decc_ref, d, i * cpt, b_end)
    s_ref[...] = jnp.zeros_like(s_ref)
    for j in range(tc // c):
        for d in dirs:
            ch = j if d == 0 else tc // c - 1 - j
            _state_step(s_ref.at[d], vc_ref[0, ch * c:(ch + 1) * c, :], kendc_ref[d, ch * c:(ch + 1) * c, :],
                        decc_ref[d, ch:ch + 1, :])

    def prep(i, carry):
        lo = pl.multiple_of(i * tile, tile)
        rows = pl.ds(lo, tile)
        prep_l = _gate_prep(q_ref[0, rows, :], k_ref[0, rows, :], lr_ref[0, rows, :],
                            w2_ref[...], b_ref[...], prefix)
        for d, (qe, ke, kend, b_end) in enumerate(prep_l):
            qe_ref[d, rows, :] = qe
            ke_ref[d, rows, :] = ke
            kend_ref[d, rows, :] = kend
            store_dec(dec_ref, d, i * cpt, b_end)
        return carry

    lax.fori_loop(0, t // tile, prep, 0, unroll=2)

    def scan(j, carry):
        for d in dirs:
            ch = j if d == 0 else nc - 1 - j
            rows = pl.ds(pl.multiple_of(ch * c, c), c)
            sbf_ref[d, ch] = s_ref[d].astype(BF16)
            _state_step(s_ref.at[d], v_ref[0, rows, :], kend_ref[d, rows, :], dec_ref[d, pl.ds(ch, 1), :])
        return carry

    lax.fori_loop(0, nc, scan, 0, unroll=8)

    ri = lax.broadcasted_iota(jnp.int32, (c, c), 0)
    ci = lax.broadcasted_iota(jnp.int32, (c, c), 1)
    masks = (ci <= ri, ci >= ri)

    group = GLA_OUT_GROUP

    def out(i, carry):
        chunks = [i * group + u for u in range(group)]
        rows = [pl.ds(pl.multiple_of(ch * c, c), c) for ch in chunks]
        qes = [[qe_ref[d, r, :] for d in dirs] for r in rows]
        scores = [[lax.dot_general(qes[u][d], ke_ref[d, rows[u], :], (((1,), (1,)), ((), ())),
                                   preferred_element_type=F32) for d in dirs] for u in range(group)]
        inter = [[lax.dot_general(qes[u][d], sbf_ref[d, chunks[u]], (((1,), (1,)), ((), ())),
                                  preferred_element_type=F32) for d in dirs] for u in range(group)]
        for u in range(group):
            v = v_ref[0, rows[u], :]
            o = inter[u][0] + inter[u][1]
            for d in dirs:
                a = jnp.where(masks[d], scores[u][d], 0.0).astype(BF16)
                o = o + jnp.dot(a, v, preferred_element_type=F32)
            on = o * lax.rsqrt(jnp.mean(o * o, axis=-1, keepdims=True) + EPS) * ng_ref[...]
            og = og_ref[0, rows[u], :]
            o_ref[0, rows[u], :] = (on * (og * jax.nn.sigmoid(og))).astype(BF16)
        return carry

    lax.fori_loop(0, nc // group, out, 0)


def _gla(gq, gk, gv, lr, og, gqc, gkc, gvc, lrc, w2, bias, ng):
    b, t, _ = gq.shape
    tc = gqc.shape[1]
    hk = lambda bi, h: (bi, 0, h)
    h0 = lambda bi, h: (bi, 0, 0)
    return pl.pallas_call(
        _gla_kernel,
        grid=(b, GLA_HEADS),
        in_specs=[
            pl.BlockSpec((1, t, GLA_DK), hk),
            pl.BlockSpec((1, t, GLA_DK), hk),
            pl.BlockSpec((1, t, GLA_DV), hk),
            pl.BlockSpec((1, t, LANES), h0),
            pl.BlockSpec((1, t, GLA_DV), hk),
            pl.BlockSpec((1, tc, GLA_DK), hk),
            pl.BlockSpec((1, tc, GLA_DK), hk),
            pl.BlockSpec((1, tc, GLA_DV), hk),
            pl.BlockSpec((1, tc, LANES), h0),
            pl.BlockSpec((None, LANES, 2 * GLA_DK), lambda bi, h: (h, 0, 0)),
            pl.BlockSpec((None, 1, 2 * GLA_DK), lambda bi, h: (h, 0, 0)),
            pl.BlockSpec((1, GLA_DV), lambda bi, h: (0, 0)),
        ],
        out_specs=pl.BlockSpec((1, t, GLA_DV), hk),
        out_shape=jax.ShapeDtypeStruct((b, t, GLA_V_W), BF16),
        scratch_shapes=[
            pltpu.VMEM((2, GLA_DV, GLA_DK), F32),
            pltpu.VMEM((2, t, GLA_DK), BF16),
            pltpu.VMEM((2, t, GLA_DK), BF16),
            pltpu.VMEM((2, t, GLA_DK), BF16),
            pltpu.VMEM((2, t // GLA_CHUNK, GLA_DK), F32),
            pltpu.VMEM((2, tc, GLA_DK), BF16),
            pltpu.VMEM((2, tc // GLA_CHUNK, GLA_DK), F32),
            pltpu.VMEM((2, t // GLA_CHUNK, GLA_DV, GLA_DK), BF16),
        ],
        compiler_params=_cparams("arbitrary", "arbitrary"),
        name="gla",
    )(gq, gk, gv, lr, og, gqc, gkc, gvc, lrc, w2, bias, ng)


def _store_token_tiles(ref, x):
    n = x.shape[0]
    k = x.shape[1] // LANES
    for s in range(k):
        ref[pl.ds(s, n, stride=k), :] = x[:, s * LANES:(s + 1) * LANES]


def _load_token_tiles(ref, n):
    k = ref.shape[0] // n
    return jnp.concatenate([ref[pl.ds(s, n, stride=k), :] for s in range(k)], axis=1)


def _out_proj_kernel(attn_ref, gla_ref, wo_ref, x_ref, gt_ref, g2_ref, sc_ref, sh_ref, wr_ref, br_ref,
                     x1_ref, h2_ref, ids_ref, wts_ref):
    y = jnp.dot(attn_ref[0], wo_ref[0:ATTN_Q_W, :], preferred_element_type=F32)
    y = y + jnp.dot(gla_ref[0], wo_ref[ATTN_Q_W:, :], preferred_element_type=F32)
    x1 = x_ref[0] + gt_ref[0] * y
    x1_ref[0] = x1
    xn = x1 * lax.rsqrt(jnp.mean(x1 * x1, axis=-1, keepdims=True) + EPS)
    h2 = xn * g2_ref[...] * (1.0 + sc_ref[0]) + sh_ref[0]
    hi = h2.astype(BF16)
    hi_f = hi.astype(F32)

    _store_token_tiles(h2_ref.at[0], h2)

    mid = (h2 - hi_f).astype(BF16)
    both = jnp.dot(hi, wr_ref[...], preferred_element_type=F32)
    logits = (both[:, :LANES] + both[:, LANES:]
              + jnp.dot(mid, wr_ref[:, :LANES], preferred_element_type=F32)) + br_ref[...]
    lane = lax.broadcasted_iota(jnp.int32, logits.shape, 1)
    lane_f = lane.astype(F32)
    neg = jnp.float32(-jnp.inf)

    def first_argmax(vals):
        m = jnp.max(vals, axis=-1, keepdims=True)
        idx = jnp.min(jnp.where(vals == m, lane_f, float(LANES)), axis=-1, keepdims=True)
        return m, idx

    lg = jnp.where(lane < N_GROUPS, logits, neg)
    mg, grp = first_argmax(lg)
    pg_sel = 1.0 / jnp.sum(jnp.exp(lg - mg), axis=-1, keepdims=True)
    lo = N_GROUPS + grp * EXPERTS_PER_GROUP
    in_grp = (lane_f >= lo) & (lane_f < lo + EXPERTS_PER_GROUP)
    le = jnp.where(in_grp, logits, neg)
    v1, i1 = first_argmax(le)
    v2, i2 = first_argmax(jnp.where(lane_f == i1, neg, le))
    e2 = jnp.exp(v2 - v1)
    w1 = pg_sel / (1.0 + e2)
    w2 = pg_sel * e2 / (1.0 + e2)
    ids = jnp.where(lane == 0, i1 - N_GROUPS, jnp.where(lane == 1, i2 - N_GROUPS, 0.0))
    ids_ref[0] = ids.astype(jnp.int32)
    wts_ref[0] = jnp.where(lane == 0, w1, jnp.where(lane == 1, w2, 0.0))


def _out_proj(attn, gla, wo, x, gt1, g2, sc2, sh2, wr, br, tm):
    b, t, d = x.shape
    row = lambda bi, i: (bi, i, 0)
    vec = lambda bi, i: (bi, 0, 0)
    const = lambda bi, i: (0, 0)
    return pl.pallas_call(
        _out_proj_kernel,
        grid=(b, t // tm),
        in_specs=[
            pl.BlockSpec((1, tm, ATTN_Q_W), row),
            pl.BlockSpec((1, tm, GLA_V_W), row),
            pl.BlockSpec(wo.shape, const, pipeline_mode=pl.Buffered(1)),
            pl.BlockSpec((1, tm, d), row),
            pl.BlockSpec((1, 1, d), vec),
            pl.BlockSpec((1, d), const),
            pl.BlockSpec((1, 1, d), vec),
            pl.BlockSpec((1, 1, d), vec),
            pl.BlockSpec((d, 2 * LANES), const),
            pl.BlockSpec((1, LANES), const),
        ],
        out_specs=[
            pl.BlockSpec((1, tm, d), row),
            pl.BlockSpec((1, tm * (d // LANES), LANES), row),
            pl.BlockSpec((1, tm, LANES), row),
            pl.BlockSpec((1, tm, LANES), row),
        ],
        out_shape=[
            jax.ShapeDtypeStruct((b, t, d), F32),
            jax.ShapeDtypeStruct((b, t * (d // LANES), LANES), F32),
            jax.ShapeDtypeStruct((b, t, LANES), jnp.int32),
            jax.ShapeDtypeStruct((b, t, LANES), F32),
        ],
        compiler_params=_cparams("arbitrary", "arbitrary"),
        name="out_proj",
    )(attn, gla, wo, x, gt1, g2, sc2, sh2, wr, br)


def _moe_kernel(eidx_ref, eblk0_ref, enb_ref, bs0_ref, bn_ref, nblk_ref, order_ref,
                h2_hbm, w1_ref, w3_ref, w2_ref, y_hbm,
                xbuf, ybuf, w1b, w3b, w2b, gsem, ssem):
    del eidx_ref
    e = pl.program_id(0)
    nblk = nblk_ref[0]
    kx = xbuf.shape[1] // MOE_BLOCK
    ky = ybuf.shape[1] // MOE_BLOCK
    tokens = y_hbm.shape[0] // (TOP_K * ky)

    def gather_copy(sl, hbm_row, j, rows=1):
        return pltpu.make_async_copy(h2_hbm.at[pl.ds(pl.multiple_of(hbm_row * kx, kx), rows * kx)],
                                     xbuf.at[sl, pl.ds(pl.multiple_of(j * kx, kx), rows * kx)], gsem.at[sl])

    def scatter_copy(sl, hbm_row, j, rows=1):
        return pltpu.make_async_copy(ybuf.at[sl, pl.ds(pl.multiple_of(j * ky, ky), rows * ky)],
                                     y_hbm.at[pl.ds(pl.multiple_of(hbm_row * ky, ky), rows * ky)], ssem.at[sl])

    def gather_row(s, j):
        return lax.shift_right_logical(order_ref[s + j], 1)

    def scatter_row(s, j):
        a = order_ref[s + j]
        return (a & 1) * tokens + lax.shift_right_logical(a, 1)

    def for_rows(n, per_group, per_row):
        ng = lax.shift_right_logical(n, ROW_GROUP_LOG2)
        lax.fori_loop(0, ng, lambda g, c: (per_group(g * ROW_GROUP), c)[1], 0)
        lax.fori_loop(ng * ROW_GROUP, n, lambda j, c: (per_row(j), c)[1], 0)

    def start_rows(copy, hbm_row, blk, sl):
        s = bs0_ref[blk]

        def group(j0):
            for u in range(ROW_GROUP):
                copy(sl, hbm_row(s, j0 + u), j0 + u).start()

        for_rows(bn_ref[blk], group, lambda j: copy(sl, hbm_row(s, j), j).start())

    def wait_rows(copy, blk, sl):
        for_rows(bn_ref[blk], lambda j0: copy(sl, 0, 0, ROW_GROUP).wait(), lambda j: copy(sl, 0, 0).wait())

    start_gather = functools.partial(start_rows, gather_copy, gather_row)
    wait_gather = functools.partial(wait_rows, gather_copy)
    start_scatter = functools.partial(start_rows, scatter_copy, scatter_row)
    wait_scatter = functools.partial(wait_rows, scatter_copy)

    @pl.when(e == 0)
    def _():
        xbuf[...] = jnp.zeros_like(xbuf)
        for g0 in range(GATHER_AHEAD):
            @pl.when(g0 < nblk)
            def _():
                start_gather(g0, g0)

    def block(g, carry):
        slot = g & 1
        xslot = lax.rem(g, GATHER_AHEAD + 1)

        @pl.when(g + GATHER_AHEAD < nblk)
        def _():
            start_gather(g + GATHER_AHEAD, lax.rem(g + GATHER_AHEAD, GATHER_AHEAD + 1))

        wait_gather(g, xslot)
        x = _load_token_tiles(xbuf.at[xslot], MOE_BLOCK).astype(BF16)
        a = jnp.dot(x, w1b[...], preferred_element_type=F32)
        gate = jnp.dot(x, w3b[...], preferred_element_type=F32)
        hid = (a * jax.nn.sigmoid(a) * gate).astype(BF16)
        y = jnp.dot(hid, w2b[...], preferred_element_type=F32)

        @pl.when(g >= 2)
        def _():
            wait_scatter(g - 2, slot)

        _store_token_tiles(ybuf.at[slot], y)
        start_scatter(g, slot)
        return carry

    @pl.when(enb_ref[e] > 0)
    def _():
        w1b[...] = w1_ref[...].astype(BF16)
        w3b[...] = w3_ref[...].astype(BF16)
        w2b[...] = w2_ref[...].astype(BF16)
        lax.fori_loop(eblk0_ref[e], eblk0_ref[e] + enb_ref[e], block, 0)

    @pl.when(e == pl.num_programs(0) - 1)
    def _():
        @pl.when(nblk >= 2)
        def _():
            wait_scatter(nblk - 2, nblk & 1)

        wait_scatter(nblk - 1, (nblk - 1) & 1)


def _moe(h2t, w1, w3, w2, eidx, eblk0, enb, bs0, bn, nblk, order):
    ne, d, ff = w1.shape
    kx = d // LANES
    ky = d // LANES
    a = order.shape[0]
    wmap = lambda e, eidx, *_: (eidx[e], 0, 0)
    grid_spec = pltpu.PrefetchScalarGridSpec(
        num_scalar_prefetch=7,
        grid=(ne,),
        in_specs=[
            pl.BlockSpec(memory_space=pl.ANY),
            pl.BlockSpec((None, d, ff), wmap),
            pl.BlockSpec((None, d, ff), wmap),
            pl.BlockSpec((None, ff, d), wmap),
        ],
        out_specs=pl.BlockSpec(memory_space=pl.ANY),
        scratch_shapes=[
            pltpu.VMEM((GATHER_AHEAD + 1, MOE_BLOCK * kx, LANES), F32),
            pltpu.VMEM((2, MOE_BLOCK * ky, LANES), F32),
            pltpu.VMEM((d, ff), BF16),
            pltpu.VMEM((d, ff), BF16),
            pltpu.VMEM((ff, d), BF16),
            pltpu.SemaphoreType.DMA((GATHER_AHEAD + 1,)),
            pltpu.SemaphoreType.DMA((2,)),
        ],
    )
    return pl.pallas_call(
        _moe_kernel,
        grid_spec=grid_spec,
        out_shape=jax.ShapeDtypeStruct((a * ky, LANES), F32),
        compiler_params=_cparams("arbitrary"),
        name="moe",
    )(eidx, eblk0, enb, bs0, bn, nblk, order, h2t, w1, w3, w2)


def _combine_kernel(x1_ref, y0_ref, y1_ref, wts_ref, gt_ref, o_ref):
    tm = x1_ref.shape[1]
    w = wts_ref[0]
    ff = w[:, 0:1] * _load_token_tiles(y0_ref.at[0], tm) + w[:, 1:2] * _load_token_tiles(y1_ref.at[0], tm)
    o_ref[0] = x1_ref[0] + gt_ref[0] * ff


def _combine(x1, y2, wts, gt2, tm):
    b, t, d = x1.shape
    ky = d // LANES
    row = lambda bi, i: (bi, i, 0)
    vec = lambda bi, i: (bi, 0, 0)
    return pl.pallas_call(
        _combine_kernel,
        grid=(b, t // tm),
        in_specs=[
            pl.BlockSpec((1, tm, d), row),
            pl.BlockSpec((None, 1, tm * ky, LANES), lambda bi, i: (0, bi, i, 0)),
            pl.BlockSpec((None, 1, tm * ky, LANES), lambda bi, i: (1, bi, i, 0)),
            pl.BlockSpec((1, tm, LANES), row),
            pl.BlockSpec((1, 1, d), vec),
        ],
        out_specs=pl.BlockSpec((1, tm, d), row),
        out_shape=jax.ShapeDtypeStruct((b, t, d), F32),
        compiler_params=_cparams("arbitrary", "arbitrary"),
        name="combine",
    )(x1, y2, y2, wts, gt2)


def _rope_tables(t):
    rows = t // GRID_W
    n_freq = HEAD_DIM // 4
    inv = ROPE_THETA ** (-jnp.arange(n_freq, dtype=F32) / n_freq)
    ar = jnp.arange(rows, dtype=F32)[:, None] * inv
    ac = jnp.arange(GRID_W, dtype=F32)[:, None] * inv

    def expand(fr, fc, sign):
        by_row = jnp.broadcast_to(fr[:, None, :], (rows, GRID_W, n_freq))
        by_col = jnp.broadcast_to(fc[None, :, :], (rows, GRID_W, n_freq))
        return jnp.concatenate([sign * by_row, by_row, sign * by_col, by_col], axis=2).reshape(t, HEAD_DIM)

    return expand(jnp.cos(ar), jnp.cos(ac), 1.0), expand(jnp.sin(ar), jnp.sin(ac), -1.0)


def _gate_weights(w2, bias):
    r = GLA_GATE_RANK
    wh = w2.reshape(2, r, GLA_HEADS, GLA_DK).transpose(2, 0, 1, 3)
    w = jnp.zeros((GLA_HEADS, 2 * r, 2 * GLA_DK), F32)
    w = w.at[:, 0:r, 0:GLA_DK].set(wh[:, 0]).at[:, r:2 * r, GLA_DK:].set(wh[:, 1])
    hi = w.astype(BF16)
    mid = (w - hi.astype(F32)).astype(BF16)
    pad = jnp.zeros((GLA_HEADS, LANES - 6 * r, 2 * GLA_DK), BF16)
    w2s = jnp.concatenate([hi, hi, mid, pad], axis=1)
    bias2 = bias.reshape(2, GLA_HEADS, GLA_DK).transpose(1, 0, 2).reshape(GLA_HEADS, 1, 2 * GLA_DK)
    return w2s, bias2


def _block_plan(eid_flat, nb):
    order = jnp.argsort(eid_flat).astype(jnp.int32)
    eids = jnp.arange(N_EXPERTS, dtype=jnp.int32)
    counts = jnp.sum((eid_flat[:, None] == eids[None, :]).astype(jnp.int32), axis=0)
    starts = jnp.cumsum(counts) - counts
    nblk_e = (counts + MOE_BLOCK - 1) // MOE_BLOCK
    bends = jnp.cumsum(nblk_e)
    bstarts = bends - nblk_e
    nblk = bends[-1]
    prev_used = lax.cummax(jnp.where(counts > 0, eids, -1))
    eidx = jnp.where(prev_used >= 0, prev_used, jnp.argmax(counts > 0)).astype(jnp.int32)
    bi = jnp.arange(nb, dtype=jnp.int32)
    bic = jnp.minimum(bi, jnp.maximum(nblk - 1, 0))
    bexp = jnp.minimum(jnp.sum(bends[None, :] <= bic[:, None], axis=1), N_EXPERTS - 1)
    of_block = (bexp[:, None] == eids[None, :]).astype(jnp.int32)

    def lookup(per_expert):
        return jnp.sum(of_block * per_expert[None, :], axis=1)

    r0 = (bic - lookup(bstarts)) * MOE_BLOCK
    bs0 = (lookup(starts) + r0).astype(jnp.int32)
    bn = jnp.where(bi < nblk, jnp.minimum(lookup(counts) - r0, MOE_BLOCK), 0).astype(jnp.int32)
    return (eidx, bstarts.astype(jnp.int32), nblk_e.astype(jnp.int32), bs0, bn,
            nblk.reshape(1).astype(jnp.int32), order)


def kernel(x, c, ctx, c_ctx, w_ada, b_ada, norm1_g, w_in, q_norm_g, k_norm_g, gla_gate_w2, gla_gate_b, gla_norm_g, w_out, norm2_g, router_grp_w, router_grp_b, router_exp_w, router_exp_b, moe_w1, moe_w3, moe_w2):
    b, t, d = x.shape
    tc = ctx.shape[1]
    depth = w_ada.shape[0]
    assert depth == 1, "single-layer stack: the context stream only feeds keys/values and GLA states"
    layer = 0

    c8 = jnp.zeros((8, d), F32).at[0:b].set(c).at[b].set(c_ctx)
    mod = _ada(c8, w_ada[layer], b_ada[layer])
    sh1, sc1, gt1, sh2, sc2, gt2 = [mod[0:b, i * d:(i + 1) * d].reshape(b, 1, d) for i in range(6)]
    sh1c, sc1c = [jnp.broadcast_to(mod[b, i * d:(i + 1) * d].reshape(1, 1, d), (b, 1, d)) for i in range(2)]

    w_main, w_tail = _w_in_parts(w_in[layer])

    cos_t, sin_t = _rope_tables(t)
    g1 = norm1_g[layer].reshape(1, d)
    qg = q_norm_g[layer].reshape(1, HEAD_DIM)
    kg = k_norm_g[layer].reshape(1, HEAD_DIM)
    q, k, v, gq, gk, gv, og, lr = _in_proj(x, g1, sc1, sh1, w_main, w_tail, qg, kg, cos_t, sin_t, 512)
    ones_t = jnp.ones((tc, HEAD_DIM), F32)
    _, kc, vc, gqc, gkc, gvc, _, lrc = _in_proj(ctx, g1, sc1c, sh1c, w_main, w_tail, qg, kg,
                                                ones_t, jnp.zeros_like(ones_t), tc)

    attn = _attn(q, jnp.concatenate([kc, k], axis=1), jnp.concatenate([vc, v], axis=1), 256)
    w2s, bias2 = _gate_weights(gla_gate_w2[layer], gla_gate_b[layer])
    gla = _gla(gq, gk, gv, lr, og, gqc, gkc, gvc, lrc, w2s, bias2, gla_norm_g[layer].reshape(1, GLA_DV))

    wr = jnp.concatenate([router_grp_w[layer], router_exp_w[layer],
                          jnp.zeros((d, LANES - N_GROUPS - N_EXPERTS), F32)], axis=1)
    br = jnp.concatenate([router_grp_b[layer], router_exp_b[layer],
                          jnp.zeros((LANES - N_GROUPS - N_EXPERTS,), F32)]).reshape(1, LANES)
    wr_hi = wr.astype(BF16)
    wr_parts = jnp.concatenate([wr_hi, (wr - wr_hi.astype(F32)).astype(BF16)], axis=1)
    x1, h2t, ids, wts = _out_proj(attn, gla, w_out[layer].astype(BF16), x, gt1, norm2_g[layer].reshape(1, d),
                                  sc2, sh2, wr_parts, br, 512)

    m = b * t
    n_assign = m * TOP_K
    nb = -(-(n_assign + N_EXPERTS * (MOE_BLOCK - 1)) // MOE_BLOCK)
    eid_flat = ids[:, :, 0:TOP_K].reshape(n_assign)
    plan = _block_plan(eid_flat, nb)
    y2 = _moe(h2t.reshape(-1, LANES), moe_w1[layer], moe_w3[layer], moe_w2[layer], *plan)
    return _combine(x1, y2.reshape(TOP_K, b, t * (d // LANES), LANES), wts, gt2, 256)
```

```python
import functools

import jax
import jax.numpy as jnp
from jax import lax
from jax.experimental import pallas as pl
from jax.experimental.pallas import tpu as pltpu

EPS = 1e-6
GRID_W = 64
ROPE_THETA = 10000.0

ATTN_HEADS = 8
ATTN_KV_HEADS = 2
HEAD_DIM = 128
GQA_GROUP = ATTN_HEADS // ATTN_KV_HEADS

GLA_HEADS = 4
GLA_DK = 128
GLA_DV = 256
GLA_GATE_RANK = 16
GLA_GATE_NORMALIZER = 16.0
GLA_CHUNK = 64
GLA_CHUNK_LOG2 = 6
GLA_PREP_TILE = 256
GLA_OUT_GROUP = 8

N_GROUPS = 8
EXPERTS_PER_GROUP = 8
N_EXPERTS = N_GROUPS * EXPERTS_PER_GROUP
TOP_K = 2
MOE_BLOCK = 256
ROW_GROUP_LOG2 = 3
ROW_GROUP = 1 << ROW_GROUP_LOG2
GATHER_AHEAD = 2

LOG2_E = 1.4426950408889634
LANES = 128
VMEM_LIMIT = 56 * 1024 * 1024

ATTN_Q_W = ATTN_HEADS * HEAD_DIM
ATTN_KV_W = ATTN_KV_HEADS * HEAD_DIM
GLA_K_W = GLA_HEADS * GLA_DK
GLA_V_W = GLA_HEADS * GLA_DV

BF16 = jnp.bfloat16
F32 = jnp.float32


def _cparams(*sem):
    return pltpu.CompilerParams(dimension_semantics=sem, vmem_limit_bytes=VMEM_LIMIT)


def _ada_kernel(c_ref, w_ref, b_ref, o_ref):
    c = c_ref[...]
    s = c * jax.nn.sigmoid(c)
    o_ref[...] = jnp.dot(s.astype(BF16), w_ref[...].astype(BF16), preferred_element_type=F32) + b_ref[...]


def _ada(c8, w, b):
    d, n = w.shape
    tn = 1024
    return pl.pallas_call(
        _ada_kernel,
        grid=(n // tn,),
        in_specs=[
            pl.BlockSpec((8, d), lambda j: (0, 0)),
            pl.BlockSpec((d, tn), lambda j: (0, j)),
            pl.BlockSpec((1, tn), lambda j: (0, j)),
        ],
        out_specs=pl.BlockSpec((8, tn), lambda j: (0, j)),
        out_shape=jax.ShapeDtypeStruct((8, n), F32),
        compiler_params=_cparams("arbitrary"),
        name="ada",
    )(c8, w, b.reshape(1, n))


def _w_main_kernel(wt_ref, o_ref):
    o_ref[...] = wt_ref[...].T.astype(BF16)


def _w_tail_kernel(wt_ref, o_ref):
    r = 2 * GLA_GATE_RANK
    o_ref[:, :GLA_V_W] = wt_ref[r:, :].T.astype(BF16)
    lowrank = jnp.concatenate([wt_ref[:r, :], jnp.zeros((LANES - r, wt_ref.shape[1]), F32)], axis=0)
    o_ref[:, GLA_V_W:] = lowrank.T.astype(BF16)


def _w_in_parts(wt):
    n, d = wt.shape
    n_main = n - GLA_V_W - 2 * GLA_GATE_RANK
    tn = 512
    main = pl.pallas_call(
        _w_main_kernel,
        grid=(n_main // tn,),
        in_specs=[pl.BlockSpec((tn, d), lambda j: (j, 0))],
        out_specs=pl.BlockSpec((d, tn), lambda j: (0, j)),
        out_shape=jax.ShapeDtypeStruct((d, n_main), BF16),
        compiler_params=_cparams("arbitrary"),
        name="w_in_main",
    )(wt)
    tail = pl.pallas_call(
        _w_tail_kernel,
        out_shape=jax.ShapeDtypeStruct((d, GLA_V_W + LANES), BF16),
        compiler_params=pltpu.CompilerParams(vmem_limit_bytes=VMEM_LIMIT),
        name="w_in_tail",
    )(wt[n_main:])
    return main, tail


def _swap32(y):
    lane = lax.broadcasted_iota(jnp.int32, y.shape, 1)
    return jnp.where((lane & 63) < 32, pltpu.roll(y, 96, 1), pltpu.roll(y, 32, 1))


def _head_norm_rope(a, g, cos, sin):
    y = a * lax.rsqrt(jnp.mean(a * a, axis=-1, keepdims=True) + EPS) * g
    return y * cos + _swap32(y) * sin


def _in_proj_kernel(x_ref, g1_ref, sc_ref, sh_ref, w_ref, wt_ref, qg_ref, kg_ref, cos_ref, sin_ref,
                    q_ref, k_ref, v_ref, gq_ref, gk_ref, gv_ref, og_ref, lr_ref):
    x = x_ref[0]
    xn = x * lax.rsqrt(jnp.mean(x * x, axis=-1, keepdims=True) + EPS)
    h = (xn * g1_ref[...] * (1.0 + sc_ref[0]) + sh_ref[0]).astype(BF16)
    cos = cos_ref[...]
    sin = sin_ref[...]

    def proj(off, width, w=w_ref):
        return jnp.dot(h, w[:, off:off + width], preferred_element_type=F32)

    qg = qg_ref[...] * (HEAD_DIM ** -0.5 * LOG2_E)
    off = 0
    for out_ref, gain, heads in ((q_ref, qg, ATTN_HEADS), (k_ref, kg_ref[...], ATTN_KV_HEADS)):
        for pair in range(heads // 2):
            a2 = proj(off, 2 * HEAD_DIM)
            for u in range(2):
                lo = (2 * pair + u) * HEAD_DIM
                a = a2[:, u * HEAD_DIM:(u + 1) * HEAD_DIM]
                out_ref[0, :, lo:lo + HEAD_DIM] = _head_norm_rope(a, gain, cos, sin).astype(BF16)
            off += 2 * HEAD_DIM
    v_ref[0] = proj(off, ATTN_KV_W).astype(BF16)
    off += ATTN_KV_W
    gq_ref[0] = proj(off, GLA_K_W)
    off += GLA_K_W
    gk_ref[0] = proj(off, GLA_K_W)
    off += GLA_K_W
    gv_ref[0] = proj(off, GLA_V_W).astype(BF16)
    og_ref[0] = proj(0, GLA_V_W, wt_ref)
    lr_ref[0] = proj(GLA_V_W, LANES, wt_ref)


def _in_proj(x, g1, sc, sh, w_main, w_tail, qg, kg, cos_t, sin_t, tm):
    b, t, d = x.shape
    row = lambda bi, i: (bi, i, 0)
    vec = lambda bi, i: (bi, 0, 0)
    const = lambda bi, i: (0, 0)
    tab = lambda bi, i: (i, 0)
    widths = [(ATTN_Q_W, BF16), (ATTN_KV_W, BF16), (ATTN_KV_W, BF16), (GLA_K_W, F32), (GLA_K_W, F32),
              (GLA_V_W, BF16), (GLA_V_W, F32), (LANES, F32)]
    return pl.pallas_call(
        _in_proj_kernel,
        grid=(b, t // tm),
        in_specs=[
            pl.BlockSpec((1, tm, d), row),
            pl.BlockSpec((1, d), const),
            pl.BlockSpec((1, 1, d), vec),
            pl.BlockSpec((1, 1, d), vec),
            pl.BlockSpec(w_main.shape, const, pipeline_mode=pl.Buffered(1)),
            pl.BlockSpec(w_tail.shape, const, pipeline_mode=pl.Buffered(1)),
            pl.BlockSpec((1, HEAD_DIM), const),
            pl.BlockSpec((1, HEAD_DIM), const),
            pl.BlockSpec((tm, HEAD_DIM), tab),
            pl.BlockSpec((tm, HEAD_DIM), tab),
        ],
        out_specs=[pl.BlockSpec((1, tm, wd), row) for wd, _ in widths],
        out_shape=[jax.ShapeDtypeStruct((b, t, wd), dt) for wd, dt in widths],
        compiler_params=_cparams("arbitrary", "arbitrary"),
        name="in_proj",
    )(x, g1, sc, sh, w_main, w_tail, qg, kg, cos_t, sin_t)


def _attn_kernel(q_ref, k_ref, v_ref, o_ref):
    k = k_ref[0]
    v = v_ref[0]

    def scores(g):
        q = q_ref[0, :, g * HEAD_DIM:(g + 1) * HEAD_DIM]
        return lax.dot_general(q, k, (((1,), (1,)), ((), ())), preferred_element_type=F32)

    s_next = scores(0)
    for g in range(GQA_GROUP):
        s = s_next
        if g + 1 < GQA_GROUP:
            s_next = scores(g + 1)
        p = jnp.exp2(s - jnp.max(s, axis=-1, keepdims=True))
        l = jnp.sum(p, axis=-1, keepdims=True)
        o = jnp.dot(p.astype(BF16), v, preferred_element_type=F32)
        o_ref[0, :, g * HEAD_DIM:(g + 1) * HEAD_DIM] = (o / l).astype(BF16)


def _attn(q, k_all, v_all, tq):
    b, t, _ = q.shape
    tk = k_all.shape[1]
    gw = GQA_GROUP * HEAD_DIM
    return pl.pallas_call(
        _attn_kernel,
        grid=(b, ATTN_KV_HEADS, t // tq),
        in_specs=[
            pl.BlockSpec((1, tq, gw), lambda bi, hk, i: (bi, i, hk)),
            pl.BlockSpec((1, tk, HEAD_DIM), lambda bi, hk, i: (bi, 0, hk)),
            pl.BlockSpec((1, tk, HEAD_DIM), lambda bi, hk, i: (bi, 0, hk)),
        ],
        out_specs=pl.BlockSpec((1, tq, gw), lambda bi, hk, i: (bi, i, hk)),
        out_shape=jax.ShapeDtypeStruct((b, t, ATTN_Q_W), BF16),
        compiler_params=_cparams("arbitrary", "arbitrary", "arbitrary"),
        name="attn",
    )(q, k_all, v_all)


def _log_sigmoid(x):
    return jnp.minimum(x, 0.0) - jnp.log(1.0 + jnp.exp(-jnp.abs(x)))


def _split3(x):
    hi = x.astype(BF16)
    r = x - hi.astype(F32)
    mid = r.astype(BF16)
    lo = (r - mid.astype(F32)).astype(BF16)
    return hi, mid, lo


def _dot_exact_lhs(m, x):
    return sum(jnp.dot(m, part, preferred_element_type=F32) for part in _split3(x))


def _prefix_operator(n):
    ri = lax.broadcasted_iota(jnp.int32, (n, n), 0)
    ci = lax.broadcasted_iota(jnp.int32, (n, n), 1)
    same = lax.shift_right_logical(ri, GLA_CHUNK_LOG2) == lax.shift_right_logical(ci, GLA_CHUNK_LOG2)
    return (same & (ci <= ri)).astype(BF16)


def _gate_prep(q, k, lr, w2s, bias2, prefix):
    c = GLA_CHUNK
    hi = lr.astype(BF16).astype(F32)
    mid = (lr - hi).astype(BF16).astype(F32)
    lhs = (hi + pltpu.roll(mid, 2 * GLA_GATE_RANK, 1) + pltpu.roll(hi, 4 * GLA_GATE_RANK, 1)).astype(BF16)
    logits = jnp.dot(lhs, w2s, preferred_element_type=F32) + bias2
    g = _log_sigmoid(logits) * (1.0 / GLA_GATE_NORMALIZER)
    pre = _dot_exact_lhs(prefix, g)
    tot = jnp.concatenate([jnp.broadcast_to(pre[lo + c - 1:lo + c, :], (c, pre.shape[1]))
                           for lo in range(0, pre.shape[0], c)], axis=0)
    dk = GLA_DK
    bcs = (pre[:, :dk], tot[:, dk:] - pre[:, dk:] + g[:, dk:])
    out = []
    for d, bc in enumerate(bcs):
        b_end = tot[:, d * dk:(d + 1) * dk]
        qe = (q * (GLA_DK ** -0.5) * jnp.exp(bc)).astype(BF16)
        ke = (k * jnp.exp(-bc)).astype(BF16)
        kend = (k * jnp.exp(b_end - bc)).astype(BF16)
        out.append((qe, ke, kend, b_end))
    return out


def _state_step(s_ref, v, kend, dec):
    upd = lax.dot_general(v, kend, (((0,), (0,)), ((), ())), preferred_element_type=F32)
    s_ref[...] = s_ref[...] * dec + upd


def _gla_kernel(q_ref, k_ref, v_ref, lr_ref, og_ref, qc_ref, kc_ref, vc_ref, lrc_ref,
                w2_ref, b_ref, ng_ref, o_ref,
                s_ref, qe_ref, ke_ref, kend_ref, dec_ref, kendc_ref, decc_ref, sbf_ref):
    c = GLA_CHUNK
    t = q_ref.shape[1]
    tc = qc_ref.shape[1]
    nc = t // c
    tile = GLA_PREP_TILE
    cpt = tile // c
    dirs = (0, 1)

    def store_dec(ref, d, base, b_end):
        for ch in range(b_end.shape[0] // c):
            ref[d, pl.ds(base + ch, 1), :] = jnp.exp(b_end[ch * c:ch * c + 1, :])

    prefix = _prefix_operator(tile)

    for i in range(tc // tile):
        rows = slice(i * tile, (i + 1) * tile)
        prep_c = _gate_prep(qc_ref[0, rows, :], kc_ref[0, rows, :], lrc_ref[0, rows, :],
                            w2_ref[...], b_ref[...], prefix)
        for d, (_, _, kend, b_end) in enumerate(prep_c):
            kendc_ref[d, rows, :] = kend
            store_dec(decc_ref, d, i * cpt, b_end)
    s_ref[...] = jnp.zeros_like(s_ref)
    for j in range(tc // c):
        for d in dirs:
            ch = j if d == 0 else tc // c - 1 - j
            _state_step(s_ref.at[d], vc_ref[0, ch * c:(ch + 1) * c, :], kendc_ref[d, ch * c:(ch + 1) * c, :],
                        decc_ref[d, ch:ch + 1, :])

    def prep(i, carry):
        lo = pl.multiple_of(i * tile, tile)
        rows = pl.ds(lo, tile)
        prep_l = _gate_prep(q_ref[0, rows, :], k_ref[0, rows, :], lr_ref[0, rows, :],
                            w2_ref[...], b_ref[...], prefix)
        for d, (qe, ke, kend, b_end) in enumerate(prep_l):
            qe_ref[d, rows, :] = qe
            ke_ref[d, rows, :] = ke
            kend_ref[d, rows, :] = kend
            store_dec(dec_ref, d, i * cpt, b_end)
        return carry

    lax.fori_loop(0, t // tile, prep, 0, unroll=2)

    def scan(j, carry):
        for d in dirs:
            ch = j if d == 0 else nc - 1 - j
            rows = pl.ds(pl.multiple_of(ch * c, c), c)
            sbf_ref[d, ch] = s_ref[d].astype(BF16)
            _state_step(s_ref.at[d], v_ref[0, rows, :], kend_ref[d, rows, :], dec_ref[d, pl.ds(ch, 1), :])
        return carry

    lax.fori_loop(0, nc, scan, 0, unroll=8)

    ri = lax.broadcasted_iota(jnp.int32, (c, c), 0)
    ci = lax.broadcasted_iota(jnp.int32, (c, c), 1)
    masks = (ci <= ri, ci >= ri)

    group = GLA_OUT_GROUP

    def out(i, carry):
        chunks = [i * group + u for u in range(group)]
        rows = [pl.ds(pl.multiple_of(ch * c, c), c) for ch in chunks]
        qes = [[qe_ref[d, r, :] for d in dirs] for r in rows]
        scores = [[lax.dot_general(qes[u][d], ke_ref[d, rows[u], :], (((1,), (1,)), ((), ())),
                                   preferred_element_type=F32) for d in dirs] for u in range(group)]
        inter = [[lax.dot_general(qes[u][d], sbf_ref[d, chunks[u]], (((1,), (1,)), ((), ())),
                                  preferred_element_type=F32) for d in dirs] for u in range(group)]
        for u in range(group):
            v = v_ref[0, rows[u], :]
            o = inter[u][0] + inter[u][1]
            for d in dirs:
                a = jnp.where(masks[d], scores[u][d], 0.0).astype(BF16)
                o = o + jnp.dot(a, v, preferred_element_type=F32)
            on = o * lax.rsqrt(jnp.mean(o * o, axis=-1, keepdims=True) + EPS) * ng_ref[...]
            og = og_ref[0, rows[u], :]
            o_ref[0, rows[u], :] = (on * (og * jax.nn.sigmoid(og))).astype(BF16)
        return carry

    lax.fori_loop(0, nc // group, out, 0)


def _gla(gq, gk, gv, lr, og, gqc, gkc, gvc, lrc, w2, bias, ng):
    b, t, _ = gq.shape
    tc = gqc.shape[1]
    hk = lambda bi, h: (bi, 0, h)
    h0 = lambda bi, h: (bi, 0, 0)
    return pl.pallas_call(
        _gla_kernel,
        grid=(b, GLA_HEADS),
        in_specs=[
            pl.BlockSpec((1, t, GLA_DK), hk),
            pl.BlockSpec((1, t, GLA_DK), hk),
            pl.BlockSpec((1, t, GLA_DV), hk),
            pl.BlockSpec((1, t, LANES), h0),
            pl.BlockSpec((1, t, GLA_DV), hk),
            pl.BlockSpec((1, tc, GLA_DK), hk),
            pl.BlockSpec((1, tc, GLA_DK), hk),
            pl.BlockSpec((1, tc, GLA_DV), hk),
            pl.BlockSpec((1, tc, LANES), h0),
            pl.BlockSpec((None, LANES, 2 * GLA_DK), lambda bi, h: (h, 0, 0)),
            pl.BlockSpec((None, 1, 2 * GLA_DK), lambda bi, h: (h, 0, 0)),
            pl.BlockSpec((1, GLA_DV), lambda bi, h: (0, 0)),
        ],
        out_specs=pl.BlockSpec((1, t, GLA_DV), hk),
        out_shape=jax.ShapeDtypeStruct((b, t, GLA_V_W), BF16),
        scratch_shapes=[
            pltpu.VMEM((2, GLA_DV, GLA_DK), F32),
            pltpu.VMEM((2, t, GLA_DK), BF16),
            pltpu.VMEM((2, t, GLA_DK), BF16),
            pltpu.VMEM((2, t, GLA_DK), BF16),
            pltpu.VMEM((2, t // GLA_CHUNK, GLA_DK), F32),
            pltpu.VMEM((2, tc, GLA_DK), BF16),
            pltpu.VMEM((2, tc // GLA_CHUNK, GLA_DK), F32),
            pltpu.VMEM((2, t // GLA_CHUNK, GLA_DV, GLA_DK), BF16),
        ],
        compiler_params=_cparams("arbitrary", "arbitrary"),
        name="gla",
    )(gq, gk, gv, lr, og, gqc, gkc, gvc, lrc, w2, bias, ng)


def _store_token_tiles(ref, x):
    n = x.shape[0]
    k = x.shape[1] // LANES
    for s in range(k):
        ref[pl.ds(s, n, stride=k), :] = x[:, s * LANES:(s + 1) * LANES]


def _load_token_tiles(ref, n):
    k = ref.shape[0] // n
    return jnp.concatenate([ref[pl.ds(s, n, stride=k), :] for s in range(k)], axis=1)


def _out_proj_kernel(attn_ref, gla_ref, wo_ref, x_ref, gt_ref, g2_ref, sc_ref, sh_ref, wr_ref, br_ref,
                     x1_ref, h2_ref, ids_ref, wts_ref):
    y = jnp.dot(attn_ref[0], wo_ref[0:ATTN_Q_W, :], preferred_element_type=F32)
    y = y + jnp.dot(gla_ref[0], wo_ref[ATTN_Q_W:, :], preferred_element_type=F32)
    x1 = x_ref[0] + gt_ref[0] * y
    x1_ref[0] = x1
    xn = x1 * lax.rsqrt(jnp.mean(x1 * x1, axis=-1, keepdims=True) + EPS)
    h2 = xn * g2_ref[...] * (1.0 + sc_ref[0]) + sh_ref[0]
    hi = h2.astype(BF16)
    hi_f = hi.astype(F32)

    _store_token_tiles(h2_ref.at[0], h2)

    mid = (h2 - hi_f).astype(BF16)
    both = jnp.dot(hi, wr_ref[...], preferred_element_type=F32)
    logits = (both[:, :LANES] + both[:, LANES:]
              + jnp.dot(mid, wr_ref[:, :LANES], preferred_element_type=F32)) + br_ref[...]
    lane = lax.broadcasted_iota(jnp.int32, logits.shape, 1)
    lane_f = lane.astype(F32)
    neg = jnp.float32(-jnp.inf)

    def first_argmax(vals):
        m = jnp.max(vals, axis=-1, keepdims=True)
        idx = jnp.min(jnp.where(vals == m, lane_f, float(LANES)), axis=-1, keepdims=True)
        return m, idx

    lg = jnp.where(lane < N_GROUPS, logits, neg)
    mg, grp = first_argmax(lg)
    pg_sel = 1.0 / jnp.sum(jnp.exp(lg - mg), axis=-1, keepdims=True)
    lo = N_GROUPS + grp * EXPERTS_PER_GROUP
    in_grp = (lane_f >= lo) & (lane_f < lo + EXPERTS_PER_GROUP)
    le = jnp.where(in_grp, logits, neg)
    v1, i1 = first_argmax(le)
    v2, i2 = first_argmax(jnp.where(lane_f == i1, neg, le))
    e2 = jnp.exp(v2 - v1)
    w1 = pg_sel / (1.0 + e2)
    w2 = pg_sel * e2 / (1.0 + e2)
    ids = jnp.where(lane == 0, i1 - N_GROUPS, jnp.where(lane == 1, i2 - N_GROUPS, 0.0))
    ids_ref[0] = ids.astype(jnp.int32)
    wts_ref[0] = jnp.where(lane == 0, w1, jnp.where(lane == 1, w2, 0.0))


def _out_proj(attn, gla, wo, x, gt1, g2, sc2, sh2, wr, br, tm):
    b, t, d = x.shape
    row = lambda bi, i: (bi, i, 0)
    vec = lambda bi, i: (bi, 0, 0)
    const = lambda bi, i: (0, 0)
    return pl.pallas_call(
        _out_proj_kernel,
        grid=(b, t // tm),
        in_specs=[
            pl.BlockSpec((1, tm, ATTN_Q_W), row),
            pl.BlockSpec((1, tm, GLA_V_W), row),
            pl.BlockSpec(wo.shape, const, pipeline_mode=pl.Buffered(1)),
            pl.BlockSpec((1, tm, d), row),
            pl.BlockSpec((1, 1, d), vec),
            pl.BlockSpec((1, d), const),
            pl.BlockSpec((1, 1, d), vec),
            pl.BlockSpec((1, 1, d), vec),
            pl.BlockSpec((d, 2 * LANES), const),
            pl.BlockSpec((1, LANES), const),
        ],
        out_specs=[
            pl.BlockSpec((1, tm, d), row),
            pl.BlockSpec((1, tm * (d // LANES), LANES), row),
            pl.BlockSpec((1, tm, LANES), row),
            pl.BlockSpec((1, tm, LANES), row),
        ],
        out_shape=[
            jax.ShapeDtypeStruct((b, t, d), F32),
            jax.ShapeDtypeStruct((b, t * (d // LANES), LANES), F32),
            jax.ShapeDtypeStruct((b, t, LANES), jnp.int32),
            jax.ShapeDtypeStruct((b, t, LANES), F32),
        ],
        compiler_params=_cparams("arbitrary", "arbitrary"),
        name="out_proj",
    )(attn, gla, wo, x, gt1, g2, sc2, sh2, wr, br)


def _moe_kernel(eidx_ref, eblk0_ref, enb_ref, bs0_ref, bn_ref, nblk_ref, order_ref,
                h2_hbm, w1_ref, w3_ref, w2_ref, y_hbm,
                xbuf, ybuf, w1b, w3b, w2b, gsem, ssem):
    del eidx_ref
    e = pl.program_id(0)
    nblk = nblk_ref[0]
    kx = xbuf.shape[1] // MOE_BLOCK
    ky = ybuf.shape[1] // MOE_BLOCK

    def gather_copy(sl, hbm_row, j, rows=1):
        return pltpu.make_async_copy(h2_hbm.at[pl.ds(pl.multiple_of(hbm_row * kx, kx), rows * kx)],
                                     xbuf.at[sl, pl.ds(pl.multiple_of(j * kx, kx), rows * kx)], gsem.at[sl])

    def for_rows(n, per_group, per_row):
        ng = lax.shift_right_logical(n, ROW_GROUP_LOG2)
        lax.fori_loop(0, ng, lambda g, c: (per_group(g * ROW_GROUP), c)[1], 0)
        lax.fori_loop(ng * ROW_GROUP, n, lambda j, c: (per_row(j), c)[1], 0)

    def start_gather(blk, sl):
        s = bs0_ref[blk]

        def start_row(j):
            gather_copy(sl, lax.shift_right_logical(order_ref[s + j], 1), j).start()

        def group(j0):
            for u in range(ROW_GROUP):
                start_row(j0 + u)

        for_rows(bn_ref[blk], group, start_row)

    def wait_gather(blk, sl):
        for_rows(bn_ref[blk], lambda j0: gather_copy(sl, 0, 0, ROW_GROUP).wait(),
                 lambda j: gather_copy(sl, 0, 0).wait())

    def output_copy(blk, sl):
        rows = MOE_BLOCK * ky
        return pltpu.make_async_copy(ybuf.at[sl], y_hbm.at[pl.ds(pl.multiple_of(blk * rows, rows), rows)],
                                     ssem.at[sl])

    @pl.when(e == 0)
    def _():
        xbuf[...] = jnp.zeros_like(xbuf)
        for g0 in range(GATHER_AHEAD):
            @pl.when(g0 < nblk)
            def _():
                start_gather(g0, g0)

    def block(g, carry):
        slot = g & 1
        xslot = lax.rem(g, GATHER_AHEAD + 1)

        @pl.when(g + GATHER_AHEAD < nblk)
        def _():
            start_gather(g + GATHER_AHEAD, lax.rem(g + GATHER_AHEAD, GATHER_AHEAD + 1))

        wait_gather(g, xslot)
        x = _load_token_tiles(xbuf.at[xslot], MOE_BLOCK).astype(BF16)
        a = jnp.dot(x, w1b[...], preferred_element_type=F32)
        gate = jnp.dot(x, w3b[...], preferred_element_type=F32)
        hid = (a * jax.nn.sigmoid(a) * gate).astype(BF16)
        y = jnp.dot(hid, w2b[...], preferred_element_type=F32)

        @pl.when(g >= 2)
        def _():
            output_copy(g - 2, slot).wait()

        _store_token_tiles(ybuf.at[slot], y)
        output_copy(g, slot).start()
        return carry

    @pl.when(enb_ref[e] > 0)
    def _():
        w1b[...] = w1_ref[...].astype(BF16)
        w3b[...] = w3_ref[...].astype(BF16)
        w2b[...] = w2_ref[...].astype(BF16)
        lax.fori_loop(eblk0_ref[e], eblk0_ref[e] + enb_ref[e], block, 0)

    @pl.when(e == pl.num_programs(0) - 1)
    def _():
        @pl.when(nblk >= 2)
        def _():
            output_copy(nblk - 2, nblk & 1).wait()

        output_copy(nblk - 1, (nblk - 1) & 1).wait()


def _moe(h2t, w1, w3, w2, eidx, eblk0, enb, bs0, bn, nblk, order):
    ne, d, ff = w1.shape
    kx = d // LANES
    ky = d // LANES
    nb = bs0.shape[0]
    wmap = lambda e, eidx, *_: (eidx[e], 0, 0)
    grid_spec = pltpu.PrefetchScalarGridSpec(
        num_scalar_prefetch=7,
        grid=(ne,),
        in_specs=[
            pl.BlockSpec(memory_space=pl.ANY),
            pl.BlockSpec((None, d, ff), wmap),
            pl.BlockSpec((None, d, ff), wmap),
            pl.BlockSpec((None, ff, d), wmap),
        ],
        out_specs=pl.BlockSpec(memory_space=pl.ANY),
        scratch_shapes=[
            pltpu.VMEM((GATHER_AHEAD + 1, MOE_BLOCK * kx, LANES), F32),
            pltpu.VMEM((2, MOE_BLOCK * ky, LANES), F32),
            pltpu.VMEM((d, ff), BF16),
            pltpu.VMEM((d, ff), BF16),
            pltpu.VMEM((ff, d), BF16),
            pltpu.SemaphoreType.DMA((GATHER_AHEAD + 1,)),
            pltpu.SemaphoreType.DMA((2,)),
        ],
    )
    return pl.pallas_call(
        _moe_kernel,
        grid_spec=grid_spec,
        out_shape=jax.ShapeDtypeStruct((nb * MOE_BLOCK * ky, LANES), F32),
        compiler_params=_cparams("arbitrary"),
        name="moe",
    )(eidx, eblk0, enb, bs0, bn, nblk, order, h2t, w1, w3, w2)


def _combine_kernel(pos_ref, x1_ref, ys_hbm, wts_ref, gt_ref, o_ref, ybuf, sem):
    tm = x1_ref.shape[1]
    ky = ybuf.shape[2] // tm
    nt = pl.num_programs(1)
    n = pl.program_id(0) * nt + pl.program_id(1)
    slot = n & 1

    def row_copy(sl, k, pos, j, rows=1):
        return pltpu.make_async_copy(ys_hbm.at[pl.ds(pl.multiple_of(pos * ky, ky), rows * ky)],
                                     ybuf.at[sl, k, pl.ds(pl.multiple_of(j * ky, ky), rows * ky)], sem.at[sl])

    def start_tile(tile, sl):
        base = tile * (tm * TOP_K)

        def group(g, carry):
            for u in range(ROW_GROUP):
                row_copy(sl, u % TOP_K, pos_ref[base + g * ROW_GROUP + u],
                         g * (ROW_GROUP // TOP_K) + u // TOP_K).start()
            return carry

        lax.fori_loop(0, tm * TOP_K // ROW_GROUP, group, 0)

    @pl.when(n == 0)
    def _():
        start_tile(0, 0)

    @pl.when(n + 1 < pl.num_programs(0) * nt)
    def _():
        start_tile(n + 1, 1 - slot)

    for k in range(TOP_K):
        row_copy(slot, k, 0, 0, rows=tm).wait()
    w = wts_ref[0]
    ff = (w[:, 0:1] * _load_token_tiles(ybuf.at[slot, 0], tm)
          + w[:, 1:2] * _load_token_tiles(ybuf.at[slot, 1], tm))
    o_ref[0] = x1_ref[0] + gt_ref[0] * ff


def _combine(x1, ys, pos, wts, gt2, tm):
    b, t, d = x1.shape
    ky = d // LANES
    row = lambda bi, i, pos: (bi, i, 0)
    vec = lambda bi, i, pos: (bi, 0, 0)
    grid_spec = pltpu.PrefetchScalarGridSpec(
        num_scalar_prefetch=1,
        grid=(b, t // tm),
        in_specs=[
            pl.BlockSpec((1, tm, d), row),
            pl.BlockSpec(memory_space=pl.ANY),
            pl.BlockSpec((1, tm, LANES), row),
            pl.BlockSpec((1, 1, d), vec),
        ],
        out_specs=pl.BlockSpec((1, tm, d), row),
        scratch_shapes=[
            pltpu.VMEM((2, TOP_K, tm * ky, LANES), F32),
            pltpu.SemaphoreType.DMA((2,)),
        ],
    )
    return pl.pallas_call(
        _combine_kernel,
        grid_spec=grid_spec,
        out_shape=jax.ShapeDtypeStruct((b, t, d), F32),
        compiler_params=_cparams("arbitrary", "arbitrary"),
        name="combine",
    )(pos, x1, ys, wts, gt2)


def _rope_tables(t):
    rows = t // GRID_W
    n_freq = HEAD_DIM // 4
    inv = ROPE_THETA ** (-jnp.arange(n_freq, dtype=F32) / n_freq)
    ar = jnp.arange(rows, dtype=F32)[:, None] * inv
    ac = jnp.arange(GRID_W, dtype=F32)[:, None] * inv

    def expand(fr, fc, sign):
        by_row = jnp.broadcast_to(fr[:, None, :], (rows, GRID_W, n_freq))
        by_col = jnp.broadcast_to(fc[None, :, :], (rows, GRID_W, n_freq))
        return jnp.concatenate([sign * by_row, by_row, sign * by_col, by_col], axis=2).reshape(t, HEAD_DIM)

    return expand(jnp.cos(ar), jnp.cos(ac), 1.0), expand(jnp.sin(ar), jnp.sin(ac), -1.0)


def _gate_weights(w2, bias):
    r = GLA_GATE_RANK
    wh = w2.reshape(2, r, GLA_HEADS, GLA_DK).transpose(2, 0, 1, 3)
    w = jnp.zeros((GLA_HEADS, 2 * r, 2 * GLA_DK), F32)
    w = w.at[:, 0:r, 0:GLA_DK].set(wh[:, 0]).at[:, r:2 * r, GLA_DK:].set(wh[:, 1])
    hi = w.astype(BF16)
    mid = (w - hi.astype(F32)).astype(BF16)
    pad = jnp.zeros((GLA_HEADS, LANES - 6 * r, 2 * GLA_DK), BF16)
    w2s = jnp.concatenate([hi, hi, mid, pad], axis=1)
    bias2 = bias.reshape(2, GLA_HEADS, GLA_DK).transpose(1, 0, 2).reshape(GLA_HEADS, 1, 2 * GLA_DK)
    return w2s, bias2


def _block_plan(eid_flat, nb):
    order = jnp.argsort(eid_flat).astype(jnp.int32)
    eids = jnp.arange(N_EXPERTS, dtype=jnp.int32)
    of_assign = (eid_flat[:, None] == eids[None, :]).astype(jnp.int32)
    counts = jnp.sum(of_assign, axis=0)
    starts = jnp.cumsum(counts) - counts
    nblk_e = (counts + MOE_BLOCK - 1) // MOE_BLOCK
    bends = jnp.cumsum(nblk_e)
    bstarts = bends - nblk_e
    nblk = bends[-1]
    prev_used = lax.cummax(jnp.where(counts > 0, eids, -1))
    eidx = jnp.where(prev_used >= 0, prev_used, jnp.argmax(counts > 0)).astype(jnp.int32)
    bi = jnp.arange(nb, dtype=jnp.int32)
    bic = jnp.minimum(bi, jnp.maximum(nblk - 1, 0))
    bexp = jnp.minimum(jnp.sum(bends[None, :] <= bic[:, None], axis=1), N_EXPERTS - 1)
    of_block = (bexp[:, None] == eids[None, :]).astype(jnp.int32)

    def lookup(per_expert):
        return jnp.sum(of_block * per_expert[None, :], axis=1)

    r0 = (bic - lookup(bstarts)) * MOE_BLOCK
    bs0 = (lookup(starts) + r0).astype(jnp.int32)
    bn = jnp.where(bi < nblk, jnp.minimum(lookup(counts) - r0, MOE_BLOCK), 0).astype(jnp.int32)
    rank0 = (bstarts * MOE_BLOCK - starts).astype(jnp.int32)
    pos = jnp.argsort(order).astype(jnp.int32) + jnp.sum(of_assign * rank0[None, :], axis=1)
    plan = (eidx, bstarts.astype(jnp.int32), nblk_e.astype(jnp.int32), bs0, bn,
            nblk.reshape(1).astype(jnp.int32), order)
    return plan, pos


def kernel(x, c, ctx, c_ctx, w_ada, b_ada, norm1_g, w_in, q_norm_g, k_norm_g, gla_gate_w2, gla_gate_b, gla_norm_g, w_out, norm2_g, router_grp_w, router_grp_b, router_exp_w, router_exp_b, moe_w1, moe_w3, moe_w2):
    b, t, d = x.shape
    tc = ctx.shape[1]
    depth = w_ada.shape[0]
    assert depth == 1, "single-layer stack: the context stream only feeds keys/values and GLA states"
    layer = 0

    c8 = jnp.zeros((8, d), F32).at[0:b].set(c).at[b].set(c_ctx)
    mod = _ada(c8, w_ada[layer], b_ada[layer])
    sh1, sc1, gt1, sh2, sc2, gt2 = [mod[0:b, i * d:(i + 1) * d].reshape(b, 1, d) for i in range(6)]
    sh1c, sc1c = [jnp.broadcast_to(mod[b, i * d:(i + 1) * d].reshape(1, 1, d), (b, 1, d)) for i in range(2)]

    w_main, w_tail = _w_in_parts(jnp.swapaxes(w_in[layer], 0, 1))

    cos_t, sin_t = _rope_tables(t)
    g1 = norm1_g[layer].reshape(1, d)
    qg = q_norm_g[layer].reshape(1, HEAD_DIM)
    kg = k_norm_g[layer].reshape(1, HEAD_DIM)
    q, k, v, gq, gk, gv, og, lr = _in_proj(x, g1, sc1, sh1, w_main, w_tail, qg, kg, cos_t, sin_t, 512)
    ones_t = jnp.ones((tc, HEAD_DIM), F32)
    _, kc, vc, gqc, gkc, gvc, _, lrc = _in_proj(ctx, g1, sc1c, sh1c, w_main, w_tail, qg, kg,
                                                ones_t, jnp.zeros_like(ones_t), tc)

    attn = _attn(q, jnp.concatenate([kc, k], axis=1), jnp.concatenate([vc, v], axis=1), 256)
    w2s, bias2 = _gate_weights(gla_gate_w2[layer], gla_gate_b[layer])
    gla = _gla(gq, gk, gv, lr, og, gqc, gkc, gvc, lrc, w2s, bias2, gla_norm_g[layer].reshape(1, GLA_DV))

    wr = jnp.concatenate([router_grp_w[layer], router_exp_w[layer],
                          jnp.zeros((d, LANES - N_GROUPS - N_EXPERTS), F32)], axis=1)
    br = jnp.concatenate([router_grp_b[layer], router_exp_b[layer],
                          jnp.zeros((LANES - N_GROUPS - N_EXPERTS,), F32)]).reshape(1, LANES)
    wr_hi = wr.astype(BF16)
    wr_parts = jnp.concatenate([wr_hi, (wr - wr_hi.astype(F32)).astype(BF16)], axis=1)
    x1, h2t, ids, wts = _out_proj(attn, gla, w_out[layer].astype(BF16), x, gt1, norm2_g[layer].reshape(1, d),
                                  sc2, sh2, wr_parts, br, 512)

    m = b * t
    n_assign = m * TOP_K
    nb = -(-(n_assign + N_EXPERTS * (MOE_BLOCK - 1)) // MOE_BLOCK)
    eid_flat = ids[:, :, 0:TOP_K].reshape(n_assign)
    plan, pos = _block_plan(eid_flat, nb)
    ys = _moe(h2t.reshape(-1, LANES), moe_w1[layer], moe_w3[layer], moe_w2[layer], *plan)
    return _combine(x1, ys, pos, wts, gt2, 256)
```

```python
import functools

import jax
import jax.numpy as jnp
from jax import lax
from jax.experimental import pallas as pl
from jax.experimental.pallas import tpu as pltpu

EPS = 1e-6
GRID_W = 64
ROPE_THETA = 10000.0

ATTN_HEADS = 8
ATTN_KV_HEADS = 2
HEAD_DIM = 128
GQA_GROUP = ATTN_HEADS // ATTN_KV_HEADS

GLA_HEADS = 4
GLA_DK = 128
GLA_DV = 256
GLA_GATE_RANK = 16
GLA_GATE_NORMALIZER = 16.0
GLA_CHUNK = 64
GLA_CHUNK_LOG2 = 6
GLA_PREP_TILE = 256
GLA_OUT_GROUP = 8

N_GROUPS = 8
EXPERTS_PER_GROUP = 8
N_EXPERTS = N_GROUPS * EXPERTS_PER_GROUP
TOP_K = 2
MOE_BLOCK = 256
ROW_GROUP_LOG2 = 3
ROW_GROUP = 1 << ROW_GROUP_LOG2
GATHER_AHEAD = 2

LOG2_E = 1.4426950408889634
LANES = 128
VMEM_LIMIT = 56 * 1024 * 1024

ATTN_Q_W = ATTN_HEADS * HEAD_DIM
ATTN_KV_W = ATTN_KV_HEADS * HEAD_DIM
GLA_K_W = GLA_HEADS * GLA_DK
GLA_V_W = GLA_HEADS * GLA_DV

BF16 = jnp.bfloat16
F32 = jnp.float32


def _cparams(*sem):
    return pltpu.CompilerParams(dimension_semantics=sem, vmem_limit_bytes=VMEM_LIMIT)


def _ada_kernel(c_ref, w_ref, b_ref, o_ref):
    c = c_ref[...]
    s = c * jax.nn.sigmoid(c)
    o_ref[...] = jnp.dot(s.astype(BF16), w_ref[...].astype(BF16), preferred_element_type=F32) + b_ref[...]


def _ada(c8, w, b):
    d, n = w.shape
    tn = 1024
    return pl.pallas_call(
        _ada_kernel,
        grid=(n // tn,),
        in_specs=[
            pl.BlockSpec((8, d), lambda j: (0, 0)),
            pl.BlockSpec((d, tn), lambda j: (0, j)),
            pl.BlockSpec((1, tn), lambda j: (0, j)),
        ],
        out_specs=pl.BlockSpec((8, tn), lambda j: (0, j)),
        out_shape=jax.ShapeDtypeStruct((8, n), F32),
        compiler_params=_cparams("arbitrary"),
        name="ada",
    )(c8, w, b.reshape(1, n))


def _w_main_kernel(wt_ref, o_ref):
    o_ref[...] = wt_ref[...].T.astype(BF16)


def _w_tail_kernel(wt_ref, o_ref):
    r = 2 * GLA_GATE_RANK
    o_ref[:, :GLA_V_W] = wt_ref[r:, :].T.astype(BF16)
    lowrank = jnp.concatenate([wt_ref[:r, :], jnp.zeros((LANES - r, wt_ref.shape[1]), F32)], axis=0)
    o_ref[:, GLA_V_W:] = lowrank.T.astype(BF16)


def _w_in_parts(wt):
    n, d = wt.shape
    n_main = n - GLA_V_W - 2 * GLA_GATE_RANK
    tn = 512
    main = pl.pallas_call(
        _w_main_kernel,
        grid=(n_main // tn,),
        in_specs=[pl.BlockSpec((tn, d), lambda j: (j, 0))],
        out_specs=pl.BlockSpec((d, tn), lambda j: (0, j)),
        out_shape=jax.ShapeDtypeStruct((d, n_main), BF16),
        compiler_params=_cparams("arbitrary"),
        name="w_in_main",
    )(wt)
    tail = pl.pallas_call(
        _w_tail_kernel,
        out_shape=jax.ShapeDtypeStruct((d, GLA_V_W + LANES), BF16),
        compiler_params=pltpu.CompilerParams(vmem_limit_bytes=VMEM_LIMIT),
        name="w_in_tail",
    )(wt[n_main:])
    return main, tail


def _swap32(y):
    lane = lax.broadcasted_iota(jnp.int32, y.shape, 1)
    return jnp.where((lane & 63) < 32, pltpu.roll(y, 96, 1), pltpu.roll(y, 32, 1))


def _head_norm_rope(a, g, cos, sin):
    y = a * lax.rsqrt(jnp.mean(a * a, axis=-1, keepdims=True) + EPS) * g
    return y * cos + _swap32(y) * sin


def _in_proj_kernel(x_ref, g1_ref, sc_ref, sh_ref, w_ref, wt_ref, qg_ref, kg_ref, cos_ref, sin_ref,
                    q_ref, k_ref, v_ref, gq_ref, gk_ref, gv_ref, og_ref, lr_ref):
    x = x_ref[0]
    xn = x * lax.rsqrt(jnp.mean(x * x, axis=-1, keepdims=True) + EPS)
    h = (xn * g1_ref[...] * (1.0 + sc_ref[0]) + sh_ref[0]).astype(BF16)
    cos = cos_ref[...]
    sin = sin_ref[...]

    def proj(off, width, w=w_ref):
        return jnp.dot(h, w[:, off:off + width], preferred_element_type=F32)

    qg = qg_ref[...] * (HEAD_DIM ** -0.5 * LOG2_E)
    off = 0
    for out_ref, gain, heads in ((q_ref, qg, ATTN_HEADS), (k_ref, kg_ref[...], ATTN_KV_HEADS)):
        for pair in range(heads // 2):
            a2 = proj(off, 2 * HEAD_DIM)
            for u in range(2):
                lo = (2 * pair + u) * HEAD_DIM
                a = a2[:, u * HEAD_DIM:(u + 1) * HEAD_DIM]
                out_ref[0, :, lo:lo + HEAD_DIM] = _head_norm_rope(a, gain, cos, sin).astype(BF16)
            off += 2 * HEAD_DIM
    v_ref[0] = proj(off, ATTN_KV_W).astype(BF16)
    off += ATTN_KV_W
    gq_ref[0] = proj(off, GLA_K_W)
    off += GLA_K_W
    gk_ref[0] = proj(off, GLA_K_W)
    off += GLA_K_W
    gv_ref[0] = proj(off, GLA_V_W).astype(BF16)
    og_ref[0] = proj(0, GLA_V_W, wt_ref)
    lr_ref[0] = proj(GLA_V_W, LANES, wt_ref)


def _in_proj(x, g1, sc, sh, w_main, w_tail, qg, kg, cos_t, sin_t, tm):
    b, t, d = x.shape
    row = lambda bi, i: (bi, i, 0)
    vec = lambda bi, i: (bi, 0, 0)
    const = lambda bi, i: (0, 0)
    tab = lambda bi, i: (i, 0)
    widths = [(ATTN_Q_W, BF16), (ATTN_KV_W, BF16), (ATTN_KV_W, BF16), (GLA_K_W, F32), (GLA_K_W, F32),
              (GLA_V_W, BF16), (GLA_V_W, F32), (LANES, F32)]
    return pl.pallas_call(
        _in_proj_kernel,
        grid=(b, t // tm),
        in_specs=[
            pl.BlockSpec((1, tm, d), row),
            pl.BlockSpec((1, d), const),
            pl.BlockSpec((1, 1, d), vec),
            pl.BlockSpec((1, 1, d), vec),
            pl.BlockSpec(w_main.shape, const, pipeline_mode=pl.Buffered(1)),
            pl.BlockSpec(w_tail.shape, const, pipeline_mode=pl.Buffered(1)),
            pl.BlockSpec((1, HEAD_DIM), const),
            pl.BlockSpec((1, HEAD_DIM), const),
            pl.BlockSpec((tm, HEAD_DIM), tab),
            pl.BlockSpec((tm, HEAD_DIM), tab),
        ],
        out_specs=[pl.BlockSpec((1, tm, wd), row) for wd, _ in widths],
        out_shape=[jax.ShapeDtypeStruct((b, t, wd), dt) for wd, dt in widths],
        compiler_params=_cparams("arbitrary", "arbitrary"),
        name="in_proj",
    )(x, g1, sc, sh, w_main, w_tail, qg, kg, cos_t, sin_t)


def _attn_kernel(q_ref, k_ref, v_ref, o_ref):
    k = k_ref[0]
    v = v_ref[0]

    def scores(g):
        q = q_ref[0, :, g * HEAD_DIM:(g + 1) * HEAD_DIM]
        return lax.dot_general(q, k, (((1,), (1,)), ((), ())), preferred_element_type=F32)

    s_next = scores(0)
    for g in range(GQA_GROUP):
        s = s_next
        if g + 1 < GQA_GROUP:
            s_next = scores(g + 1)
        p = jnp.exp2(s - jnp.max(s, axis=-1, keepdims=True))
        l = jnp.sum(p, axis=-1, keepdims=True)
        o = jnp.dot(p.astype(BF16), v, preferred_element_type=F32)
        o_ref[0, :, g * HEAD_DIM:(g + 1) * HEAD_DIM] = (o / l).astype(BF16)


def _attn(q, k_all, v_all, tq):
    b, t, _ = q.shape
    tk = k_all.shape[1]
    gw = GQA_GROUP * HEAD_DIM
    return pl.pallas_call(
        _attn_kernel,
        grid=(b, ATTN_KV_HEADS, t // tq),
        in_specs=[
            pl.BlockSpec((1, tq, gw), lambda bi, hk, i: (bi, i, hk)),
            pl.BlockSpec((1, tk, HEAD_DIM), lambda bi, hk, i: (bi, 0, hk)),
            pl.BlockSpec((1, tk, HEAD_DIM), lambda bi, hk, i: (bi, 0, hk)),
        ],
        out_specs=pl.BlockSpec((1, tq, gw), lambda bi, hk, i: (bi, i, hk)),
        out_shape=jax.ShapeDtypeStruct((b, t, ATTN_Q_W), BF16),
        compiler_params=_cparams("arbitrary", "arbitrary", "arbitrary"),
        name="attn",
    )(q, k_all, v_all)


def _log_sigmoid(x):
    return jnp.minimum(x, 0.0) - jnp.log(1.0 + jnp.exp(-jnp.abs(x)))


def _split3(x):
    hi = x.astype(BF16)
    r = x - hi.astype(F32)
    mid = r.astype(BF16)
    lo = (r - mid.astype(F32)).astype(BF16)
    return hi, mid, lo


def _dot_exact_lhs(m, x):
    return sum(jnp.dot(m, part, preferred_element_type=F32) for part in _split3(x))


def _prefix_operator(n):
    ri = lax.broadcasted_iota(jnp.int32, (n, n), 0)
    ci = lax.broadcasted_iota(jnp.int32, (n, n), 1)
    same = lax.shift_right_logical(ri, GLA_CHUNK_LOG2) == lax.shift_right_logical(ci, GLA_CHUNK_LOG2)
    return (same & (ci <= ri)).astype(BF16)


def _gate_prep(q, k, lr, w2s, bias2, prefix):
    c = GLA_CHUNK
    hi = lr.astype(BF16).astype(F32)
    mid = (lr - hi).astype(BF16).astype(F32)
    lhs = (hi + pltpu.roll(mid, 2 * GLA_GATE_RANK, 1) + pltpu.roll(hi, 4 * GLA_GATE_RANK, 1)).astype(BF16)
    logits = jnp.dot(lhs, w2s, preferred_element_type=F32) + bias2
    g = _log_sigmoid(logits) * (1.0 / GLA_GATE_NORMALIZER)
    pre = _dot_exact_lhs(prefix, g)
    tot = jnp.concatenate([jnp.broadcast_to(pre[lo + c - 1:lo + c, :], (c, pre.shape[1]))
                           for lo in range(0, pre.shape[0], c)], axis=0)
    dk = GLA_DK
    bcs = (pre[:, :dk], tot[:, dk:] - pre[:, dk:] + g[:, dk:])
    out = []
    for d, bc in enumerate(bcs):
        b_end = tot[:, d * dk:(d + 1) * dk]
        qe = (q * (GLA_DK ** -0.5) * jnp.exp(bc)).astype(BF16)
        ke = (k * jnp.exp(-bc)).astype(BF16)
        kend = (k * jnp.exp(b_end - bc)).astype(BF16)
        out.append((qe, ke, kend, b_end))
    return out


def _state_step(s_ref, v, kend, dec):
    upd = lax.dot_general(v, kend, (((0,), (0,)), ((), ())), preferred_element_type=F32)
    s_ref[...] = s_ref[...] * dec + upd


def _gla_kernel(q_ref, k_ref, v_ref, lr_ref, og_ref, qc_ref, kc_ref, vc_ref, lrc_ref,
                w2_ref, b_ref, ng_ref, o_ref,
                s_ref, qe_ref, ke_ref, kend_ref, dec_ref, kendc_ref, decc_ref, sbf_ref):
    c = GLA_CHUNK
    t = q_ref.shape[1]
    tc = qc_ref.shape[1]
    nc = t // c
    tile = GLA_PREP_TILE
    cpt = tile // c
    dirs = (0, 1)

    def store_dec(ref, d, base, b_end):
        for ch in range(b_end.shape[0] // c):
            ref[d, pl.ds(base + ch, 1), :] = jnp.exp(b_end[ch * c:ch * c + 1, :])

    prefix = _prefix_operator(tile)

    for i in range(tc // tile):
        rows = slice(i * tile, (i + 1) * tile)
        prep_c = _gate_prep(qc_ref[0, rows, :], kc_ref[0, rows, :], lrc_ref[0, rows, :],
                            w2_ref[...], b_ref[...], prefix)
        for d, (_, _, kend, b_end) in enumerate(prep_c):
            kendc_ref[d, rows, :] = kend
            store_dec(decc_ref, d, i * cpt, b_end)
    s_ref[...] = jnp.zeros_like(s_ref)
    for j in range(tc // c):
        for d in dirs:
            ch = j if d == 0 else tc // c - 1 - j
            _state_step(s_ref.at[d], vc_ref[0, ch * c:(ch + 1) * c, :], kendc_ref[d, ch * c:(ch + 1) * c, :],
                        decc_ref[d, ch:ch + 1, :])

    def prep(i, carry):
        lo = pl.multiple_of(i * tile, tile)
        rows = pl.ds(lo, tile)
        prep_l = _gate_prep(q_ref[0, rows, :], k_ref[0, rows, :], lr_ref[0, rows, :],
                            w2_ref[...], b_ref[...], prefix)
        for d, (qe, ke, kend, b_end) in enumerate(prep_l):
            qe_ref[d, rows, :] = qe
            ke_ref[d, rows, :] = ke
            kend_ref[d, rows, :] = kend
            store_dec(dec_ref, d, i * cpt, b_end)
        return carry

    lax.fori_loop(0, t // tile, prep, 0, unroll=2)

    def scan(j, carry):
        for d in dirs:
            ch = j if d == 0 else nc - 1 - j
            rows = pl.ds(pl.multiple_of(ch * c, c), c)
            sbf_ref[d, ch] = s_ref[d].astype(BF16)
            _state_step(s_ref.at[d], v_ref[0, rows, :], kend_ref[d, rows, :], dec_ref[d, pl.ds(ch, 1), :])
        return carry

    lax.fori_loop(0, nc, scan, 0, unroll=8)

    ri = lax.broadcasted_iota(jnp.int32, (c, c), 0)
    ci = lax.broadcasted_iota(jnp.int32, (c, c), 1)
    masks = (ci <= ri, ci >= ri)

    group = GLA_OUT_GROUP

    def out(i, carry):
        chunks = [i * group + u for u in range(group)]
        rows = [pl.ds(pl.multiple_of(ch * c, c), c) for ch in chunks]
        qes = [[qe_ref[d, r, :] for d in dirs] for r in rows]
        scores = [[lax.dot_general(qes[u][d], ke_ref[d, rows[u], :], (((1,), (1,)), ((), ())),
                                   preferred_element_type=F32) for d in dirs] for u in range(group)]
        inter = [[lax.dot_general(qes[u][d], sbf_ref[d, chunks[u]], (((1,), (1,)), ((), ())),
                                  preferred_element_type=F32) for d in dirs] for u in range(group)]
        for u in range(group):
            v = v_ref[0, rows[u], :]
            o = inter[u][0] + inter[u][1]
            for d in dirs:
                a = jnp.where(masks[d], scores[u][d], 0.0).astype(BF16)
                o = o + jnp.dot(a, v, preferred_element_type=F32)
            on = o * lax.rsqrt(jnp.mean(o * o, axis=-1, keepdims=True) + EPS) * ng_ref[...]
            og = og_ref[0, rows[u], :]
            o_ref[0, rows[u], :] = (on * (og * jax.nn.sigmoid(og))).astype(BF16)
        return carry

    lax.fori_loop(0, nc // group, out, 0)


def _gla(gq, gk, gv, lr, og, gqc, gkc, gvc, lrc, w2, bias, ng):
    b, t, _ = gq.shape
    tc = gqc.shape[1]
    hk = lambda bi, h: (bi, 0, h)
    h0 = lambda bi, h: (bi, 0, 0)
    return pl.pallas_call(
        _gla_kernel,
        grid=(b, GLA_HEADS),
        in_specs=[
            pl.BlockSpec((1, t, GLA_DK), hk),
            pl.BlockSpec((1, t, GLA_DK), hk),
            pl.BlockSpec((1, t, GLA_DV), hk),
            pl.BlockSpec((1, t, LANES), h0),
            pl.BlockSpec((1, t, GLA_DV), hk),
            pl.BlockSpec((1, tc, GLA_DK), hk),
            pl.BlockSpec((1, tc, GLA_DK), hk),
            pl.BlockSpec((1, tc, GLA_DV), hk),
            pl.BlockSpec((1, tc, LANES), h0),
            pl.BlockSpec((None, LANES, 2 * GLA_DK), lambda bi, h: (h, 0, 0)),
            pl.BlockSpec((None, 1, 2 * GLA_DK), lambda bi, h: (h, 0, 0)),
            pl.BlockSpec((1, GLA_DV), lambda bi, h: (0, 0)),
        ],
        out_specs=pl.BlockSpec((1, t, GLA_DV), hk),
        out_shape=jax.ShapeDtypeStruct((b, t, GLA_V_W), BF16),
        scratch_shapes=[
            pltpu.VMEM((2, GLA_DV, GLA_DK), F32),
            pltpu.VMEM((2, t, GLA_DK), BF16),
            pltpu.VMEM((2, t, GLA_DK), BF16),
            pltpu.VMEM((2, t, GLA_DK), BF16),
            pltpu.VMEM((2, t // GLA_CHUNK, GLA_DK), F32),
            pltpu.VMEM((2, tc, GLA_DK), BF16),
            pltpu.VMEM((2, tc // GLA_CHUNK, GLA_DK), F32),
            pltpu.VMEM((2, t // GLA_CHUNK, GLA_DV, GLA_DK), BF16),
        ],
        compiler_params=_cparams("arbitrary", "arbitrary"),
        name="gla",
    )(gq, gk, gv, lr, og, gqc, gkc, gvc, lrc, w2, bias, ng)


def _token_pitch(d):
    rows = d // LANES
    return rows + 4 if rows % 8 == 0 else rows


def _store_token_tiles(ref, x):
    n = x.shape[0]
    k = x.shape[1] // LANES
    pitch = ref.shape[0] // n
    for s in range(pitch):
        slab = x[:, s * LANES:(s + 1) * LANES] if s < k else jnp.zeros((n, LANES), x.dtype)
        ref[pl.ds(s, n, stride=pitch), :] = slab


def _load_token_tiles(ref, n, d):
    pitch = ref.shape[0] // n
    return jnp.concatenate([ref[pl.ds(s, n, stride=pitch), :] for s in range(d // LANES)], axis=1)


def _out_proj_kernel(attn_ref, gla_ref, wo_ref, x_ref, gt_ref, g2_ref, sc_ref, sh_ref, wr_ref, br_ref,
                     x1_ref, h2_ref, ids_ref, wts_ref):
    y = jnp.dot(attn_ref[0], wo_ref[0:ATTN_Q_W, :], preferred_element_type=F32)
    y = y + jnp.dot(gla_ref[0], wo_ref[ATTN_Q_W:, :], preferred_element_type=F32)
    x1 = x_ref[0] + gt_ref[0] * y
    x1_ref[0] = x1
    xn = x1 * lax.rsqrt(jnp.mean(x1 * x1, axis=-1, keepdims=True) + EPS)
    h2 = xn * g2_ref[...] * (1.0 + sc_ref[0]) + sh_ref[0]
    hi = h2.astype(BF16)
    hi_f = hi.astype(F32)

    _store_token_tiles(h2_ref.at[0], h2)

    mid = (h2 - hi_f).astype(BF16)
    both = jnp.dot(hi, wr_ref[...], preferred_element_type=F32)
    logits = (both[:, :LANES] + both[:, LANES:]
              + jnp.dot(mid, wr_ref[:, :LANES], preferred_element_type=F32)) + br_ref[...]
    lane = lax.broadcasted_iota(jnp.int32, logits.shape, 1)
    lane_f = lane.astype(F32)
    neg = jnp.float32(-jnp.inf)

    def first_argmax(vals):
        m = jnp.max(vals, axis=-1, keepdims=True)
        idx = jnp.min(jnp.where(vals == m, lane_f, float(LANES)), axis=-1, keepdims=True)
        return m, idx

    lg = jnp.where(lane < N_GROUPS, logits, neg)
    mg, grp = first_argmax(lg)
    pg_sel = 1.0 / jnp.sum(jnp.exp(lg - mg), axis=-1, keepdims=True)
    lo = N_GROUPS + grp * EXPERTS_PER_GROUP
    in_grp = (lane_f >= lo) & (lane_f < lo + EXPERTS_PER_GROUP)
    le = jnp.where(in_grp, logits, neg)
    v1, i1 = first_argmax(le)
    v2, i2 = first_argmax(jnp.where(lane_f == i1, neg, le))
    e2 = jnp.exp(v2 - v1)
    w1 = pg_sel / (1.0 + e2)
    w2 = pg_sel * e2 / (1.0 + e2)
    ids = jnp.where(lane == 0, i1 - N_GROUPS, jnp.where(lane == 1, i2 - N_GROUPS, 0.0))
    ids_ref[0] = ids.astype(jnp.int32)
    wts_ref[0] = jnp.where(lane == 0, w1, jnp.where(lane == 1, w2, 0.0))


def _out_proj(attn, gla, wo, x, gt1, g2, sc2, sh2, wr, br, tm):
    b, t, d = x.shape
    row = lambda bi, i: (bi, i, 0)
    vec = lambda bi, i: (bi, 0, 0)
    const = lambda bi, i: (0, 0)
    return pl.pallas_call(
        _out_proj_kernel,
        grid=(b, t // tm),
        in_specs=[
            pl.BlockSpec((1, tm, ATTN_Q_W), row),
            pl.BlockSpec((1, tm, GLA_V_W), row),
            pl.BlockSpec(wo.shape, const, pipeline_mode=pl.Buffered(1)),
            pl.BlockSpec((1, tm, d), row),
            pl.BlockSpec((1, 1, d), vec),
            pl.BlockSpec((1, d), const),
            pl.BlockSpec((1, 1, d), vec),
            pl.BlockSpec((1, 1, d), vec),
            pl.BlockSpec((d, 2 * LANES), const),
            pl.BlockSpec((1, LANES), const),
        ],
        out_specs=[
            pl.BlockSpec((1, tm, d), row),
            pl.BlockSpec((1, tm * _token_pitch(d), LANES), row),
            pl.BlockSpec((1, tm, LANES), row),
            pl.BlockSpec((1, tm, LANES), row),
        ],
        out_shape=[
            jax.ShapeDtypeStruct((b, t, d), F32),
            jax.ShapeDtypeStruct((b, t * _token_pitch(d), LANES), F32),
            jax.ShapeDtypeStruct((b, t, LANES), jnp.int32),
            jax.ShapeDtypeStruct((b, t, LANES), F32),
        ],
        compiler_params=_cparams("arbitrary", "arbitrary"),
        name="out_proj",
    )(attn, gla, wo, x, gt1, g2, sc2, sh2, wr, br)


def _moe_kernel(eidx_ref, eblk0_ref, enb_ref, bs0_ref, bn_ref, nblk_ref, order_ref,
                h2_hbm, w1_ref, w3_ref, w2_ref, y_hbm,
                xbuf, ybuf, w1b, w3b, w2b, gsem, ssem):
    del eidx_ref
    e = pl.program_id(0)
    nblk = nblk_ref[0]
    kx = xbuf.shape[1] // MOE_BLOCK
    ky = ybuf.shape[1] // MOE_BLOCK

    def gather_copy(sl, hbm_row, j, rows=1):
        return pltpu.make_async_copy(h2_hbm.at[pl.ds(hbm_row * kx, rows * kx)],
                                     xbuf.at[sl, pl.ds(j * kx, rows * kx)], gsem.at[sl])

    def for_rows(n, per_group, per_row):
        ng = lax.shift_right_logical(n, ROW_GROUP_LOG2)
        lax.fori_loop(0, ng, lambda g, c: (per_group(g * ROW_GROUP), c)[1], 0)
        lax.fori_loop(ng * ROW_GROUP, n, lambda j, c: (per_row(j), c)[1], 0)

    def start_gather(blk, sl):
        s = bs0_ref[blk]

        def start_row(j):
            gather_copy(sl, lax.shift_right_logical(order_ref[s + j], 1), j).start()

        def group(j0):
            for u in range(ROW_GROUP):
                start_row(j0 + u)

        for_rows(bn_ref[blk], group, start_row)

    def wait_gather(blk, sl):
        for_rows(bn_ref[blk], lambda j0: gather_copy(sl, 0, 0, ROW_GROUP).wait(),
                 lambda j: gather_copy(sl, 0, 0).wait())

    def output_copy(blk, sl):
        return pltpu.make_async_copy(
            ybuf.at[sl], y_hbm.at[pl.ds(bs0_ref[blk] * ky, MOE_BLOCK * ky)], ssem.at[sl])

    @pl.when(e == 0)
    def _():
        xbuf[...] = jnp.zeros_like(xbuf)
        for g0 in range(GATHER_AHEAD):
            @pl.when(g0 < nblk)
            def _():
                start_gather(g0, g0)

    def block(g, carry):
        slot = g & 1
        xslot = lax.rem(g, GATHER_AHEAD + 1)

        @pl.when(g + GATHER_AHEAD < nblk)
        def _():
            start_gather(g + GATHER_AHEAD, lax.rem(g + GATHER_AHEAD, GATHER_AHEAD + 1))

        wait_gather(g, xslot)
        x = _load_token_tiles(xbuf.at[xslot], MOE_BLOCK, w1b.shape[0]).astype(BF16)
        a = jnp.dot(x, w1b[...], preferred_element_type=F32)
        gate = jnp.dot(x, w3b[...], preferred_element_type=F32)
        hid = (a * jax.nn.sigmoid(a) * gate).astype(BF16)
        y = jnp.dot(hid, w2b[...], preferred_element_type=F32)

        _store_token_tiles(ybuf.at[slot], y)

        @pl.when(g >= 1)
        def _():
            output_copy(g - 1, 1 - slot).wait()

        output_copy(g, slot).start()
        return carry

    @pl.when(enb_ref[e] > 0)
    def _():
        w1b[...] = w1_ref[...].astype(BF16)
        w3b[...] = w3_ref[...].astype(BF16)
        w2b[...] = w2_ref[...].astype(BF16)
        lax.fori_loop(eblk0_ref[e], eblk0_ref[e] + enb_ref[e], block, 0)

    @pl.when(e == pl.num_programs(0) - 1)
    def _():
        output_copy(nblk - 1, (nblk - 1) & 1).wait()


def _moe(h2t, w1, w3, w2, eidx, eblk0, enb, bs0, bn, nblk, order):
    ne, d, ff = w1.shape
    kx = ky = _token_pitch(d)
    wmap = lambda e, eidx, *_: (eidx[e], 0, 0)
    grid_spec = pltpu.PrefetchScalarGridSpec(
        num_scalar_prefetch=7,
        grid=(ne,),
        in_specs=[
            pl.BlockSpec(memory_space=pl.ANY),
            pl.BlockSpec((None, d, ff), wmap),
            pl.BlockSpec((None, d, ff), wmap),
            pl.BlockSpec((None, ff, d), wmap),
        ],
        out_specs=pl.BlockSpec(memory_space=pl.ANY),
        scratch_shapes=[
            pltpu.VMEM((GATHER_AHEAD + 1, MOE_BLOCK * kx, LANES), F32),
            pltpu.VMEM((2, MOE_BLOCK * ky, LANES), F32),
            pltpu.VMEM((d, ff), BF16),
            pltpu.VMEM((d, ff), BF16),
            pltpu.VMEM((ff, d), BF16),
            pltpu.SemaphoreType.DMA((GATHER_AHEAD + 1,)),
            pltpu.SemaphoreType.DMA((2,)),
        ],
    )
    return pl.pallas_call(
        _moe_kernel,
        grid_spec=grid_spec,
        out_shape=jax.ShapeDtypeStruct(((order.shape[0] + MOE_BLOCK) * ky, LANES), F32),
        compiler_params=_cparams("arbitrary"),
        name="moe",
    )(eidx, eblk0, enb, bs0, bn, nblk, order, h2t, w1, w3, w2)


def _combine_kernel(pos_ref, x1_ref, ys_hbm, wts_ref, gt_ref, o_ref, ybuf, sem):
    tm = x1_ref.shape[1]
    ky = ybuf.shape[2] // tm
    nt = pl.num_programs(1)
    n = pl.program_id(0) * nt + pl.program_id(1)
    slot = n & 1

    def row_copy(sl, k, pos, j, rows=1):
        return pltpu.make_async_copy(ys_hbm.at[pl.ds(pos * ky, rows * ky)],
                                     ybuf.at[sl, k, pl.ds(j * ky, rows * ky)], sem.at[sl])

    def start_tile(tile, sl):
        base = tile * (tm * TOP_K)

        def group(g, carry):
            for u in range(ROW_GROUP):
                row_copy(sl, u % TOP_K, pos_ref[base + g * ROW_GROUP + u],
                         g * (ROW_GROUP // TOP_K) + u // TOP_K).start()
            return carry

        lax.fori_loop(0, tm * TOP_K // ROW_GROUP, group, 0)

    @pl.when(n == 0)
    def _():
        start_tile(0, 0)

    @pl.when(n + 1 < pl.num_programs(0) * nt)
    def _():
        start_tile(n + 1, 1 - slot)

    for k in range(TOP_K):
        row_copy(slot, k, 0, 0, rows=tm).wait()
    w = wts_ref[0]
    d = x1_ref.shape[2]
    ff = (w[:, 0:1] * _load_token_tiles(ybuf.at[slot, 0], tm, d)
          + w[:, 1:2] * _load_token_tiles(ybuf.at[slot, 1], tm, d))
    o_ref[0] = x1_ref[0] + gt_ref[0] * ff


def _combine(x1, ys, pos, wts, gt2, tm):
    b, t, d = x1.shape
    ky = _token_pitch(d)
    row = lambda bi, i, pos: (bi, i, 0)
    vec = lambda bi, i, pos: (bi, 0, 0)
    grid_spec = pltpu.PrefetchScalarGridSpec(
        num_scalar_prefetch=1,
        grid=(b, t // tm),
        in_specs=[
            pl.BlockSpec((1, tm, d), row),
            pl.BlockSpec(memory_space=pl.ANY),
            pl.BlockSpec((1, tm, LANES), row),
            pl.BlockSpec((1, 1, d), vec),
        ],
        out_specs=pl.BlockSpec((1, tm, d), row),
        scratch_shapes=[
            pltpu.VMEM((2, TOP_K, tm * ky, LANES), F32),
            pltpu.SemaphoreType.DMA((2,)),
        ],
    )
    return pl.pallas_call(
        _combine_kernel,
        grid_spec=grid_spec,
        out_shape=jax.ShapeDtypeStruct((b, t, d), F32),
        compiler_params=_cparams("arbitrary", "arbitrary"),
        name="combine",
    )(pos, x1, ys, wts, gt2)


def _rope_tables(t):
    rows = t // GRID_W
    n_freq = HEAD_DIM // 4
    inv = ROPE_THETA ** (-jnp.arange(n_freq, dtype=F32) / n_freq)
    ar = jnp.arange(rows, dtype=F32)[:, None] * inv
    ac = jnp.arange(GRID_W, dtype=F32)[:, None] * inv

    def expand(fr, fc, sign):
        by_row = jnp.broadcast_to(fr[:, None, :], (rows, GRID_W, n_freq))
        by_col = jnp.broadcast_to(fc[None, :, :], (rows, GRID_W, n_freq))
        return jnp.concatenate([sign * by_row, by_row, sign * by_col, by_col], axis=2).reshape(t, HEAD_DIM)

    return expand(jnp.cos(ar), jnp.cos(ac), 1.0), expand(jnp.sin(ar), jnp.sin(ac), -1.0)


def _gate_weights(w2, bias):
    r = GLA_GATE_RANK
    wh = w2.reshape(2, r, GLA_HEADS, GLA_DK).transpose(2, 0, 1, 3)
    w = jnp.zeros((GLA_HEADS, 2 * r, 2 * GLA_DK), F32)
    w = w.at[:, 0:r, 0:GLA_DK].set(wh[:, 0]).at[:, r:2 * r, GLA_DK:].set(wh[:, 1])
    hi = w.astype(BF16)
    mid = (w - hi.astype(F32)).astype(BF16)
    pad = jnp.zeros((GLA_HEADS, LANES - 6 * r, 2 * GLA_DK), BF16)
    w2s = jnp.concatenate([hi, hi, mid, pad], axis=1)
    bias2 = bias.reshape(2, GLA_HEADS, GLA_DK).transpose(1, 0, 2).reshape(GLA_HEADS, 1, 2 * GLA_DK)
    return w2s, bias2


def _block_plan(eid_flat, nb):
    order = jnp.argsort(eid_flat).astype(jnp.int32)
    eids = jnp.arange(N_EXPERTS, dtype=jnp.int32)
    of_assign = (eid_flat[:, None] == eids[None, :]).astype(jnp.int32)
    counts = jnp.sum(of_assign, axis=0)
    starts = jnp.cumsum(counts) - counts
    nblk_e = (counts + MOE_BLOCK - 1) // MOE_BLOCK
    bends = jnp.cumsum(nblk_e)
    bstarts = bends - nblk_e
    nblk = bends[-1]
    prev_used = lax.cummax(jnp.where(counts > 0, eids, -1))
    eidx = jnp.where(prev_used >= 0, prev_used, jnp.argmax(counts > 0)).astype(jnp.int32)
    bi = jnp.arange(nb, dtype=jnp.int32)
    bic = jnp.minimum(bi, jnp.maximum(nblk - 1, 0))
    bexp = jnp.minimum(jnp.sum(bends[None, :] <= bic[:, None], axis=1), N_EXPERTS - 1)
    of_block = (bexp[:, None] == eids[None, :]).astype(jnp.int32)

    def lookup(per_expert):
        return jnp.sum(of_block * per_expert[None, :], axis=1)

    r0 = (bic - lookup(bstarts)) * MOE_BLOCK
    bs0 = (lookup(starts) + r0).astype(jnp.int32)
    bn = jnp.where(bi < nblk, jnp.minimum(lookup(counts) - r0, MOE_BLOCK), 0).astype(jnp.int32)
    pos = jnp.argsort(order).astype(jnp.int32)
    plan = (eidx, bstarts.astype(jnp.int32), nblk_e.astype(jnp.int32), bs0, bn,
            nblk.reshape(1).astype(jnp.int32), order)
    return plan, pos


def kernel(x, c, ctx, c_ctx, w_ada, b_ada, norm1_g, w_in, q_norm_g, k_norm_g, gla_gate_w2, gla_gate_b, gla_norm_g, w_out, norm2_g, router_grp_w, router_grp_b, router_exp_w, router_exp_b, moe_w1, moe_w3, moe_w2):
    b, t, d = x.shape
    tc = ctx.shape[1]
    depth = w_ada.shape[0]
    assert depth == 1, "single-layer stack: the context stream only feeds keys/values and GLA states"
    layer = 0

    c8 = jnp.zeros((8, d), F32).at[0:b].set(c).at[b].set(c_ctx)
    mod = _ada(c8, w_ada[layer], b_ada[layer])
    sh1, sc1, gt1, sh2, sc2, gt2 = [mod[0:b, i * d:(i + 1) * d].reshape(b, 1, d) for i in range(6)]
    sh1c, sc1c = [jnp.broadcast_to(mod[b, i * d:(i + 1) * d].reshape(1, 1, d), (b, 1, d)) for i in range(2)]

    w_main, w_tail = _w_in_parts(jnp.swapaxes(w_in[layer], 0, 1))

    cos_t, sin_t = _rope_tables(t)
    g1 = norm1_g[layer].reshape(1, d)
    qg = q_norm_g[layer].reshape(1, HEAD_DIM)
    kg = k_norm_g[layer].reshape(1, HEAD_DIM)
    q, k, v, gq, gk, gv, og, lr = _in_proj(x, g1, sc1, sh1, w_main, w_tail, qg, kg, cos_t, sin_t, 512)
    ones_t = jnp.ones((tc, HEAD_DIM), F32)
    _, kc, vc, gqc, gkc, gvc, _, lrc = _in_proj(ctx, g1, sc1c, sh1c, w_main, w_tail, qg, kg,
                                                ones_t, jnp.zeros_like(ones_t), tc)

    attn = _attn(q, jnp.concatenate([kc, k], axis=1), jnp.concatenate([vc, v], axis=1), 256)
    w2s, bias2 = _gate_weights(gla_gate_w2[layer], gla_gate_b[layer])
    gla = _gla(gq, gk, gv, lr, og, gqc, gkc, gvc, lrc, w2s, bias2, gla_norm_g[layer].reshape(1, GLA_DV))

    wr = jnp.concatenate([router_grp_w[layer], router_exp_w[layer],
                          jnp.zeros((d, LANES - N_GROUPS - N_EXPERTS), F32)], axis=1)
    br = jnp.concatenate([router_grp_b[layer], router_exp_b[layer],
                          jnp.zeros((LANES - N_GROUPS - N_EXPERTS,), F32)]).reshape(1, LANES)
    wr_hi = wr.astype(BF16)
    wr_parts = jnp.concatenate([wr_hi, (wr - wr_hi.astype(F32)).astype(BF16)], axis=1)
    x1, h2t, ids, wts = _out_proj(attn, gla, w_out[layer].astype(BF16), x, gt1, norm2_g[layer].reshape(1, d),
                                  sc2, sh2, wr_parts, br, 512)

    m = b * t
    n_assign = m * TOP_K
    nb = -(-(n_assign + N_EXPERTS * (MOE_BLOCK - 1)) // MOE_BLOCK)
    eid_flat = ids[:, :, 0:TOP_K].reshape(n_assign)
    plan, pos = _block_plan(eid_flat, nb)
    ys = _moe(h2t.reshape(-1, LANES), moe_w1[layer], moe_w3[layer], moe_w2[layer], *plan)
    return _combine(x1, ys, pos, wts, gt2, 256)
```

```python
import functools

import jax
import jax.numpy as jnp
from jax import lax
from jax.experimental import pallas as pl
from jax.experimental.pallas import tpu as pltpu

EPS = 1e-6
GRID_W = 64
ROPE_THETA = 10000.0

ATTN_HEADS = 8
ATTN_KV_HEADS = 2
HEAD_DIM = 128
GQA_GROUP = ATTN_HEADS // ATTN_KV_HEADS

GLA_HEADS = 4
GLA_DK = 128
GLA_DV = 256
GLA_GATE_RANK = 16
GLA_GATE_NORMALIZER = 16.0
GLA_CHUNK = 64
GLA_CHUNK_LOG2 = 6
GLA_PREP_TILE = 256
GLA_OUT_GROUP = 8

N_GROUPS = 8
EXPERTS_PER_GROUP = 8
N_EXPERTS = N_GROUPS * EXPERTS_PER_GROUP
TOP_K = 2
MOE_BLOCK = 256
ROW_GROUP_LOG2 = 3
ROW_GROUP = 1 << ROW_GROUP_LOG2
GATHER_AHEAD = 2

LOG2_E = 1.4426950408889634
LANES = 128
VMEM_LIMIT = 56 * 1024 * 1024

ATTN_Q_W = ATTN_HEADS * HEAD_DIM
ATTN_KV_W = ATTN_KV_HEADS * HEAD_DIM
GLA_K_W = GLA_HEADS * GLA_DK
GLA_V_W = GLA_HEADS * GLA_DV

BF16 = jnp.bfloat16
F32 = jnp.float32


def _cparams(*sem):
    return pltpu.CompilerParams(dimension_semantics=sem, vmem_limit_bytes=VMEM_LIMIT)


def _ada_kernel(c_ref, w_ref, b_ref, o_ref):
    c = c_ref[...]
    s = c * jax.nn.sigmoid(c)
    o_ref[...] = jnp.dot(s.astype(BF16), w_ref[...].astype(BF16), preferred_element_type=F32) + b_ref[...]


def _ada(c8, w, b):
    d, n = w.shape
    tn = 1024
    return pl.pallas_call(
        _ada_kernel,
        grid=(n // tn,),
        in_specs=[
            pl.BlockSpec((8, d), lambda j: (0, 0)),
            pl.BlockSpec((d, tn), lambda j: (0, j)),
            pl.BlockSpec((1, tn), lambda j: (0, j)),
        ],
        out_specs=pl.BlockSpec((8, tn), lambda j: (0, j)),
        out_shape=jax.ShapeDtypeStruct((8, n), F32),
        compiler_params=_cparams("arbitrary"),
        name="ada",
    )(c8, w, b.reshape(1, n))


def _w_main_kernel(wt_ref, o_ref):
    o_ref[...] = wt_ref[...].T.astype(BF16)


def _w_tail_kernel(wt_ref, o_ref):
    r = 2 * GLA_GATE_RANK
    o_ref[:, :GLA_V_W] = wt_ref[r:, :].T.astype(BF16)
    lowrank = jnp.concatenate([wt_ref[:r, :], jnp.zeros((LANES - r, wt_ref.shape[1]), F32)], axis=0)
    o_ref[:, GLA_V_W:] = lowrank.T.astype(BF16)


def _w_in_parts(wt):
    n, d = wt.shape
    n_main = n - GLA_V_W - 2 * GLA_GATE_RANK
    tn = 512
    main = pl.pallas_call(
        _w_main_kernel,
        grid=(n_main // tn,),
        in_specs=[pl.BlockSpec((tn, d), lambda j: (j, 0))],
        out_specs=pl.BlockSpec((d, tn), lambda j: (0, j)),
        out_shape=jax.ShapeDtypeStruct((d, n_main), BF16),
        compiler_params=_cparams("arbitrary"),
        name="w_in_main",
    )(wt)
    tail = pl.pallas_call(
        _w_tail_kernel,
        out_shape=jax.ShapeDtypeStruct((d, GLA_V_W + LANES), BF16),
        compiler_params=pltpu.CompilerParams(vmem_limit_bytes=VMEM_LIMIT),
        name="w_in_tail",
    )(wt[n_main:])
    return main, tail


def _swap32(y):
    lane = lax.broadcasted_iota(jnp.int32, y.shape, 1)
    return jnp.where((lane & 63) < 32, pltpu.roll(y, 96, 1), pltpu.roll(y, 32, 1))


def _head_norm_rope(a, g, cos, sin):
    y = a * lax.rsqrt(jnp.mean(a * a, axis=-1, keepdims=True) + EPS) * g
    return y * cos + _swap32(y) * sin


def _in_proj_kernel(x_ref, g1_ref, sc_ref, sh_ref, w_ref, wt_ref, qg_ref, kg_ref, cos_ref, sin_ref,
                    q_ref, k_ref, v_ref, gq_ref, gk_ref, gv_ref, og_ref, lr_ref):
    x = x_ref[0]
    xn = x * lax.rsqrt(jnp.mean(x * x, axis=-1, keepdims=True) + EPS)
    h = (xn * g1_ref[...] * (1.0 + sc_ref[0]) + sh_ref[0]).astype(BF16)
    cos = cos_ref[...]
    sin = sin_ref[...]

    def proj(off, width, w=w_ref):
        return jnp.dot(h, w[:, off:off + width], preferred_element_type=F32)

    qg = qg_ref[...] * (HEAD_DIM ** -0.5 * LOG2_E)
    off = 0
    for out_ref, gain, heads in ((q_ref, qg, ATTN_HEADS), (k_ref, kg_ref[...], ATTN_KV_HEADS)):
        for pair in range(heads // 2):
            a2 = proj(off, 2 * HEAD_DIM)
            for u in range(2):
                lo = (2 * pair + u) * HEAD_DIM
                a = a2[:, u * HEAD_DIM:(u + 1) * HEAD_DIM]
                out_ref[0, :, lo:lo + HEAD_DIM] = _head_norm_rope(a, gain, cos, sin).astype(BF16)
            off += 2 * HEAD_DIM
    v_ref[0] = proj(off, ATTN_KV_W).astype(BF16)
    off += ATTN_KV_W
    gq_ref[0] = proj(off, GLA_K_W)
    off += GLA_K_W
    gk_ref[0] = proj(off, GLA_K_W)
    off += GLA_K_W
    gv_ref[0] = proj(off, GLA_V_W).astype(BF16)
    og_ref[0] = proj(0, GLA_V_W, wt_ref)
    lr_ref[0] = proj(GLA_V_W, LANES, wt_ref)


def _in_proj(x, g1, sc, sh, w_main, w_tail, qg, kg, cos_t, sin_t, tm):
    b, t, d = x.shape
    row = lambda bi, i: (bi, i, 0)
    vec = lambda bi, i: (bi, 0, 0)
    const = lambda bi, i: (0, 0)
    tab = lambda bi, i: (i, 0)
    widths = [(ATTN_Q_W, BF16), (ATTN_KV_W, BF16), (ATTN_KV_W, BF16), (GLA_K_W, F32), (GLA_K_W, F32),
              (GLA_V_W, BF16), (GLA_V_W, F32), (LANES, F32)]
    return pl.pallas_call(
        _in_proj_kernel,
        grid=(b, t // tm),
        in_specs=[
            pl.BlockSpec((1, tm, d), row),
            pl.BlockSpec((1, d), const),
            pl.BlockSpec((1, 1, d), vec),
            pl.BlockSpec((1, 1, d), vec),
            pl.BlockSpec(w_main.shape, const, pipeline_mode=pl.Buffered(1)),
            pl.BlockSpec(w_tail.shape, const, pipeline_mode=pl.Buffered(1)),
            pl.BlockSpec((1, HEAD_DIM), const),
            pl.BlockSpec((1, HEAD_DIM), const),
            pl.BlockSpec((tm, HEAD_DIM), tab),
            pl.BlockSpec((tm, HEAD_DIM), tab),
        ],
        out_specs=[pl.BlockSpec((1, tm, wd), row) for wd, _ in widths],
        out_shape=[jax.ShapeDtypeStruct((b, t, wd), dt) for wd, dt in widths],
        compiler_params=_cparams("arbitrary", "arbitrary"),
        name="in_proj",
    )(x, g1, sc, sh, w_main, w_tail, qg, kg, cos_t, sin_t)


def _attn_kernel(q_ref, k_ref, v_ref, o_ref):
    def kv(h, ref):
        hk = h // GQA_GROUP
        return ref[0, :, hk * HEAD_DIM:(hk + 1) * HEAD_DIM]

    def scores(h):
        q = q_ref[0, :, h * HEAD_DIM:(h + 1) * HEAD_DIM]
        return lax.dot_general(q, kv(h, k_ref), (((1,), (1,)), ((), ())), preferred_element_type=F32)

    s_next = scores(0)
    for h in range(ATTN_HEADS):
        s = s_next
        if h + 1 < ATTN_HEADS:
            s_next = scores(h + 1)
        p = jnp.exp2(s - jnp.max(s, axis=-1, keepdims=True))
        l = jnp.sum(p, axis=-1, keepdims=True)
        o = jnp.dot(p.astype(BF16), kv(h, v_ref), preferred_element_type=F32)
        o_ref[0, :, h * HEAD_DIM:(h + 1) * HEAD_DIM] = (o / l).astype(BF16)


def _attn(q, k_all, v_all, tq):
    b, t, _ = q.shape
    tk = k_all.shape[1]
    return pl.pallas_call(
        _attn_kernel,
        grid=(b, t // tq),
        in_specs=[
            pl.BlockSpec((1, tq, ATTN_Q_W), lambda bi, i: (bi, i, 0)),
            pl.BlockSpec((1, tk, ATTN_KV_W), lambda bi, i: (bi, 0, 0)),
            pl.BlockSpec((1, tk, ATTN_KV_W), lambda bi, i: (bi, 0, 0)),
        ],
        out_specs=pl.BlockSpec((1, tq, ATTN_Q_W), lambda bi, i: (bi, i, 0)),
        out_shape=jax.ShapeDtypeStruct((b, t, ATTN_Q_W), BF16),
        compiler_params=_cparams("arbitrary", "arbitrary"),
        name="attn",
    )(q, k_all, v_all)


def _log_sigmoid(x):
    return jnp.minimum(x, 0.0) - jnp.log(1.0 + jnp.exp(-jnp.abs(x)))


def _split3(x):
    hi = x.astype(BF16)
    r = x - hi.astype(F32)
    mid = r.astype(BF16)
    lo = (r - mid.astype(F32)).astype(BF16)
    return hi, mid, lo


def _dot_exact_lhs(m, x):
    return sum(jnp.dot(m, part, preferred_element_type=F32) for part in _split3(x))


def _prefix_operator(n):
    ri = lax.broadcasted_iota(jnp.int32, (n, n), 0)
    ci = lax.broadcasted_iota(jnp.int32, (n, n), 1)
    same = lax.shift_right_logical(ri, GLA_CHUNK_LOG2) == lax.shift_right_logical(ci, GLA_CHUNK_LOG2)
    return (same & (ci <= ri)).astype(BF16)


def _gate_prep(q, k, lr, w2s, bias2, prefix):
    c = GLA_CHUNK
    hi = lr.astype(BF16).astype(F32)
    mid = (lr - hi).astype(BF16).astype(F32)
    lhs = (hi + pltpu.roll(mid, 2 * GLA_GATE_RANK, 1) + pltpu.roll(hi, 4 * GLA_GATE_RANK, 1)).astype(BF16)
    logits = jnp.dot(lhs, w2s, preferred_element_type=F32) + bias2
    g = _log_sigmoid(logits) * (1.0 / GLA_GATE_NORMALIZER)
    pre = _dot_exact_lhs(prefix, g)
    tot = jnp.concatenate([jnp.broadcast_to(pre[lo + c - 1:lo + c, :], (c, pre.shape[1]))
                           for lo in range(0, pre.shape[0], c)], axis=0)
    dk = GLA_DK
    bcs = (pre[:, :dk], tot[:, dk:] - pre[:, dk:] + g[:, dk:])
    out = []
    for d, bc in enumerate(bcs):
        b_end = tot[:, d * dk:(d + 1) * dk]
        qe = (q * (GLA_DK ** -0.5) * jnp.exp(bc)).astype(BF16)
        ke = (k * jnp.exp(-bc)).astype(BF16)
        kend = (k * jnp.exp(b_end - bc)).astype(BF16)
        out.append((qe, ke, kend, b_end))
    return out


def _state_step(s_ref, v, kend, dec):
    upd = lax.dot_general(v, kend, (((0,), (0,)), ((), ())), preferred_element_type=F32)
    s_ref[...] = s_ref[...] * dec + upd


def _gla_kernel(q_ref, k_ref, v_ref, lr_ref, og_ref, qc_ref, kc_ref, vc_ref, lrc_ref,
                w2_ref, b_ref, ng_ref, o_ref,
                s_ref, qe_ref, ke_ref, kend_ref, dec_ref, kendc_ref, decc_ref, sbf_ref):
    c = GLA_CHUNK
    t = q_ref.shape[1]
    tc = qc_ref.shape[1]
    nc = t // c
    tile = GLA_PREP_TILE
    cpt = tile // c
    dirs = (0, 1)

    def store_dec(ref, d, base, b_end):
        for ch in range(b_end.shape[0] // c):
            ref[d, pl.ds(base + ch, 1), :] = jnp.exp(b_end[ch * c:ch * c + 1, :])

    prefix = _prefix_operator(tile)

    for i in range(tc // tile):
        rows = slice(i * tile, (i + 1) * tile)
        prep_c = _gate_prep(qc_ref[0, rows, :], kc_ref[0, rows, :], lrc_ref[0, rows, :],
                            w2_ref[...], b_ref[...], prefix)
        for d, (_, _, kend, b_end) in enumerate(prep_c):
            kendc_ref[d, rows, :] = kend
            store_dec(decc_ref, d, i * cpt, b_end)
    s_ref[...] = jnp.zeros_like(s_ref)
    for j in range(tc // c):
        for d in dirs:
            ch = j if d == 0 else tc // c - 1 - j
            _state_step(s_ref.at[d], vc_ref[0, ch * c:(ch + 1) * c, :], kendc_ref[d, ch * c:(ch + 1) * c, :],
                        decc_ref[d, ch:ch + 1, :])

    def prep(i, carry):
        lo = pl.multiple_of(i * tile, tile)
        rows = pl.ds(lo, tile)
        prep_l = _gate_prep(q_ref[0, rows, :], k_ref[0, rows, :], lr_ref[0, rows, :],
                            w2_ref[...], b_ref[...], prefix)
        for d, (qe, ke, kend, b_end) in enumerate(prep_l):
            qe_ref[d, rows, :] = qe
            ke_ref[d, rows, :] = ke
            kend_ref[d, rows, :] = kend
            store_dec(dec_ref, d, i * cpt, b_end)
        return carry

    lax.fori_loop(0, t // tile, prep, 0, unroll=2)

    def scan(j, carry):
        for d in dirs:
            ch = j if d == 0 else nc - 1 - j
            rows = pl.ds(pl.multiple_of(ch * c, c), c)
            sbf_ref[d, ch] = s_ref[d].astype(BF16)
            _state_step(s_ref.at[d], v_ref[0, rows, :], kend_ref[d, rows, :], dec_ref[d, pl.ds(ch, 1), :])
        return carry

    lax.fori_loop(0, nc, scan, 0, unroll=8)

    ri = lax.broadcasted_iota(jnp.int32, (c, c), 0)
    ci = lax.broadcasted_iota(jnp.int32, (c, c), 1)
    masks = (ci <= ri, ci >= ri)

    group = GLA_OUT_GROUP

    def out(i, carry):
        chunks = [i * group + u for u in range(group)]
        rows = [pl.ds(pl.multiple_of(ch * c, c), c) for ch in chunks]
        qes = [[qe_ref[d, r, :] for d in dirs] for r in rows]
        scores = [[lax.dot_general(qes[u][d], ke_ref[d, rows[u], :], (((1,), (1,)), ((), ())),
                                   preferred_element_type=F32) for d in dirs] for u in range(group)]
        inter = [[lax.dot_general(qes[u][d], sbf_ref[d, chunks[u]], (((1,), (1,)), ((), ())),
                                  preferred_element_type=F32) for d in dirs] for u in range(group)]
        for u in range(group):
            v = v_ref[0, rows[u], :]
            o = inter[u][0] + inter[u][1]
            for d in dirs:
                a = jnp.where(masks[d], scores[u][d], 0.0).astype(BF16)
                o = o + jnp.dot(a, v, preferred_element_type=F32)
            on = o * lax.rsqrt(jnp.mean(o * o, axis=-1, keepdims=True) + EPS) * ng_ref[...]
            og = og_ref[0, rows[u], :]
            o_ref[0, rows[u], :] = (on * (og * jax.nn.sigmoid(og))).astype(BF16)
        return carry

    lax.fori_loop(0, nc // group, out, 0)


def _gla(gq, gk, gv, lr, og, gqc, gkc, gvc, lrc, w2, bias, ng):
    b, t, _ = gq.shape
    tc = gqc.shape[1]
    hk = lambda bi, h: (bi, 0, h)
    h0 = lambda bi, h: (bi, 0, 0)
    return pl.pallas_call(
        _gla_kernel,
        grid=(b, GLA_HEADS),
        in_specs=[
            pl.BlockSpec((1, t, GLA_DK), hk),
            pl.BlockSpec((1, t, GLA_DK), hk),
            pl.BlockSpec((1, t, GLA_DV), hk),
            pl.BlockSpec((1, t, LANES), h0),
            pl.BlockSpec((1, t, GLA_DV), hk),
            pl.BlockSpec((1, tc, GLA_DK), hk),
            pl.BlockSpec((1, tc, GLA_DK), hk),
            pl.BlockSpec((1, tc, GLA_DV), hk),
            pl.BlockSpec((1, tc, LANES), h0),
            pl.BlockSpec((None, LANES, 2 * GLA_DK), lambda bi, h: (h, 0, 0)),
            pl.BlockSpec((None, 1, 2 * GLA_DK), lambda bi, h: (h, 0, 0)),
            pl.BlockSpec((1, GLA_DV), lambda bi, h: (0, 0)),
        ],
        out_specs=pl.BlockSpec((1, t, GLA_DV), hk),
        out_shape=jax.ShapeDtypeStruct((b, t, GLA_V_W), BF16),
        scratch_shapes=[
            pltpu.VMEM((2, GLA_DV, GLA_DK), F32),
            pltpu.VMEM((2, t, GLA_DK), BF16),
            pltpu.VMEM((2, t, GLA_DK), BF16),
            pltpu.VMEM((2, t, GLA_DK), BF16),
            pltpu.VMEM((2, t // GLA_CHUNK, GLA_DK), F32),
            pltpu.VMEM((2, tc, GLA_DK), BF16),
            pltpu.VMEM((2, tc // GLA_CHUNK, GLA_DK), F32),
            pltpu.VMEM((2, t // GLA_CHUNK, GLA_DV, GLA_DK), BF16),
        ],
        compiler_params=_cparams("arbitrary", "arbitrary"),
        name="gla",
    )(gq, gk, gv, lr, og, gqc, gkc, gvc, lrc, w2, bias, ng)


def _token_pitch(d):
    rows = d // LANES
    return rows + 4 if rows % 8 == 0 else rows


def _store_token_tiles(ref, x):
    n = x.shape[0]
    k = x.shape[1] // LANES
    pitch = ref.shape[0] // n
    for s in range(pitch):
        slab = x[:, s * LANES:(s + 1) * LANES] if s < k else jnp.zeros((n, LANES), x.dtype)
        ref[pl.ds(s, n, stride=pitch), :] = slab


def _load_token_tiles(ref, n, d):
    pitch = ref.shape[0] // n
    return jnp.concatenate([ref[pl.ds(s, n, stride=pitch), :] for s in range(d // LANES)], axis=1)


def _out_proj_kernel(attn_ref, gla_ref, wo_ref, x_ref, gt_ref, g2_ref, sc_ref, sh_ref, wr_ref, br_ref,
                     x1_ref, h2_ref, ids_ref, wts_ref):
    y = jnp.dot(attn_ref[0], wo_ref[0:ATTN_Q_W, :], preferred_element_type=F32)
    y = y + jnp.dot(gla_ref[0], wo_ref[ATTN_Q_W:, :], preferred_element_type=F32)
    x1 = x_ref[0] + gt_ref[0] * y
    x1_ref[0] = x1
    xn = x1 * lax.rsqrt(jnp.mean(x1 * x1, axis=-1, keepdims=True) + EPS)
    h2 = xn * g2_ref[...] * (1.0 + sc_ref[0]) + sh_ref[0]
    hi = h2.astype(BF16)
    hi_f = hi.astype(F32)

    _store_token_tiles(h2_ref.at[0], h2)

    mid = (h2 - hi_f).astype(BF16)
    both = jnp.dot(hi, wr_ref[...], preferred_element_type=F32)
    logits = (both[:, :LANES] + both[:, LANES:]
              + jnp.dot(mid, wr_ref[:, :LANES], preferred_element_type=F32)) + br_ref[...]
    lane = lax.broadcasted_iota(jnp.int32, logits.shape, 1)
    lane_f = lane.astype(F32)
    neg = jnp.float32(-jnp.inf)

    def first_argmax(vals):
        m = jnp.max(vals, axis=-1, keepdims=True)
        idx = jnp.min(jnp.where(vals == m, lane_f, float(LANES)), axis=-1, keepdims=True)
        return m, idx

    lg = jnp.where(lane < N_GROUPS, logits, neg)
    mg, grp = first_argmax(lg)
    pg_sel = 1.0 / jnp.sum(jnp.exp(lg - mg), axis=-1, keepdims=True)
    lo = N_GROUPS + grp * EXPERTS_PER_GROUP
    in_grp = (lane_f >= lo) & (lane_f < lo + EXPERTS_PER_GROUP)
    le = jnp.where(in_grp, logits, neg)
    v1, i1 = first_argmax(le)
    v2, i2 = first_argmax(jnp.where(lane_f == i1, neg, le))
    e2 = jnp.exp(v2 - v1)
    w1 = pg_sel / (1.0 + e2)
    w2 = pg_sel * e2 / (1.0 + e2)
    ids = jnp.where(lane == 0, i1 - N_GROUPS, jnp.where(lane == 1, i2 - N_GROUPS, 0.0))
    ids_ref[0] = ids.astype(jnp.int32)
    wts_ref[0] = jnp.where(lane == 0, w1, jnp.where(lane == 1, w2, 0.0))


def _out_proj(attn, gla, wo, x, gt1, g2, sc2, sh2, wr, br, tm):
    b, t, d = x.shape
    row = lambda bi, i: (bi, i, 0)
    vec = lambda bi, i: (bi, 0, 0)
    const = lambda bi, i: (0, 0)
    return pl.pallas_call(
        _out_proj_kernel,
        grid=(b, t // tm),
        in_specs=[
            pl.BlockSpec((1, tm, ATTN_Q_W), row),
            pl.BlockSpec((1, tm, GLA_V_W), row),
            pl.BlockSpec(wo.shape, const, pipeline_mode=pl.Buffered(1)),
            pl.BlockSpec((1, tm, d), row),
            pl.BlockSpec((1, 1, d), vec),
            pl.BlockSpec((1, d), const),
            pl.BlockSpec((1, 1, d), vec),
            pl.BlockSpec((1, 1, d), vec),
            pl.BlockSpec((d, 2 * LANES), const),
            pl.BlockSpec((1, LANES), const),
        ],
        out_specs=[
            pl.BlockSpec((1, tm, d), row),
            pl.BlockSpec((1, tm * _token_pitch(d), LANES), row),
            pl.BlockSpec((1, tm, LANES), row),
            pl.BlockSpec((1, tm, LANES), row),
        ],
        out_shape=[
            jax.ShapeDtypeStruct((b, t, d), F32),
            jax.ShapeDtypeStruct((b, t * _token_pitch(d), LANES), F32),
            jax.ShapeDtypeStruct((b, t, LANES), jnp.int32),
            jax.ShapeDtypeStruct((b, t, LANES), F32),
        ],
        compiler_params=_cparams("arbitrary", "arbitrary"),
        name="out_proj",
    )(attn, gla, wo, x, gt1, g2, sc2, sh2, wr, br)


def _moe_kernel(eidx_ref, eblk0_ref, enb_ref, bs0_ref, bn_ref, nblk_ref, order_ref,
                h2_hbm, w1_ref, w3_ref, w2_ref, y_hbm,
                xbuf, ybuf, w1b, w3b, w2b, gsem, ssem):
    del eidx_ref
    e = pl.program_id(0)
    nblk = nblk_ref[0]
    kx = xbuf.shape[1] // MOE_BLOCK
    ky = ybuf.shape[1] // MOE_BLOCK

    def gather_copy(sl, hbm_row, j, rows=1):
        return pltpu.make_async_copy(h2_hbm.at[pl.ds(hbm_row * kx, rows * kx)],
                                     xbuf.at[sl, pl.ds(j * kx, rows * kx)], gsem.at[sl])

    def for_rows(n, per_group, per_row):
        ng = lax.shift_right_logical(n, ROW_GROUP_LOG2)
        lax.fori_loop(0, ng, lambda g, c: (per_group(g * ROW_GROUP), c)[1], 0)
        lax.fori_loop(ng * ROW_GROUP, n, lambda j, c: (per_row(j), c)[1], 0)

    def start_gather(blk, sl):
        s = bs0_ref[blk]

        def start_row(j):
            gather_copy(sl, lax.shift_right_logical(order_ref[s + j], 1), j).start()

        def group(j0):
            for u in range(ROW_GROUP):
                start_row(j0 + u)

        for_rows(bn_ref[blk], group, start_row)

    def wait_gather(blk, sl):
        for_rows(bn_ref[blk], lambda j0: gather_copy(sl, 0, 0, ROW_GROUP).wait(),
                 lambda j: gather_copy(sl, 0, 0).wait())

    def output_copy(blk, sl):
        return pltpu.make_async_copy(
            ybuf.at[sl], y_hbm.at[pl.ds(bs0_ref[blk] * ky, MOE_BLOCK * ky)], ssem.at[sl])

    @pl.when(e == 0)
    def _():
        xbuf[...] = jnp.zeros_like(xbuf)
        for g0 in range(GATHER_AHEAD):
            @pl.when(g0 < nblk)
            def _():
                start_gather(g0, g0)

    def block(g, carry):
        slot = g & 1
        xslot = lax.rem(g, GATHER_AHEAD + 1)

        @pl.when(g + GATHER_AHEAD < nblk)
        def _():
            start_gather(g + GATHER_AHEAD, lax.rem(g + GATHER_AHEAD, GATHER_AHEAD + 1))

        wait_gather(g, xslot)
        x = _load_token_tiles(xbuf.at[xslot], MOE_BLOCK, w1b.shape[0]).astype(BF16)
        a = jnp.dot(x, w1b[...], preferred_element_type=F32)
        gate = jnp.dot(x, w3b[...], preferred_element_type=F32)
        hid = (a * jax.nn.sigmoid(a) * gate).astype(BF16)
        y = jnp.dot(hid, w2b[...], preferred_element_type=F32)

        _store_token_tiles(ybuf.at[slot], y)

        @pl.when(g >= 1)
        def _():
            output_copy(g - 1, 1 - slot).wait()

        output_copy(g, slot).start()
        return carry

    @pl.when(enb_ref[e] > 0)
    def _():
        w1b[...] = w1_ref[...].astype(BF16)
        w3b[...] = w3_ref[...].astype(BF16)
        w2b[...] = w2_ref[...].astype(BF16)
        lax.fori_loop(eblk0_ref[e], eblk0_ref[e] + enb_ref[e], block, 0)

    @pl.when(e == pl.num_programs(0) - 1)
    def _():
        output_copy(nblk - 1, (nblk - 1) & 1).wait()
        ybuf[0] = jnp.zeros_like(ybuf[0])
        pad_rows = MOE_BLOCK * ky
        pad = pltpu.make_async_copy(ybuf.at[0], y_hbm.at[pl.ds(y_hbm.shape[0] - pad_rows, pad_rows)], ssem.at[0])
        pad.start()
        pad.wait()


def _moe(h2t, w1, w3, w2, eidx, eblk0, enb, bs0, bn, nblk, order):
    ne, d, ff = w1.shape
    kx = ky = _token_pitch(d)
    wmap = lambda e, eidx, *_: (eidx[e], 0, 0)
    grid_spec = pltpu.PrefetchScalarGridSpec(
        num_scalar_prefetch=7,
        grid=(ne,),
        in_specs=[
            pl.BlockSpec(memory_space=pl.ANY),
            pl.BlockSpec((None, d, ff), wmap),
            pl.BlockSpec((None, d, ff), wmap),
            pl.BlockSpec((None, ff, d), wmap),
        ],
        out_specs=pl.BlockSpec(memory_space=pl.ANY),
        scratch_shapes=[
            pltpu.VMEM((GATHER_AHEAD + 1, MOE_BLOCK * kx, LANES), F32),
            pltpu.VMEM((2, MOE_BLOCK * ky, LANES), F32),
            pltpu.VMEM((d, ff), BF16),
            pltpu.VMEM((d, ff), BF16),
            pltpu.VMEM((ff, d), BF16),
            pltpu.SemaphoreType.DMA((GATHER_AHEAD + 1,)),
            pltpu.SemaphoreType.DMA((2,)),
        ],
    )
    return pl.pallas_call(
        _moe_kernel,
        grid_spec=grid_spec,
        out_shape=jax.ShapeDtypeStruct(((order.shape[0] + MOE_BLOCK) * ky, LANES), F32),
        compiler_params=_cparams("arbitrary"),
        name="moe",
    )(eidx, eblk0, enb, bs0, bn, nblk, order, h2t, w1, w3, w2)


def _combine_kernel(pos_ref, x1_ref, ys_hbm, wts_ref, gt_ref, o_ref, ybuf, sem):
    tm = x1_ref.shape[1]
    ky = ybuf.shape[2] // tm
    nt = pl.num_programs(1)
    n = pl.program_id(0) * nt + pl.program_id(1)
    slot = n & 1

    def row_copy(sl, k, pos, j, rows=1):
        return pltpu.make_async_copy(ys_hbm.at[pl.ds(pos * ky, rows * ky)],
                                     ybuf.at[sl, k, pl.ds(j * ky, rows * ky)], sem.at[sl])

    def start_tile(tile, sl):
        base = tile * (tm * TOP_K)

        def group(g, carry):
            for u in range(ROW_GROUP):
                row_copy(sl, u % TOP_K, pos_ref[base + g * ROW_GROUP + u],
                         g * (ROW_GROUP // TOP_K) + u // TOP_K).start()
            return carry

        lax.fori_loop(0, tm * TOP_K // ROW_GROUP, group, 0)

    @pl.when(n == 0)
    def _():
        start_tile(0, 0)

    @pl.when(n + 1 < pl.num_programs(0) * nt)
    def _():
        start_tile(n + 1, 1 - slot)

    for k in range(TOP_K):
        row_copy(slot, k, 0, 0, rows=tm).wait()
    w = wts_ref[0]
    d = x1_ref.shape[2]
    ff = (w[:, 0:1] * _load_token_tiles(ybuf.at[slot, 0], tm, d)
          + w[:, 1:2] * _load_token_tiles(ybuf.at[slot, 1], tm, d))
    o_ref[0] = x1_ref[0] + gt_ref[0] * ff


def _combine(x1, ys, pos, wts, gt2, tm):
    b, t, d = x1.shape
    ky = _token_pitch(d)
    row = lambda bi, i, pos: (bi, i, 0)
    vec = lambda bi, i, pos: (bi, 0, 0)
    grid_spec = pltpu.PrefetchScalarGridSpec(
        num_scalar_prefetch=1,
        grid=(b, t // tm),
        in_specs=[
            pl.BlockSpec((1, tm, d), row),
            pl.BlockSpec(memory_space=pl.ANY),
            pl.BlockSpec((1, tm, LANES), row),
            pl.BlockSpec((1, 1, d), vec),
        ],
        out_specs=pl.BlockSpec((1, tm, d), row),
        scratch_shapes=[
            pltpu.VMEM((2, TOP_K, tm * ky, LANES), F32),
            pltpu.SemaphoreType.DMA((2,)),
        ],
    )
    return pl.pallas_call(
        _combine_kernel,
        grid_spec=grid_spec,
        out_shape=jax.ShapeDtypeStruct((b, t, d), F32),
        compiler_params=_cparams("arbitrary", "arbitrary"),
        name="combine",
    )(pos, x1, ys, wts, gt2)


def _rope_tables(t):
    rows = t // GRID_W
    n_freq = HEAD_DIM // 4
    inv = ROPE_THETA ** (-jnp.arange(n_freq, dtype=F32) / n_freq)
    ar = jnp.arange(rows, dtype=F32)[:, None] * inv
    ac = jnp.arange(GRID_W, dtype=F32)[:, None] * inv

    def expand(fr, fc, sign):
        by_row = jnp.broadcast_to(fr[:, None, :], (rows, GRID_W, n_freq))
        by_col = jnp.broadcast_to(fc[None, :, :], (rows, GRID_W, n_freq))
        return jnp.concatenate([sign * by_row, by_row, sign * by_col, by_col], axis=2).reshape(t, HEAD_DIM)

    return expand(jnp.cos(ar), jnp.cos(ac), 1.0), expand(jnp.sin(ar), jnp.sin(ac), -1.0)


def _gate_weights(w2, bias):
    r = GLA_GATE_RANK
    wh = w2.reshape(2, r, GLA_HEADS, GLA_DK).transpose(2, 0, 1, 3)
    w = jnp.zeros((GLA_HEADS, 2 * r, 2 * GLA_DK), F32)
    w = w.at[:, 0:r, 0:GLA_DK].set(wh[:, 0]).at[:, r:2 * r, GLA_DK:].set(wh[:, 1])
    hi = w.astype(BF16)
    mid = (w - hi.astype(F32)).astype(BF16)
    pad = jnp.zeros((GLA_HEADS, LANES - 6 * r, 2 * GLA_DK), BF16)
    w2s = jnp.concatenate([hi, hi, mid, pad], axis=1)
    bias2 = bias.reshape(2, GLA_HEADS, GLA_DK).transpose(1, 0, 2).reshape(GLA_HEADS, 1, 2 * GLA_DK)
    return w2s, bias2


def _block_plan(eid_flat, nb):
    order = jnp.argsort(eid_flat).astype(jnp.int32)
    eids = jnp.arange(N_EXPERTS, dtype=jnp.int32)
    of_assign = (eid_flat[:, None] == eids[None, :]).astype(jnp.int32)
    counts = jnp.sum(of_assign, axis=0)
    starts = jnp.cumsum(counts) - counts
    nblk_e = (counts + MOE_BLOCK - 1) // MOE_BLOCK
    bends = jnp.cumsum(nblk_e)
    bstarts = bends - nblk_e
    nblk = bends[-1]
    prev_used = lax.cummax(jnp.where(counts > 0, eids, -1))
    eidx = jnp.where(prev_used >= 0, prev_used, jnp.argmax(counts > 0)).astype(jnp.int32)
    bi = jnp.arange(nb, dtype=jnp.int32)
    bic = jnp.minimum(bi, jnp.maximum(nblk - 1, 0))
    bexp = jnp.minimum(jnp.sum(bends[None, :] <= bic[:, None], axis=1), N_EXPERTS - 1)
    of_block = (bexp[:, None] == eids[None, :]).astype(jnp.int32)

    def lookup(per_expert):
        return jnp.sum(of_block * per_expert[None, :], axis=1)

    r0 = (bic - lookup(bstarts)) * MOE_BLOCK
    bs0 = (lookup(starts) + r0).astype(jnp.int32)
    bn = jnp.where(bi < nblk, jnp.minimum(lookup(counts) - r0, MOE_BLOCK), 0).astype(jnp.int32)
    pos = jnp.argsort(order).astype(jnp.int32)
    plan = (eidx, bstarts.astype(jnp.int32), nblk_e.astype(jnp.int32), bs0, bn,
            nblk.reshape(1).astype(jnp.int32), order)
    return plan, pos


def kernel(x, c, ctx, c_ctx, w_ada, b_ada, norm1_g, w_in, q_norm_g, k_norm_g, gla_gate_w2, gla_gate_b, gla_norm_g, w_out, norm2_g, router_grp_w, router_grp_b, router_exp_w, router_exp_b, moe_w1, moe_w3, moe_w2):
    b, t, d = x.shape
    tc = ctx.shape[1]
    depth = w_ada.shape[0]
    assert depth == 1, "single-layer stack: the context stream only feeds keys/values and GLA states"
    layer = 0

    c8 = jnp.zeros((8, d), F32).at[0:b].set(c).at[b].set(c_ctx)
    mod = _ada(c8, w_ada[layer], b_ada[layer])
    sh1, sc1, gt1, sh2, sc2, gt2 = [mod[0:b, i * d:(i + 1) * d].reshape(b, 1, d) for i in range(6)]
    sh1c, sc1c = [jnp.broadcast_to(mod[b, i * d:(i + 1) * d].reshape(1, 1, d), (b, 1, d)) for i in range(2)]

    w_main, w_tail = _w_in_parts(jnp.swapaxes(w_in[layer], 0, 1))

    cos_t, sin_t = _rope_tables(t)
    g1 = norm1_g[layer].reshape(1, d)
    qg = q_norm_g[layer].reshape(1, HEAD_DIM)
    kg = k_norm_g[layer].reshape(1, HEAD_DIM)
    q, k, v, gq, gk, gv, og, lr = _in_proj(x, g1, sc1, sh1, w_main, w_tail, qg, kg, cos_t, sin_t, 512)
    ones_t = jnp.ones((tc, HEAD_DIM), F32)
    _, kc, vc, gqc, gkc, gvc, _, lrc = _in_proj(ctx, g1, sc1c, sh1c, w_main, w_tail, qg, kg,
                                                ones_t, jnp.zeros_like(ones_t), tc)

    attn = _attn(q, jnp.concatenate([kc, k], axis=1), jnp.concatenate([vc, v], axis=1), 256)
    w2s, bias2 = _gate_weights(gla_gate_w2[layer], gla_gate_b[layer])
    gla = _gla(gq, gk, gv, lr, og, gqc, gkc, gvc, lrc, w2s, bias2, gla_norm_g[layer].reshape(1, GLA_DV))

    wr = jnp.concatenate([router_grp_w[layer], router_exp_w[layer],
                          jnp.zeros((d, LANES - N_GROUPS - N_EXPERTS), F32)], axis=1)
    br = jnp.concatenate([router_grp_b[layer], router_exp_b[layer],
                          jnp.zeros((LANES - N_GROUPS - N_EXPERTS,), F32)]).reshape(1, LANES)
    wr_hi = wr.astype(BF16)
    wr_parts = jnp.concatenate([wr_hi, (wr - wr_hi.astype(F32)).astype(BF16)], axis=1)
    x1, h2t, ids, wts = _out_proj(attn, gla, w_out[layer].astype(BF16), x, gt1, norm2_g[layer].reshape(1, d),
                                  sc2, sh2, wr_parts, br, 512)

    m = b * t
    n_assign = m * TOP_K
    nb = -(-(n_assign + N_EXPERTS * (MOE_BLOCK - 1)) // MOE_BLOCK)
    eid_flat = ids[:, :, 0:TOP_K].reshape(n_assign)
    plan, pos = _block_plan(eid_flat, nb)
    ys = _moe(h2t.reshape(-1, LANES), moe_w1[layer], moe_w3[layer], moe_w2[layer], *plan)
    return _combine(x1, ys, pos, wts, gt2, 256)
```

```python
import functools

import jax
import jax.numpy as jnp
from jax import lax
from jax.experimental import pallas as pl
from jax.experimental.pallas import tpu as pltpu

EPS = 1e-6
GRID_W = 64
ROPE_THETA = 10000.0

ATTN_HEADS = 8
ATTN_KV_HEADS = 2
HEAD_DIM = 128
GQA_GROUP = ATTN_HEADS // ATTN_KV_HEADS

GLA_HEADS = 4
GLA_DK = 128
GLA_DV = 256
GLA_GATE_RANK = 16
GLA_GATE_NORMALIZER = 16.0
GLA_CHUNK = 64
GLA_CHUNK_LOG2 = 6
GLA_PREP_TILE = 256
GLA_OUT_GROUP = 8

N_GROUPS = 8
EXPERTS_PER_GROUP = 8
N_EXPERTS = N_GROUPS * EXPERTS_PER_GROUP
TOP_K = 2
MOE_BLOCK = 256
ROW_GROUP_LOG2 = 3
ROW_GROUP = 1 << ROW_GROUP_LOG2
GATHER_AHEAD = 2

LOG2_E = 1.4426950408889634
LANES = 128
VMEM_LIMIT = 56 * 1024 * 1024

ATTN_Q_W = ATTN_HEADS * HEAD_DIM
ATTN_KV_W = ATTN_KV_HEADS * HEAD_DIM
GLA_K_W = GLA_HEADS * GLA_DK
GLA_V_W = GLA_HEADS * GLA_DV

BF16 = jnp.bfloat16
F32 = jnp.float32


def _cparams(*sem):
    return pltpu.CompilerParams(dimension_semantics=sem, vmem_limit_bytes=VMEM_LIMIT)


def _ada_kernel(c_ref, w_ref, b_ref, o_ref):
    c = c_ref[...]
    s = c * jax.nn.sigmoid(c)
    o_ref[...] = jnp.dot(s.astype(BF16), w_ref[...].astype(BF16), preferred_element_type=F32) + b_ref[...]


def _ada(c8, w, b):
    d, n = w.shape
    tn = 1024
    return pl.pallas_call(
        _ada_kernel,
        grid=(n // tn,),
        in_specs=[
            pl.BlockSpec((8, d), lambda j: (0, 0)),
            pl.BlockSpec((d, tn), lambda j: (0, j)),
            pl.BlockSpec((1, tn), lambda j: (0, j)),
        ],
        out_specs=pl.BlockSpec((8, tn), lambda j: (0, j)),
        out_shape=jax.ShapeDtypeStruct((8, n), F32),
        compiler_params=_cparams("arbitrary"),
        name="ada",
    )(c8, w, b.reshape(1, n))


def _w_main_kernel(wt_ref, o_ref):
    o_ref[...] = wt_ref[...].T.astype(BF16)


def _w_tail_kernel(wt_ref, o_ref):
    r = 2 * GLA_GATE_RANK
    o_ref[:, :GLA_V_W] = wt_ref[r:, :].T.astype(BF16)
    lowrank = jnp.concatenate([wt_ref[:r, :], jnp.zeros((LANES - r, wt_ref.shape[1]), F32)], axis=0)
    o_ref[:, GLA_V_W:] = lowrank.T.astype(BF16)


def _w_in_parts(wt):
    n, d = wt.shape
    n_main = n - GLA_V_W - 2 * GLA_GATE_RANK
    tn = 512
    main = pl.pallas_call(
        _w_main_kernel,
        grid=(n_main // tn,),
        in_specs=[pl.BlockSpec((tn, d), lambda j: (j, 0))],
        out_specs=pl.BlockSpec((d, tn), lambda j: (0, j)),
        out_shape=jax.ShapeDtypeStruct((d, n_main), BF16),
        compiler_params=_cparams("arbitrary"),
        name="w_in_main",
    )(wt)
    tail = pl.pallas_call(
        _w_tail_kernel,
        out_shape=jax.ShapeDtypeStruct((d, GLA_V_W + LANES), BF16),
        compiler_params=pltpu.CompilerParams(vmem_limit_bytes=VMEM_LIMIT),
        name="w_in_tail",
    )(wt[n_main:])
    return main, tail


def _swap32(y):
    lane = lax.broadcasted_iota(jnp.int32, y.shape, 1)
    return jnp.where((lane & 63) < 32, pltpu.roll(y, 96, 1), pltpu.roll(y, 32, 1))


def _head_norm_rope(a, g, cos, sin):
    y = a * lax.rsqrt(jnp.mean(a * a, axis=-1, keepdims=True) + EPS) * g
    return y * cos + _swap32(y) * sin


def _in_proj_kernel(x_ref, g1_ref, sc_ref, sh_ref, w_ref, wt_ref, qg_ref, kg_ref, cos_ref, sin_ref,
                    q_ref, k_ref, v_ref, gq_ref, gk_ref, gv_ref, og_ref, lr_ref):
    x = x_ref[0]
    xn = x * lax.rsqrt(jnp.mean(x * x, axis=-1, keepdims=True) + EPS)
    h = (xn * g1_ref[...] * (1.0 + sc_ref[0]) + sh_ref[0]).astype(BF16)
    cos = cos_ref[...]
    sin = sin_ref[...]

    def proj(off, width, w=w_ref):
        return jnp.dot(h, w[:, off:off + width], preferred_element_type=F32)

    qg = qg_ref[...] * (HEAD_DIM ** -0.5 * LOG2_E)
    off = 0
    for out_ref, gain, heads in ((q_ref, qg, ATTN_HEADS), (k_ref, kg_ref[...], ATTN_KV_HEADS)):
        for pair in range(heads // 2):
            a2 = proj(off, 2 * HEAD_DIM)
            for u in range(2):
                lo = (2 * pair + u) * HEAD_DIM
                a = a2[:, u * HEAD_DIM:(u + 1) * HEAD_DIM]
                out_ref[0, :, lo:lo + HEAD_DIM] = _head_norm_rope(a, gain, cos, sin).astype(BF16)
            off += 2 * HEAD_DIM
    v_ref[0] = proj(off, ATTN_KV_W).astype(BF16)
    off += ATTN_KV_W
    gq_ref[0] = proj(off, GLA_K_W)
    off += GLA_K_W
    gk_ref[0] = proj(off, GLA_K_W)
    off += GLA_K_W
    gv_ref[0] = proj(off, GLA_V_W).astype(BF16)
    og_ref[0] = proj(0, GLA_V_W, wt_ref)
    lr_ref[0] = proj(GLA_V_W, LANES, wt_ref)


def _in_proj(x, g1, sc, sh, w_main, w_tail, qg, kg, cos_t, sin_t, tm):
    b, t, d = x.shape
    row = lambda bi, i: (bi, i, 0)
    vec = lambda bi, i: (bi, 0, 0)
    const = lambda bi, i: (0, 0)
    tab = lambda bi, i: (i, 0)
    widths = [(ATTN_Q_W, BF16), (ATTN_KV_W, BF16), (ATTN_KV_W, BF16), (GLA_K_W, F32), (GLA_K_W, F32),
              (GLA_V_W, BF16), (GLA_V_W, F32), (LANES, F32)]
    return pl.pallas_call(
        _in_proj_kernel,
        grid=(b, t // tm),
        in_specs=[
            pl.BlockSpec((1, tm, d), row),
            pl.BlockSpec((1, d), const),
            pl.BlockSpec((1, 1, d), vec),
            pl.BlockSpec((1, 1, d), vec),
            pl.BlockSpec(w_main.shape, const, pipeline_mode=pl.Buffered(1)),
            pl.BlockSpec(w_tail.shape, const, pipeline_mode=pl.Buffered(1)),
            pl.BlockSpec((1, HEAD_DIM), const),
            pl.BlockSpec((1, HEAD_DIM), const),
            pl.BlockSpec((tm, HEAD_DIM), tab),
            pl.BlockSpec((tm, HEAD_DIM), tab),
        ],
        out_specs=[pl.BlockSpec((1, tm, wd), row) for wd, _ in widths],
        out_shape=[jax.ShapeDtypeStruct((b, t, wd), dt) for wd, dt in widths],
        compiler_params=_cparams("arbitrary", "arbitrary"),
        name="in_proj",
    )(x, g1, sc, sh, w_main, w_tail, qg, kg, cos_t, sin_t)


def _attn_kernel(q_ref, k_ref, v_ref, o_ref):
    def scores(h):
        hk = h // GQA_GROUP
        q = q_ref[0, :, h * HEAD_DIM:(h + 1) * HEAD_DIM]
        k = k_ref[0, :, hk * HEAD_DIM:(hk + 1) * HEAD_DIM]
        return lax.dot_general(q, k, (((1,), (1,)), ((), ())), preferred_element_type=F32)

    s_next = scores(0)
    for h in range(ATTN_HEADS):
        hk = h // GQA_GROUP
        s = s_next
        if h + 1 < ATTN_HEADS:
            s_next = scores(h + 1)
        p = jnp.exp2(s - jnp.max(s, axis=-1, keepdims=True))
        ol = jnp.dot(p.astype(BF16), v_ref[0, :, 2 * hk * HEAD_DIM:2 * (hk + 1) * HEAD_DIM],
                     preferred_element_type=F32)
        o_ref[0, :, h * HEAD_DIM:(h + 1) * HEAD_DIM] = (
            ol[:, :HEAD_DIM] / ol[:, HEAD_DIM:HEAD_DIM + 1]).astype(BF16)


def _attn(q, k_all, v_all, tq):
    b, t, _ = q.shape
    tk = k_all.shape[1]
    return pl.pallas_call(
        _attn_kernel,
        grid=(b, t // tq),
        in_specs=[
            pl.BlockSpec((1, tq, ATTN_Q_W), lambda bi, i: (bi, i, 0)),
            pl.BlockSpec((1, tk, ATTN_KV_W), lambda bi, i: (bi, 0, 0)),
            pl.BlockSpec((1, tk, 2 * ATTN_KV_W), lambda bi, i: (bi, 0, 0)),
        ],
        out_specs=pl.BlockSpec((1, tq, ATTN_Q_W), lambda bi, i: (bi, i, 0)),
        out_shape=jax.ShapeDtypeStruct((b, t, ATTN_Q_W), BF16),
        compiler_params=_cparams("arbitrary", "arbitrary"),
        name="attn",
    )(q, k_all, v_all)


def _log_sigmoid(x):
    return jnp.minimum(x, 0.0) - jnp.log(1.0 + jnp.exp(-jnp.abs(x)))


def _split3(x):
    hi = x.astype(BF16)
    r = x - hi.astype(F32)
    mid = r.astype(BF16)
    lo = (r - mid.astype(F32)).astype(BF16)
    return hi, mid, lo


def _dot_exact_lhs(m, x):
    return sum(jnp.dot(m, part, preferred_element_type=F32) for part in _split3(x))


def _prefix_operator(n):
    ri = lax.broadcasted_iota(jnp.int32, (n, n), 0)
    ci = lax.broadcasted_iota(jnp.int32, (n, n), 1)
    same = lax.shift_right_logical(ri, GLA_CHUNK_LOG2) == lax.shift_right_logical(ci, GLA_CHUNK_LOG2)
    return (same & (ci <= ri)).astype(BF16)


def _gate_prep(q, k, lr, w2s, bias2, prefix):
    c = GLA_CHUNK
    hi = lr.astype(BF16).astype(F32)
    mid = (lr - hi).astype(BF16).astype(F32)
    lhs = (hi + pltpu.roll(mid, 2 * GLA_GATE_RANK, 1) + pltpu.roll(hi, 4 * GLA_GATE_RANK, 1)).astype(BF16)
    logits = jnp.dot(lhs, w2s, preferred_element_type=F32) + bias2
    g = _log_sigmoid(logits) * (1.0 / GLA_GATE_NORMALIZER)
    pre = _dot_exact_lhs(prefix, g)
    tot = jnp.concatenate([jnp.broadcast_to(pre[lo + c - 1:lo + c, :], (c, pre.shape[1]))
                           for lo in range(0, pre.shape[0], c)], axis=0)
    dk = GLA_DK
    bcs = (pre[:, :dk], tot[:, dk:] - pre[:, dk:] + g[:, dk:])
    out = []
    for d, bc in enumerate(bcs):
        b_end = tot[:, d * dk:(d + 1) * dk]
        qe = (q * (GLA_DK ** -0.5) * jnp.exp(bc)).astype(BF16)
        ke = (k * jnp.exp(-bc)).astype(BF16)
        kend = (k * jnp.exp(b_end - bc)).astype(BF16)
        out.append((qe, ke, kend, b_end))
    return out


def _state_step(s_ref, v, kend, dec):
    upd = lax.dot_general(v, kend, (((0,), (0,)), ((), ())), preferred_element_type=F32)
    s_ref[...] = s_ref[...] * dec + upd


def _gla_kernel(q_ref, k_ref, v_ref, lr_ref, og_ref, qc_ref, kc_ref, vc_ref, lrc_ref,
                w2_ref, b_ref, ng_ref, o_ref,
                s_ref, qe_ref, ke_ref, kend_ref, dec_ref, kendc_ref, decc_ref, sbf_ref):
    c = GLA_CHUNK
    t = q_ref.shape[1]
    tc = qc_ref.shape[1]
    nc = t // c
    tile = GLA_PREP_TILE
    cpt = tile // c
    dirs = (0, 1)

    def store_dec(ref, d, base, b_end):
        for ch in range(b_end.shape[0] // c):
            ref[d, pl.ds(base + ch, 1), :] = jnp.exp(b_end[ch * c:ch * c + 1, :])

    prefix = _prefix_operator(tile)

    for i in range(tc // tile):
        rows = slice(i * tile, (i + 1) * tile)
        prep_c = _gate_prep(qc_ref[0, rows, :], kc_ref[0, rows, :], lrc_ref[0, rows, :],
                            w2_ref[...], b_ref[...], prefix)
        for d, (_, _, kend, b_end) in enumerate(prep_c):
            kendc_ref[d, rows, :] = kend
            store_dec(decc_ref, d, i * cpt, b_end)
    s_ref[...] = jnp.zeros_like(s_ref)
    for j in range(tc // c):
        for d in dirs:
            ch = j if d == 0 else tc // c - 1 - j
            _state_step(s_ref.at[d], vc_ref[0, ch * c:(ch + 1) * c, :], kendc_ref[d, ch * c:(ch + 1) * c, :],
                        decc_ref[d, ch:ch + 1, :])

    def prep(i, carry):
        lo = pl.multiple_of(i * tile, tile)
        rows = pl.ds(lo, tile)
        prep_l = _gate_prep(q_ref[0, rows, :], k_ref[0, rows, :], lr_ref[0, rows, :],
                            w2_ref[...], b_ref[...], prefix)
        for d, (qe, ke, kend, b_end) in enumerate(prep_l):
            qe_ref[d, rows, :] = qe
            ke_ref[d, rows, :] = ke
            kend_ref[d, rows, :] = kend
            store_dec(dec_ref, d, i * cpt, b_end)
        return carry

    lax.fori_loop(0, t // tile, prep, 0, unroll=2)

    def scan(j, carry):
        for d in dirs:
            ch = j if d == 0 else nc - 1 - j
            rows = pl.ds(pl.multiple_of(ch * c, c), c)
            sbf_ref[d, ch] = s_ref[d].astype(BF16)
            _state_step(s_ref.at[d], v_ref[0, rows, :], kend_ref[d, rows, :], dec_ref[d, pl.ds(ch, 1), :])
        return carry

    lax.fori_loop(0, nc, scan, 0, unroll=8)

    ri = lax.broadcasted_iota(jnp.int32, (c, c), 0)
    ci = lax.broadcasted_iota(jnp.int32, (c, c), 1)
    masks = (ci <= ri, ci >= ri)

    group = GLA_OUT_GROUP

    def out(i, carry):
        chunks = [i * group + u for u in range(group)]
        rows = [pl.ds(pl.multiple_of(ch * c, c), c) for ch in chunks]
        qes = [[qe_ref[d, r, :] for d in dirs] for r in rows]
        scores = [[lax.dot_general(qes[u][d], ke_ref[d, rows[u], :], (((1,), (1,)), ((), ())),
                                   preferred_element_type=F32) for d in dirs] for u in range(group)]
        inter = [[lax.dot_general(qes[u][d], sbf_ref[d, chunks[u]], (((1,), (1,)), ((), ())),
                                  preferred_element_type=F32) for d in dirs] for u in range(group)]
        for u in range(group):
            v = v_ref[0, rows[u], :]
            o = inter[u][0] + inter[u][1]
            for d in dirs:
                a = jnp.where(masks[d], scores[u][d], 0.0).astype(BF16)
                o = o + jnp.dot(a, v, preferred_element_type=F32)
            on = o * lax.rsqrt(jnp.mean(o * o, axis=-1, keepdims=True) + EPS) * ng_ref[...]
            og = og_ref[0, rows[u], :]
            o_ref[0, rows[u], :] = (on * (og * jax.nn.sigmoid(og))).astype(BF16)
        return carry

    lax.fori_loop(0, nc // group, out, 0)


def _gla(gq, gk, gv, lr, og, gqc, gkc, gvc, lrc, w2, bias, ng):
    b, t, _ = gq.shape
    tc = gqc.shape[1]
    hk = lambda bi, h: (bi, 0, h)
    h0 = lambda bi, h: (bi, 0, 0)
    return pl.pallas_call(
        _gla_kernel,
        grid=(b, GLA_HEADS),
        in_specs=[
            pl.BlockSpec((1, t, GLA_DK), hk),
            pl.BlockSpec((1, t, GLA_DK), hk),
            pl.BlockSpec((1, t, GLA_DV), hk),
            pl.BlockSpec((1, t, LANES), h0),
            pl.BlockSpec((1, t, GLA_DV), hk),
            pl.BlockSpec((1, tc, GLA_DK), hk),
            pl.BlockSpec((1, tc, GLA_DK), hk),
            pl.BlockSpec((1, tc, GLA_DV), hk),
            pl.BlockSpec((1, tc, LANES), h0),
            pl.BlockSpec((None, LANES, 2 * GLA_DK), lambda bi, h: (h, 0, 0)),
            pl.BlockSpec((None, 1, 2 * GLA_DK), lambda bi, h: (h, 0, 0)),
            pl.BlockSpec((1, GLA_DV), lambda bi, h: (0, 0)),
        ],
        out_specs=pl.BlockSpec((1, t, GLA_DV), hk),
        out_shape=jax.ShapeDtypeStruct((b, t, GLA_V_W), BF16),
        scratch_shapes=[
            pltpu.VMEM((2, GLA_DV, GLA_DK), F32),
            pltpu.VMEM((2, t, GLA_DK), BF16),
            pltpu.VMEM((2, t, GLA_DK), BF16),
            pltpu.VMEM((2, t, GLA_DK), BF16),
            pltpu.VMEM((2, t // GLA_CHUNK, GLA_DK), F32),
            pltpu.VMEM((2, tc, GLA_DK), BF16),
            pltpu.VMEM((2, tc // GLA_CHUNK, GLA_DK), F32),
            pltpu.VMEM((2, t // GLA_CHUNK, GLA_DV, GLA_DK), BF16),
        ],
        compiler_params=_cparams("arbitrary", "arbitrary"),
        name="gla",
    )(gq, gk, gv, lr, og, gqc, gkc, gvc, lrc, w2, bias, ng)


def _token_pitch(d):
    rows = d // LANES
    return rows + 4 if rows % 8 == 0 else rows


def _store_token_tiles(ref, x):
    n = x.shape[0]
    k = x.shape[1] // LANES
    pitch = ref.shape[0] // n
    for s in range(pitch):
        slab = x[:, s * LANES:(s + 1) * LANES] if s < k else jnp.zeros((n, LANES), x.dtype)
        ref[pl.ds(s, n, stride=pitch), :] = slab


def _load_token_tiles(ref, n, d):
    pitch = ref.shape[0] // n
    return jnp.concatenate([ref[pl.ds(s, n, stride=pitch), :] for s in range(d // LANES)], axis=1)


def _out_proj_kernel(attn_ref, gla_ref, wo_ref, x_ref, gt_ref, g2_ref, sc_ref, sh_ref, wr_ref, br_ref,
                     x1_ref, h2_ref, ids_ref, wts_ref):
    y = jnp.dot(attn_ref[0], wo_ref[0:ATTN_Q_W, :], preferred_element_type=F32)
    y = y + jnp.dot(gla_ref[0], wo_ref[ATTN_Q_W:, :], preferred_element_type=F32)
    x1 = x_ref[0] + gt_ref[0] * y
    x1_ref[0] = x1
    xn = x1 * lax.rsqrt(jnp.mean(x1 * x1, axis=-1, keepdims=True) + EPS)
    h2 = xn * g2_ref[...] * (1.0 + sc_ref[0]) + sh_ref[0]
    hi = h2.astype(BF16)
    hi_f = hi.astype(F32)

    _store_token_tiles(h2_ref.at[0], h2)

    mid = (h2 - hi_f).astype(BF16)
    both = jnp.dot(hi, wr_ref[...], preferred_element_type=F32)
    logits = (both[:, :LANES] + both[:, LANES:]
              + jnp.dot(mid, wr_ref[:, :LANES], preferred_element_type=F32)) + br_ref[...]
    lane = lax.broadcasted_iota(jnp.int32, logits.shape, 1)
    lane_f = lane.astype(F32)
    neg = jnp.float32(-jnp.inf)

    def first_argmax(vals):
        m = jnp.max(vals, axis=-1, keepdims=True)
        idx = jnp.min(jnp.where(vals == m, lane_f, float(LANES)), axis=-1, keepdims=True)
        return m, idx

    lg = jnp.where(lane < N_GROUPS, logits, neg)
    mg, grp = first_argmax(lg)
    pg_sel = 1.0 / jnp.sum(jnp.exp(lg - mg), axis=-1, keepdims=True)
    lo = N_GROUPS + grp * EXPERTS_PER_GROUP
    in_grp = (lane_f >= lo) & (lane_f < lo + EXPERTS_PER_GROUP)
    le = jnp.where(in_grp, logits, neg)
    v1, i1 = first_argmax(le)
    v2, i2 = first_argmax(jnp.where(lane_f == i1, neg, le))
    e2 = jnp.exp(v2 - v1)
    w1 = pg_sel / (1.0 + e2)
    w2 = pg_sel * e2 / (1.0 + e2)
    ids = jnp.where(lane == 0, i1 - N_GROUPS, jnp.where(lane == 1, i2 - N_GROUPS, 0.0))
    ids_ref[0] = ids.astype(jnp.int32)
    wts_ref[0] = jnp.where(lane == 0, w1, jnp.where(lane == 1, w2, 0.0))


def _out_proj(attn, gla, wo, x, gt1, g2, sc2, sh2, wr, br, tm):
    b, t, d = x.shape
    row = lambda bi, i: (bi, i, 0)
    vec = lambda bi, i: (bi, 0, 0)
    const = lambda bi, i: (0, 0)
    return pl.pallas_call(
        _out_proj_kernel,
        grid=(b, t // tm),
        in_specs=[
            pl.BlockSpec((1, tm, ATTN_Q_W), row),
            pl.BlockSpec((1, tm, GLA_V_W), row),
            pl.BlockSpec(wo.shape, const, pipeline_mode=pl.Buffered(1)),
            pl.BlockSpec((1, tm, d), row),
            pl.BlockSpec((1, 1, d), vec),
            pl.BlockSpec((1, d), const),
            pl.BlockSpec((1, 1, d), vec),
            pl.BlockSpec((1, 1, d), vec),
            pl.BlockSpec((d, 2 * LANES), const),
            pl.BlockSpec((1, LANES), const),
        ],
        out_specs=[
            pl.BlockSpec((1, tm, d), row),
            pl.BlockSpec((1, tm * _token_pitch(d), LANES), row),
            pl.BlockSpec((1, tm, LANES), row),
            pl.BlockSpec((1, tm, LANES), row),
        ],
        out_shape=[
            jax.ShapeDtypeStruct((b, t, d), F32),
            jax.ShapeDtypeStruct((b, t * _token_pitch(d), LANES), F32),
            jax.ShapeDtypeStruct((b, t, LANES), jnp.int32),
            jax.ShapeDtypeStruct((b, t, LANES), F32),
        ],
        compiler_params=_cparams("arbitrary", "arbitrary"),
        name="out_proj",
    )(attn, gla, wo, x, gt1, g2, sc2, sh2, wr, br)


def _moe_kernel(eidx_ref, eblk0_ref, enb_ref, bs0_ref, bn_ref, nblk_ref, order_ref,
                h2_hbm, w1_ref, w3_ref, w2_ref, y_hbm,
                xbuf, ybuf, w1b, w3b, w2b, gsem, ssem):
    del eidx_ref
    e = pl.program_id(0)
    nblk = nblk_ref[0]
    kx = xbuf.shape[1] // MOE_BLOCK
    ky = ybuf.shape[1] // MOE_BLOCK

    def gather_copy(sl, hbm_row, j, rows=1):
        return pltpu.make_async_copy(h2_hbm.at[pl.ds(hbm_row * kx, rows * kx)],
                                     xbuf.at[sl, pl.ds(j * kx, rows * kx)], gsem.at[sl])

    def for_rows(n, per_group, per_row):
        ng = lax.shift_right_logical(n, ROW_GROUP_LOG2)
        lax.fori_loop(0, ng, lambda g, c: (per_group(g * ROW_GROUP), c)[1], 0)
        lax.fori_loop(ng * ROW_GROUP, n, lambda j, c: (per_row(j), c)[1], 0)

    def start_gather(blk, sl):
        s = bs0_ref[blk]

        def start_row(j):
            gather_copy(sl, lax.shift_right_logical(order_ref[s + j], 1), j).start()

        def group(j0):
            for u in range(ROW_GROUP):
                start_row(j0 + u)

        for_rows(bn_ref[blk], group, start_row)

    def wait_gather(blk, sl):
        for_rows(bn_ref[blk], lambda j0: gather_copy(sl, 0, 0, ROW_GROUP).wait(),
                 lambda j: gather_copy(sl, 0, 0).wait())

    def output_copy(blk, sl):
        return pltpu.make_async_copy(
            ybuf.at[sl], y_hbm.at[pl.ds(bs0_ref[blk] * ky, MOE_BLOCK * ky)], ssem.at[sl])

    @pl.when(e == 0)
    def _():
        xbuf[...] = jnp.zeros_like(xbuf)
        for g0 in range(GATHER_AHEAD):
            @pl.when(g0 < nblk)
            def _():
                start_gather(g0, g0)

    def block(g, carry):
        slot = g & 1
        xslot = lax.rem(g, GATHER_AHEAD + 1)

        @pl.when(g + GATHER_AHEAD < nblk)
        def _():
            start_gather(g + GATHER_AHEAD, lax.rem(g + GATHER_AHEAD, GATHER_AHEAD + 1))

        wait_gather(g, xslot)
        x = _load_token_tiles(xbuf.at[xslot], MOE_BLOCK, w1b.shape[0]).astype(BF16)
        a = jnp.dot(x, w1b[...], preferred_element_type=F32)
        gate = jnp.dot(x, w3b[...], preferred_element_type=F32)
        hid = (a * jax.nn.sigmoid(a) * gate).astype(BF16)
        y = jnp.dot(hid, w2b[...], preferred_element_type=F32)

        _store_token_tiles(ybuf.at[slot], y)

        @pl.when(g >= 1)
        def _():
            output_copy(g - 1, 1 - slot).wait()

        output_copy(g, slot).start()
        return carry

    @pl.when(enb_ref[e] > 0)
    def _():
        w1b[...] = w1_ref[...].astype(BF16)
        w3b[...] = w3_ref[...].astype(BF16)
        w2b[...] = w2_ref[...].astype(BF16)
        lax.fori_loop(eblk0_ref[e], eblk0_ref[e] + enb_ref[e], block, 0)

    @pl.when(e == pl.num_programs(0) - 1)
    def _():
        output_copy(nblk - 1, (nblk - 1) & 1).wait()
        ybuf[0] = jnp.zeros_like(ybuf[0])
        pad_rows = MOE_BLOCK * ky
        pad = pltpu.make_async_copy(ybuf.at[0], y_hbm.at[pl.ds(y_hbm.shape[0] - pad_rows, pad_rows)], ssem.at[0])
        pad.start()
        pad.wait()


def _moe(h2t, w1, w3, w2, eidx, eblk0, enb, bs0, bn, nblk, order):
    ne, d, ff = w1.shape
    kx = ky = _token_pitch(d)
    wmap = lambda e, eidx, *_: (eidx[e], 0, 0)
    grid_spec = pltpu.PrefetchScalarGridSpec(
        num_scalar_prefetch=7,
        grid=(ne,),
        in_specs=[
            pl.BlockSpec(memory_space=pl.ANY),
            pl.BlockSpec((None, d, ff), wmap),
            pl.BlockSpec((None, d, ff), wmap),
            pl.BlockSpec((None, ff, d), wmap),
        ],
        out_specs=pl.BlockSpec(memory_space=pl.ANY),
        scratch_shapes=[
            pltpu.VMEM((GATHER_AHEAD + 1, MOE_BLOCK * kx, LANES), F32),
            pltpu.VMEM((2, MOE_BLOCK * ky, LANES), F32),
            pltpu.VMEM((d, ff), BF16),
            pltpu.VMEM((d, ff), BF16),
            pltpu.VMEM((ff, d), BF16),
            pltpu.SemaphoreType.DMA((GATHER_AHEAD + 1,)),
            pltpu.SemaphoreType.DMA((2,)),
        ],
    )
    return pl.pallas_call(
        _moe_kernel,
        grid_spec=grid_spec,
        out_shape=jax.ShapeDtypeStruct(((order.shape[0] + MOE_BLOCK) * ky, LANES), F32),
        compiler_params=_cparams("arbitrary"),
        name="moe",
    )(eidx, eblk0, enb, bs0, bn, nblk, order, h2t, w1, w3, w2)


def _combine_kernel(pos_ref, x1_ref, ys_hbm, wts_ref, gt_ref, o_ref, ybuf, sem):
    tm = x1_ref.shape[1]
    ky = ybuf.shape[2] // tm
    nt = pl.num_programs(1)
    n = pl.program_id(0) * nt + pl.program_id(1)
    slot = n & 1

    def row_copy(sl, k, pos, j, rows=1):
        return pltpu.make_async_copy(ys_hbm.at[pl.ds(pos * ky, rows * ky)],
                                     ybuf.at[sl, k, pl.ds(j * ky, rows * ky)], sem.at[sl])

    def start_tile(tile, sl):
        base = tile * (tm * TOP_K)

        def group(g, carry):
            for u in range(ROW_GROUP):
                row_copy(sl, u % TOP_K, pos_ref[base + g * ROW_GROUP + u],
                         g * (ROW_GROUP // TOP_K) + u // TOP_K).start()
            return carry

        lax.fori_loop(0, tm * TOP_K // ROW_GROUP, group, 0)

    @pl.when(n == 0)
    def _():
        start_tile(0, 0)

    @pl.when(n + 1 < pl.num_programs(0) * nt)
    def _():
        start_tile(n + 1, 1 - slot)

    for k in range(TOP_K):
        row_copy(slot, k, 0, 0, rows=tm).wait()
    w = wts_ref[0]
    d = x1_ref.shape[2]
    ff = (w[:, 0:1] * _load_token_tiles(ybuf.at[slot, 0], tm, d)
          + w[:, 1:2] * _load_token_tiles(ybuf.at[slot, 1], tm, d))
    o_ref[0] = x1_ref[0] + gt_ref[0] * ff


def _combine(x1, ys, pos, wts, gt2, tm):
    b, t, d = x1.shape
    ky = _token_pitch(d)
    row = lambda bi, i, pos: (bi, i, 0)
    vec = lambda bi, i, pos: (bi, 0, 0)
    grid_spec = pltpu.PrefetchScalarGridSpec(
        num_scalar_prefetch=1,
        grid=(b, t // tm),
        in_specs=[
            pl.BlockSpec((1, tm, d), row),
            pl.BlockSpec(memory_space=pl.ANY),
            pl.BlockSpec((1, tm, LANES), row),
            pl.BlockSpec((1, 1, d), vec),
        ],
        out_specs=pl.BlockSpec((1, tm, d), row),
        scratch_shapes=[
            pltpu.VMEM((2, TOP_K, tm * ky, LANES), F32),
            pltpu.SemaphoreType.DMA((2,)),
        ],
    )
    return pl.pallas_call(
        _combine_kernel,
        grid_spec=grid_spec,
        out_shape=jax.ShapeDtypeStruct((b, t, d), F32),
        compiler_params=_cparams("arbitrary", "arbitrary"),
        name="combine",
    )(pos, x1, ys, wts, gt2)


def _rope_tables(t):
    rows = t // GRID_W
    n_freq = HEAD_DIM // 4
    inv = ROPE_THETA ** (-jnp.arange(n_freq, dtype=F32) / n_freq)
    ar = jnp.arange(rows, dtype=F32)[:, None] * inv
    ac = jnp.arange(GRID_W, dtype=F32)[:, None] * inv

    def expand(fr, fc, sign):
        by_row = jnp.broadcast_to(fr[:, None, :], (rows, GRID_W, n_freq))
        by_col = jnp.broadcast_to(fc[None, :, :], (rows, GRID_W, n_freq))
        return jnp.concatenate([sign * by_row, by_row, sign * by_col, by_col], axis=2).reshape(t, HEAD_DIM)

    return expand(jnp.cos(ar), jnp.cos(ac), 1.0), expand(jnp.sin(ar), jnp.sin(ac), -1.0)


def _gate_weights(w2, bias):
    r = GLA_GATE_RANK
    wh = w2.reshape(2, r, GLA_HEADS, GLA_DK).transpose(2, 0, 1, 3)
    w = jnp.zeros((GLA_HEADS, 2 * r, 2 * GLA_DK), F32)
    w = w.at[:, 0:r, 0:GLA_DK].set(wh[:, 0]).at[:, r:2 * r, GLA_DK:].set(wh[:, 1])
    hi = w.astype(BF16)
    mid = (w - hi.astype(F32)).astype(BF16)
    pad = jnp.zeros((GLA_HEADS, LANES - 6 * r, 2 * GLA_DK), BF16)
    w2s = jnp.concatenate([hi, hi, mid, pad], axis=1)
    bias2 = bias.reshape(2, GLA_HEADS, GLA_DK).transpose(1, 0, 2).reshape(GLA_HEADS, 1, 2 * GLA_DK)
    return w2s, bias2


def _block_plan(eid_flat, nb):
    order = jnp.argsort(eid_flat).astype(jnp.int32)
    eids = jnp.arange(N_EXPERTS, dtype=jnp.int32)
    of_assign = (eid_flat[:, None] == eids[None, :]).astype(jnp.int32)
    counts = jnp.sum(of_assign, axis=0)
    starts = jnp.cumsum(counts) - counts
    nblk_e = (counts + MOE_BLOCK - 1) // MOE_BLOCK
    bends = jnp.cumsum(nblk_e)
    bstarts = bends - nblk_e
    nblk = bends[-1]
    prev_used = lax.cummax(jnp.where(counts > 0, eids, -1))
    eidx = jnp.where(prev_used >= 0, prev_used, jnp.argmax(counts > 0)).astype(jnp.int32)
    bi = jnp.arange(nb, dtype=jnp.int32)
    bic = jnp.minimum(bi, jnp.maximum(nblk - 1, 0))
    bexp = jnp.minimum(jnp.sum(bends[None, :] <= bic[:, None], axis=1), N_EXPERTS - 1)
    of_block = (bexp[:, None] == eids[None, :]).astype(jnp.int32)

    def lookup(per_expert):
        return jnp.sum(of_block * per_expert[None, :], axis=1)

    r0 = (bic - lookup(bstarts)) * MOE_BLOCK
    bs0 = (lookup(starts) + r0).astype(jnp.int32)
    bn = jnp.where(bi < nblk, jnp.minimum(lookup(counts) - r0, MOE_BLOCK), 0).astype(jnp.int32)
    pos = jnp.argsort(order).astype(jnp.int32)
    plan = (eidx, bstarts.astype(jnp.int32), nblk_e.astype(jnp.int32), bs0, bn,
            nblk.reshape(1).astype(jnp.int32), order)
    return plan, pos


def kernel(x, c, ctx, c_ctx, w_ada, b_ada, norm1_g, w_in, q_norm_g, k_norm_g, gla_gate_w2, gla_gate_b, gla_norm_g, w_out, norm2_g, router_grp_w, router_grp_b, router_exp_w, router_exp_b, moe_w1, moe_w3, moe_w2):
    b, t, d = x.shape
    tc = ctx.shape[1]
    depth = w_ada.shape[0]
    assert depth == 1, "single-layer stack: the context stream only feeds keys/values and GLA states"
    layer = 0

    c8 = jnp.zeros((8, d), F32).at[0:b].set(c).at[b].set(c_ctx)
    mod = _ada(c8, w_ada[layer], b_ada[layer])
    sh1, sc1, gt1, sh2, sc2, gt2 = [mod[0:b, i * d:(i + 1) * d].reshape(b, 1, d) for i in range(6)]
    sh1c, sc1c = [jnp.broadcast_to(mod[b, i * d:(i + 1) * d].reshape(1, 1, d), (b, 1, d)) for i in range(2)]

    w_main, w_tail = _w_in_parts(jnp.swapaxes(w_in[layer], 0, 1))

    cos_t, sin_t = _rope_tables(t)
    g1 = norm1_g[layer].reshape(1, d)
    qg = q_norm_g[layer].reshape(1, HEAD_DIM)
    kg = k_norm_g[layer].reshape(1, HEAD_DIM)
    q, k, v, gq, gk, gv, og, lr = _in_proj(x, g1, sc1, sh1, w_main, w_tail, qg, kg, cos_t, sin_t, 512)
    ones_t = jnp.ones((tc, HEAD_DIM), F32)
    _, kc, vc, gqc, gkc, gvc, _, lrc = _in_proj(ctx, g1, sc1c, sh1c, w_main, w_tail, qg, kg,
                                                ones_t, jnp.zeros_like(ones_t), tc)

    v_all = jnp.concatenate([vc, v], axis=1)
    keys = v_all.shape[1]
    v_ones = jnp.concatenate([v_all.reshape(b, keys, ATTN_KV_HEADS, HEAD_DIM),
                              jnp.ones((b, keys, ATTN_KV_HEADS, HEAD_DIM), BF16)], axis=3)
    attn = _attn(q, jnp.concatenate([kc, k], axis=1), v_ones.reshape(b, keys, 2 * ATTN_KV_W), 256)
    w2s, bias2 = _gate_weights(gla_gate_w2[layer], gla_gate_b[layer])
    gla = _gla(gq, gk, gv, lr, og, gqc, gkc, gvc, lrc, w2s, bias2, gla_norm_g[layer].reshape(1, GLA_DV))

    wr = jnp.concatenate([router_grp_w[layer], router_exp_w[layer],
                          jnp.zeros((d, LANES - N_GROUPS - N_EXPERTS), F32)], axis=1)
    br = jnp.concatenate([router_grp_b[layer], router_exp_b[layer],
                          jnp.zeros((LANES - N_GROUPS - N_EXPERTS,), F32)]).reshape(1, LANES)
    wr_hi = wr.astype(BF16)
    wr_parts = jnp.concatenate([wr_hi, (wr - wr_hi.astype(F32)).astype(BF16)], axis=1)
    x1, h2t, ids, wts = _out_proj(attn, gla, w_out[layer].astype(BF16), x, gt1, norm2_g[layer].reshape(1, d),
                                  sc2, sh2, wr_parts, br, 512)

    m = b * t
    n_assign = m * TOP_K
    nb = -(-(n_assign + N_EXPERTS * (MOE_BLOCK - 1)) // MOE_BLOCK)
    eid_flat = ids[:, :, 0:TOP_K].reshape(n_assign)
    plan, pos = _block_plan(eid_flat, nb)
    ys = _moe(h2t.reshape(-1, LANES), moe_w1[layer], moe_w3[layer], moe_w2[layer], *plan)
    return _combine(x1, ys, pos, wts, gt2, 256)
```

```python
import functools

import jax
import jax.numpy as jnp
from jax import lax
from jax.experimental import pallas as pl
from jax.experimental.pallas import tpu as pltpu

EPS = 1e-6
GRID_W = 64
ROPE_THETA = 10000.0

ATTN_HEADS = 8
ATTN_KV_HEADS = 2
HEAD_DIM = 128
GQA_GROUP = ATTN_HEADS // ATTN_KV_HEADS

GLA_HEADS = 4
GLA_DK = 128
GLA_DV = 256
GLA_GATE_RANK = 16
GLA_GATE_NORMALIZER = 16.0
GLA_CHUNK = 64
GLA_CHUNK_LOG2 = 6
GLA_PREP_TILE = 256
GLA_OUT_GROUP = 8

N_GROUPS = 8
EXPERTS_PER_GROUP = 8
N_EXPERTS = N_GROUPS * EXPERTS_PER_GROUP
TOP_K = 2
MOE_BLOCK = 256
ROW_GROUP_LOG2 = 3
ROW_GROUP = 1 << ROW_GROUP_LOG2
GATHER_AHEAD = 2

LOG2_E = 1.4426950408889634
LANES = 128
VMEM_LIMIT = 56 * 1024 * 1024

ATTN_Q_W = ATTN_HEADS * HEAD_DIM
ATTN_KV_W = ATTN_KV_HEADS * HEAD_DIM
GLA_K_W = GLA_HEADS * GLA_DK
GLA_V_W = GLA_HEADS * GLA_DV

BF16 = jnp.bfloat16
F32 = jnp.float32


def _cparams(*sem):
    return pltpu.CompilerParams(dimension_semantics=sem, vmem_limit_bytes=VMEM_LIMIT)


def _ada_kernel(c_ref, w_ref, b_ref, o_ref):
    c = c_ref[...]
    s = c * jax.nn.sigmoid(c)
    o_ref[...] = jnp.dot(s.astype(BF16), w_ref[...].astype(BF16), preferred_element_type=F32) + b_ref[...]


def _ada(c8, w, b):
    d, n = w.shape
    tn = 1024
    return pl.pallas_call(
        _ada_kernel,
        grid=(n // tn,),
        in_specs=[
            pl.BlockSpec((8, d), lambda j: (0, 0)),
            pl.BlockSpec((d, tn), lambda j: (0, j)),
            pl.BlockSpec((1, tn), lambda j: (0, j)),
        ],
        out_specs=pl.BlockSpec((8, tn), lambda j: (0, j)),
        out_shape=jax.ShapeDtypeStruct((8, n), F32),
        compiler_params=_cparams("arbitrary"),
        name="ada",
    )(c8, w, b.reshape(1, n))


def _w_main_kernel(wt_ref, o_ref):
    o_ref[...] = wt_ref[...].T.astype(BF16)


def _w_tail_kernel(wt_hbm, o_ref, tail, sem):
    r = 2 * GLA_GATE_RANK
    n_tail = tail.shape[0]
    copy = pltpu.make_async_copy(wt_hbm.at[pl.ds(wt_hbm.shape[0] - n_tail, n_tail)], tail, sem)
    copy.start()
    copy.wait()
    o_ref[:, :GLA_V_W] = tail[r:, :].T.astype(BF16)
    lowrank = jnp.concatenate([tail[:r, :], jnp.zeros((LANES - r, tail.shape[1]), F32)], axis=0)
    o_ref[:, GLA_V_W:] = lowrank.T.astype(BF16)


def _w_in_parts(wt):
    n, d = wt.shape
    n_main = n - GLA_V_W - 2 * GLA_GATE_RANK
    tn = 512
    main = pl.pallas_call(
        _w_main_kernel,
        grid=(n_main // tn,),
        in_specs=[pl.BlockSpec((tn, d), lambda j: (j, 0))],
        out_specs=pl.BlockSpec((d, tn), lambda j: (0, j)),
        out_shape=jax.ShapeDtypeStruct((d, n_main), BF16),
        compiler_params=_cparams("arbitrary"),
        name="w_in_main",
    )(wt)
    tail = pl.pallas_call(
        _w_tail_kernel,
        in_specs=[pl.BlockSpec(memory_space=pl.ANY)],
        out_shape=jax.ShapeDtypeStruct((d, GLA_V_W + LANES), BF16),
        scratch_shapes=[pltpu.VMEM((n - n_main, d), F32), pltpu.SemaphoreType.DMA],
        compiler_params=pltpu.CompilerParams(vmem_limit_bytes=VMEM_LIMIT),
        name="w_in_tail",
    )(wt)
    return main, tail


def _swap32(y):
    lane = lax.broadcasted_iota(jnp.int32, y.shape, 1)
    return jnp.where((lane & 63) < 32, pltpu.roll(y, 96, 1), pltpu.roll(y, 32, 1))


def _head_norm_rope(a, g, cos, sin):
    y = a * lax.rsqrt(jnp.mean(a * a, axis=-1, keepdims=True) + EPS) * g
    return y * cos + _swap32(y) * sin


def _in_proj_kernel(x_ref, g1_ref, sc_ref, sh_ref, w_ref, wt_ref, qg_ref, kg_ref, cos_ref, sin_ref,
                    q_ref, k_ref, v_ref, gq_ref, gk_ref, gv_ref, og_ref, lr_ref):
    x = x_ref[0]
    xn = x * lax.rsqrt(jnp.mean(x * x, axis=-1, keepdims=True) + EPS)
    h = (xn * g1_ref[...] * (1.0 + sc_ref[0]) + sh_ref[0]).astype(BF16)
    cos = cos_ref[...]
    sin = sin_ref[...]

    def proj(off, width, w=w_ref):
        return jnp.dot(h, w[:, off:off + width], preferred_element_type=F32)

    qg = qg_ref[...] * (HEAD_DIM ** -0.5 * LOG2_E)
    off = 0
    for out_ref, gain, heads in ((q_ref, qg, ATTN_HEADS), (k_ref, kg_ref[...], ATTN_KV_HEADS)):
        for pair in range(heads // 2):
            a2 = proj(off, 2 * HEAD_DIM)
            for u in range(2):
                lo = (2 * pair + u) * HEAD_DIM
                a = a2[:, u * HEAD_DIM:(u + 1) * HEAD_DIM]
                out_ref[0, :, lo:lo + HEAD_DIM] = _head_norm_rope(a, gain, cos, sin).astype(BF16)
            off += 2 * HEAD_DIM
    vv = proj(off, ATTN_KV_W).astype(BF16)
    for hk in range(ATTN_KV_HEADS):
        v_ref[0, :, 2 * hk * HEAD_DIM:(2 * hk + 1) * HEAD_DIM] = vv[:, hk * HEAD_DIM:(hk + 1) * HEAD_DIM]
        v_ref[0, :, (2 * hk + 1) * HEAD_DIM:(2 * hk + 2) * HEAD_DIM] = jnp.ones((vv.shape[0], HEAD_DIM), BF16)
    off += ATTN_KV_W
    gq_ref[0] = proj(off, GLA_K_W)
    off += GLA_K_W
    gk_ref[0] = proj(off, GLA_K_W)
    off += GLA_K_W
    gv_ref[0] = proj(off, GLA_V_W).astype(BF16)
    og_ref[0] = proj(0, GLA_V_W, wt_ref)
    lr_ref[0] = proj(GLA_V_W, LANES, wt_ref)


def _in_proj(x, g1, sc, sh, w_main, w_tail, qg, kg, cos_t, sin_t, tm):
    b, t, d = x.shape
    row = lambda bi, i: (bi, i, 0)
    vec = lambda bi, i: (bi, 0, 0)
    const = lambda bi, i: (0, 0)
    tab = lambda bi, i: (i, 0)
    widths = [(ATTN_Q_W, BF16), (ATTN_KV_W, BF16), (2 * ATTN_KV_W, BF16), (GLA_K_W, F32), (GLA_K_W, F32),
              (GLA_V_W, BF16), (GLA_V_W, F32), (LANES, F32)]
    return pl.pallas_call(
        _in_proj_kernel,
        grid=(b, t // tm),
        in_specs=[
            pl.BlockSpec((1, tm, d), row),
            pl.BlockSpec((1, d), const),
            pl.BlockSpec((1, 1, d), vec),
            pl.BlockSpec((1, 1, d), vec),
            pl.BlockSpec(w_main.shape, const, pipeline_mode=pl.Buffered(1)),
            pl.BlockSpec(w_tail.shape, const, pipeline_mode=pl.Buffered(1)),
            pl.BlockSpec((1, HEAD_DIM), const),
            pl.BlockSpec((1, HEAD_DIM), const),
            pl.BlockSpec((tm, HEAD_DIM), tab),
            pl.BlockSpec((tm, HEAD_DIM), tab),
        ],
        out_specs=[pl.BlockSpec((1, tm, wd), row) for wd, _ in widths],
        out_shape=[jax.ShapeDtypeStruct((b, t, wd), dt) for wd, dt in widths],
        compiler_params=_cparams("arbitrary", "arbitrary"),
        name="in_proj",
    )(x, g1, sc, sh, w_main, w_tail, qg, kg, cos_t, sin_t)


def _attn_kernel(q_ref, k_ref, v_ref, o_ref):
    def scores(h):
        hk = h // GQA_GROUP
        q = q_ref[0, :, h * HEAD_DIM:(h + 1) * HEAD_DIM]
        k = k_ref[0, :, hk * HEAD_DIM:(hk + 1) * HEAD_DIM]
        return lax.dot_general(q, k, (((1,), (1,)), ((), ())), preferred_element_type=F32)

    s_next = scores(0)
    for h in range(ATTN_HEADS):
        hk = h // GQA_GROUP
        s = s_next
        if h + 1 < ATTN_HEADS:
            s_next = scores(h + 1)
        p = jnp.exp2(s - jnp.max(s, axis=-1, keepdims=True))
        ol = jnp.dot(p.astype(BF16), v_ref[0, :, 2 * hk * HEAD_DIM:2 * (hk + 1) * HEAD_DIM],
                     preferred_element_type=F32)
        o_ref[0, :, h * HEAD_DIM:(h + 1) * HEAD_DIM] = (
            ol[:, :HEAD_DIM] / ol[:, HEAD_DIM:HEAD_DIM + 1]).astype(BF16)


def _attn(q, k_all, v_all, tq):
    b, t, _ = q.shape
    tk = k_all.shape[1]
    return pl.pallas_call(
        _attn_kernel,
        grid=(b, t // tq),
        in_specs=[
            pl.BlockSpec((1, tq, ATTN_Q_W), lambda bi, i: (bi, i, 0)),
            pl.BlockSpec((1, tk, ATTN_KV_W), lambda bi, i: (bi, 0, 0)),
            pl.BlockSpec((1, tk, 2 * ATTN_KV_W), lambda bi, i: (bi, 0, 0)),
        ],
        out_specs=pl.BlockSpec((1, tq, ATTN_Q_W), lambda bi, i: (bi, i, 0)),
        out_shape=jax.ShapeDtypeStruct((b, t, ATTN_Q_W), BF16),
        compiler_params=_cparams("arbitrary", "arbitrary"),
        name="attn",
    )(q, k_all, v_all)


def _log_sigmoid(x):
    return jnp.minimum(x, 0.0) - jnp.log(1.0 + jnp.exp(-jnp.abs(x)))


def _split3(x):
    hi = x.astype(BF16)
    r = x - hi.astype(F32)
    mid = r.astype(BF16)
    lo = (r - mid.astype(F32)).astype(BF16)
    return hi, mid, lo


def _dot_exact_lhs(m, x):
    return sum(jnp.dot(m, part, preferred_element_type=F32) for part in _split3(x))


def _prefix_operator(n):
    ri = lax.broadcasted_iota(jnp.int32, (n, n), 0)
    ci = lax.broadcasted_iota(jnp.int32, (n, n), 1)
    same = lax.shift_right_logical(ri, GLA_CHUNK_LOG2) == lax.shift_right_logical(ci, GLA_CHUNK_LOG2)
    return (same & (ci <= ri)).astype(BF16)


def _gate_prep(q, k, lr, w2s, bias2, prefix):
    c = GLA_CHUNK
    hi = lr.astype(BF16).astype(F32)
    mid = (lr - hi).astype(BF16).astype(F32)
    lhs = (hi + pltpu.roll(mid, 2 * GLA_GATE_RANK, 1) + pltpu.roll(hi, 4 * GLA_GATE_RANK, 1)).astype(BF16)
    logits = jnp.dot(lhs, w2s, preferred_element_type=F32) + bias2
    g = _log_sigmoid(logits) * (1.0 / GLA_GATE_NORMALIZER)
    pre = _dot_exact_lhs(prefix, g)
    tot = jnp.concatenate([jnp.broadcast_to(pre[lo + c - 1:lo + c, :], (c, pre.shape[1]))
                           for lo in range(0, pre.shape[0], c)], axis=0)
    dk = GLA_DK
    bcs = (pre[:, :dk], tot[:, dk:] - pre[:, dk:] + g[:, dk:])
    out = []
    for d, bc in enumerate(bcs):
        b_end = tot[:, d * dk:(d + 1) * dk]
        qe = (q * (GLA_DK ** -0.5) * jnp.exp(bc)).astype(BF16)
        ke = (k * jnp.exp(-bc)).astype(BF16)
        kend = (k * jnp.exp(b_end - bc)).astype(BF16)
        out.append((qe, ke, kend, b_end))
    return out


def _state_step(s_ref, v, kend, dec):
    upd = lax.dot_general(v, kend, (((0,), (0,)), ((), ())), preferred_element_type=F32)
    s_ref[...] = s_ref[...] * dec + upd


def _gla_kernel(q_ref, k_ref, v_ref, lr_ref, og_ref, qc_ref, kc_ref, vc_ref, lrc_ref,
                w2_ref, b_ref, ng_ref, o_ref,
                s_ref, qe_ref, ke_ref, kend_ref, dec_ref, kendc_ref, decc_ref, sbf_ref):
    c = GLA_CHUNK
    t = q_ref.shape[1]
    tc = qc_ref.shape[1]
    nc = t // c
    tile = GLA_PREP_TILE
    cpt = tile // c
    dirs = (0, 1)

    def store_dec(ref, d, base, b_end):
        for ch in range(b_end.shape[0] // c):
            ref[d, pl.ds(base + ch, 1), :] = jnp.exp(b_end[ch * c:ch * c + 1, :])

    prefix = _prefix_operator(tile)

    for i in range(tc // tile):
        rows = slice(i * tile, (i + 1) * tile)
        prep_c = _gate_prep(qc_ref[0, rows, :], kc_ref[0, rows, :], lrc_ref[0, rows, :],
                            w2_ref[...], b_ref[...], prefix)
        for d, (_, _, kend, b_end) in enumerate(prep_c):
            kendc_ref[d, rows, :] = kend
            store_dec(decc_ref, d, i * cpt, b_end)
    s_ref[...] = jnp.zeros_like(s_ref)
    for j in range(tc // c):
        for d in dirs:
            ch = j if d == 0 else tc // c - 1 - j
            _state_step(s_ref.at[d], vc_ref[0, ch * c:(ch + 1) * c, :], kendc_ref[d, ch * c:(ch + 1) * c, :],
                        decc_ref[d, ch:ch + 1, :])

    def prep(i, carry):
        lo = pl.multiple_of(i * tile, tile)
        rows = pl.ds(lo, tile)
        prep_l = _gate_prep(q_ref[0, rows, :], k_ref[0, rows, :], lr_ref[0, rows, :],
                            w2_ref[...], b_ref[...], prefix)
        for d, (qe, ke, kend, b_end) in enumerate(prep_l):
            qe_ref[d, rows, :] = qe
            ke_ref[d, rows, :] = ke
            kend_ref[d, rows, :] = kend
            store_dec(dec_ref, d, i * cpt, b_end)
        return carry

    lax.fori_loop(0, t // tile, prep, 0, unroll=2)

    def scan(j, carry):
        for d in dirs:
            ch = j if d == 0 else nc - 1 - j
            rows = pl.ds(pl.multiple_of(ch * c, c), c)
            sbf_ref[d, ch] = s_ref[d].astype(BF16)
            _state_step(s_ref.at[d], v_ref[0, rows, :], kend_ref[d, rows, :], dec_ref[d, pl.ds(ch, 1), :])
        return carry

    lax.fori_loop(0, nc, scan, 0, unroll=8)

    ri = lax.broadcasted_iota(jnp.int32, (c, c), 0)
    ci = lax.broadcasted_iota(jnp.int32, (c, c), 1)
    masks = (ci <= ri, ci >= ri)

    group = GLA_OUT_GROUP

    def out(i, carry):
        chunks = [i * group + u for u in range(group)]
        rows = [pl.ds(pl.multiple_of(ch * c, c), c) for ch in chunks]
        qes = [[qe_ref[d, r, :] for d in dirs] for r in rows]
        scores = [[lax.dot_general(qes[u][d], ke_ref[d, rows[u], :], (((1,), (1,)), ((), ())),
                                   preferred_element_type=F32) for d in dirs] for u in range(group)]
        inter = [[lax.dot_general(qes[u][d], sbf_ref[d, chunks[u]], (((1,), (1,)), ((), ())),
                                  preferred_element_type=F32) for d in dirs] for u in range(group)]
        for u in range(group):
            v = v_ref[0, rows[u], :]
            o = inter[u][0] + inter[u][1]
            for d in dirs:
                a = jnp.where(masks[d], scores[u][d], 0.0).astype(BF16)
                o = o + jnp.dot(a, v, preferred_element_type=F32)
            on = o * lax.rsqrt(jnp.mean(o * o, axis=-1, keepdims=True) + EPS) * ng_ref[...]
            og = og_ref[0, rows[u], :]
            o_ref[0, rows[u], :] = (on * (og * jax.nn.sigmoid(og))).astype(BF16)
        return carry

    lax.fori_loop(0, nc // group, out, 0)


def _gla(gq, gk, gv, lr, og, gqc, gkc, gvc, lrc, w2, bias, ng):
    b, t, _ = gq.shape
    tc = gqc.shape[1]
    hk = lambda bi, h: (bi, 0, h)
    h0 = lambda bi, h: (bi, 0, 0)
    return pl.pallas_call(
        _gla_kernel,
        grid=(b, GLA_HEADS),
        in_specs=[
            pl.BlockSpec((1, t, GLA_DK), hk),
            pl.BlockSpec((1, t, GLA_DK), hk),
            pl.BlockSpec((1, t, GLA_DV), hk),
            pl.BlockSpec((1, t, LANES), h0),
            pl.BlockSpec((1, t, GLA_DV), hk),
            pl.BlockSpec((1, tc, GLA_DK), hk),
            pl.BlockSpec((1, tc, GLA_DK), hk),
            pl.BlockSpec((1, tc, GLA_DV), hk),
            pl.BlockSpec((1, tc, LANES), h0),
            pl.BlockSpec((None, LANES, 2 * GLA_DK), lambda bi, h: (h, 0, 0)),
            pl.BlockSpec((None, 1, 2 * GLA_DK), lambda bi, h: (h, 0, 0)),
            pl.BlockSpec((1, GLA_DV), lambda bi, h: (0, 0)),
        ],
        out_specs=pl.BlockSpec((1, t, GLA_DV), hk),
        out_shape=jax.ShapeDtypeStruct((b, t, GLA_V_W), BF16),
        scratch_shapes=[
            pltpu.VMEM((2, GLA_DV, GLA_DK), F32),
            pltpu.VMEM((2, t, GLA_DK), BF16),
            pltpu.VMEM((2, t, GLA_DK), BF16),
            pltpu.VMEM((2, t, GLA_DK), BF16),
            pltpu.VMEM((2, t // GLA_CHUNK, GLA_DK), F32),
            pltpu.VMEM((2, tc, GLA_DK), BF16),
            pltpu.VMEM((2, tc // GLA_CHUNK, GLA_DK), F32),
            pltpu.VMEM((2, t // GLA_CHUNK, GLA_DV, GLA_DK), BF16),
        ],
        compiler_params=_cparams("arbitrary", "arbitrary"),
        name="gla",
    )(gq, gk, gv, lr, og, gqc, gkc, gvc, lrc, w2, bias, ng)


def _token_pitch(d):
    rows = d // LANES
    return rows + 4 if rows % 8 == 0 else rows


def _store_token_tiles(ref, x):
    n = x.shape[0]
    k = x.shape[1] // LANES
    pitch = ref.shape[0] // n
    for s in range(pitch):
        slab = x[:, s * LANES:(s + 1) * LANES] if s < k else jnp.zeros((n, LANES), x.dtype)
        ref[pl.ds(s, n, stride=pitch), :] = slab


def _load_token_tiles(ref, n, d):
    pitch = ref.shape[0] // n
    return jnp.concatenate([ref[pl.ds(s, n, stride=pitch), :] for s in range(d // LANES)], axis=1)


def _out_proj_kernel(attn_ref, gla_ref, wo_ref, x_ref, gt_ref, g2_ref, sc_ref, sh_ref, wr_ref, br_ref,
                     x1_ref, h2_ref, ids_ref, wts_ref):
    y = jnp.dot(attn_ref[0], wo_ref[0:ATTN_Q_W, :], preferred_element_type=F32)
    y = y + jnp.dot(gla_ref[0], wo_ref[ATTN_Q_W:, :], preferred_element_type=F32)
    x1 = x_ref[0] + gt_ref[0] * y
    x1_ref[0] = x1
    xn = x1 * lax.rsqrt(jnp.mean(x1 * x1, axis=-1, keepdims=True) + EPS)
    h2 = xn * g2_ref[...] * (1.0 + sc_ref[0]) + sh_ref[0]
    hi = h2.astype(BF16)
    hi_f = hi.astype(F32)

    _store_token_tiles(h2_ref.at[0], h2)

    mid = (h2 - hi_f).astype(BF16)
    both = jnp.dot(hi, wr_ref[...], preferred_element_type=F32)
    logits = (both[:, :LANES] + both[:, LANES:]
              + jnp.dot(mid, wr_ref[:, :LANES], preferred_element_type=F32)) + br_ref[...]
    lane = lax.broadcasted_iota(jnp.int32, logits.shape, 1)
    lane_f = lane.astype(F32)
    neg = jnp.float32(-jnp.inf)

    def first_argmax(vals):
        m = jnp.max(vals, axis=-1, keepdims=True)
        idx = jnp.min(jnp.where(vals == m, lane_f, float(LANES)), axis=-1, keepdims=True)
        return m, idx

    lg = jnp.where(lane < N_GROUPS, logits, neg)
    mg, grp = first_argmax(lg)
    pg_sel = 1.0 / jnp.sum(jnp.exp(lg - mg), axis=-1, keepdims=True)
    lo = N_GROUPS + grp * EXPERTS_PER_GROUP
    in_grp = (lane_f >= lo) & (lane_f < lo + EXPERTS_PER_GROUP)
    le = jnp.where(in_grp, logits, neg)
    v1, i1 = first_argmax(le)
    v2, i2 = first_argmax(jnp.where(lane_f == i1, neg, le))
    e2 = jnp.exp(v2 - v1)
    w1 = pg_sel / (1.0 + e2)
    w2 = pg_sel * e2 / (1.0 + e2)
    ids = jnp.where(lane == 0, i1 - N_GROUPS, jnp.where(lane == 1, i2 - N_GROUPS, 0.0))
    ids_ref[0] = ids.astype(jnp.int32)
    wts_ref[0] = jnp.where(lane == 0, w1, jnp.where(lane == 1, w2, 0.0))


def _out_proj(attn, gla, wo, x, gt1, g2, sc2, sh2, wr, br, tm):
    b, t, d = x.shape
    row = lambda bi, i: (bi, i, 0)
    vec = lambda bi, i: (bi, 0, 0)
    const = lambda bi, i: (0, 0)
    return pl.pallas_call(
        _out_proj_kernel,
        grid=(b, t // tm),
        in_specs=[
            pl.BlockSpec((1, tm, ATTN_Q_W), row),
            pl.BlockSpec((1, tm, GLA_V_W), row),
            pl.BlockSpec(wo.shape, const, pipeline_mode=pl.Buffered(1)),
            pl.BlockSpec((1, tm, d), row),
            pl.BlockSpec((1, 1, d), vec),
            pl.BlockSpec((1, d), const),
            pl.BlockSpec((1, 1, d), vec),
            pl.BlockSpec((1, 1, d), vec),
            pl.BlockSpec((d, 2 * LANES), const),
            pl.BlockSpec((1, LANES), const),
        ],
        out_specs=[
            pl.BlockSpec((1, tm, d), row),
            pl.BlockSpec((1, tm * _token_pitch(d), LANES), row),
            pl.BlockSpec((1, tm, LANES), row),
            pl.BlockSpec((1, tm, LANES), row),
        ],
        out_shape=[
            jax.ShapeDtypeStruct((b, t, d), F32),
            jax.ShapeDtypeStruct((b, t * _token_pitch(d), LANES), F32),
            jax.ShapeDtypeStruct((b, t, LANES), jnp.int32),
            jax.ShapeDtypeStruct((b, t, LANES), F32),
        ],
        compiler_params=_cparams("arbitrary", "arbitrary"),
        name="out_proj",
    )(attn, gla, wo, x, gt1, g2, sc2, sh2, wr, br)


def _moe_kernel(eidx_ref, eblk0_ref, enb_ref, bs0_ref, bn_ref, nblk_ref, order_ref,
                h2_hbm, w1_ref, w3_ref, w2_ref, y_hbm,
                xbuf, ybuf, w1b, w3b, w2b, gsem, ssem):
    del eidx_ref
    e = pl.program_id(0)
    nblk = nblk_ref[0]
    kx = xbuf.shape[1] // MOE_BLOCK
    ky = ybuf.shape[1] // MOE_BLOCK

    def gather_copy(sl, hbm_row, j, rows=1):
        return pltpu.make_async_copy(h2_hbm.at[pl.ds(hbm_row * kx, rows * kx)],
                                     xbuf.at[sl, pl.ds(j * kx, rows * kx)], gsem.at[sl])

    def for_rows(n, per_group, per_row):
        ng = lax.shift_right_logical(n, ROW_GROUP_LOG2)
        lax.fori_loop(0, ng, lambda g, c: (per_group(g * ROW_GROUP), c)[1], 0)
        lax.fori_loop(ng * ROW_GROUP, n, lambda j, c: (per_row(j), c)[1], 0)

    def start_gather(blk, sl):
        s = bs0_ref[blk]

        def start_row(j):
            gather_copy(sl, lax.shift_right_logical(order_ref[s + j], 1), j).start()

        def group(j0):
            for u in range(ROW_GROUP):
                start_row(j0 + u)

        for_rows(bn_ref[blk], group, start_row)

    def wait_gather(blk, sl):
        for_rows(bn_ref[blk], lambda j0: gather_copy(sl, 0, 0, ROW_GROUP).wait(),
                 lambda j: gather_copy(sl, 0, 0).wait())

    def output_copy(blk, sl):
        return pltpu.make_async_copy(
            ybuf.at[sl], y_hbm.at[pl.ds(bs0_ref[blk] * ky, MOE_BLOCK * ky)], ssem.at[sl])

    @pl.when(e == 0)
    def _():
        xbuf[...] = jnp.zeros_like(xbuf)
        ybuf[...] = jnp.zeros_like(ybuf)
        for g0 in range(GATHER_AHEAD):
            @pl.when(g0 < nblk)
            def _():
                start_gather(g0, g0)

    def block(g, carry):
        slot = g & 1
        xslot = lax.rem(g, GATHER_AHEAD + 1)

        @pl.when(g + GATHER_AHEAD < nblk)
        def _():
            start_gather(g + GATHER_AHEAD, lax.rem(g + GATHER_AHEAD, GATHER_AHEAD + 1))

        wait_gather(g, xslot)

        def expert_rows(rows):
            x = _load_token_tiles(xbuf.at[xslot, pl.ds(0, rows * kx)], rows, w1b.shape[0]).astype(BF16)
            a = jnp.dot(x, w1b[...], preferred_element_type=F32)
            gate = jnp.dot(x, w3b[...], preferred_element_type=F32)
            hid = (a * jax.nn.sigmoid(a) * gate).astype(BF16)
            y = jnp.dot(hid, w2b[...], preferred_element_type=F32)
            _store_token_tiles(ybuf.at[slot, pl.ds(0, rows * ky)], y)

        half = MOE_BLOCK // 2

        @pl.when(bn_ref[g] > half)
        def _():
            expert_rows(MOE_BLOCK)

        @pl.when(bn_ref[g] <= half)
        def _():
            expert_rows(half)

        @pl.when(g >= 1)
        def _():
            output_copy(g - 1, 1 - slot).wait()

        output_copy(g, slot).start()
        return carry

    @pl.when(enb_ref[e] > 0)
    def _():
        w1b[...] = w1_ref[...].astype(BF16)
        w3b[...] = w3_ref[...].astype(BF16)
        w2b[...] = w2_ref[...].astype(BF16)
        lax.fori_loop(eblk0_ref[e], eblk0_ref[e] + enb_ref[e], block, 0)

    @pl.when(e == pl.num_programs(0) - 1)
    def _():
        output_copy(nblk - 1, (nblk - 1) & 1).wait()
        ybuf[0] = jnp.zeros_like(ybuf[0])
        pad_rows = MOE_BLOCK * ky
        pad = pltpu.make_async_copy(ybuf.at[0], y_hbm.at[pl.ds(y_hbm.shape[0] - pad_rows, pad_rows)], ssem.at[0])
        pad.start()
        pad.wait()


def _moe(h2t, w1, w3, w2, eidx, eblk0, enb, bs0, bn, nblk, order):
    ne, d, ff = w1.shape
    kx = ky = _token_pitch(d)
    wmap = lambda e, eidx, *_: (eidx[e], 0, 0)
    grid_spec = pltpu.PrefetchScalarGridSpec(
        num_scalar_prefetch=7,
        grid=(ne,),
        in_specs=[
            pl.BlockSpec(memory_space=pl.ANY),
            pl.BlockSpec((None, d, ff), wmap),
            pl.BlockSpec((None, d, ff), wmap),
            pl.BlockSpec((None, ff, d), wmap),
        ],
        out_specs=pl.BlockSpec(memory_space=pl.ANY),
        scratch_shapes=[
            pltpu.VMEM((GATHER_AHEAD + 1, MOE_BLOCK * kx, LANES), F32),
            pltpu.VMEM((2, MOE_BLOCK * ky, LANES), F32),
            pltpu.VMEM((d, ff), BF16),
            pltpu.VMEM((d, ff), BF16),
            pltpu.VMEM((ff, d), BF16),
            pltpu.SemaphoreType.DMA((GATHER_AHEAD + 1,)),
            pltpu.SemaphoreType.DMA((2,)),
        ],
    )
    return pl.pallas_call(
        _moe_kernel,
        grid_spec=grid_spec,
        out_shape=jax.ShapeDtypeStruct(((order.shape[0] + MOE_BLOCK) * ky, LANES), F32),
        compiler_params=_cparams("arbitrary"),
        name="moe",
    )(eidx, eblk0, enb, bs0, bn, nblk, order, h2t, w1, w3, w2)


def _combine_kernel(pos_ref, x1_ref, ys_hbm, wts_ref, gt_ref, o_ref, ybuf, sem):
    tm = x1_ref.shape[1]
    ky = ybuf.shape[2] // tm
    nt = pl.num_programs(1)
    n = pl.program_id(0) * nt + pl.program_id(1)
    slot = n & 1

    def row_copy(sl, k, pos, j, rows=1):
        return pltpu.make_async_copy(ys_hbm.at[pl.ds(pos * ky, rows * ky)],
                                     ybuf.at[sl, k, pl.ds(j * ky, rows * ky)], sem.at[sl])

    def start_tile(tile, sl):
        base = tile * (tm * TOP_K)

        def group(g, carry):
            for u in range(ROW_GROUP):
                row_copy(sl, u % TOP_K, pos_ref[base + g * ROW_GROUP + u],
                         g * (ROW_GROUP // TOP_K) + u // TOP_K).start()
            return carry

        lax.fori_loop(0, tm * TOP_K // ROW_GROUP, group, 0)

    @pl.when(n == 0)
    def _():
        start_tile(0, 0)

    @pl.when(n + 1 < pl.num_programs(0) * nt)
    def _():
        start_tile(n + 1, 1 - slot)

    for k in range(TOP_K):
        row_copy(slot, k, 0, 0, rows=tm).wait()
    w = wts_ref[0]
    d = x1_ref.shape[2]
    ff = (w[:, 0:1] * _load_token_tiles(ybuf.at[slot, 0], tm, d)
          + w[:, 1:2] * _load_token_tiles(ybuf.at[slot, 1], tm, d))
    o_ref[0] = x1_ref[0] + gt_ref[0] * ff


def _combine(x1, ys, pos, wts, gt2, tm):
    b, t, d = x1.shape
    ky = _token_pitch(d)
    row = lambda bi, i, pos: (bi, i, 0)
    vec = lambda bi, i, pos: (bi, 0, 0)
    grid_spec = pltpu.PrefetchScalarGridSpec(
        num_scalar_prefetch=1,
        grid=(b, t // tm),
        in_specs=[
            pl.BlockSpec((1, tm, d), row),
            pl.BlockSpec(memory_space=pl.ANY),
            pl.BlockSpec((1, tm, LANES), row),
            pl.BlockSpec((1, 1, d), vec),
        ],
        out_specs=pl.BlockSpec((1, tm, d), row),
        scratch_shapes=[
            pltpu.VMEM((2, TOP_K, tm * ky, LANES), F32),
            pltpu.SemaphoreType.DMA((2,)),
        ],
    )
    return pl.pallas_call(
        _combine_kernel,
        grid_spec=grid_spec,
        out_shape=jax.ShapeDtypeStruct((b, t, d), F32),
        compiler_params=_cparams("arbitrary", "arbitrary"),
        name="combine",
    )(pos, x1, ys, wts, gt2)


def _rope_tables(t):
    rows = t // GRID_W
    n_freq = HEAD_DIM // 4
    inv = ROPE_THETA ** (-jnp.arange(n_freq, dtype=F32) / n_freq)
    ar = jnp.arange(rows, dtype=F32)[:, None] * inv
    ac = jnp.arange(GRID_W, dtype=F32)[:, None] * inv

    def expand(fr, fc, sign):
        by_row = jnp.broadcast_to(fr[:, None, :], (rows, GRID_W, n_freq))
        by_col = jnp.broadcast_to(fc[None, :, :], (rows, GRID_W, n_freq))
        return jnp.concatenate([sign * by_row, by_row, sign * by_col, by_col], axis=2).reshape(t, HEAD_DIM)

    return expand(jnp.cos(ar), jnp.cos(ac), 1.0), expand(jnp.sin(ar), jnp.sin(ac), -1.0)


def _gate_weights(w2, bias):
    r = GLA_GATE_RANK
    wh = w2.reshape(2, r, GLA_HEADS, GLA_DK).transpose(2, 0, 1, 3)
    w = jnp.zeros((GLA_HEADS, 2 * r, 2 * GLA_DK), F32)
    w = w.at[:, 0:r, 0:GLA_DK].set(wh[:, 0]).at[:, r:2 * r, GLA_DK:].set(wh[:, 1])
    hi = w.astype(BF16)
    mid = (w - hi.astype(F32)).astype(BF16)
    pad = jnp.zeros((GLA_HEADS, LANES - 6 * r, 2 * GLA_DK), BF16)
    w2s = jnp.concatenate([hi, hi, mid, pad], axis=1)
    bias2 = bias.reshape(2, GLA_HEADS, GLA_DK).transpose(1, 0, 2).reshape(GLA_HEADS, 1, 2 * GLA_DK)
    return w2s, bias2


def _block_plan(eid_flat, nb):
    order = jnp.argsort(eid_flat).astype(jnp.int32)
    eids = jnp.arange(N_EXPERTS, dtype=jnp.int32)
    of_assign = (eid_flat[:, None] == eids[None, :]).astype(jnp.int32)
    counts = jnp.sum(of_assign, axis=0)
    starts = jnp.cumsum(counts) - counts
    nblk_e = (counts + MOE_BLOCK - 1) // MOE_BLOCK
    bends = jnp.cumsum(nblk_e)
    bstarts = bends - nblk_e
    nblk = bends[-1]
    prev_used = lax.cummax(jnp.where(counts > 0, eids, -1))
    eidx = jnp.where(prev_used >= 0, prev_used, jnp.argmax(counts > 0)).astype(jnp.int32)
    bi = jnp.arange(nb, dtype=jnp.int32)
    bic = jnp.minimum(bi, jnp.maximum(nblk - 1, 0))
    bexp = jnp.minimum(jnp.sum(bends[None, :] <= bic[:, None], axis=1), N_EXPERTS - 1)
    of_block = (bexp[:, None] == eids[None, :]).astype(jnp.int32)

    def lookup(per_expert):
        return jnp.sum(of_block * per_expert[None, :], axis=1)

    r0 = (bic - lookup(bstarts)) * MOE_BLOCK
    bs0 = (lookup(starts) + r0).astype(jnp.int32)
    bn = jnp.where(bi < nblk, jnp.minimum(lookup(counts) - r0, MOE_BLOCK), 0).astype(jnp.int32)
    pos = jnp.argsort(order).astype(jnp.int32)
    plan = (eidx, bstarts.astype(jnp.int32), nblk_e.astype(jnp.int32), bs0, bn,
            nblk.reshape(1).astype(jnp.int32), order)
    return plan, pos


def kernel(x, c, ctx, c_ctx, w_ada, b_ada, norm1_g, w_in, q_norm_g, k_norm_g, gla_gate_w2, gla_gate_b, gla_norm_g, w_out, norm2_g, router_grp_w, router_grp_b, router_exp_w, router_exp_b, moe_w1, moe_w3, moe_w2):
    b, t, d = x.shape
    tc = ctx.shape[1]
    depth = w_ada.shape[0]
    assert depth == 1, "single-layer stack: the context stream only feeds keys/values and GLA states"
    layer = 0

    c8 = jnp.zeros((8, d), F32).at[0:b].set(c).at[b].set(c_ctx)
    mod = _ada(c8, w_ada[layer], b_ada[layer])
    sh1, sc1, gt1, sh2, sc2, gt2 = [mod[0:b, i * d:(i + 1) * d].reshape(b, 1, d) for i in range(6)]
    sh1c, sc1c = [jnp.broadcast_to(mod[b, i * d:(i + 1) * d].reshape(1, 1, d), (b, 1, d)) for i in range(2)]

    w_main, w_tail = _w_in_parts(jnp.swapaxes(w_in[layer], 0, 1))

    cos_t, sin_t = _rope_tables(t)
    g1 = norm1_g[layer].reshape(1, d)
    qg = q_norm_g[layer].reshape(1, HEAD_DIM)
    kg = k_norm_g[layer].reshape(1, HEAD_DIM)
    q, k, v, gq, gk, gv, og, lr = _in_proj(x, g1, sc1, sh1, w_main, w_tail, qg, kg, cos_t, sin_t, 512)
    ones_t = jnp.ones((tc, HEAD_DIM), F32)
    _, kc, vc, gqc, gkc, gvc, _, lrc = _in_proj(ctx, g1, sc1c, sh1c, w_main, w_tail, qg, kg,
                                                ones_t, jnp.zeros_like(ones_t), tc)

    attn = _attn(q, jnp.concatenate([kc, k], axis=1), jnp.concatenate([vc, v], axis=1), 256)
    w2s, bias2 = _gate_weights(gla_gate_w2[layer], gla_gate_b[layer])
    gla = _gla(gq, gk, gv, lr, og, gqc, gkc, gvc, lrc, w2s, bias2, gla_norm_g[layer].reshape(1, GLA_DV))

    wr = jnp.concatenate([router_grp_w[layer], router_exp_w[layer],
                          jnp.zeros((d, LANES - N_GROUPS - N_EXPERTS), F32)], axis=1)
    br = jnp.concatenate([router_grp_b[layer], router_exp_b[layer],
                          jnp.zeros((LANES - N_GROUPS - N_EXPERTS,), F32)]).reshape(1, LANES)
    wr_hi = wr.astype(BF16)
    wr_parts = jnp.concatenate([wr_hi, (wr - wr_hi.astype(F32)).astype(BF16)], axis=1)
    x1, h2t, ids, wts = _out_proj(attn, gla, w_out[layer].astype(BF16), x, gt1, norm2_g[layer].reshape(1, d),
                                  sc2, sh2, wr_parts, br, 512)

    m = b * t
    n_assign = m * TOP_K
    nb = -(-(n_assign + N_EXPERTS * (MOE_BLOCK - 1)) // MOE_BLOCK)
    eid_flat = ids[:, :, 0:TOP_K].reshape(n_assign)
    plan, pos = _block_plan(eid_flat, nb)
    ys = _moe(h2t.reshape(-1, LANES), moe_w1[layer], moe_w3[layer], moe_w2[layer], *plan)
    return _combine(x1, ys, pos, wts, gt2, 256)
```

```python
import functools

import jax
import jax.numpy as jnp
from jax import lax
from jax.experimental import pallas as pl
from jax.experimental.pallas import tpu as pltpu

EPS = 1e-6
GRID_W = 64
ROPE_THETA = 10000.0

ATTN_HEADS = 8
ATTN_KV_HEADS = 2
HEAD_DIM = 128
GQA_GROUP = ATTN_HEADS // ATTN_KV_HEADS

GLA_HEADS = 4
GLA_DK = 128
GLA_DV = 256
GLA_GATE_RANK = 16
GLA_GATE_NORMALIZER = 16.0
GLA_CHUNK = 64
GLA_CHUNK_LOG2 = 6
GLA_PREP_TILE = 256
GLA_OUT_GROUP = 8

N_GROUPS = 8
EXPERTS_PER_GROUP = 8
N_EXPERTS = N_GROUPS * EXPERTS_PER_GROUP
TOP_K = 2
MOE_BLOCK = 256
ROW_GROUP_LOG2 = 3
ROW_GROUP = 1 << ROW_GROUP_LOG2
GATHER_AHEAD = 2

LOG2_E = 1.4426950408889634
LANES = 128
VMEM_LIMIT = 56 * 1024 * 1024

ATTN_Q_W = ATTN_HEADS * HEAD_DIM
ATTN_KV_W = ATTN_KV_HEADS * HEAD_DIM
GLA_K_W = GLA_HEADS * GLA_DK
GLA_V_W = GLA_HEADS * GLA_DV

BF16 = jnp.bfloat16
F32 = jnp.float32


def _cparams(*sem):
    return pltpu.CompilerParams(dimension_semantics=sem, vmem_limit_bytes=VMEM_LIMIT)


def _ada_kernel(c_ref, w_ref, b_ref, o_ref):
    c = c_ref[...]
    s = c * jax.nn.sigmoid(c)
    o_ref[...] = jnp.dot(s.astype(BF16), w_ref[...].astype(BF16), preferred_element_type=F32) + b_ref[...]


def _ada(c8, w, b):
    d, n = w.shape
    tn = 1024
    return pl.pallas_call(
        _ada_kernel,
        grid=(n // tn,),
        in_specs=[
            pl.BlockSpec((8, d), lambda j: (0, 0)),
            pl.BlockSpec((d, tn), lambda j: (0, j)),
            pl.BlockSpec((1, tn), lambda j: (0, j)),
        ],
        out_specs=pl.BlockSpec((8, tn), lambda j: (0, j)),
        out_shape=jax.ShapeDtypeStruct((8, n), F32),
        compiler_params=_cparams("arbitrary"),
        name="ada",
    )(c8, w, b.reshape(1, n))


def _w_main_kernel(wt_ref, o_ref):
    o_ref[...] = wt_ref[...].T.astype(BF16)


def _w_tail_kernel(wt_hbm, o_ref, tail, sem):
    r = 2 * GLA_GATE_RANK
    n_tail = tail.shape[0]
    copy = pltpu.make_async_copy(wt_hbm.at[pl.ds(wt_hbm.shape[0] - n_tail, n_tail)], tail, sem)
    copy.start()
    copy.wait()
    o_ref[:, :GLA_V_W] = tail[r:, :].T.astype(BF16)
    lowrank = jnp.concatenate([tail[:r, :], jnp.zeros((LANES - r, tail.shape[1]), F32)], axis=0)
    o_ref[:, GLA_V_W:] = lowrank.T.astype(BF16)


def _w_in_parts(wt):
    n, d = wt.shape
    n_main = n - GLA_V_W - 2 * GLA_GATE_RANK
    tn = 512
    main = pl.pallas_call(
        _w_main_kernel,
        grid=(n_main // tn,),
        in_specs=[pl.BlockSpec((tn, d), lambda j: (j, 0))],
        out_specs=pl.BlockSpec((d, tn), lambda j: (0, j)),
        out_shape=jax.ShapeDtypeStruct((d, n_main), BF16),
        compiler_params=_cparams("arbitrary"),
        name="w_in_main",
    )(wt)
    tail = pl.pallas_call(
        _w_tail_kernel,
        in_specs=[pl.BlockSpec(memory_space=pl.ANY)],
        out_shape=jax.ShapeDtypeStruct((d, GLA_V_W + LANES), BF16),
        scratch_shapes=[pltpu.VMEM((n - n_main, d), F32), pltpu.SemaphoreType.DMA],
        compiler_params=pltpu.CompilerParams(vmem_limit_bytes=VMEM_LIMIT),
        name="w_in_tail",
    )(wt)
    return main, tail


def _swap32(y):
    lane = lax.broadcasted_iota(jnp.int32, y.shape, 1)
    return jnp.where((lane & 63) < 32, pltpu.roll(y, 96, 1), pltpu.roll(y, 32, 1))


def _head_norm_rope(a, g, cos, sin):
    y = a * lax.rsqrt(jnp.mean(a * a, axis=-1, keepdims=True) + EPS) * g
    return y * cos + _swap32(y) * sin


def _in_proj_kernel(x_ref, g1_ref, sc_ref, sh_ref, w_ref, wt_ref, qg_ref, kg_ref, cos_ref, sin_ref,
                    q_ref, k_ref, v_ref, gq_ref, gk_ref, gv_ref, og_ref, lr_ref):
    x = x_ref[0]
    xn = x * lax.rsqrt(jnp.mean(x * x, axis=-1, keepdims=True) + EPS)
    h = (xn * g1_ref[...] * (1.0 + sc_ref[0]) + sh_ref[0]).astype(BF16)
    cos = cos_ref[...]
    sin = sin_ref[...]

    def proj(off, width, w=w_ref):
        return jnp.dot(h, w[:, off:off + width], preferred_element_type=F32)

    qg = qg_ref[...] * (HEAD_DIM ** -0.5 * LOG2_E)
    off = 0
    for out_ref, gain, heads in ((q_ref, qg, ATTN_HEADS), (k_ref, kg_ref[...], ATTN_KV_HEADS)):
        for pair in range(heads // 2):
            a2 = proj(off, 2 * HEAD_DIM)
            for u in range(2):
                lo = (2 * pair + u) * HEAD_DIM
                a = a2[:, u * HEAD_DIM:(u + 1) * HEAD_DIM]
                out_ref[0, :, lo:lo + HEAD_DIM] = _head_norm_rope(a, gain, cos, sin).astype(BF16)
            off += 2 * HEAD_DIM
    vv = proj(off, ATTN_KV_W).astype(BF16)
    for hk in range(ATTN_KV_HEADS):
        v_ref[0, :, 2 * hk * HEAD_DIM:(2 * hk + 1) * HEAD_DIM] = vv[:, hk * HEAD_DIM:(hk + 1) * HEAD_DIM]
        v_ref[0, :, (2 * hk + 1) * HEAD_DIM:(2 * hk + 2) * HEAD_DIM] = jnp.ones((vv.shape[0], HEAD_DIM), BF16)
    off += ATTN_KV_W
    gq_ref[0] = proj(off, GLA_K_W)
    off += GLA_K_W
    gk_ref[0] = proj(off, GLA_K_W)
    off += GLA_K_W
    gv_ref[0] = proj(off, GLA_V_W).astype(BF16)
    og_ref[0] = proj(0, GLA_V_W, wt_ref)
    lr_ref[0] = proj(GLA_V_W, LANES, wt_ref)


def _in_proj(x, g1, sc, sh, w_main, w_tail, qg, kg, cos_t, sin_t, tm):
    b, t, d = x.shape
    row = lambda bi, i: (bi, i, 0)
    vec = lambda bi, i: (bi, 0, 0)
    const = lambda bi, i: (0, 0)
    tab = lambda bi, i: (i, 0)
    widths = [(ATTN_Q_W, BF16), (ATTN_KV_W, BF16), (2 * ATTN_KV_W, BF16), (GLA_K_W, F32), (GLA_K_W, F32),
              (GLA_V_W, BF16), (GLA_V_W, F32), (LANES, F32)]
    return pl.pallas_call(
        _in_proj_kernel,
        grid=(b, t // tm),
        in_specs=[
            pl.BlockSpec((1, tm, d), row),
            pl.BlockSpec((1, d), const),
            pl.BlockSpec((1, 1, d), vec),
            pl.BlockSpec((1, 1, d), vec),
            pl.BlockSpec(w_main.shape, const, pipeline_mode=pl.Buffered(1)),
            pl.BlockSpec(w_tail.shape, const, pipeline_mode=pl.Buffered(1)),
            pl.BlockSpec((1, HEAD_DIM), const),
            pl.BlockSpec((1, HEAD_DIM), const),
            pl.BlockSpec((tm, HEAD_DIM), tab),
            pl.BlockSpec((tm, HEAD_DIM), tab),
        ],
        out_specs=[pl.BlockSpec((1, tm, wd), row) for wd, _ in widths],
        out_shape=[jax.ShapeDtypeStruct((b, t, wd), dt) for wd, dt in widths],
        compiler_params=_cparams("arbitrary", "arbitrary"),
        name="in_proj",
    )(x, g1, sc, sh, w_main, w_tail, qg, kg, cos_t, sin_t)


def _attn_kernel(q_ref, k_ref, v_ref, o_ref):
    def scores(h):
        hk = h // GQA_GROUP
        q = q_ref[0, :, h * HEAD_DIM:(h + 1) * HEAD_DIM]
        k = k_ref[0, :, hk * HEAD_DIM:(hk + 1) * HEAD_DIM]
        return lax.dot_general(q, k, (((1,), (1,)), ((), ())), preferred_element_type=F32)

    s_next = scores(0)
    for h in range(ATTN_HEADS):
        hk = h // GQA_GROUP
        s = s_next
        if h + 1 < ATTN_HEADS:
            s_next = scores(h + 1)
        p = jnp.exp2(s - jnp.max(s, axis=-1, keepdims=True))
        ol = jnp.dot(p.astype(BF16), v_ref[0, :, 2 * hk * HEAD_DIM:2 * (hk + 1) * HEAD_DIM],
                     preferred_element_type=F32)
        o_ref[0, :, h * HEAD_DIM:(h + 1) * HEAD_DIM] = (
            ol[:, :HEAD_DIM] / ol[:, HEAD_DIM:HEAD_DIM + 1]).astype(BF16)


def _attn(q, k_all, v_all, tq):
    b, t, _ = q.shape
    tk = k_all.shape[1]
    return pl.pallas_call(
        _attn_kernel,
        grid=(b, t // tq),
        in_specs=[
            pl.BlockSpec((1, tq, ATTN_Q_W), lambda bi, i: (bi, i, 0)),
            pl.BlockSpec((1, tk, ATTN_KV_W), lambda bi, i: (bi, 0, 0)),
            pl.BlockSpec((1, tk, 2 * ATTN_KV_W), lambda bi, i: (bi, 0, 0)),
        ],
        out_specs=pl.BlockSpec((1, tq, ATTN_Q_W), lambda bi, i: (bi, i, 0)),
        out_shape=jax.ShapeDtypeStruct((b, t, ATTN_Q_W), BF16),
        compiler_params=_cparams("arbitrary", "arbitrary"),
        name="attn",
    )(q, k_all, v_all)


def _log_sigmoid(x):
    return jnp.minimum(x, 0.0) - jnp.log(1.0 + jnp.exp(-jnp.abs(x)))


def _split3(x):
    hi = x.astype(BF16)
    r = x - hi.astype(F32)
    mid = r.astype(BF16)
    lo = (r - mid.astype(F32)).astype(BF16)
    return hi, mid, lo


def _dot_exact_lhs(m, x):
    return sum(jnp.dot(m, part, preferred_element_type=F32) for part in _split3(x))


def _prefix_operator(n):
    ri = lax.broadcasted_iota(jnp.int32, (n, n), 0)
    ci = lax.broadcasted_iota(jnp.int32, (n, n), 1)
    same = lax.shift_right_logical(ri, GLA_CHUNK_LOG2) == lax.shift_right_logical(ci, GLA_CHUNK_LOG2)
    return (same & (ci <= ri)).astype(BF16)


def _gate_prep(q, k, lr, w2s, bias2, prefix):
    c = GLA_CHUNK
    hi = lr.astype(BF16).astype(F32)
    mid = (lr - hi).astype(BF16).astype(F32)
    lhs = (hi + pltpu.roll(mid, 2 * GLA_GATE_RANK, 1) + pltpu.roll(hi, 4 * GLA_GATE_RANK, 1)).astype(BF16)
    logits = jnp.dot(lhs, w2s, preferred_element_type=F32) + bias2
    g = _log_sigmoid(logits) * (1.0 / GLA_GATE_NORMALIZER)
    pre = _dot_exact_lhs(prefix, g)
    tot = jnp.concatenate([jnp.broadcast_to(pre[lo + c - 1:lo + c, :], (c, pre.shape[1]))
                           for lo in range(0, pre.shape[0], c)], axis=0)
    dk = GLA_DK
    bcs = (pre[:, :dk], tot[:, dk:] - pre[:, dk:] + g[:, dk:])
    out = []
    for d, bc in enumerate(bcs):
        b_end = tot[:, d * dk:(d + 1) * dk]
        qe = (q * (GLA_DK ** -0.5) * jnp.exp(bc)).astype(BF16)
        ke = (k * jnp.exp(-bc)).astype(BF16)
        kend = (k * jnp.exp(b_end - bc)).astype(BF16)
        out.append((qe, ke, kend, b_end))
    return out


def _state_step(s_ref, v, kend, dec):
    upd = lax.dot_general(v, kend, (((0,), (0,)), ((), ())), preferred_element_type=F32)
    s_ref[...] = s_ref[...] * dec + upd


def _gla_kernel(q_ref, k_ref, v_ref, lr_ref, og_ref, qc_ref, kc_ref, vc_ref, lrc_ref,
                w2_ref, b_ref, ng_ref, o_ref,
                s_ref, qe_ref, ke_ref, kend_ref, dec_ref, kendc_ref, decc_ref, sbf_ref):
    c = GLA_CHUNK
    t = q_ref.shape[1]
    tc = qc_ref.shape[1]
    nc = t // c
    tile = GLA_PREP_TILE
    cpt = tile // c
    dirs = (0, 1)

    def store_dec(ref, d, base, b_end):
        for ch in range(b_end.shape[0] // c):
            ref[d, pl.ds(base + ch, 1), :] = jnp.exp(b_end[ch * c:ch * c + 1, :])

    prefix = _prefix_operator(tile)

    for i in range(tc // tile):
        rows = slice(i * tile, (i + 1) * tile)
        prep_c = _gate_prep(qc_ref[0, rows, :], kc_ref[0, rows, :], lrc_ref[0, rows, :],
                            w2_ref[...], b_ref[...], prefix)
        for d, (_, _, kend, b_end) in enumerate(prep_c):
            kendc_ref[d, rows, :] = kend
            store_dec(decc_ref, d, i * cpt, b_end)
    s_ref[...] = jnp.zeros_like(s_ref)
    for j in range(tc // c):
        for d in dirs:
            ch = j if d == 0 else tc // c - 1 - j
            _state_step(s_ref.at[d], vc_ref[0, ch * c:(ch + 1) * c, :], kendc_ref[d, ch * c:(ch + 1) * c, :],
                        decc_ref[d, ch:ch + 1, :])

    def prep(i, carry):
        lo = pl.multiple_of(i * tile, tile)
        rows = pl.ds(lo, tile)
        prep_l = _gate_prep(q_ref[0, rows, :], k_ref[0, rows, :], lr_ref[0, rows, :],
                            w2_ref[...], b_ref[...], prefix)
        for d, (qe, ke, kend, b_end) in enumerate(prep_l):
            qe_ref[d, rows, :] = qe
            ke_ref[d, rows, :] = ke
            kend_ref[d, rows, :] = kend
            store_dec(dec_ref, d, i * cpt, b_end)
        return carry

    lax.fori_loop(0, t // tile, prep, 0, unroll=2)

    def scan(j, carry):
        for d in dirs:
            ch = j if d == 0 else nc - 1 - j
            rows = pl.ds(pl.multiple_of(ch * c, c), c)
            sbf_ref[d, ch] = s_ref[d].astype(BF16)
            _state_step(s_ref.at[d], v_ref[0, rows, :], kend_ref[d, rows, :], dec_ref[d, pl.ds(ch, 1), :])
        return carry

    lax.fori_loop(0, nc, scan, 0, unroll=8)

    ri = lax.broadcasted_iota(jnp.int32, (c, c), 0)
    ci = lax.broadcasted_iota(jnp.int32, (c, c), 1)
    masks = (ci <= ri, ci >= ri)

    group = GLA_OUT_GROUP

    def out(i, carry):
        chunks = [i * group + u for u in range(group)]
        rows = [pl.ds(pl.multiple_of(ch * c, c), c) for ch in chunks]
        qes = [[qe_ref[d, r, :] for d in dirs] for r in rows]
        scores = [[lax.dot_general(qes[u][d], ke_ref[d, rows[u], :], (((1,), (1,)), ((), ())),
                                   preferred_element_type=F32) for d in dirs] for u in range(group)]
        inter = [[lax.dot_general(qes[u][d], sbf_ref[d, chunks[u]], (((1,), (1,)), ((), ())),
                                  preferred_element_type=F32) for d in dirs] for u in range(group)]
        for u in range(group):
            v = v_ref[0, rows[u], :]
            o = inter[u][0] + inter[u][1]
            for d in dirs:
                a = jnp.where(masks[d], scores[u][d], 0.0).astype(BF16)
                o = o + jnp.dot(a, v, preferred_element_type=F32)
            on = o * lax.rsqrt(jnp.mean(o * o, axis=-1, keepdims=True) + EPS) * ng_ref[...]
            og = og_ref[0, rows[u], :]
            o_ref[0, rows[u], :] = (on * (og * jax.nn.sigmoid(og))).astype(BF16)
        return carry

    lax.fori_loop(0, nc // group, out, 0)


def _gla(gq, gk, gv, lr, og, gqc, gkc, gvc, lrc, w2, bias, ng):
    b, t, _ = gq.shape
    tc = gqc.shape[1]
    hk = lambda bi, h: (bi, 0, h)
    h0 = lambda bi, h: (bi, 0, 0)
    return pl.pallas_call(
        _gla_kernel,
        grid=(b, GLA_HEADS),
        in_specs=[
            pl.BlockSpec((1, t, GLA_DK), hk),
            pl.BlockSpec((1, t, GLA_DK), hk),
            pl.BlockSpec((1, t, GLA_DV), hk),
            pl.BlockSpec((1, t, LANES), h0),
            pl.BlockSpec((1, t, GLA_DV), hk),
            pl.BlockSpec((1, tc, GLA_DK), hk),
            pl.BlockSpec((1, tc, GLA_DK), hk),
            pl.BlockSpec((1, tc, GLA_DV), hk),
            pl.BlockSpec((1, tc, LANES), h0),
            pl.BlockSpec((None, LANES, 2 * GLA_DK), lambda bi, h: (h, 0, 0)),
            pl.BlockSpec((None, 1, 2 * GLA_DK), lambda bi, h: (h, 0, 0)),
            pl.BlockSpec((1, GLA_DV), lambda bi, h: (0, 0)),
        ],
        out_specs=pl.BlockSpec((1, t, GLA_DV), hk),
        out_shape=jax.ShapeDtypeStruct((b, t, GLA_V_W), BF16),
        scratch_shapes=[
            pltpu.VMEM((2, GLA_DV, GLA_DK), F32),
            pltpu.VMEM((2, t, GLA_DK), BF16),
            pltpu.VMEM((2, t, GLA_DK), BF16),
            pltpu.VMEM((2, t, GLA_DK), BF16),
            pltpu.VMEM((2, t // GLA_CHUNK, GLA_DK), F32),
            pltpu.VMEM((2, tc, GLA_DK), BF16),
            pltpu.VMEM((2, tc // GLA_CHUNK, GLA_DK), F32),
            pltpu.VMEM((2, t // GLA_CHUNK, GLA_DV, GLA_DK), BF16),
        ],
        compiler_params=_cparams("arbitrary", "arbitrary"),
        name="gla",
    )(gq, gk, gv, lr, og, gqc, gkc, gvc, lrc, w2, bias, ng)


def _token_pitch(d):
    rows = d // LANES
    return rows + 4 if rows % 8 == 0 else rows


def _store_token_tiles(ref, x):
    n = x.shape[0]
    k = x.shape[1] // LANES
    pitch = ref.shape[0] // n
    for s in range(pitch):
        slab = x[:, s * LANES:(s + 1) * LANES] if s < k else jnp.zeros((n, LANES), x.dtype)
        ref[pl.ds(s, n, stride=pitch), :] = slab


def _load_token_tiles(ref, n, d):
    pitch = ref.shape[0] // n
    return jnp.concatenate([ref[pl.ds(s, n, stride=pitch), :] for s in range(d // LANES)], axis=1)


def _out_proj_kernel(attn_ref, gla_ref, wo_ref, x_ref, gt_ref, g2_ref, sc_ref, sh_ref, wr_ref, br_ref,
                     x1_ref, h2_ref, ids_ref, wts_ref):
    y = jnp.dot(attn_ref[0], wo_ref[0:ATTN_Q_W, :], preferred_element_type=F32)
    y = y + jnp.dot(gla_ref[0], wo_ref[ATTN_Q_W:, :], preferred_element_type=F32)
    x1 = x_ref[0] + gt_ref[0] * y
    x1_ref[0] = x1
    xn = x1 * lax.rsqrt(jnp.mean(x1 * x1, axis=-1, keepdims=True) + EPS)
    h2 = xn * g2_ref[...] * (1.0 + sc_ref[0]) + sh_ref[0]
    hi = h2.astype(BF16)
    hi_f = hi.astype(F32)

    _store_token_tiles(h2_ref.at[0], h2)

    mid = (h2 - hi_f).astype(BF16)
    both = jnp.dot(hi, wr_ref[...], preferred_element_type=F32)
    logits = (both[:, :LANES] + both[:, LANES:]
              + jnp.dot(mid, wr_ref[:, :LANES], preferred_element_type=F32)) + br_ref[...]
    lane = lax.broadcasted_iota(jnp.int32, logits.shape, 1)
    lane_f = lane.astype(F32)
    neg = jnp.float32(-jnp.inf)

    def first_argmax(vals):
        m = jnp.max(vals, axis=-1, keepdims=True)
        idx = jnp.min(jnp.where(vals == m, lane_f, float(LANES)), axis=-1, keepdims=True)
        return m, idx

    lg = jnp.where(lane < N_GROUPS, logits, neg)
    mg, grp = first_argmax(lg)
    pg_sel = 1.0 / jnp.sum(jnp.exp(lg - mg), axis=-1, keepdims=True)
    lo = N_GROUPS + grp * EXPERTS_PER_GROUP
    in_grp = (lane_f >= lo) & (lane_f < lo + EXPERTS_PER_GROUP)
    le = jnp.where(in_grp, logits, neg)
    v1, i1 = first_argmax(le)
    v2, i2 = first_argmax(jnp.where(lane_f == i1, neg, le))
    e2 = jnp.exp(v2 - v1)
    w1 = pg_sel / (1.0 + e2)
    w2 = pg_sel * e2 / (1.0 + e2)
    ids = jnp.where(lane == 0, i1 - N_GROUPS, jnp.where(lane == 1, i2 - N_GROUPS, 0.0))
    ids_ref[0] = ids.astype(jnp.int32)
    wts_ref[0] = jnp.where(lane == 0, w1, jnp.where(lane == 1, w2, 0.0))


def _out_proj(attn, gla, wo, x, gt1, g2, sc2, sh2, wr, br, tm):
    b, t, d = x.shape
    row = lambda bi, i: (bi, i, 0)
    vec = lambda bi, i: (bi, 0, 0)
    const = lambda bi, i: (0, 0)
    return pl.pallas_call(
        _out_proj_kernel,
        grid=(b, t // tm),
        in_specs=[
            pl.BlockSpec((1, tm, ATTN_Q_W), row),
            pl.BlockSpec((1, tm, GLA_V_W), row),
            pl.BlockSpec(wo.shape, const, pipeline_mode=pl.Buffered(1)),
            pl.BlockSpec((1, tm, d), row),
            pl.BlockSpec((1, 1, d), vec),
            pl.BlockSpec((1, d), const),
            pl.BlockSpec((1, 1, d), vec),
            pl.BlockSpec((1, 1, d), vec),
            pl.BlockSpec((d, 2 * LANES), const),
            pl.BlockSpec((1, LANES), const),
        ],
        out_specs=[
            pl.BlockSpec((1, tm, d), row),
            pl.BlockSpec((1, tm * _token_pitch(d), LANES), row),
            pl.BlockSpec((1, tm, LANES), row),
            pl.BlockSpec((1, tm, LANES), row),
        ],
        out_shape=[
            jax.ShapeDtypeStruct((b, t, d), F32),
            jax.ShapeDtypeStruct((b, t * _token_pitch(d), LANES), F32),
            jax.ShapeDtypeStruct((b, t, LANES), jnp.int32),
            jax.ShapeDtypeStruct((b, t, LANES), F32),
        ],
        compiler_params=_cparams("arbitrary", "arbitrary"),
        name="out_proj",
    )(attn, gla, wo, x, gt1, g2, sc2, sh2, wr, br)


def _moe_kernel(eidx_ref, eblk0_ref, enb_ref, bs0_ref, bn_ref, nblk_ref, order_ref,
                h2_hbm, w1_ref, w3_ref, w2_ref, y_hbm,
                xbuf, ybuf, gsem, ssem):
    del eidx_ref
    e = pl.program_id(0)
    nblk = nblk_ref[0]
    kx = xbuf.shape[1] // MOE_BLOCK
    ky = ybuf.shape[1] // MOE_BLOCK

    def gather_copy(sl, hbm_row, j, rows=1):
        return pltpu.make_async_copy(h2_hbm.at[pl.ds(hbm_row * kx, rows * kx)],
                                     xbuf.at[sl, pl.ds(j * kx, rows * kx)], gsem.at[sl])

    def for_rows(n, per_group, per_row):
        ng = lax.shift_right_logical(n, ROW_GROUP_LOG2)
        lax.fori_loop(0, ng, lambda g, c: (per_group(g * ROW_GROUP), c)[1], 0)
        lax.fori_loop(ng * ROW_GROUP, n, lambda j, c: (per_row(j), c)[1], 0)

    def start_gather(blk, sl):
        s = bs0_ref[blk]

        def start_row(j):
            gather_copy(sl, lax.shift_right_logical(order_ref[s + j], 1), j).start()

        def group(j0):
            for u in range(ROW_GROUP):
                start_row(j0 + u)

        for_rows(bn_ref[blk], group, start_row)

    def wait_gather(blk, sl):
        for_rows(bn_ref[blk], lambda j0: gather_copy(sl, 0, 0, ROW_GROUP).wait(),
                 lambda j: gather_copy(sl, 0, 0).wait())

    def output_copy(blk, sl):
        return pltpu.make_async_copy(
            ybuf.at[sl], y_hbm.at[pl.ds(bs0_ref[blk] * ky, MOE_BLOCK * ky)], ssem.at[sl])

    @pl.when(e == 0)
    def _():
        xbuf[...] = jnp.zeros_like(xbuf)
        ybuf[...] = jnp.zeros_like(ybuf)
        for g0 in range(GATHER_AHEAD):
            @pl.when(g0 < nblk)
            def _():
                start_gather(g0, g0)

    def block(g, carry):
        slot = g & 1
        xslot = lax.rem(g, GATHER_AHEAD + 1)

        @pl.when(g + GATHER_AHEAD < nblk)
        def _():
            start_gather(g + GATHER_AHEAD, lax.rem(g + GATHER_AHEAD, GATHER_AHEAD + 1))

        wait_gather(g, xslot)

        def expert_rows(rows):
            x = _load_token_tiles(xbuf.at[xslot, pl.ds(0, rows * kx)], rows, w1_ref.shape[0])
            a = jnp.dot(x, w1_ref[...], preferred_element_type=F32)
            gate = jnp.dot(x, w3_ref[...], preferred_element_type=F32)
            hid = a * jax.nn.sigmoid(a) * gate
            y = jnp.dot(hid, w2_ref[...], preferred_element_type=F32)
            _store_token_tiles(ybuf.at[slot, pl.ds(0, rows * ky)], y)

        half = MOE_BLOCK // 2

        @pl.when(bn_ref[g] > half)
        def _():
            expert_rows(MOE_BLOCK)

        @pl.when(bn_ref[g] <= half)
        def _():
            expert_rows(half)

        @pl.when(g >= 1)
        def _():
            output_copy(g - 1, 1 - slot).wait()

        output_copy(g, slot).start()
        return carry

    @pl.when(enb_ref[e] > 0)
    def _():
        lax.fori_loop(eblk0_ref[e], eblk0_ref[e] + enb_ref[e], block, 0)

    @pl.when(e == pl.num_programs(0) - 1)
    def _():
        output_copy(nblk - 1, (nblk - 1) & 1).wait()
        ybuf[0] = jnp.zeros_like(ybuf[0])
        pad_rows = MOE_BLOCK * ky
        pad = pltpu.make_async_copy(ybuf.at[0], y_hbm.at[pl.ds(y_hbm.shape[0] - pad_rows, pad_rows)], ssem.at[0])
        pad.start()
        pad.wait()


def _moe(h2t, w1, w3, w2, eidx, eblk0, enb, bs0, bn, nblk, order):
    ne, d, ff = w1.shape
    kx = ky = _token_pitch(d)
    wmap = lambda e, eidx, *_: (eidx[e], 0, 0)
    grid_spec = pltpu.PrefetchScalarGridSpec(
        num_scalar_prefetch=7,
        grid=(ne,),
        in_specs=[
            pl.BlockSpec(memory_space=pl.ANY),
            pl.BlockSpec((None, d, ff), wmap),
            pl.BlockSpec((None, d, ff), wmap),
            pl.BlockSpec((None, ff, d), wmap),
        ],
        out_specs=pl.BlockSpec(memory_space=pl.ANY),
        scratch_shapes=[
            pltpu.VMEM((GATHER_AHEAD + 1, MOE_BLOCK * kx, LANES), F32),
            pltpu.VMEM((2, MOE_BLOCK * ky, LANES), F32),
            pltpu.SemaphoreType.DMA((GATHER_AHEAD + 1,)),
            pltpu.SemaphoreType.DMA((2,)),
        ],
    )
    return pl.pallas_call(
        _moe_kernel,
        grid_spec=grid_spec,
        out_shape=jax.ShapeDtypeStruct(((order.shape[0] + MOE_BLOCK) * ky, LANES), F32),
        compiler_params=_cparams("arbitrary"),
        name="moe",
    )(eidx, eblk0, enb, bs0, bn, nblk, order, h2t, w1, w3, w2)


def _combine_kernel(pos_ref, x1_ref, ys_hbm, wts_ref, gt_ref, o_ref, ybuf, sem):
    tm = x1_ref.shape[1]
    ky = ybuf.shape[2] // tm
    nt = pl.num_programs(1)
    n = pl.program_id(0) * nt + pl.program_id(1)
    slot = n & 1

    def row_copy(sl, k, pos, j, rows=1):
        return pltpu.make_async_copy(ys_hbm.at[pl.ds(pos * ky, rows * ky)],
                                     ybuf.at[sl, k, pl.ds(j * ky, rows * ky)], sem.at[sl])

    def start_tile(tile, sl):
        base = tile * (tm * TOP_K)

        def group(g, carry):
            for u in range(ROW_GROUP):
                row_copy(sl, u % TOP_K, pos_ref[base + g * ROW_GROUP + u],
                         g * (ROW_GROUP // TOP_K) + u // TOP_K).start()
            return carry

        lax.fori_loop(0, tm * TOP_K // ROW_GROUP, group, 0)

    @pl.when(n == 0)
    def _():
        start_tile(0, 0)

    @pl.when(n + 1 < pl.num_programs(0) * nt)
    def _():
        start_tile(n + 1, 1 - slot)

    for k in range(TOP_K):
        row_copy(slot, k, 0, 0, rows=tm).wait()
    w = wts_ref[0]
    d = x1_ref.shape[2]
    ff = (w[:, 0:1] * _load_token_tiles(ybuf.at[slot, 0], tm, d)
          + w[:, 1:2] * _load_token_tiles(ybuf.at[slot, 1], tm, d))
    o_ref[0] = x1_ref[0] + gt_ref[0] * ff


def _combine(x1, ys, pos, wts, gt2, tm):
    b, t, d = x1.shape
    ky = _token_pitch(d)
    row = lambda bi, i, pos: (bi, i, 0)
    vec = lambda bi, i, pos: (bi, 0, 0)
    grid_spec = pltpu.PrefetchScalarGridSpec(
        num_scalar_prefetch=1,
        grid=(b, t // tm),
        in_specs=[
            pl.BlockSpec((1, tm, d), row),
            pl.BlockSpec(memory_space=pl.ANY),
            pl.BlockSpec((1, tm, LANES), row),
            pl.BlockSpec((1, 1, d), vec),
        ],
        out_specs=pl.BlockSpec((1, tm, d), row),
        scratch_shapes=[
            pltpu.VMEM((2, TOP_K, tm * ky, LANES), F32),
            pltpu.SemaphoreType.DMA((2,)),
        ],
    )
    return pl.pallas_call(
        _combine_kernel,
        grid_spec=grid_spec,
        out_shape=jax.ShapeDtypeStruct((b, t, d), F32),
        compiler_params=_cparams("arbitrary", "arbitrary"),
        name="combine",
    )(pos, x1, ys, wts, gt2)


def _rope_tables(t):
    rows = t // GRID_W
    n_freq = HEAD_DIM // 4
    inv = ROPE_THETA ** (-jnp.arange(n_freq, dtype=F32) / n_freq)
    ar = jnp.arange(rows, dtype=F32)[:, None] * inv
    ac = jnp.arange(GRID_W, dtype=F32)[:, None] * inv

    def expand(fr, fc, sign):
        by_row = jnp.broadcast_to(fr[:, None, :], (rows, GRID_W, n_freq))
        by_col = jnp.broadcast_to(fc[None, :, :], (rows, GRID_W, n_freq))
        return jnp.concatenate([sign * by_row, by_row, sign * by_col, by_col], axis=2).reshape(t, HEAD_DIM)

    return expand(jnp.cos(ar), jnp.cos(ac), 1.0), expand(jnp.sin(ar), jnp.sin(ac), -1.0)


def _gate_weights(w2, bias):
    r = GLA_GATE_RANK
    wh = w2.reshape(2, r, GLA_HEADS, GLA_DK).transpose(2, 0, 1, 3)
    w = jnp.zeros((GLA_HEADS, 2 * r, 2 * GLA_DK), F32)
    w = w.at[:, 0:r, 0:GLA_DK].set(wh[:, 0]).at[:, r:2 * r, GLA_DK:].set(wh[:, 1])
    hi = w.astype(BF16)
    mid = (w - hi.astype(F32)).astype(BF16)
    pad = jnp.zeros((GLA_HEADS, LANES - 6 * r, 2 * GLA_DK), BF16)
    w2s = jnp.concatenate([hi, hi, mid, pad], axis=1)
    bias2 = bias.reshape(2, GLA_HEADS, GLA_DK).transpose(1, 0, 2).reshape(GLA_HEADS, 1, 2 * GLA_DK)
    return w2s, bias2


def _block_plan(eid_flat, nb):
    order = jnp.argsort(eid_flat).astype(jnp.int32)
    eids = jnp.arange(N_EXPERTS, dtype=jnp.int32)
    of_assign = (eid_flat[:, None] == eids[None, :]).astype(jnp.int32)
    counts = jnp.sum(of_assign, axis=0)
    starts = jnp.cumsum(counts) - counts
    nblk_e = (counts + MOE_BLOCK - 1) // MOE_BLOCK
    bends = jnp.cumsum(nblk_e)
    bstarts = bends - nblk_e
    nblk = bends[-1]
    prev_used = lax.cummax(jnp.where(counts > 0, eids, -1))
    eidx = jnp.where(prev_used >= 0, prev_used, jnp.argmax(counts > 0)).astype(jnp.int32)
    bi = jnp.arange(nb, dtype=jnp.int32)
    bic = jnp.minimum(bi, jnp.maximum(nblk - 1, 0))
    bexp = jnp.minimum(jnp.sum(bends[None, :] <= bic[:, None], axis=1), N_EXPERTS - 1)
    of_block = (bexp[:, None] == eids[None, :]).astype(jnp.int32)

    def lookup(per_expert):
        return jnp.sum(of_block * per_expert[None, :], axis=1)

    r0 = (bic - lookup(bstarts)) * MOE_BLOCK
    bs0 = (lookup(starts) + r0).astype(jnp.int32)
    bn = jnp.where(bi < nblk, jnp.minimum(lookup(counts) - r0, MOE_BLOCK), 0).astype(jnp.int32)
    pos = jnp.argsort(order).astype(jnp.int32)
    plan = (eidx, bstarts.astype(jnp.int32), nblk_e.astype(jnp.int32), bs0, bn,
            nblk.reshape(1).astype(jnp.int32), order)
    return plan, pos


def kernel(x, c, ctx, c_ctx, w_ada, b_ada, norm1_g, w_in, q_norm_g, k_norm_g, gla_gate_w2, gla_gate_b, gla_norm_g, w_out, norm2_g, router_grp_w, router_grp_b, router_exp_w, router_exp_b, moe_w1, moe_w3, moe_w2):
    b, t, d = x.shape
    tc = ctx.shape[1]
    depth = w_ada.shape[0]
    assert depth == 1, "single-layer stack: the context stream only feeds keys/values and GLA states"
    layer = 0

    c8 = jnp.zeros((8, d), F32).at[0:b].set(c).at[b].set(c_ctx)
    mod = _ada(c8, w_ada[layer], b_ada[layer])
    sh1, sc1, gt1, sh2, sc2, gt2 = [mod[0:b, i * d:(i + 1) * d].reshape(b, 1, d) for i in range(6)]
    sh1c, sc1c = [jnp.broadcast_to(mod[b, i * d:(i + 1) * d].reshape(1, 1, d), (b, 1, d)) for i in range(2)]

    w_main, w_tail = _w_in_parts(jnp.swapaxes(w_in[layer], 0, 1))

    cos_t, sin_t = _rope_tables(t)
    g1 = norm1_g[layer].reshape(1, d)
    qg = q_norm_g[layer].reshape(1, HEAD_DIM)
    kg = k_norm_g[layer].reshape(1, HEAD_DIM)
    q, k, v, gq, gk, gv, og, lr = _in_proj(x, g1, sc1, sh1, w_main, w_tail, qg, kg, cos_t, sin_t, 512)
    ones_t = jnp.ones((tc, HEAD_DIM), F32)
    _, kc, vc, gqc, gkc, gvc, _, lrc = _in_proj(ctx, g1, sc1c, sh1c, w_main, w_tail, qg, kg,
                                                ones_t, jnp.zeros_like(ones_t), tc)

    attn = _attn(q, jnp.concatenate([kc, k], axis=1), jnp.concatenate([vc, v], axis=1), 256)
    w2s, bias2 = _gate_weights(gla_gate_w2[layer], gla_gate_b[layer])
    gla = _gla(gq, gk, gv, lr, og, gqc, gkc, gvc, lrc, w2s, bias2, gla_norm_g[layer].reshape(1, GLA_DV))

    wr = jnp.concatenate([router_grp_w[layer], router_exp_w[layer],
                          jnp.zeros((d, LANES - N_GROUPS - N_EXPERTS), F32)], axis=1)
    br = jnp.concatenate([router_grp_b[layer], router_exp_b[layer],
                          jnp.zeros((LANES - N_GROUPS - N_EXPERTS,), F32)]).reshape(1, LANES)
    wr_hi = wr.astype(BF16)
    wr_parts = jnp.concatenate([wr_hi, (wr - wr_hi.astype(F32)).astype(BF16)], axis=1)
    x1, h2t, ids, wts = _out_proj(attn, gla, w_out[layer].astype(BF16), x, gt1, norm2_g[layer].reshape(1, d),
                                  sc2, sh2, wr_parts, br, 512)

    m = b * t
    n_assign = m * TOP_K
    nb = -(-(n_assign + N_EXPERTS * (MOE_BLOCK - 1)) // MOE_BLOCK)
    eid_flat = ids[:, :, 0:TOP_K].reshape(n_assign)
    plan, pos = _block_plan(eid_flat, nb)
    ys = _moe(h2t.reshape(-1, LANES), moe_w1[layer], moe_w3[layer], moe_w2[layer], *plan)
    return _combine(x1, ys, pos, wts, gt2, 256)
```

```python
import functools

import jax
import jax.numpy as jnp
from jax import lax
from jax.experimental import pallas as pl
from jax.experimental.pallas import tpu as pltpu

EPS = 1e-6
GRID_W = 64
ROPE_THETA = 10000.0

ATTN_HEADS = 8
ATTN_KV_HEADS = 2
HEAD_DIM = 128
GQA_GROUP = ATTN_HEADS // ATTN_KV_HEADS

GLA_HEADS = 4
GLA_DK = 128
GLA_DV = 256
GLA_GATE_RANK = 16
GLA_GATE_NORMALIZER = 16.0
GLA_CHUNK = 64
GLA_CHUNK_LOG2 = 6
GLA_PREP_TILE = 256
GLA_OUT_GROUP = 8

N_GROUPS = 8
EXPERTS_PER_GROUP = 8
N_EXPERTS = N_GROUPS * EXPERTS_PER_GROUP
TOP_K = 2
MOE_BLOCK = 256
ROW_GROUP_LOG2 = 3
ROW_GROUP = 1 << ROW_GROUP_LOG2
GATHER_AHEAD = 2

LOG2_E = 1.4426950408889634
LANES = 128
VMEM_LIMIT = 56 * 1024 * 1024

ATTN_Q_W = ATTN_HEADS * HEAD_DIM
ATTN_KV_W = ATTN_KV_HEADS * HEAD_DIM
GLA_K_W = GLA_HEADS * GLA_DK
GLA_V_W = GLA_HEADS * GLA_DV

BF16 = jnp.bfloat16
F32 = jnp.float32


def _cparams(*sem):
    return pltpu.CompilerParams(dimension_semantics=sem, vmem_limit_bytes=VMEM_LIMIT)


def _ada_kernel(c_ref, w_ref, b_ref, o_ref):
    c = c_ref[...]
    s = c * jax.nn.sigmoid(c)
    o_ref[...] = jnp.dot(s.astype(BF16), w_ref[...].astype(BF16), preferred_element_type=F32) + b_ref[...]


def _ada(c8, w, b):
    d, n = w.shape
    tn = 1024
    return pl.pallas_call(
        _ada_kernel,
        grid=(n // tn,),
        in_specs=[
            pl.BlockSpec((8, d), lambda j: (0, 0)),
            pl.BlockSpec((d, tn), lambda j: (0, j)),
            pl.BlockSpec((1, tn), lambda j: (0, j)),
        ],
        out_specs=pl.BlockSpec((8, tn), lambda j: (0, j)),
        out_shape=jax.ShapeDtypeStruct((8, n), F32),
        compiler_params=_cparams("arbitrary"),
        name="ada",
    )(c8, w, b.reshape(1, n))


def _w_main_kernel(wt_ref, o_ref):
    o_ref[...] = wt_ref[...].T.astype(BF16)


def _w_tail_kernel(wt_hbm, o_ref, tail, sem):
    r = 2 * GLA_GATE_RANK
    n_tail = tail.shape[0]
    copy = pltpu.make_async_copy(wt_hbm.at[pl.ds(wt_hbm.shape[0] - n_tail, n_tail)], tail, sem)
    copy.start()
    copy.wait()
    o_ref[:, :GLA_V_W] = tail[r:, :].T.astype(BF16)
    lowrank = jnp.concatenate([tail[:r, :], jnp.zeros((LANES - r, tail.shape[1]), F32)], axis=0)
    o_ref[:, GLA_V_W:] = lowrank.T.astype(BF16)


def _w_in_parts(wt):
    n, d = wt.shape
    n_main = n - GLA_V_W - 2 * GLA_GATE_RANK
    tn = 512
    main = pl.pallas_call(
        _w_main_kernel,
        grid=(n_main // tn,),
        in_specs=[pl.BlockSpec((tn, d), lambda j: (j, 0))],
        out_specs=pl.BlockSpec((d, tn), lambda j: (0, j)),
        out_shape=jax.ShapeDtypeStruct((d, n_main), BF16),
        compiler_params=_cparams("arbitrary"),
        name="w_in_main",
    )(wt)
    tail = pl.pallas_call(
        _w_tail_kernel,
        in_specs=[pl.BlockSpec(memory_space=pl.ANY)],
        out_shape=jax.ShapeDtypeStruct((d, GLA_V_W + LANES), BF16),
        scratch_shapes=[pltpu.VMEM((n - n_main, d), F32), pltpu.SemaphoreType.DMA],
        compiler_params=pltpu.CompilerParams(vmem_limit_bytes=VMEM_LIMIT),
        name="w_in_tail",
    )(wt)
    return main, tail


def _swap32(y):
    lane = lax.broadcasted_iota(jnp.int32, y.shape, 1)
    return jnp.where((lane & 63) < 32, pltpu.roll(y, 96, 1), pltpu.roll(y, 32, 1))


def _head_norm_rope(a, g, cos, sin):
    y = a * lax.rsqrt(jnp.mean(a * a, axis=-1, keepdims=True) + EPS) * g
    return y * cos + _swap32(y) * sin


def _in_proj_kernel(x_ref, g1_ref, sc_ref, sh_ref, w_ref, wt_ref, qg_ref, kg_ref, cos_ref, sin_ref,
                    q_ref, k_ref, v_ref, gq_ref, gk_ref, gv_ref, og_ref, lr_ref):
    x = x_ref[0]
    xn = x * lax.rsqrt(jnp.mean(x * x, axis=-1, keepdims=True) + EPS)
    h = (xn * g1_ref[...] * (1.0 + sc_ref[0]) + sh_ref[0]).astype(BF16)
    cos = cos_ref[...]
    sin = sin_ref[...]

    def proj(off, width, w=w_ref):
        return jnp.dot(h, w[:, off:off + width], preferred_element_type=F32)

    qg = qg_ref[...] * (HEAD_DIM ** -0.5 * LOG2_E)
    off = 0
    for out_ref, gain, heads in ((q_ref, qg, ATTN_HEADS), (k_ref, kg_ref[...], ATTN_KV_HEADS)):
        for pair in range(heads // 2):
            a2 = proj(off, 2 * HEAD_DIM)
            for u in range(2):
                lo = (2 * pair + u) * HEAD_DIM
                a = a2[:, u * HEAD_DIM:(u + 1) * HEAD_DIM]
                out_ref[0, :, lo:lo + HEAD_DIM] = _head_norm_rope(a, gain, cos, sin).astype(BF16)
            off += 2 * HEAD_DIM
    vv = proj(off, ATTN_KV_W).astype(BF16)
    for hk in range(ATTN_KV_HEADS):
        v_ref[0, :, 2 * hk * HEAD_DIM:(2 * hk + 1) * HEAD_DIM] = vv[:, hk * HEAD_DIM:(hk + 1) * HEAD_DIM]
        v_ref[0, :, (2 * hk + 1) * HEAD_DIM:(2 * hk + 2) * HEAD_DIM] = jnp.ones((vv.shape[0], HEAD_DIM), BF16)
    off += ATTN_KV_W
    gq_ref[0] = proj(off, GLA_K_W)
    off += GLA_K_W
    gk_ref[0] = proj(off, GLA_K_W)
    off += GLA_K_W
    gv_ref[0] = proj(off, GLA_V_W).astype(BF16)
    og_ref[0] = proj(0, GLA_V_W, wt_ref)
    lr_ref[0] = proj(GLA_V_W, LANES, wt_ref)


def _in_proj(x, g1, sc, sh, w_main, w_tail, qg, kg, cos_t, sin_t, tm):
    b, t, d = x.shape
    row = lambda bi, i: (bi, i, 0)
    vec = lambda bi, i: (bi, 0, 0)
    const = lambda bi, i: (0, 0)
    tab = lambda bi, i: (i, 0)
    widths = [(ATTN_Q_W, BF16), (ATTN_KV_W, BF16), (2 * ATTN_KV_W, BF16), (GLA_K_W, F32), (GLA_K_W, F32),
              (GLA_V_W, BF16), (GLA_V_W, F32), (LANES, F32)]
    return pl.pallas_call(
        _in_proj_kernel,
        grid=(b, t // tm),
        in_specs=[
            pl.BlockSpec((1, tm, d), row),
            pl.BlockSpec((1, d), const),
            pl.BlockSpec((1, 1, d), vec),
            pl.BlockSpec((1, 1, d), vec),
            pl.BlockSpec(w_main.shape, const, pipeline_mode=pl.Buffered(1)),
            pl.BlockSpec(w_tail.shape, const, pipeline_mode=pl.Buffered(1)),
            pl.BlockSpec((1, HEAD_DIM), const),
            pl.BlockSpec((1, HEAD_DIM), const),
            pl.BlockSpec((tm, HEAD_DIM), tab),
            pl.BlockSpec((tm, HEAD_DIM), tab),
        ],
        out_specs=[pl.BlockSpec((1, tm, wd), row) for wd, _ in widths],
        out_shape=[jax.ShapeDtypeStruct((b, t, wd), dt) for wd, dt in widths],
        compiler_params=_cparams("arbitrary", "arbitrary"),
        name="in_proj",
    )(x, g1, sc, sh, w_main, w_tail, qg, kg, cos_t, sin_t)


def _attn_kernel(q_ref, k_ref, v_ref, o_ref):
    def scores(h):
        hk = h // GQA_GROUP
        q = q_ref[0, :, h * HEAD_DIM:(h + 1) * HEAD_DIM]
        k = k_ref[0, :, hk * HEAD_DIM:(hk + 1) * HEAD_DIM]
        return lax.dot_general(q, k, (((1,), (1,)), ((), ())), preferred_element_type=F32)

    s_next = scores(0)
    for h in range(ATTN_HEADS):
        hk = h // GQA_GROUP
        s = s_next
        if h + 1 < ATTN_HEADS:
            s_next = scores(h + 1)
        p = jnp.exp2(s - jnp.max(s, axis=-1, keepdims=True))
        ol = jnp.dot(p.astype(BF16), v_ref[0, :, 2 * hk * HEAD_DIM:2 * (hk + 1) * HEAD_DIM],
                     preferred_element_type=F32)
        o_ref[0, :, h * HEAD_DIM:(h + 1) * HEAD_DIM] = (
            ol[:, :HEAD_DIM] / ol[:, HEAD_DIM:HEAD_DIM + 1]).astype(BF16)


def _attn(q, k_all, v_all, tq):
    b, t, _ = q.shape
    tk = k_all.shape[1]
    return pl.pallas_call(
        _attn_kernel,
        grid=(b, t // tq),
        in_specs=[
            pl.BlockSpec((1, tq, ATTN_Q_W), lambda bi, i: (bi, i, 0)),
            pl.BlockSpec((1, tk, ATTN_KV_W), lambda bi, i: (bi, 0, 0)),
            pl.BlockSpec((1, tk, 2 * ATTN_KV_W), lambda bi, i: (bi, 0, 0)),
        ],
        out_specs=pl.BlockSpec((1, tq, ATTN_Q_W), lambda bi, i: (bi, i, 0)),
        out_shape=jax.ShapeDtypeStruct((b, t, ATTN_Q_W), BF16),
        compiler_params=_cparams("arbitrary", "arbitrary"),
        name="attn",
    )(q, k_all, v_all)


def _log_sigmoid(x):
    return jnp.minimum(x, 0.0) - jnp.log(1.0 + jnp.exp(-jnp.abs(x)))


def _split3(x):
    hi = x.astype(BF16)
    r = x - hi.astype(F32)
    mid = r.astype(BF16)
    lo = (r - mid.astype(F32)).astype(BF16)
    return hi, mid, lo


def _dot_exact_lhs(m, x):
    return sum(jnp.dot(m, part, preferred_element_type=F32) for part in _split3(x))


def _prefix_operator(n):
    ri = lax.broadcasted_iota(jnp.int32, (n, n), 0)
    ci = lax.broadcasted_iota(jnp.int32, (n, n), 1)
    same = lax.shift_right_logical(ri, GLA_CHUNK_LOG2) == lax.shift_right_logical(ci, GLA_CHUNK_LOG2)
    return (same & (ci <= ri)).astype(BF16)


def _gate_prep(q, k, lr, w2s, bias2, prefix):
    c = GLA_CHUNK
    hi = lr.astype(BF16).astype(F32)
    mid = (lr - hi).astype(BF16).astype(F32)
    lhs = (hi + pltpu.roll(mid, 2 * GLA_GATE_RANK, 1) + pltpu.roll(hi, 4 * GLA_GATE_RANK, 1)).astype(BF16)
    logits = jnp.dot(lhs, w2s, preferred_element_type=F32) + bias2
    g = _log_sigmoid(logits) * (1.0 / GLA_GATE_NORMALIZER)
    pre = _dot_exact_lhs(prefix, g)
    tot = jnp.concatenate([jnp.broadcast_to(pre[lo + c - 1:lo + c, :], (c, pre.shape[1]))
                           for lo in range(0, pre.shape[0], c)], axis=0)
    dk = GLA_DK
    bcs = (pre[:, :dk], tot[:, dk:] - pre[:, dk:] + g[:, dk:])
    out = []
    for d, bc in enumerate(bcs):
        b_end = tot[:, d * dk:(d + 1) * dk]
        qe = (q * (GLA_DK ** -0.5) * jnp.exp(bc)).astype(BF16)
        ke = (k * jnp.exp(-bc)).astype(BF16)
        kend = (k * jnp.exp(b_end - bc)).astype(BF16)
        out.append((qe, ke, kend, b_end))
    return out


def _state_step(s_ref, v, kend, dec):
    upd = lax.dot_general(v, kend, (((0,), (0,)), ((), ())), preferred_element_type=F32)
    s_ref[...] = s_ref[...] * dec + upd


def _gla_kernel(q_ref, k_ref, v_ref, lr_ref, og_ref, qc_ref, kc_ref, vc_ref, lrc_ref,
                w2_ref, b_ref, ng_ref, o_ref,
                s_ref, qe_ref, ke_ref, kend_ref, dec_ref, kendc_ref, decc_ref, sbf_ref):
    c = GLA_CHUNK
    t = q_ref.shape[1]
    tc = qc_ref.shape[1]
    nc = t // c
    tile = GLA_PREP_TILE
    cpt = tile // c
    dirs = (0, 1)

    def store_dec(ref, d, base, b_end):
        for ch in range(b_end.shape[0] // c):
            ref[d, pl.ds(base + ch, 1), :] = jnp.exp(b_end[ch * c:ch * c + 1, :])

    prefix = _prefix_operator(tile)

    for i in range(tc // tile):
        rows = slice(i * tile, (i + 1) * tile)
        prep_c = _gate_prep(qc_ref[0, rows, :], kc_ref[0, rows, :], lrc_ref[0, rows, :],
                            w2_ref[...], b_ref[...], prefix)
        for d, (_, _, kend, b_end) in enumerate(prep_c):
            kendc_ref[d, rows, :] = kend
            store_dec(decc_ref, d, i * cpt, b_end)
    s_ref[...] = jnp.zeros_like(s_ref)
    for j in range(tc // c):
        for d in dirs:
            ch = j if d == 0 else tc // c - 1 - j
            _state_step(s_ref.at[d], vc_ref[0, ch * c:(ch + 1) * c, :], kendc_ref[d, ch * c:(ch + 1) * c, :],
                        decc_ref[d, ch:ch + 1, :])

    def prep(i, carry):
        lo = pl.multiple_of(i * tile, tile)
        rows = pl.ds(lo, tile)
        prep_l = _gate_prep(q_ref[0, rows, :], k_ref[0, rows, :], lr_ref[0, rows, :],
                            w2_ref[...], b_ref[...], prefix)
        for d, (qe, ke, kend, b_end) in enumerate(prep_l):
            qe_ref[d, rows, :] = qe
            ke_ref[d, rows, :] = ke
            kend_ref[d, rows, :] = kend
            store_dec(dec_ref, d, i * cpt, b_end)
        return carry

    lax.fori_loop(0, t // tile, prep, 0, unroll=2)

    def scan(j, carry):
        for d in dirs:
            ch = j if d == 0 else nc - 1 - j
            rows = pl.ds(pl.multiple_of(ch * c, c), c)
            sbf_ref[d, ch] = s_ref[d].astype(BF16)
            _state_step(s_ref.at[d], v_ref[0, rows, :], kend_ref[d, rows, :], dec_ref[d, pl.ds(ch, 1), :])
        return carry

    lax.fori_loop(0, nc, scan, 0, unroll=8)

    ri = lax.broadcasted_iota(jnp.int32, (c, c), 0)
    ci = lax.broadcasted_iota(jnp.int32, (c, c), 1)
    masks = (ci <= ri, ci >= ri)

    group = GLA_OUT_GROUP

    def out(i, carry):
        chunks = [i * group + u for u in range(group)]
        rows = [pl.ds(pl.multiple_of(ch * c, c), c) for ch in chunks]
        qes = [[qe_ref[d, r, :] for d in dirs] for r in rows]
        scores = [[lax.dot_general(qes[u][d], ke_ref[d, rows[u], :], (((1,), (1,)), ((), ())),
                                   preferred_element_type=F32) for d in dirs] for u in range(group)]
        inter = [[lax.dot_general(qes[u][d], sbf_ref[d, chunks[u]], (((1,), (1,)), ((), ())),
                                  preferred_element_type=F32) for d in dirs] for u in range(group)]
        for u in range(group):
            v = v_ref[0, rows[u], :]
            o = inter[u][0] + inter[u][1]
            for d in dirs:
                a = jnp.where(masks[d], scores[u][d], 0.0).astype(BF16)
                o = o + jnp.dot(a, v, preferred_element_type=F32)
            on = o * lax.rsqrt(jnp.mean(o * o, axis=-1, keepdims=True) + EPS) * ng_ref[...]
            og = og_ref[0, rows[u], :]
            o_ref[0, rows[u], :] = (on * (og * jax.nn.sigmoid(og))).astype(BF16)
        return carry

    lax.fori_loop(0, nc // group, out, 0)


def _gla(gq, gk, gv, lr, og, gqc, gkc, gvc, lrc, w2, bias, ng):
    b, t, _ = gq.shape
    tc = gqc.shape[1]
    hk = lambda bi, h: (bi, 0, h)
    h0 = lambda bi, h: (bi, 0, 0)
    return pl.pallas_call(
        _gla_kernel,
        grid=(b, GLA_HEADS),
        in_specs=[
            pl.BlockSpec((1, t, GLA_DK), hk),
            pl.BlockSpec((1, t, GLA_DK), hk),
            pl.BlockSpec((1, t, GLA_DV), hk),
            pl.BlockSpec((1, t, LANES), h0),
            pl.BlockSpec((1, t, GLA_DV), hk),
            pl.BlockSpec((1, tc, GLA_DK), hk),
            pl.BlockSpec((1, tc, GLA_DK), hk),
            pl.BlockSpec((1, tc, GLA_DV), hk),
            pl.BlockSpec((1, tc, LANES), h0),
            pl.BlockSpec((None, LANES, 2 * GLA_DK), lambda bi, h: (h, 0, 0)),
            pl.BlockSpec((None, 1, 2 * GLA_DK), lambda bi, h: (h, 0, 0)),
            pl.BlockSpec((1, GLA_DV), lambda bi, h: (0, 0)),
        ],
        out_specs=pl.BlockSpec((1, t, GLA_DV), hk),
        out_shape=jax.ShapeDtypeStruct((b, t, GLA_V_W), BF16),
        scratch_shapes=[
            pltpu.VMEM((2, GLA_DV, GLA_DK), F32),
            pltpu.VMEM((2, t, GLA_DK), BF16),
            pltpu.VMEM((2, t, GLA_DK), BF16),
            pltpu.VMEM((2, t, GLA_DK), BF16),
            pltpu.VMEM((2, t // GLA_CHUNK, GLA_DK), F32),
            pltpu.VMEM((2, tc, GLA_DK), BF16),
            pltpu.VMEM((2, tc // GLA_CHUNK, GLA_DK), F32),
            pltpu.VMEM((2, t // GLA_CHUNK, GLA_DV, GLA_DK), BF16),
        ],
        compiler_params=_cparams("arbitrary", "arbitrary"),
        name="gla",
    )(gq, gk, gv, lr, og, gqc, gkc, gvc, lrc, w2, bias, ng)


def _token_pitch(d):
    rows = d // LANES
    return rows + 4 if rows % 8 == 0 else rows


def _store_token_tiles(ref, x):
    n = x.shape[0]
    k = x.shape[1] // LANES
    pitch = ref.shape[0] // n
    for s in range(pitch):
        slab = x[:, s * LANES:(s + 1) * LANES] if s < k else jnp.zeros((n, LANES), x.dtype)
        ref[pl.ds(s, n, stride=pitch), :] = slab


def _load_token_tiles(ref, n, d):
    pitch = ref.shape[0] // n
    return jnp.concatenate([ref[pl.ds(s, n, stride=pitch), :] for s in range(d // LANES)], axis=1)


def _out_proj_kernel(attn_ref, gla_ref, wo_ref, x_ref, gt_ref, g2_ref, sc_ref, sh_ref, wr_ref, br_ref,
                     x1_ref, h2_ref, ids_ref, wts_ref):
    y = jnp.dot(attn_ref[0], wo_ref[0:ATTN_Q_W, :], preferred_element_type=F32)
    y = y + jnp.dot(gla_ref[0], wo_ref[ATTN_Q_W:, :], preferred_element_type=F32)
    x1 = x_ref[0] + gt_ref[0] * y
    x1_ref[0] = x1
    xn = x1 * lax.rsqrt(jnp.mean(x1 * x1, axis=-1, keepdims=True) + EPS)
    h2 = xn * g2_ref[...] * (1.0 + sc_ref[0]) + sh_ref[0]
    hi = h2.astype(BF16)
    hi_f = hi.astype(F32)

    _store_token_tiles(h2_ref.at[0], h2)

    mid = (h2 - hi_f).astype(BF16)
    both = jnp.dot(hi, wr_ref[...], preferred_element_type=F32)
    logits = (both[:, :LANES] + both[:, LANES:]
              + jnp.dot(mid, wr_ref[:, :LANES], preferred_element_type=F32)) + br_ref[...]
    lane = lax.broadcasted_iota(jnp.int32, logits.shape, 1)
    lane_f = lane.astype(F32)
    neg = jnp.float32(-jnp.inf)

    def first_argmax(vals):
        m = jnp.max(vals, axis=-1, keepdims=True)
        idx = jnp.min(jnp.where(vals == m, lane_f, float(LANES)), axis=-1, keepdims=True)
        return m, idx

    lg = jnp.where(lane < N_GROUPS, logits, neg)
    mg, grp = first_argmax(lg)
    pg_sel = 1.0 / jnp.sum(jnp.exp(lg - mg), axis=-1, keepdims=True)
    lo = N_GROUPS + grp * EXPERTS_PER_GROUP
    in_grp = (lane_f >= lo) & (lane_f < lo + EXPERTS_PER_GROUP)
    le = jnp.where(in_grp, logits, neg)
    v1, i1 = first_argmax(le)
    v2, i2 = first_argmax(jnp.where(lane_f == i1, neg, le))
    e2 = jnp.exp(v2 - v1)
    w1 = pg_sel / (1.0 + e2)
    w2 = pg_sel * e2 / (1.0 + e2)
    ids = jnp.where(lane == 0, i1 - N_GROUPS, jnp.where(lane == 1, i2 - N_GROUPS, 0.0))
    ids_ref[0] = ids.astype(jnp.int32)
    wts_ref[0] = jnp.where(lane == 0, w1, jnp.where(lane == 1, w2, 0.0))


def _out_proj(attn, gla, wo, x, gt1, g2, sc2, sh2, wr, br, tm):
    b, t, d = x.shape
    row = lambda bi, i: (bi, i, 0)
    vec = lambda bi, i: (bi, 0, 0)
    const = lambda bi, i: (0, 0)
    return pl.pallas_call(
        _out_proj_kernel,
        grid=(b, t // tm),
        in_specs=[
            pl.BlockSpec((1, tm, ATTN_Q_W), row),
            pl.BlockSpec((1, tm, GLA_V_W), row),
            pl.BlockSpec(wo.shape, const, pipeline_mode=pl.Buffered(1)),
            pl.BlockSpec((1, tm, d), row),
            pl.BlockSpec((1, 1, d), vec),
            pl.BlockSpec((1, d), const),
            pl.BlockSpec((1, 1, d), vec),
            pl.BlockSpec((1, 1, d), vec),
            pl.BlockSpec((d, 2 * LANES), const),
            pl.BlockSpec((1, LANES), const),
        ],
        out_specs=[
            pl.BlockSpec((1, tm, d), row),
            pl.BlockSpec((1, tm * _token_pitch(d), LANES), row),
            pl.BlockSpec((1, tm, LANES), row),
            pl.BlockSpec((1, tm, LANES), row),
        ],
        out_shape=[
            jax.ShapeDtypeStruct((b, t, d), F32),
            jax.ShapeDtypeStruct((b, t * _token_pitch(d), LANES), F32),
            jax.ShapeDtypeStruct((b, t, LANES), jnp.int32),
            jax.ShapeDtypeStruct((b, t, LANES), F32),
        ],
        compiler_params=_cparams("arbitrary", "arbitrary"),
        name="out_proj",
    )(attn, gla, wo, x, gt1, g2, sc2, sh2, wr, br)


def _moe_kernel(eidx_ref, eblk0_ref, enb_ref, bs0_ref, bn_ref, nblk_ref, order_ref,
                h2_hbm, w1_ref, w3_ref, w2_ref, y_hbm,
                xbuf, ybuf, gsem, ssem):
    del eidx_ref
    e = pl.program_id(0)
    nblk = nblk_ref[0]
    kx = xbuf.shape[1] // MOE_BLOCK
    ky = ybuf.shape[1] // MOE_BLOCK

    data_rows = w1_ref.shape[0] // LANES

    def gather_copy(sl, hbm_row, j, tokens=1):
        return pltpu.make_async_copy(h2_hbm.at[pl.ds(hbm_row * kx, tokens * data_rows)],
                                     xbuf.at[sl, pl.ds(j * kx, tokens * data_rows)], gsem.at[sl])

    def for_rows(n, per_group, per_row):
        ng = lax.shift_right_logical(n, ROW_GROUP_LOG2)
        lax.fori_loop(0, ng, lambda g, c: (per_group(g * ROW_GROUP), c)[1], 0)
        lax.fori_loop(ng * ROW_GROUP, n, lambda j, c: (per_row(j), c)[1], 0)

    def start_gather(blk, sl):
        s = bs0_ref[blk]

        def start_row(j):
            gather_copy(sl, lax.shift_right_logical(order_ref[s + j], 1), j).start()

        def group(j0):
            for u in range(ROW_GROUP):
                start_row(j0 + u)

        for_rows(bn_ref[blk], group, start_row)

    def wait_gather(blk, sl):
        for_rows(bn_ref[blk], lambda j0: gather_copy(sl, 0, 0, ROW_GROUP).wait(),
                 lambda j: gather_copy(sl, 0, 0).wait())

    half = MOE_BLOCK // 2

    def with_block_rows(blk, fn):
        @pl.when(bn_ref[blk] > half)
        def _():
            fn(MOE_BLOCK)

        @pl.when(bn_ref[blk] <= half)
        def _():
            fn(half)

    def output_copy(blk, sl, rows):
        return pltpu.make_async_copy(ybuf.at[sl, pl.ds(0, rows * ky)],
                                     y_hbm.at[pl.ds(bs0_ref[blk] * ky, rows * ky)], ssem.at[sl])

    def wait_output(blk, sl):
        with_block_rows(blk, lambda rows: output_copy(blk, sl, rows).wait())

    @pl.when(e == 0)
    def _():
        xbuf[...] = jnp.zeros_like(xbuf)
        ybuf[...] = jnp.zeros_like(ybuf)
        for g0 in range(GATHER_AHEAD):
            @pl.when(g0 < nblk)
            def _():
                start_gather(g0, g0)

    def block(g, carry):
        slot = g & 1
        xslot = lax.rem(g, GATHER_AHEAD + 1)

        @pl.when(g + GATHER_AHEAD < nblk)
        def _():
            start_gather(g + GATHER_AHEAD, lax.rem(g + GATHER_AHEAD, GATHER_AHEAD + 1))

        wait_gather(g, xslot)

        def expert_rows(rows):
            x = _load_token_tiles(xbuf.at[xslot, pl.ds(0, rows * kx)], rows, w1_ref.shape[0])
            a = jnp.dot(x, w1_ref[...], preferred_element_type=F32)
            gate = jnp.dot(x, w3_ref[...], preferred_element_type=F32)
            hid = a * jax.nn.sigmoid(a) * gate
            y = jnp.dot(hid, w2_ref[...], preferred_element_type=F32)
            _store_token_tiles(ybuf.at[slot, pl.ds(0, rows * ky)], y)

            @pl.when(g >= 1)
            def _():
                wait_output(g - 1, 1 - slot)

            output_copy(g, slot, rows).start()

        with_block_rows(g, expert_rows)
        return carry

    @pl.when(enb_ref[e] > 0)
    def _():
        lax.fori_loop(eblk0_ref[e], eblk0_ref[e] + enb_ref[e], block, 0)

    @pl.when(e == pl.num_programs(0) - 1)
    def _():
        wait_output(nblk - 1, (nblk - 1) & 1)
        ybuf[0] = jnp.zeros_like(ybuf[0])
        pad_rows = MOE_BLOCK * ky
        pad = pltpu.make_async_copy(ybuf.at[0], y_hbm.at[pl.ds(y_hbm.shape[0] - pad_rows, pad_rows)], ssem.at[0])
        pad.start()
        pad.wait()


def _moe(h2t, w1, w3, w2, eidx, eblk0, enb, bs0, bn, nblk, order):
    ne, d, ff = w1.shape
    kx = ky = _token_pitch(d)
    wmap = lambda e, eidx, *_: (eidx[e], 0, 0)
    grid_spec = pltpu.PrefetchScalarGridSpec(
        num_scalar_prefetch=7,
        grid=(ne,),
        in_specs=[
            pl.BlockSpec(memory_space=pl.ANY),
            pl.BlockSpec((None, d, ff), wmap),
            pl.BlockSpec((None, d, ff), wmap),
            pl.BlockSpec((None, ff, d), wmap),
        ],
        out_specs=pl.BlockSpec(memory_space=pl.ANY),
        scratch_shapes=[
            pltpu.VMEM((GATHER_AHEAD + 1, MOE_BLOCK * kx, LANES), F32),
            pltpu.VMEM((2, MOE_BLOCK * ky, LANES), F32),
            pltpu.SemaphoreType.DMA((GATHER_AHEAD + 1,)),
            pltpu.SemaphoreType.DMA((2,)),
        ],
    )
    return pl.pallas_call(
        _moe_kernel,
        grid_spec=grid_spec,
        out_shape=jax.ShapeDtypeStruct(((order.shape[0] + MOE_BLOCK) * ky, LANES), F32),
        compiler_params=_cparams("arbitrary"),
        name="moe",
    )(eidx, eblk0, enb, bs0, bn, nblk, order, h2t, w1, w3, w2)


def _combine_kernel(pos_ref, x1_ref, ys_hbm, wts_ref, gt_ref, o_ref, ybuf, sem):
    tm = x1_ref.shape[1]
    ky = ybuf.shape[2] // tm
    nt = pl.num_programs(1)
    n = pl.program_id(0) * nt + pl.program_id(1)
    slot = n & 1

    data_rows = x1_ref.shape[2] // LANES

    def row_copy(sl, k, pos, j, tokens=1):
        return pltpu.make_async_copy(ys_hbm.at[pl.ds(pos * ky, tokens * data_rows)],
                                     ybuf.at[sl, k, pl.ds(j * ky, tokens * data_rows)], sem.at[sl])

    def start_tile(tile, sl):
        base = tile * (tm * TOP_K)

        def group(g, carry):
            for u in range(ROW_GROUP):
                row_copy(sl, u % TOP_K, pos_ref[base + g * ROW_GROUP + u],
                         g * (ROW_GROUP // TOP_K) + u // TOP_K).start()
            return carry

        lax.fori_loop(0, tm * TOP_K // ROW_GROUP, group, 0)

    @pl.when(n == 0)
    def _():
        start_tile(0, 0)

    @pl.when(n + 1 < pl.num_programs(0) * nt)
    def _():
        start_tile(n + 1, 1 - slot)

    for k in range(TOP_K):
        row_copy(slot, k, 0, 0, tokens=tm).wait()
    w = wts_ref[0]
    d = x1_ref.shape[2]
    ff = (w[:, 0:1] * _load_token_tiles(ybuf.at[slot, 0], tm, d)
          + w[:, 1:2] * _load_token_tiles(ybuf.at[slot, 1], tm, d))
    o_ref[0] = x1_ref[0] + gt_ref[0] * ff


def _combine(x1, ys, pos, wts, gt2, tm):
    b, t, d = x1.shape
    ky = _token_pitch(d)
    row = lambda bi, i, pos: (bi, i, 0)
    vec = lambda bi, i, pos: (bi, 0, 0)
    grid_spec = pltpu.PrefetchScalarGridSpec(
        num_scalar_prefetch=1,
        grid=(b, t // tm),
        in_specs=[
            pl.BlockSpec((1, tm, d), row),
            pl.BlockSpec(memory_space=pl.ANY),
            pl.BlockSpec((1, tm, LANES), row),
            pl.BlockSpec((1, 1, d), vec),
        ],
        out_specs=pl.BlockSpec((1, tm, d), row),
        scratch_shapes=[
            pltpu.VMEM((2, TOP_K, tm * ky, LANES), F32),
            pltpu.SemaphoreType.DMA((2,)),
        ],
    )
    return pl.pallas_call(
        _combine_kernel,
        grid_spec=grid_spec,
        out_shape=jax.ShapeDtypeStruct((b, t, d), F32),
        compiler_params=_cparams("arbitrary", "arbitrary"),
        name="combine",
    )(pos, x1, ys, wts, gt2)


def _rope_tables(t):
    rows = t // GRID_W
    n_freq = HEAD_DIM // 4
    inv = ROPE_THETA ** (-jnp.arange(n_freq, dtype=F32) / n_freq)
    ar = jnp.arange(rows, dtype=F32)[:, None] * inv
    ac = jnp.arange(GRID_W, dtype=F32)[:, None] * inv

    def expand(fr, fc, sign):
        by_row = jnp.broadcast_to(fr[:, None, :], (rows, GRID_W, n_freq))
        by_col = jnp.broadcast_to(fc[None, :, :], (rows, GRID_W, n_freq))
        return jnp.concatenate([sign * by_row, by_row, sign * by_col, by_col], axis=2).reshape(t, HEAD_DIM)

    return expand(jnp.cos(ar), jnp.cos(ac), 1.0), expand(jnp.sin(ar), jnp.sin(ac), -1.0)


def _gate_weights(w2, bias):
    r = GLA_GATE_RANK
    wh = w2.reshape(2, r, GLA_HEADS, GLA_DK).transpose(2, 0, 1, 3)
    w = jnp.zeros((GLA_HEADS, 2 * r, 2 * GLA_DK), F32)
    w = w.at[:, 0:r, 0:GLA_DK].set(wh[:, 0]).at[:, r:2 * r, GLA_DK:].set(wh[:, 1])
    hi = w.astype(BF16)
    mid = (w - hi.astype(F32)).astype(BF16)
    pad = jnp.zeros((GLA_HEADS, LANES - 6 * r, 2 * GLA_DK), BF16)
    w2s = jnp.concatenate([hi, hi, mid, pad], axis=1)
    bias2 = bias.reshape(2, GLA_HEADS, GLA_DK).transpose(1, 0, 2).reshape(GLA_HEADS, 1, 2 * GLA_DK)
    return w2s, bias2


def _block_plan(eid_flat, nb):
    order = jnp.argsort(eid_flat).astype(jnp.int32)
    eids = jnp.arange(N_EXPERTS, dtype=jnp.int32)
    of_assign = (eid_flat[:, None] == eids[None, :]).astype(jnp.int32)
    counts = jnp.sum(of_assign, axis=0)
    starts = jnp.cumsum(counts) - counts
    nblk_e = (counts + MOE_BLOCK - 1) // MOE_BLOCK
    bends = jnp.cumsum(nblk_e)
    bstarts = bends - nblk_e
    nblk = bends[-1]
    prev_used = lax.cummax(jnp.where(counts > 0, eids, -1))
    eidx = jnp.where(prev_used >= 0, prev_used, jnp.argmax(counts > 0)).astype(jnp.int32)
    bi = jnp.arange(nb, dtype=jnp.int32)
    bic = jnp.minimum(bi, jnp.maximum(nblk - 1, 0))
    bexp = jnp.minimum(jnp.sum(bends[None, :] <= bic[:, None], axis=1), N_EXPERTS - 1)
    of_block = (bexp[:, None] == eids[None, :]).astype(jnp.int32)

    def lookup(per_expert):
        return jnp.sum(of_block * per_expert[None, :], axis=1)

    r0 = (bic - lookup(bstarts)) * MOE_BLOCK
    bs0 = (lookup(starts) + r0).astype(jnp.int32)
    bn = jnp.where(bi < nblk, jnp.minimum(lookup(counts) - r0, MOE_BLOCK), 0).astype(jnp.int32)
    pos = jnp.argsort(order).astype(jnp.int32)
    plan = (eidx, bstarts.astype(jnp.int32), nblk_e.astype(jnp.int32), bs0, bn,
            nblk.reshape(1).astype(jnp.int32), order)
    return plan, pos


def kernel(x, c, ctx, c_ctx, w_ada, b_ada, norm1_g, w_in, q_norm_g, k_norm_g, gla_gate_w2, gla_gate_b, gla_norm_g, w_out, norm2_g, router_grp_w, router_grp_b, router_exp_w, router_exp_b, moe_w1, moe_w3, moe_w2):
    b, t, d = x.shape
    tc = ctx.shape[1]
    depth = w_ada.shape[0]
    assert depth == 1, "single-layer stack: the context stream only feeds keys/values and GLA states"
    layer = 0

    c8 = jnp.zeros((8, d), F32).at[0:b].set(c).at[b].set(c_ctx)
    mod = _ada(c8, w_ada[layer], b_ada[layer])
    sh1, sc1, gt1, sh2, sc2, gt2 = [mod[0:b, i * d:(i + 1) * d].reshape(b, 1, d) for i in range(6)]
    sh1c, sc1c = [jnp.broadcast_to(mod[b, i * d:(i + 1) * d].reshape(1, 1, d), (b, 1, d)) for i in range(2)]

    w_main, w_tail = _w_in_parts(jnp.swapaxes(w_in[layer], 0, 1))

    cos_t, sin_t = _rope_tables(t)
    g1 = norm1_g[layer].reshape(1, d)
    qg = q_norm_g[layer].reshape(1, HEAD_DIM)
    kg = k_norm_g[layer].reshape(1, HEAD_DIM)
    q, k, v, gq, gk, gv, og, lr = _in_proj(x, g1, sc1, sh1, w_main, w_tail, qg, kg, cos_t, sin_t, 512)
    ones_t = jnp.ones((tc, HEAD_DIM), F32)
    _, kc, vc, gqc, gkc, gvc, _, lrc = _in_proj(ctx, g1, sc1c, sh1c, w_main, w_tail, qg, kg,
                                                ones_t, jnp.zeros_like(ones_t), tc)

    attn = _attn(q, jnp.concatenate([kc, k], axis=1), jnp.concatenate([vc, v], axis=1), 256)
    w2s, bias2 = _gate_weights(gla_gate_w2[layer], gla_gate_b[layer])
    gla = _gla(gq, gk, gv, lr, og, gqc, gkc, gvc, lrc, w2s, bias2, gla_norm_g[layer].reshape(1, GLA_DV))

    wr = jnp.concatenate([router_grp_w[layer], router_exp_w[layer],
                          jnp.zeros((d, LANES - N_GROUPS - N_EXPERTS), F32)], axis=1)
    br = jnp.concatenate([router_grp_b[layer], router_exp_b[layer],
                          jnp.zeros((LANES - N_GROUPS - N_EXPERTS,), F32)]).reshape(1, LANES)
    wr_hi = wr.astype(BF16)
    wr_parts = jnp.concatenate([wr_hi, (wr - wr_hi.astype(F32)).astype(BF16)], axis=1)
    x1, h2t, ids, wts = _out_proj(attn, gla, w_out[layer].astype(BF16), x, gt1, norm2_g[layer].reshape(1, d),
                                  sc2, sh2, wr_parts, br, 512)

    m = b * t
    n_assign = m * TOP_K
    nb = -(-(n_assign + N_EXPERTS * (MOE_BLOCK - 1)) // MOE_BLOCK)
    eid_flat = ids[:, :, 0:TOP_K].reshape(n_assign)
    plan, pos = _block_plan(eid_flat, nb)
    ys = _moe(h2t.reshape(-1, LANES), moe_w1[layer], moe_w3[layer], moe_w2[layer], *plan)
    return _combine(x1, ys, pos, wts, gt2, 256)
```

```python
import functools

import jax
import jax.numpy as jnp
from jax import lax
from jax.experimental import pallas as pl
from jax.experimental.pallas import tpu as pltpu

EPS = 1e-6
GRID_W = 64
ROPE_THETA = 10000.0

ATTN_HEADS = 8
ATTN_KV_HEADS = 2
HEAD_DIM = 128
GQA_GROUP = ATTN_HEADS // ATTN_KV_HEADS

GLA_HEADS = 4
GLA_DK = 128
GLA_DV = 256
GLA_GATE_RANK = 16
GLA_GATE_NORMALIZER = 16.0
GLA_CHUNK = 64
GLA_CHUNK_LOG2 = 6
GLA_PREP_TILE = 256
GLA_OUT_GROUP = 16

N_GROUPS = 8
EXPERTS_PER_GROUP = 8
N_EXPERTS = N_GROUPS * EXPERTS_PER_GROUP
TOP_K = 2
MOE_BLOCK = 256
ROW_GROUP_LOG2 = 3
ROW_GROUP = 1 << ROW_GROUP_LOG2
GATHER_AHEAD = 2

LOG2_E = 1.4426950408889634
LANES = 128
VMEM_LIMIT = 56 * 1024 * 1024

ATTN_Q_W = ATTN_HEADS * HEAD_DIM
ATTN_KV_W = ATTN_KV_HEADS * HEAD_DIM
GLA_K_W = GLA_HEADS * GLA_DK
GLA_V_W = GLA_HEADS * GLA_DV

BF16 = jnp.bfloat16
F32 = jnp.float32


def _cparams(*sem):
    return pltpu.CompilerParams(dimension_semantics=sem, vmem_limit_bytes=VMEM_LIMIT)


def _ada_kernel(c_ref, w_ref, b_ref, o_ref):
    c = c_ref[...]
    s = c * jax.nn.sigmoid(c)
    o_ref[...] = jnp.dot(s.astype(BF16), w_ref[...].astype(BF16), preferred_element_type=F32) + b_ref[...]


def _ada(c8, w, b):
    d, n = w.shape
    tn = 1024
    return pl.pallas_call(
        _ada_kernel,
        grid=(n // tn,),
        in_specs=[
            pl.BlockSpec((8, d), lambda j: (0, 0)),
            pl.BlockSpec((d, tn), lambda j: (0, j)),
            pl.BlockSpec((1, tn), lambda j: (0, j)),
        ],
        out_specs=pl.BlockSpec((8, tn), lambda j: (0, j)),
        out_shape=jax.ShapeDtypeStruct((8, n), F32),
        compiler_params=_cparams("arbitrary"),
        name="ada",
    )(c8, w, b.reshape(1, n))


def _w_main_kernel(wt_ref, o_ref):
    o_ref[...] = wt_ref[...].T.astype(BF16)


def _w_tail_kernel(wt_hbm, o_ref, tail, sem):
    r = 2 * GLA_GATE_RANK
    n_tail = tail.shape[0]
    copy = pltpu.make_async_copy(wt_hbm.at[pl.ds(wt_hbm.shape[0] - n_tail, n_tail)], tail, sem)
    copy.start()
    copy.wait()
    o_ref[:, :GLA_V_W] = tail[r:, :].T.astype(BF16)
    lowrank = jnp.concatenate([tail[:r, :], jnp.zeros((LANES - r, tail.shape[1]), F32)], axis=0)
    o_ref[:, GLA_V_W:] = lowrank.T.astype(BF16)


def _w_in_parts(wt):
    n, d = wt.shape
    n_main = n - GLA_V_W - 2 * GLA_GATE_RANK
    tn = 512
    main = pl.pallas_call(
        _w_main_kernel,
        grid=(n_main // tn,),
        in_specs=[pl.BlockSpec((tn, d), lambda j: (j, 0))],
        out_specs=pl.BlockSpec((d, tn), lambda j: (0, j)),
        out_shape=jax.ShapeDtypeStruct((d, n_main), BF16),
        compiler_params=_cparams("arbitrary"),
        name="w_in_main",
    )(wt)
    tail = pl.pallas_call(
        _w_tail_kernel,
        in_specs=[pl.BlockSpec(memory_space=pl.ANY)],
        out_shape=jax.ShapeDtypeStruct((d, GLA_V_W + LANES), BF16),
        scratch_shapes=[pltpu.VMEM((n - n_main, d), F32), pltpu.SemaphoreType.DMA],
        compiler_params=pltpu.CompilerParams(vmem_limit_bytes=VMEM_LIMIT),
        name="w_in_tail",
    )(wt)
    return main, tail


def _swap32(y):
    lane = lax.broadcasted_iota(jnp.int32, y.shape, 1)
    return jnp.where((lane & 63) < 32, pltpu.roll(y, 96, 1), pltpu.roll(y, 32, 1))


def _head_norm_rope(a, g, cos, sin):
    y = a * lax.rsqrt(jnp.mean(a * a, axis=-1, keepdims=True) + EPS) * g
    return y * cos + _swap32(y) * sin


def _in_proj_kernel(x_ref, g1_ref, sc_ref, sh_ref, w_ref, wt_ref, qg_ref, kg_ref, cos_ref, sin_ref,
                    q_ref, k_ref, v_ref, gq_ref, gk_ref, gv_ref, og_ref, lr_ref):
    x = x_ref[0]
    xn = x * lax.rsqrt(jnp.mean(x * x, axis=-1, keepdims=True) + EPS)
    h = (xn * g1_ref[...] * (1.0 + sc_ref[0]) + sh_ref[0]).astype(BF16)
    cos = cos_ref[...]
    sin = sin_ref[...]

    def proj(off, width, w=w_ref):
        return jnp.dot(h, w[:, off:off + width], preferred_element_type=F32)

    qg = qg_ref[...] * (HEAD_DIM ** -0.5 * LOG2_E)
    off = 0
    for out_ref, gain, heads in ((q_ref, qg, ATTN_HEADS), (k_ref, kg_ref[...], ATTN_KV_HEADS)):
        for pair in range(heads // 2):
            a2 = proj(off, 2 * HEAD_DIM)
            for u in range(2):
                lo = (2 * pair + u) * HEAD_DIM
                a = a2[:, u * HEAD_DIM:(u + 1) * HEAD_DIM]
                out_ref[0, :, lo:lo + HEAD_DIM] = _head_norm_rope(a, gain, cos, sin).astype(BF16)
            off += 2 * HEAD_DIM
    vv = proj(off, ATTN_KV_W).astype(BF16)
    for hk in range(ATTN_KV_HEADS):
        v_ref[0, :, 2 * hk * HEAD_DIM:(2 * hk + 1) * HEAD_DIM] = vv[:, hk * HEAD_DIM:(hk + 1) * HEAD_DIM]
        v_ref[0, :, (2 * hk + 1) * HEAD_DIM:(2 * hk + 2) * HEAD_DIM] = jnp.ones((vv.shape[0], HEAD_DIM), BF16)
    off += ATTN_KV_W
    gq_ref[0] = proj(off, GLA_K_W)
    off += GLA_K_W
    gk_ref[0] = proj(off, GLA_K_W)
    off += GLA_K_W
    gv_ref[0] = proj(off, GLA_V_W).astype(BF16)
    og_ref[0] = proj(0, GLA_V_W, wt_ref)
    lr_ref[0] = proj(GLA_V_W, LANES, wt_ref)


def _in_proj(x, g1, sc, sh, w_main, w_tail, qg, kg, cos_t, sin_t, tm):
    b, t, d = x.shape
    row = lambda bi, i: (bi, i, 0)
    vec = lambda bi, i: (bi, 0, 0)
    const = lambda bi, i: (0, 0)
    tab = lambda bi, i: (i, 0)
    widths = [(ATTN_Q_W, BF16), (ATTN_KV_W, BF16), (2 * ATTN_KV_W, BF16), (GLA_K_W, F32), (GLA_K_W, F32),
              (GLA_V_W, BF16), (GLA_V_W, F32), (LANES, F32)]
    return pl.pallas_call(
        _in_proj_kernel,
        grid=(b, t // tm),
        in_specs=[
            pl.BlockSpec((1, tm, d), row),
            pl.BlockSpec((1, d), const),
            pl.BlockSpec((1, 1, d), vec),
            pl.BlockSpec((1, 1, d), vec),
            pl.BlockSpec(w_main.shape, const, pipeline_mode=pl.Buffered(1)),
            pl.BlockSpec(w_tail.shape, const, pipeline_mode=pl.Buffered(1)),
            pl.BlockSpec((1, HEAD_DIM), const),
            pl.BlockSpec((1, HEAD_DIM), const),
            pl.BlockSpec((tm, HEAD_DIM), tab),
            pl.BlockSpec((tm, HEAD_DIM), tab),
        ],
        out_specs=[pl.BlockSpec((1, tm, wd), row) for wd, _ in widths],
        out_shape=[jax.ShapeDtypeStruct((b, t, wd), dt) for wd, dt in widths],
        compiler_params=_cparams("arbitrary", "arbitrary"),
        name="in_proj",
    )(x, g1, sc, sh, w_main, w_tail, qg, kg, cos_t, sin_t)


def _attn_kernel(q_ref, k_ref, v_ref, o_ref):
    def scores(h):
        hk = h // GQA_GROUP
        q = q_ref[0, :, h * HEAD_DIM:(h + 1) * HEAD_DIM]
        k = k_ref[0, :, hk * HEAD_DIM:(hk + 1) * HEAD_DIM]
        return lax.dot_general(q, k, (((1,), (1,)), ((), ())), preferred_element_type=F32)

    s_next = scores(0)
    for h in range(ATTN_HEADS):
        hk = h // GQA_GROUP
        s = s_next
        if h + 1 < ATTN_HEADS:
            s_next = scores(h + 1)
        p = jnp.exp2(s - jnp.max(s, axis=-1, keepdims=True))
        ol = jnp.dot(p.astype(BF16), v_ref[0, :, 2 * hk * HEAD_DIM:2 * (hk + 1) * HEAD_DIM],
                     preferred_element_type=F32)
        o_ref[0, :, h * HEAD_DIM:(h + 1) * HEAD_DIM] = (
            ol[:, :HEAD_DIM] / ol[:, HEAD_DIM:HEAD_DIM + 1]).astype(BF16)


def _attn(q, k_all, v_all, tq):
    b, t, _ = q.shape
    tk = k_all.shape[1]
    return pl.pallas_call(
        _attn_kernel,
        grid=(b, t // tq),
        in_specs=[
            pl.BlockSpec((1, tq, ATTN_Q_W), lambda bi, i: (bi, i, 0)),
            pl.BlockSpec((1, tk, ATTN_KV_W), lambda bi, i: (bi, 0, 0)),
            pl.BlockSpec((1, tk, 2 * ATTN_KV_W), lambda bi, i: (bi, 0, 0)),
        ],
        out_specs=pl.BlockSpec((1, tq, ATTN_Q_W), lambda bi, i: (bi, i, 0)),
        out_shape=jax.ShapeDtypeStruct((b, t, ATTN_Q_W), BF16),
        compiler_params=_cparams("arbitrary", "arbitrary"),
        name="attn",
    )(q, k_all, v_all)


def _log_sigmoid(x):
    return jnp.minimum(x, 0.0) - jnp.log(1.0 + jnp.exp(-jnp.abs(x)))


def _split3(x):
    hi = x.astype(BF16)
    r = x - hi.astype(F32)
    mid = r.astype(BF16)
    lo = (r - mid.astype(F32)).astype(BF16)
    return hi, mid, lo


def _dot_exact_lhs(m, x):
    return sum(jnp.dot(m, part, preferred_element_type=F32) for part in _split3(x))


def _prefix_operator(n):
    ri = lax.broadcasted_iota(jnp.int32, (n, n), 0)
    ci = lax.broadcasted_iota(jnp.int32, (n, n), 1)
    same = lax.shift_right_logical(ri, GLA_CHUNK_LOG2) == lax.shift_right_logical(ci, GLA_CHUNK_LOG2)
    return (same & (ci <= ri)).astype(BF16)


def _gate_prep(q, k, lr, w2s, bias2, prefix):
    c = GLA_CHUNK
    hi = lr.astype(BF16).astype(F32)
    mid = (lr - hi).astype(BF16).astype(F32)
    lhs = (hi + pltpu.roll(mid, 2 * GLA_GATE_RANK, 1) + pltpu.roll(hi, 4 * GLA_GATE_RANK, 1)).astype(BF16)
    logits = jnp.dot(lhs, w2s, preferred_element_type=F32) + bias2
    g = _log_sigmoid(logits) * (1.0 / GLA_GATE_NORMALIZER)
    pre = _dot_exact_lhs(prefix, g)
    tot = jnp.concatenate([jnp.broadcast_to(pre[lo + c - 1:lo + c, :], (c, pre.shape[1]))
                           for lo in range(0, pre.shape[0], c)], axis=0)
    dk = GLA_DK
    bcs = (pre[:, :dk], tot[:, dk:] - pre[:, dk:] + g[:, dk:])
    out = []
    for d, bc in enumerate(bcs):
        b_end = tot[:, d * dk:(d + 1) * dk]
        qe = (q * (GLA_DK ** -0.5) * jnp.exp(bc)).astype(BF16)
        ke = (k * jnp.exp(-bc)).astype(BF16)
        kend = (k * jnp.exp(b_end - bc)).astype(BF16)
        out.append((qe, ke, kend, b_end))
    return out


def _state_step(s_ref, v, kend, dec):
    upd = lax.dot_general(v, kend, (((0,), (0,)), ((), ())), preferred_element_type=F32)
    s_ref[...] = s_ref[...] * dec + upd


def _gla_kernel(q_ref, k_ref, v_ref, lr_ref, og_ref, qc_ref, kc_ref, vc_ref, lrc_ref,
                w2_ref, b_ref, ng_ref, o_ref,
                s_ref, qe_ref, ke_ref, kend_ref, dec_ref, kendc_ref, decc_ref, sbf_ref):
    c = GLA_CHUNK
    t = q_ref.shape[1]
    tc = qc_ref.shape[1]
    nc = t // c
    tile = GLA_PREP_TILE
    cpt = tile // c
    dirs = (0, 1)

    def store_dec(ref, d, base, b_end):
        for ch in range(b_end.shape[0] // c):
            ref[d, pl.ds(base + ch, 1), :] = jnp.exp(b_end[ch * c:ch * c + 1, :])

    prefix = _prefix_operator(tile)

    for i in range(tc // tile):
        rows = slice(i * tile, (i + 1) * tile)
        prep_c = _gate_prep(qc_ref[0, rows, :], kc_ref[0, rows, :], lrc_ref[0, rows, :],
                            w2_ref[...], b_ref[...], prefix)
        for d, (_, _, kend, b_end) in enumerate(prep_c):
            kendc_ref[d, rows, :] = kend
            store_dec(decc_ref, d, i * cpt, b_end)
    s_ref[...] = jnp.zeros_like(s_ref)
    for j in range(tc // c):
        for d in dirs:
            ch = j if d == 0 else tc // c - 1 - j
            _state_step(s_ref.at[d], vc_ref[0, ch * c:(ch + 1) * c, :], kendc_ref[d, ch * c:(ch + 1) * c, :],
                        decc_ref[d, ch:ch + 1, :])

    def prep(i, carry):
        lo = pl.multiple_of(i * tile, tile)
        rows = pl.ds(lo, tile)
        prep_l = _gate_prep(q_ref[0, rows, :], k_ref[0, rows, :], lr_ref[0, rows, :],
                            w2_ref[...], b_ref[...], prefix)
        for d, (qe, ke, kend, b_end) in enumerate(prep_l):
            qe_ref[d, rows, :] = qe
            ke_ref[d, rows, :] = ke
            kend_ref[d, rows, :] = kend
            store_dec(dec_ref, d, i * cpt, b_end)
        return carry

    lax.fori_loop(0, t // tile, prep, 0, unroll=4)

    def scan(j, carry):
        for d in dirs:
            ch = j if d == 0 else nc - 1 - j
            rows = pl.ds(pl.multiple_of(ch * c, c), c)
            sbf_ref[d, ch] = s_ref[d].astype(BF16)
            _state_step(s_ref.at[d], v_ref[0, rows, :], kend_ref[d, rows, :], dec_ref[d, pl.ds(ch, 1), :])
        return carry

    lax.fori_loop(0, nc, scan, 0, unroll=8)

    ri = lax.broadcasted_iota(jnp.int32, (c, c), 0)
    ci = lax.broadcasted_iota(jnp.int32, (c, c), 1)
    masks = (ci <= ri, ci >= ri)

    group = GLA_OUT_GROUP

    def out(i, carry):
        chunks = [i * group + u for u in range(group)]
        rows = [pl.ds(pl.multiple_of(ch * c, c), c) for ch in chunks]
        qes = [[qe_ref[d, r, :] for d in dirs] for r in rows]
        scores = [[lax.dot_general(qes[u][d], ke_ref[d, rows[u], :], (((1,), (1,)), ((), ())),
                                   preferred_element_type=F32) for d in dirs] for u in range(group)]
        inter = [[lax.dot_general(qes[u][d], sbf_ref[d, chunks[u]], (((1,), (1,)), ((), ())),
                                  preferred_element_type=F32) for d in dirs] for u in range(group)]
        for u in range(group):
            v = v_ref[0, rows[u], :]
            o = inter[u][0] + inter[u][1]
            for d in dirs:
                a = jnp.where(masks[d], scores[u][d], 0.0).astype(BF16)
                o = o + jnp.dot(a, v, preferred_element_type=F32)
            on = o * lax.rsqrt(jnp.mean(o * o, axis=-1, keepdims=True) + EPS) * ng_ref[...]
            og = og_ref[0, rows[u], :]
            o_ref[0, rows[u], :] = (on * (og * jax.nn.sigmoid(og))).astype(BF16)
        return carry

    lax.fori_loop(0, nc // group, out, 0)


def _gla(gq, gk, gv, lr, og, gqc, gkc, gvc, lrc, w2, bias, ng):
    b, t, _ = gq.shape
    tc = gqc.shape[1]
    hk = lambda bi, h: (bi, 0, h)
    h0 = lambda bi, h: (bi, 0, 0)
    return pl.pallas_call(
        _gla_kernel,
        grid=(b, GLA_HEADS),
        in_specs=[
            pl.BlockSpec((1, t, GLA_DK), hk),
            pl.BlockSpec((1, t, GLA_DK), hk),
            pl.BlockSpec((1, t, GLA_DV), hk),
            pl.BlockSpec((1, t, LANES), h0),
            pl.BlockSpec((1, t, GLA_DV), hk),
            pl.BlockSpec((1, tc, GLA_DK), hk),
            pl.BlockSpec((1, tc, GLA_DK), hk),
            pl.BlockSpec((1, tc, GLA_DV), hk),
            pl.BlockSpec((1, tc, LANES), h0),
            pl.BlockSpec((None, LANES, 2 * GLA_DK), lambda bi, h: (h, 0, 0)),
            pl.BlockSpec((None, 1, 2 * GLA_DK), lambda bi, h: (h, 0, 0)),
            pl.BlockSpec((1, GLA_DV), lambda bi, h: (0, 0)),
        ],
        out_specs=pl.BlockSpec((1, t, GLA_DV), hk),
        out_shape=jax.ShapeDtypeStruct((b, t, GLA_V_W), BF16),
        scratch_shapes=[
            pltpu.VMEM((2, GLA_DV, GLA_DK), F32),
            pltpu.VMEM((2, t, GLA_DK), BF16),
            pltpu.VMEM((2, t, GLA_DK), BF16),
            pltpu.VMEM((2, t, GLA_DK), BF16),
            pltpu.VMEM((2, t // GLA_CHUNK, GLA_DK), F32),
            pltpu.VMEM((2, tc, GLA_DK), BF16),
            pltpu.VMEM((2, tc // GLA_CHUNK, GLA_DK), F32),
            pltpu.VMEM((2, t // GLA_CHUNK, GLA_DV, GLA_DK), BF16),
        ],
        compiler_params=_cparams("arbitrary", "arbitrary"),
        name="gla",
    )(gq, gk, gv, lr, og, gqc, gkc, gvc, lrc, w2, bias, ng)


def _token_pitch(d):
    rows = d // LANES
    return rows + 4 if rows % 8 == 0 else rows


def _store_token_tiles(ref, x):
    n = x.shape[0]
    k = x.shape[1] // LANES
    pitch = ref.shape[0] // n
    for s in range(pitch):
        slab = x[:, s * LANES:(s + 1) * LANES] if s < k else jnp.zeros((n, LANES), x.dtype)
        ref[pl.ds(s, n, stride=pitch), :] = slab


def _load_token_tiles(ref, n, d):
    pitch = ref.shape[0] // n
    return jnp.concatenate([ref[pl.ds(s, n, stride=pitch), :] for s in range(d // LANES)], axis=1)


def _out_proj_kernel(attn_ref, gla_ref, wo_ref, x_ref, gt_ref, g2_ref, sc_ref, sh_ref, wr_ref, br_ref,
                     x1_ref, h2_ref, ids_ref, wts_ref):
    y = jnp.dot(attn_ref[0], wo_ref[0:ATTN_Q_W, :], preferred_element_type=F32)
    y = y + jnp.dot(gla_ref[0], wo_ref[ATTN_Q_W:, :], preferred_element_type=F32)
    x1 = x_ref[0] + gt_ref[0] * y
    x1_ref[0] = x1
    xn = x1 * lax.rsqrt(jnp.mean(x1 * x1, axis=-1, keepdims=True) + EPS)
    h2 = xn * g2_ref[...] * (1.0 + sc_ref[0]) + sh_ref[0]
    hi = h2.astype(BF16)
    hi_f = hi.astype(F32)

    _store_token_tiles(h2_ref.at[0], h2)

    mid = (h2 - hi_f).astype(BF16)
    both = jnp.dot(hi, wr_ref[...], preferred_element_type=F32)
    logits = (both[:, :LANES] + both[:, LANES:]
              + jnp.dot(mid, wr_ref[:, :LANES], preferred_element_type=F32)) + br_ref[...]
    lane = lax.broadcasted_iota(jnp.int32, logits.shape, 1)
    lane_f = lane.astype(F32)
    neg = jnp.float32(-jnp.inf)

    def first_argmax(vals):
        m = jnp.max(vals, axis=-1, keepdims=True)
        idx = jnp.min(jnp.where(vals == m, lane_f, float(LANES)), axis=-1, keepdims=True)
        return m, idx

    lg = jnp.where(lane < N_GROUPS, logits, neg)
    mg, grp = first_argmax(lg)
    pg_sel = 1.0 / jnp.sum(jnp.exp(lg - mg), axis=-1, keepdims=True)
    lo = N_GROUPS + grp * EXPERTS_PER_GROUP
    in_grp = (lane_f >= lo) & (lane_f < lo + EXPERTS_PER_GROUP)
    le = jnp.where(in_grp, logits, neg)
    v1, i1 = first_argmax(le)
    v2, i2 = first_argmax(jnp.where(lane_f == i1, neg, le))
    e2 = jnp.exp(v2 - v1)
    w1 = pg_sel / (1.0 + e2)
    w2 = pg_sel * e2 / (1.0 + e2)
    ids = jnp.where(lane == 0, i1 - N_GROUPS, jnp.where(lane == 1, i2 - N_GROUPS, 0.0))
    ids_ref[0] = ids.astype(jnp.int32)
    wts_ref[0] = jnp.where(lane == 0, w1, jnp.where(lane == 1, w2, 0.0))


def _out_proj(attn, gla, wo, x, gt1, g2, sc2, sh2, wr, br, tm):
    b, t, d = x.shape
    row = lambda bi, i: (bi, i, 0)
    vec = lambda bi, i: (bi, 0, 0)
    const = lambda bi, i: (0, 0)
    return pl.pallas_call(
        _out_proj_kernel,
        grid=(b, t // tm),
        in_specs=[
            pl.BlockSpec((1, tm, ATTN_Q_W), row),
            pl.BlockSpec((1, tm, GLA_V_W), row),
            pl.BlockSpec(wo.shape, const, pipeline_mode=pl.Buffered(1)),
            pl.BlockSpec((1, tm, d), row),
            pl.BlockSpec((1, 1, d), vec),
            pl.BlockSpec((1, d), const),
            pl.BlockSpec((1, 1, d), vec),
            pl.BlockSpec((1, 1, d), vec),
            pl.BlockSpec((d, 2 * LANES), const),
            pl.BlockSpec((1, LANES), const),
        ],
        out_specs=[
            pl.BlockSpec((1, tm, d), row),
            pl.BlockSpec((1, tm * _token_pitch(d), LANES), row),
            pl.BlockSpec((1, tm, LANES), row),
            pl.BlockSpec((1, tm, LANES), row),
        ],
        out_shape=[
            jax.ShapeDtypeStruct((b, t, d), F32),
            jax.ShapeDtypeStruct((b, t * _token_pitch(d), LANES), F32),
            jax.ShapeDtypeStruct((b, t, LANES), jnp.int32),
            jax.ShapeDtypeStruct((b, t, LANES), F32),
        ],
        compiler_params=_cparams("arbitrary", "arbitrary"),
        name="out_proj",
    )(attn, gla, wo, x, gt1, g2, sc2, sh2, wr, br)


def _moe_kernel(eidx_ref, eblk0_ref, enb_ref, bs0_ref, bn_ref, nblk_ref, order_ref,
                h2_hbm, w1_ref, w3_ref, w2_ref, y_hbm,
                xbuf, ybuf, gsem, ssem):
    del eidx_ref
    e = pl.program_id(0)
    nblk = nblk_ref[0]
    kx = xbuf.shape[1] // MOE_BLOCK
    ky = ybuf.shape[1] // MOE_BLOCK

    data_rows = w1_ref.shape[0] // LANES

    def gather_copy(sl, hbm_row, j, tokens=1):
        return pltpu.make_async_copy(h2_hbm.at[pl.ds(hbm_row * kx, tokens * data_rows)],
                                     xbuf.at[sl, pl.ds(j * kx, tokens * data_rows)], gsem.at[sl])

    def for_rows(n, per_group, per_row):
        ng = lax.shift_right_logical(n, ROW_GROUP_LOG2)
        lax.fori_loop(0, ng, lambda g, c: (per_group(g * ROW_GROUP), c)[1], 0)
        lax.fori_loop(ng * ROW_GROUP, n, lambda j, c: (per_row(j), c)[1], 0)

    def start_gather(blk, sl):
        s = bs0_ref[blk]

        def start_row(j):
            gather_copy(sl, lax.shift_right_logical(order_ref[s + j], 1), j).start()

        def group(j0):
            for u in range(ROW_GROUP):
                start_row(j0 + u)

        for_rows(bn_ref[blk], group, start_row)

    def wait_gather(blk, sl):
        for_rows(bn_ref[blk], lambda j0: gather_copy(sl, 0, 0, ROW_GROUP).wait(),
                 lambda j: gather_copy(sl, 0, 0).wait())

    half = MOE_BLOCK // 2

    def with_block_rows(blk, fn):
        @pl.when(bn_ref[blk] > half)
        def _():
            fn(MOE_BLOCK)

        @pl.when(bn_ref[blk] <= half)
        def _():
            fn(half)

    def output_copy(blk, sl, rows):
        return pltpu.make_async_copy(ybuf.at[sl, pl.ds(0, rows * ky)],
                                     y_hbm.at[pl.ds(bs0_ref[blk] * ky, rows * ky)], ssem.at[sl])

    def wait_output(blk, sl):
        with_block_rows(blk, lambda rows: output_copy(blk, sl, rows).wait())

    @pl.when(e == 0)
    def _():
        xbuf[...] = jnp.zeros_like(xbuf)
        ybuf[...] = jnp.zeros_like(ybuf)
        for g0 in range(GATHER_AHEAD):
            @pl.when(g0 < nblk)
            def _():
                start_gather(g0, g0)

    def block(g, carry):
        slot = g & 1
        xslot = lax.rem(g, GATHER_AHEAD + 1)

        @pl.when(g + GATHER_AHEAD < nblk)
        def _():
            start_gather(g + GATHER_AHEAD, lax.rem(g + GATHER_AHEAD, GATHER_AHEAD + 1))

        wait_gather(g, xslot)

        def expert_rows(rows):
            x = _load_token_tiles(xbuf.at[xslot, pl.ds(0, rows * kx)], rows, w1_ref.shape[0])
            a = jnp.dot(x, w1_ref[...], preferred_element_type=F32)
            gate = jnp.dot(x, w3_ref[...], preferred_element_type=F32)
            hid = a * jax.nn.sigmoid(a) * gate
            y = jnp.dot(hid, w2_ref[...], preferred_element_type=F32)
            _store_token_tiles(ybuf.at[slot, pl.ds(0, rows * ky)], y)

            @pl.when(g >= 1)
            def _():
                wait_output(g - 1, 1 - slot)

            output_copy(g, slot, rows).start()

        with_block_rows(g, expert_rows)
        return carry

    @pl.when(enb_ref[e] > 0)
    def _():
        lax.fori_loop(eblk0_ref[e], eblk0_ref[e] + enb_ref[e], block, 0)

    @pl.when(e == pl.num_programs(0) - 1)
    def _():
        wait_output(nblk - 1, (nblk - 1) & 1)
        ybuf[0] = jnp.zeros_like(ybuf[0])
        pad_rows = MOE_BLOCK * ky
        pad = pltpu.make_async_copy(ybuf.at[0], y_hbm.at[pl.ds(y_hbm.shape[0] - pad_rows, pad_rows)], ssem.at[0])
        pad.start()
        pad.wait()


def _moe(h2t, w1, w3, w2, eidx, eblk0, enb, bs0, bn, nblk, order):
    ne, d, ff = w1.shape
    kx = ky = _token_pitch(d)
    wmap = lambda e, eidx, *_: (eidx[e], 0, 0)
    grid_spec = pltpu.PrefetchScalarGridSpec(
        num_scalar_prefetch=7,
        grid=(ne,),
        in_specs=[
            pl.BlockSpec(memory_space=pl.ANY),
            pl.BlockSpec((None, d, ff), wmap),
            pl.BlockSpec((None, d, ff), wmap),
            pl.BlockSpec((None, ff, d), wmap),
        ],
        out_specs=pl.BlockSpec(memory_space=pl.ANY),
        scratch_shapes=[
            pltpu.VMEM((GATHER_AHEAD + 1, MOE_BLOCK * kx, LANES), F32),
            pltpu.VMEM((2, MOE_BLOCK * ky, LANES), F32),
            pltpu.SemaphoreType.DMA((GATHER_AHEAD + 1,)),
            pltpu.SemaphoreType.DMA((2,)),
        ],
    )
    return pl.pallas_call(
        _moe_kernel,
        grid_spec=grid_spec,
        out_shape=jax.ShapeDtypeStruct(((order.shape[0] + MOE_BLOCK) * ky, LANES), F32),
        compiler_params=_cparams("arbitrary"),
        name="moe",
    )(eidx, eblk0, enb, bs0, bn, nblk, order, h2t, w1, w3, w2)


def _combine_kernel(pos_ref, x1_ref, ys_hbm, wts_ref, gt_ref, o_ref, ybuf, sem):
    tm = x1_ref.shape[1]
    ky = ybuf.shape[2] // tm
    nt = pl.num_programs(1)
    n = pl.program_id(0) * nt + pl.program_id(1)
    slot = n & 1

    data_rows = x1_ref.shape[2] // LANES

    def row_copy(sl, k, pos, j, tokens=1):
        return pltpu.make_async_copy(ys_hbm.at[pl.ds(pos * ky, tokens * data_rows)],
                                     ybuf.at[sl, k, pl.ds(j * ky, tokens * data_rows)], sem.at[sl])

    def start_tile(tile, sl):
        base = tile * (tm * TOP_K)

        def group(g, carry):
            for u in range(ROW_GROUP):
                row_copy(sl, u % TOP_K, pos_ref[base + g * ROW_GROUP + u],
                         g * (ROW_GROUP // TOP_K) + u // TOP_K).start()
            return carry

        lax.fori_loop(0, tm * TOP_K // ROW_GROUP, group, 0)

    @pl.when(n == 0)
    def _():
        start_tile(0, 0)

    @pl.when(n + 1 < pl.num_programs(0) * nt)
    def _():
        start_tile(n + 1, 1 - slot)

    for k in range(TOP_K):
        row_copy(slot, k, 0, 0, tokens=tm).wait()
    w = wts_ref[0]
    d = x1_ref.shape[2]
    ff = (w[:, 0:1] * _load_token_tiles(ybuf.at[slot, 0], tm, d)
          + w[:, 1:2] * _load_token_tiles(ybuf.at[slot, 1], tm, d))
    o_ref[0] = x1_ref[0] + gt_ref[0] * ff


def _combine(x1, ys, pos, wts, gt2, tm):
    b, t, d = x1.shape
    ky = _token_pitch(d)
    row = lambda bi, i, pos: (bi, i, 0)
    vec = lambda bi, i, pos: (bi, 0, 0)
    grid_spec = pltpu.PrefetchScalarGridSpec(
        num_scalar_prefetch=1,
        grid=(b, t // tm),
        in_specs=[
            pl.BlockSpec((1, tm, d), row),
            pl.BlockSpec(memory_space=pl.ANY),
            pl.BlockSpec((1, tm, LANES), row),
            pl.BlockSpec((1, 1, d), vec),
        ],
        out_specs=pl.BlockSpec((1, tm, d), row),
        scratch_shapes=[
            pltpu.VMEM((2, TOP_K, tm * ky, LANES), F32),
            pltpu.SemaphoreType.DMA((2,)),
        ],
    )
    return pl.pallas_call(
        _combine_kernel,
        grid_spec=grid_spec,
        out_shape=jax.ShapeDtypeStruct((b, t, d), F32),
        compiler_params=_cparams("arbitrary", "arbitrary"),
        name="combine",
    )(pos, x1, ys, wts, gt2)


def _rope_tables(t):
    rows = t // GRID_W
    n_freq = HEAD_DIM // 4
    inv = ROPE_THETA ** (-jnp.arange(n_freq, dtype=F32) / n_freq)
    ar = jnp.arange(rows, dtype=F32)[:, None] * inv
    ac = jnp.arange(GRID_W, dtype=F32)[:, None] * inv

    def expand(fr, fc, sign):
        by_row = jnp.broadcast_to(fr[:, None, :], (rows, GRID_W, n_freq))
        by_col = jnp.broadcast_to(fc[None, :, :], (rows, GRID_W, n_freq))
        return jnp.concatenate([sign * by_row, by_row, sign * by_col, by_col], axis=2).reshape(t, HEAD_DIM)

    return expand(jnp.cos(ar), jnp.cos(ac), 1.0), expand(jnp.sin(ar), jnp.sin(ac), -1.0)


def _gate_weights(w2, bias):
    r = GLA_GATE_RANK
    wh = w2.reshape(2, r, GLA_HEADS, GLA_DK).transpose(2, 0, 1, 3)
    w = jnp.zeros((GLA_HEADS, 2 * r, 2 * GLA_DK), F32)
    w = w.at[:, 0:r, 0:GLA_DK].set(wh[:, 0]).at[:, r:2 * r, GLA_DK:].set(wh[:, 1])
    hi = w.astype(BF16)
    mid = (w - hi.astype(F32)).astype(BF16)
    pad = jnp.zeros((GLA_HEADS, LANES - 6 * r, 2 * GLA_DK), BF16)
    w2s = jnp.concatenate([hi, hi, mid, pad], axis=1)
    bias2 = bias.reshape(2, GLA_HEADS, GLA_DK).transpose(1, 0, 2).reshape(GLA_HEADS, 1, 2 * GLA_DK)
    return w2s, bias2


def _block_plan(eid_flat, nb):
    order = jnp.argsort(eid_flat).astype(jnp.int32)
    eids = jnp.arange(N_EXPERTS, dtype=jnp.int32)
    of_assign = (eid_flat[:, None] == eids[None, :]).astype(jnp.int32)
    counts = jnp.sum(of_assign, axis=0)
    starts = jnp.cumsum(counts) - counts
    nblk_e = (counts + MOE_BLOCK - 1) // MOE_BLOCK
    bends = jnp.cumsum(nblk_e)
    bstarts = bends - nblk_e
    nblk = bends[-1]
    prev_used = lax.cummax(jnp.where(counts > 0, eids, -1))
    eidx = jnp.where(prev_used >= 0, prev_used, jnp.argmax(counts > 0)).astype(jnp.int32)
    bi = jnp.arange(nb, dtype=jnp.int32)
    bic = jnp.minimum(bi, jnp.maximum(nblk - 1, 0))
    bexp = jnp.minimum(jnp.sum(bends[None, :] <= bic[:, None], axis=1), N_EXPERTS - 1)
    of_block = (bexp[:, None] == eids[None, :]).astype(jnp.int32)

    def lookup(per_expert):
        return jnp.sum(of_block * per_expert[None, :], axis=1)

    r0 = (bic - lookup(bstarts)) * MOE_BLOCK
    bs0 = (lookup(starts) + r0).astype(jnp.int32)
    bn = jnp.where(bi < nblk, jnp.minimum(lookup(counts) - r0, MOE_BLOCK), 0).astype(jnp.int32)
    pos = jnp.argsort(order).astype(jnp.int32)
    plan = (eidx, bstarts.astype(jnp.int32), nblk_e.astype(jnp.int32), bs0, bn,
            nblk.reshape(1).astype(jnp.int32), order)
    return plan, pos


def kernel(x, c, ctx, c_ctx, w_ada, b_ada, norm1_g, w_in, q_norm_g, k_norm_g, gla_gate_w2, gla_gate_b, gla_norm_g, w_out, norm2_g, router_grp_w, router_grp_b, router_exp_w, router_exp_b, moe_w1, moe_w3, moe_w2):
    b, t, d = x.shape
    tc = ctx.shape[1]
    depth = w_ada.shape[0]
    assert depth == 1, "single-layer stack: the context stream only feeds keys/values and GLA states"
    layer = 0

    c8 = jnp.zeros((8, d), F32).at[0:b].set(c).at[b].set(c_ctx)
    mod = _ada(c8, w_ada[layer], b_ada[layer])
    sh1, sc1, gt1, sh2, sc2, gt2 = [mod[0:b, i * d:(i + 1) * d].reshape(b, 1, d) for i in range(6)]
    sh1c, sc1c = [jnp.broadcast_to(mod[b, i * d:(i + 1) * d].reshape(1, 1, d), (b, 1, d)) for i in range(2)]

    w_main, w_tail = _w_in_parts(jnp.swapaxes(w_in[layer], 0, 1))

    cos_t, sin_t = _rope_tables(t)
    g1 = norm1_g[layer].reshape(1, d)
    qg = q_norm_g[layer].reshape(1, HEAD_DIM)
    kg = k_norm_g[layer].reshape(1, HEAD_DIM)
    q, k, v, gq, gk, gv, og, lr = _in_proj(x, g1, sc1, sh1, w_main, w_tail, qg, kg, cos_t, sin_t, 512)
    ones_t = jnp.ones((tc, HEAD_DIM), F32)
    _, kc, vc, gqc, gkc, gvc, _, lrc = _in_proj(ctx, g1, sc1c, sh1c, w_main, w_tail, qg, kg,
                                                ones_t, jnp.zeros_like(ones_t), tc)

    attn = _attn(q, jnp.concatenate([kc, k], axis=1), jnp.concatenate([vc, v], axis=1), 256)
    w2s, bias2 = _gate_weights(gla_gate_w2[layer], gla_gate_b[layer])
    gla = _gla(gq, gk, gv, lr, og, gqc, gkc, gvc, lrc, w2s, bias2, gla_norm_g[layer].reshape(1, GLA_DV))

    wr = jnp.concatenate([router_grp_w[layer], router_exp_w[layer],
                          jnp.zeros((d, LANES - N_GROUPS - N_EXPERTS), F32)], axis=1)
    br = jnp.concatenate([router_grp_b[layer], router_exp_b[layer],
                          jnp.zeros((LANES - N_GROUPS - N_EXPERTS,), F32)]).reshape(1, LANES)
    wr_hi = wr.astype(BF16)
    wr_parts = jnp.concatenate([wr_hi, (wr - wr_hi.astype(F32)).astype(BF16)], axis=1)
    x1, h2t, ids, wts = _out_proj(attn, gla, w_out[layer], x, gt1, norm2_g[layer].reshape(1, d),
                                  sc2, sh2, wr_parts, br, 512)

    m = b * t
    n_assign = m * TOP_K
    nb = -(-(n_assign + N_EXPERTS * (MOE_BLOCK - 1)) // MOE_BLOCK)
    eid_flat = ids[:, :, 0:TOP_K].reshape(n_assign)
    plan, pos = _block_plan(eid_flat, nb)
    ys = _moe(h2t.reshape(-1, LANES), moe_w1[layer], moe_w3[layer], moe_w2[layer], *plan)
    return _combine(x1, ys, pos, wts, gt2, 256)
```

```python
import functools

import jax
import jax.numpy as jnp
from jax import lax
from jax.experimental import pallas as pl
from jax.experimental.pallas import tpu as pltpu

EPS = 1e-6
GRID_W = 64
ROPE_THETA = 10000.0

ATTN_HEADS = 8
ATTN_KV_HEADS = 2
HEAD_DIM = 128
GQA_GROUP = ATTN_HEADS // ATTN_KV_HEADS
ATTN_SCORES_AHEAD = 1

GLA_HEADS = 4
GLA_DK = 128
GLA_DV = 256
GLA_GATE_RANK = 16
GLA_GATE_NORMALIZER = 16.0
GLA_CHUNK = 64
GLA_CHUNK_LOG2 = 6
GLA_PREP_TILE = 256
GLA_OUT_GROUP = 16

N_GROUPS = 8
EXPERTS_PER_GROUP = 8
N_EXPERTS = N_GROUPS * EXPERTS_PER_GROUP
TOP_K = 2
MOE_BLOCK = 256
ROW_GROUP_LOG2 = 3
ROW_GROUP = 1 << ROW_GROUP_LOG2
GATHER_AHEAD = 2

LOG2_E = 1.4426950408889634
LANES = 128
VMEM_LIMIT = 56 * 1024 * 1024

ATTN_Q_W = ATTN_HEADS * HEAD_DIM
ATTN_KV_W = ATTN_KV_HEADS * HEAD_DIM
GLA_K_W = GLA_HEADS * GLA_DK
GLA_V_W = GLA_HEADS * GLA_DV

BF16 = jnp.bfloat16
F32 = jnp.float32


def _cparams(*sem):
    return pltpu.CompilerParams(dimension_semantics=sem, vmem_limit_bytes=VMEM_LIMIT)


def _ada_kernel(c_ref, w_ref, b_ref, o_ref):
    c = c_ref[...]
    s = c * jax.nn.sigmoid(c)
    o_ref[...] = jnp.dot(s.astype(BF16), w_ref[...].astype(BF16), preferred_element_type=F32) + b_ref[...]


def _ada(c8, w, b):
    d, n = w.shape
    tn = 1024
    return pl.pallas_call(
        _ada_kernel,
        grid=(n // tn,),
        in_specs=[
            pl.BlockSpec((8, d), lambda j: (0, 0)),
            pl.BlockSpec((d, tn), lambda j: (0, j)),
            pl.BlockSpec((1, tn), lambda j: (0, j)),
        ],
        out_specs=pl.BlockSpec((8, tn), lambda j: (0, j)),
        out_shape=jax.ShapeDtypeStruct((8, n), F32),
        compiler_params=_cparams("arbitrary"),
        name="ada",
    )(c8, w, b.reshape(1, n))


def _w_main_kernel(wt_ref, o_ref):
    o_ref[...] = wt_ref[...].T.astype(BF16)


def _w_tail_kernel(wt_hbm, o_ref, tail, sem):
    r = 2 * GLA_GATE_RANK
    n_tail = tail.shape[0]
    copy = pltpu.make_async_copy(wt_hbm.at[pl.ds(wt_hbm.shape[0] - n_tail, n_tail)], tail, sem)
    copy.start()
    copy.wait()
    o_ref[:, :GLA_V_W] = tail[r:, :].T.astype(BF16)
    lowrank = jnp.concatenate([tail[:r, :], jnp.zeros((LANES - r, tail.shape[1]), F32)], axis=0)
    o_ref[:, GLA_V_W:] = lowrank.T.astype(BF16)


def _w_in_parts(wt):
    n, d = wt.shape
    n_main = n - GLA_V_W - 2 * GLA_GATE_RANK
    tn = 512
    main = pl.pallas_call(
        _w_main_kernel,
        grid=(n_main // tn,),
        in_specs=[pl.BlockSpec((tn, d), lambda j: (j, 0))],
        out_specs=pl.BlockSpec((d, tn), lambda j: (0, j)),
        out_shape=jax.ShapeDtypeStruct((d, n_main), BF16),
        compiler_params=_cparams("arbitrary"),
        name="w_in_main",
    )(wt)
    tail = pl.pallas_call(
        _w_tail_kernel,
        in_specs=[pl.BlockSpec(memory_space=pl.ANY)],
        out_shape=jax.ShapeDtypeStruct((d, GLA_V_W + LANES), BF16),
        scratch_shapes=[pltpu.VMEM((n - n_main, d), F32), pltpu.SemaphoreType.DMA],
        compiler_params=pltpu.CompilerParams(vmem_limit_bytes=VMEM_LIMIT),
        name="w_in_tail",
    )(wt)
    return main, tail


def _swap32(y):
    lane = lax.broadcasted_iota(jnp.int32, y.shape, 1)
    return jnp.where((lane & 63) < 32, pltpu.roll(y, 96, 1), pltpu.roll(y, 32, 1))


def _head_norm_rope(a, g, cos, sin):
    y = a * lax.rsqrt(jnp.mean(a * a, axis=-1, keepdims=True) + EPS) * g
    return y * cos + _swap32(y) * sin


def _in_proj_kernel(x_ref, g1_ref, sc_ref, sh_ref, w_ref, wt_ref, qg_ref, kg_ref, cos_ref, sin_ref,
                    q_ref, k_ref, v_ref, gq_ref, gk_ref, gv_ref, og_ref, lr_ref, *, context):
    x = x_ref[0]
    xn = x * lax.rsqrt(jnp.mean(x * x, axis=-1, keepdims=True) + EPS)
    h = (xn * g1_ref[...] * (1.0 + sc_ref[0]) + sh_ref[0]).astype(BF16)
    cos = cos_ref[...]
    sin = sin_ref[...]

    def proj(off, width, w=w_ref):
        return jnp.dot(h, w[:, off:off + width], preferred_element_type=F32)

    qg = qg_ref[...] * (HEAD_DIM ** -0.5 * LOG2_E)
    off = 0
    for out_ref, gain, heads in ((q_ref, qg, ATTN_HEADS), (k_ref, kg_ref[...], ATTN_KV_HEADS)):
        if context and out_ref is q_ref:
            q_ref[...] = jnp.zeros_like(q_ref)
            off += heads * HEAD_DIM
            continue
        for pair in range(heads // 2):
            a2 = proj(off, 2 * HEAD_DIM)
            for u in range(2):
                lo = (2 * pair + u) * HEAD_DIM
                a = a2[:, u * HEAD_DIM:(u + 1) * HEAD_DIM]
                out_ref[0, :, lo:lo + HEAD_DIM] = _head_norm_rope(a, gain, cos, sin).astype(BF16)
            off += 2 * HEAD_DIM
    vv = proj(off, ATTN_KV_W).astype(BF16)
    for hk in range(ATTN_KV_HEADS):
        v_ref[0, :, 2 * hk * HEAD_DIM:(2 * hk + 1) * HEAD_DIM] = vv[:, hk * HEAD_DIM:(hk + 1) * HEAD_DIM]
        v_ref[0, :, (2 * hk + 1) * HEAD_DIM:(2 * hk + 2) * HEAD_DIM] = jnp.ones((vv.shape[0], HEAD_DIM), BF16)
    off += ATTN_KV_W
    gq_ref[0] = proj(off, GLA_K_W)
    off += GLA_K_W
    gk_ref[0] = proj(off, GLA_K_W)
    off += GLA_K_W
    gv_ref[0] = proj(off, GLA_V_W).astype(BF16)
    og_ref[0] = jnp.zeros_like(og_ref[0]) if context else proj(0, GLA_V_W, wt_ref)
    lr_ref[0] = proj(GLA_V_W, LANES, wt_ref)


def _in_proj(x, g1, sc, sh, w_main, w_tail, qg, kg, cos_t, sin_t, tm, context=False):
    b, t, d = x.shape
    row = lambda bi, i: (bi, i, 0)
    vec = lambda bi, i: (bi, 0, 0)
    const = lambda bi, i: (0, 0)
    tab = lambda bi, i: (i, 0)
    widths = [(ATTN_Q_W, BF16), (ATTN_KV_W, BF16), (2 * ATTN_KV_W, BF16), (GLA_K_W, F32), (GLA_K_W, F32),
              (GLA_V_W, BF16), (GLA_V_W, F32), (LANES, F32)]
    return pl.pallas_call(
        functools.partial(_in_proj_kernel, context=context),
        grid=(b, t // tm),
        in_specs=[
            pl.BlockSpec((1, tm, d), row),
            pl.BlockSpec((1, d), const),
            pl.BlockSpec((1, 1, d), vec),
            pl.BlockSpec((1, 1, d), vec),
            pl.BlockSpec(w_main.shape, const, pipeline_mode=pl.Buffered(1)),
            pl.BlockSpec(w_tail.shape, const, pipeline_mode=pl.Buffered(1)),
            pl.BlockSpec((1, HEAD_DIM), const),
            pl.BlockSpec((1, HEAD_DIM), const),
            pl.BlockSpec((tm, HEAD_DIM), tab),
            pl.BlockSpec((tm, HEAD_DIM), tab),
        ],
        out_specs=[pl.BlockSpec((1, tm, wd), row) for wd, _ in widths],
        out_shape=[jax.ShapeDtypeStruct((b, t, wd), dt) for wd, dt in widths],
        compiler_params=_cparams("arbitrary", "arbitrary"),
        name="in_proj",
    )(x, g1, sc, sh, w_main, w_tail, qg, kg, cos_t, sin_t)


def _attn_kernel(q_ref, k_ref, v_ref, o_ref):
    def scores(h):
        hk = h // GQA_GROUP
        q = q_ref[0, :, h * HEAD_DIM:(h + 1) * HEAD_DIM]
        k = k_ref[0, :, hk * HEAD_DIM:(hk + 1) * HEAD_DIM]
        return lax.dot_general(q, k, (((1,), (1,)), ((), ())), preferred_element_type=F32)

    ahead = ATTN_SCORES_AHEAD
    pending = [scores(h) for h in range(ahead)]
    for h in range(ATTN_HEADS):
        hk = h // GQA_GROUP
        s = pending.pop(0)
        if h + ahead < ATTN_HEADS:
            pending.append(scores(h + ahead))
        p = jnp.exp2(s - jnp.max(s, axis=-1, keepdims=True))
        ol = jnp.dot(p.astype(BF16), v_ref[0, :, 2 * hk * HEAD_DIM:2 * (hk + 1) * HEAD_DIM],
                     preferred_element_type=F32)
        o_ref[0, :, h * HEAD_DIM:(h + 1) * HEAD_DIM] = (
            ol[:, :HEAD_DIM] / ol[:, HEAD_DIM:HEAD_DIM + 1]).astype(BF16)


def _attn(q, k_all, v_all, tq):
    b, t, _ = q.shape
    tk = k_all.shape[1]
    return pl.pallas_call(
        _attn_kernel,
        grid=(b, t // tq),
        in_specs=[
            pl.BlockSpec((1, tq, ATTN_Q_W), lambda bi, i: (bi, i, 0)),
            pl.BlockSpec((1, tk, ATTN_KV_W), lambda bi, i: (bi, 0, 0)),
            pl.BlockSpec((1, tk, 2 * ATTN_KV_W), lambda bi, i: (bi, 0, 0)),
        ],
        out_specs=pl.BlockSpec((1, tq, ATTN_Q_W), lambda bi, i: (bi, i, 0)),
        out_shape=jax.ShapeDtypeStruct((b, t, ATTN_Q_W), BF16),
        compiler_params=_cparams("arbitrary", "arbitrary"),
        name="attn",
    )(q, k_all, v_all)


def _log_sigmoid(x):
    return jnp.minimum(x, 0.0) - jnp.log(1.0 + jnp.exp(-jnp.abs(x)))


def _split3(x):
    hi = x.astype(BF16)
    r = x - hi.astype(F32)
    mid = r.astype(BF16)
    lo = (r - mid.astype(F32)).astype(BF16)
    return hi, mid, lo


def _dot_exact_lhs(m, x):
    return sum(jnp.dot(m, part, preferred_element_type=F32) for part in _split3(x))


def _prefix_operator(n):
    ri = lax.broadcasted_iota(jnp.int32, (n, n), 0)
    ci = lax.broadcasted_iota(jnp.int32, (n, n), 1)
    same = lax.shift_right_logical(ri, GLA_CHUNK_LOG2) == lax.shift_right_logical(ci, GLA_CHUNK_LOG2)
    return (same & (ci <= ri)).astype(BF16)


def _gate_prep(q, k, lr, w2s, bias2, prefix):
    c = GLA_CHUNK
    hi = lr.astype(BF16).astype(F32)
    mid = (lr - hi).astype(BF16).astype(F32)
    lhs = (hi + pltpu.roll(mid, 2 * GLA_GATE_RANK, 1) + pltpu.roll(hi, 4 * GLA_GATE_RANK, 1)).astype(BF16)
    logits = jnp.dot(lhs, w2s, preferred_element_type=F32) + bias2
    g = _log_sigmoid(logits) * (1.0 / GLA_GATE_NORMALIZER)
    pre = _dot_exact_lhs(prefix, g)
    tot = jnp.concatenate([jnp.broadcast_to(pre[lo + c - 1:lo + c, :], (c, pre.shape[1]))
                           for lo in range(0, pre.shape[0], c)], axis=0)
    dk = GLA_DK
    bcs = (pre[:, :dk], tot[:, dk:] - pre[:, dk:] + g[:, dk:])
    out = []
    for d, bc in enumerate(bcs):
        b_end = tot[:, d * dk:(d + 1) * dk]
        qe = (q * (GLA_DK ** -0.5) * jnp.exp(bc)).astype(BF16)
        ke = (k * jnp.exp(-bc)).astype(BF16)
        kend = (k * jnp.exp(b_end - bc)).astype(BF16)
        out.append((qe, ke, kend, b_end))
    return out


def _state_step(s_ref, v, kend, dec):
    upd = lax.dot_general(v, kend, (((0,), (0,)), ((), ())), preferred_element_type=F32)
    s_ref[...] = s_ref[...] * dec + upd


def _gla_kernel(q_ref, k_ref, v_ref, lr_ref, og_ref, qc_ref, kc_ref, vc_ref, lrc_ref,
                w2_ref, b_ref, ng_ref, o_ref,
                s_ref, qe_ref, ke_ref, kend_ref, dec_ref, kendc_ref, decc_ref, sbf_ref):
    c = GLA_CHUNK
    t = q_ref.shape[1]
    tc = qc_ref.shape[1]
    nc = t // c
    tile = GLA_PREP_TILE
    cpt = tile // c
    dirs = (0, 1)

    def store_dec(ref, d, base, b_end):
        for ch in range(b_end.shape[0] // c):
            ref[d, pl.ds(base + ch, 1), :] = jnp.exp(b_end[ch * c:ch * c + 1, :])

    prefix = _prefix_operator(tile)

    for i in range(tc // tile):
        rows = slice(i * tile, (i + 1) * tile)
        prep_c = _gate_prep(qc_ref[0, rows, :], kc_ref[0, rows, :], lrc_ref[0, rows, :],
                            w2_ref[...], b_ref[...], prefix)
        for d, (_, _, kend, b_end) in enumerate(prep_c):
            kendc_ref[d, rows, :] = kend
            store_dec(decc_ref, d, i * cpt, b_end)
    s_ref[...] = jnp.zeros_like(s_ref)
    for j in range(tc // c):
        for d in dirs:
            ch = j if d == 0 else tc // c - 1 - j
            _state_step(s_ref.at[d], vc_ref[0, ch * c:(ch + 1) * c, :], kendc_ref[d, ch * c:(ch + 1) * c, :],
                        decc_ref[d, ch:ch + 1, :])

    def prep(i, carry):
        lo = pl.multiple_of(i * tile, tile)
        rows = pl.ds(lo, tile)
        prep_l = _gate_prep(q_ref[0, rows, :], k_ref[0, rows, :], lr_ref[0, rows, :],
                            w2_ref[...], b_ref[...], prefix)
        for d, (qe, ke, kend, b_end) in enumerate(prep_l):
            qe_ref[d, rows, :] = qe
            ke_ref[d, rows, :] = ke
            kend_ref[d, rows, :] = kend
            store_dec(dec_ref, d, i * cpt, b_end)
        return carry

    lax.fori_loop(0, t // tile, prep, 0, unroll=4)

    def scan(j, carry):
        for d in dirs:
            ch = j if d == 0 else nc - 1 - j
            rows = pl.ds(pl.multiple_of(ch * c, c), c)
            sbf_ref[d, ch] = s_ref[d].astype(BF16)
            _state_step(s_ref.at[d], v_ref[0, rows, :], kend_ref[d, rows, :], dec_ref[d, pl.ds(ch, 1), :])
        return carry

    lax.fori_loop(0, nc, scan, 0, unroll=8)

    ri = lax.broadcasted_iota(jnp.int32, (c, c), 0)
    ci = lax.broadcasted_iota(jnp.int32, (c, c), 1)
    masks = (ci <= ri, ci >= ri)

    group = GLA_OUT_GROUP

    def out(i, carry):
        chunks = [i * group + u for u in range(group)]
        rows = [pl.ds(pl.multiple_of(ch * c, c), c) for ch in chunks]
        qes = [[qe_ref[d, r, :] for d in dirs] for r in rows]
        scores = [[lax.dot_general(qes[u][d], ke_ref[d, rows[u], :], (((1,), (1,)), ((), ())),
                                   preferred_element_type=F32) for d in dirs] for u in range(group)]
        inter = [[lax.dot_general(qes[u][d], sbf_ref[d, chunks[u]], (((1,), (1,)), ((), ())),
                                  preferred_element_type=F32) for d in dirs] for u in range(group)]
        for u in range(group):
            v = v_ref[0, rows[u], :]
            o = inter[u][0] + inter[u][1]
            for d in dirs:
                a = jnp.where(masks[d], scores[u][d], 0.0).astype(BF16)
                o = o + jnp.dot(a, v, preferred_element_type=F32)
            on = o * lax.rsqrt(jnp.mean(o * o, axis=-1, keepdims=True) + EPS) * ng_ref[...]
            og = og_ref[0, rows[u], :]
            o_ref[0, rows[u], :] = (on * (og * jax.nn.sigmoid(og))).astype(BF16)
        return carry

    lax.fori_loop(0, nc // group, out, 0)


def _gla(gq, gk, gv, lr, og, gqc, gkc, gvc, lrc, w2, bias, ng):
    b, t, _ = gq.shape
    tc = gqc.shape[1]
    hk = lambda bi, h: (bi, 0, h)
    h0 = lambda bi, h: (bi, 0, 0)
    return pl.pallas_call(
        _gla_kernel,
        grid=(b, GLA_HEADS),
        in_specs=[
            pl.BlockSpec((1, t, GLA_DK), hk),
            pl.BlockSpec((1, t, GLA_DK), hk),
            pl.BlockSpec((1, t, GLA_DV), hk),
            pl.BlockSpec((1, t, LANES), h0),
            pl.BlockSpec((1, t, GLA_DV), hk),
            pl.BlockSpec((1, tc, GLA_DK), hk),
            pl.BlockSpec((1, tc, GLA_DK), hk),
            pl.BlockSpec((1, tc, GLA_DV), hk),
            pl.BlockSpec((1, tc, LANES), h0),
            pl.BlockSpec((None, LANES, 2 * GLA_DK), lambda bi, h: (h, 0, 0)),
            pl.BlockSpec((None, 1, 2 * GLA_DK), lambda bi, h: (h, 0, 0)),
            pl.BlockSpec((1, GLA_DV), lambda bi, h: (0, 0)),
        ],
        out_specs=pl.BlockSpec((1, t, GLA_DV), hk),
        out_shape=jax.ShapeDtypeStruct((b, t, GLA_V_W), BF16),
        scratch_shapes=[
            pltpu.VMEM((2, GLA_DV, GLA_DK), F32),
            pltpu.VMEM((2, t, GLA_DK), BF16),
            pltpu.VMEM((2, t, GLA_DK), BF16),
            pltpu.VMEM((2, t, GLA_DK), BF16),
            pltpu.VMEM((2, t // GLA_CHUNK, GLA_DK), F32),
            pltpu.VMEM((2, tc, GLA_DK), BF16),
            pltpu.VMEM((2, tc // GLA_CHUNK, GLA_DK), F32),
            pltpu.VMEM((2, t // GLA_CHUNK, GLA_DV, GLA_DK), BF16),
        ],
        compiler_params=_cparams("arbitrary", "arbitrary"),
        name="gla",
    )(gq, gk, gv, lr, og, gqc, gkc, gvc, lrc, w2, bias, ng)


def _token_pitch(d):
    rows = d // LANES
    return rows + 4 if rows % 8 == 0 else rows


def _store_token_tiles(ref, x):
    n = x.shape[0]
    k = x.shape[1] // LANES
    pitch = ref.shape[0] // n
    for s in range(pitch):
        slab = x[:, s * LANES:(s + 1) * LANES] if s < k else jnp.zeros((n, LANES), x.dtype)
        ref[pl.ds(s, n, stride=pitch), :] = slab


def _load_token_tiles(ref, n, d):
    pitch = ref.shape[0] // n
    return jnp.concatenate([ref[pl.ds(s, n, stride=pitch), :] for s in range(d // LANES)], axis=1)


def _out_proj_kernel(attn_ref, gla_ref, wo_ref, x_ref, gt_ref, g2_ref, sc_ref, sh_ref, wr_ref, br_ref,
                     x1_ref, h2_ref, ids_ref, wts_ref):
    y = jnp.dot(attn_ref[0], wo_ref[0:ATTN_Q_W, :], preferred_element_type=F32)
    y = y + jnp.dot(gla_ref[0], wo_ref[ATTN_Q_W:, :], preferred_element_type=F32)
    x1 = x_ref[0] + gt_ref[0] * y
    x1_ref[0] = x1
    xn = x1 * lax.rsqrt(jnp.mean(x1 * x1, axis=-1, keepdims=True) + EPS)
    h2 = xn * g2_ref[...] * (1.0 + sc_ref[0]) + sh_ref[0]
    hi = h2.astype(BF16)
    hi_f = hi.astype(F32)

    _store_token_tiles(h2_ref.at[0], h2)

    mid = (h2 - hi_f).astype(BF16)
    both = jnp.dot(hi, wr_ref[...], preferred_element_type=F32)
    logits = (both[:, :LANES] + both[:, LANES:]
              + jnp.dot(mid, wr_ref[:, :LANES], preferred_element_type=F32)) + br_ref[...]
    lane = lax.broadcasted_iota(jnp.int32, logits.shape, 1)
    lane_f = lane.astype(F32)
    neg = jnp.float32(-jnp.inf)

    def first_argmax(vals):
        m = jnp.max(vals, axis=-1, keepdims=True)
        idx = jnp.min(jnp.where(vals == m, lane_f, float(LANES)), axis=-1, keepdims=True)
        return m, idx

    lg = jnp.where(lane < N_GROUPS, logits, neg)
    mg, grp = first_argmax(lg)
    pg_sel = 1.0 / jnp.sum(jnp.exp(lg - mg), axis=-1, keepdims=True)
    lo = N_GROUPS + grp * EXPERTS_PER_GROUP
    in_grp = (lane_f >= lo) & (lane_f < lo + EXPERTS_PER_GROUP)
    le = jnp.where(in_grp, logits, neg)
    v1, i1 = first_argmax(le)
    v2, i2 = first_argmax(jnp.where(lane_f == i1, neg, le))
    e2 = jnp.exp(v2 - v1)
    w1 = pg_sel / (1.0 + e2)
    w2 = pg_sel * e2 / (1.0 + e2)
    ids = jnp.where(lane == 0, i1 - N_GROUPS, jnp.where(lane == 1, i2 - N_GROUPS, 0.0))
    ids_ref[0] = ids.astype(jnp.int32)
    wts_ref[0] = jnp.where(lane == 0, w1, jnp.where(lane == 1, w2, 0.0))


def _out_proj(attn, gla, wo, x, gt1, g2, sc2, sh2, wr, br, tm):
    b, t, d = x.shape
    row = lambda bi, i: (bi, i, 0)
    vec = lambda bi, i: (bi, 0, 0)
    const = lambda bi, i: (0, 0)
    return pl.pallas_call(
        _out_proj_kernel,
        grid=(b, t // tm),
        in_specs=[
            pl.BlockSpec((1, tm, ATTN_Q_W), row),
            pl.BlockSpec((1, tm, GLA_V_W), row),
            pl.BlockSpec(wo.shape, const, pipeline_mode=pl.Buffered(1)),
            pl.BlockSpec((1, tm, d), row),
            pl.BlockSpec((1, 1, d), vec),
            pl.BlockSpec((1, d), const),
            pl.BlockSpec((1, 1, d), vec),
            pl.BlockSpec((1, 1, d), vec),
            pl.BlockSpec((d, 2 * LANES), const),
            pl.BlockSpec((1, LANES), const),
        ],
        out_specs=[
            pl.BlockSpec((1, tm, d), row),
            pl.BlockSpec((1, tm * _token_pitch(d), LANES), row),
            pl.BlockSpec((1, tm, LANES), row),
            pl.BlockSpec((1, tm, LANES), row),
        ],
        out_shape=[
            jax.ShapeDtypeStruct((b, t, d), F32),
            jax.ShapeDtypeStruct((b, t * _token_pitch(d), LANES), F32),
            jax.ShapeDtypeStruct((b, t, LANES), jnp.int32),
            jax.ShapeDtypeStruct((b, t, LANES), F32),
        ],
        compiler_params=_cparams("arbitrary", "arbitrary"),
        name="out_proj",
    )(attn, gla, wo, x, gt1, g2, sc2, sh2, wr, br)


def _moe_kernel(eidx_ref, eblk0_ref, enb_ref, bs0_ref, bn_ref, nblk_ref, order_ref,
                h2_hbm, w1_ref, w3_ref, w2_ref, y_hbm,
                xbuf, ybuf, gsem, ssem):
    del eidx_ref
    e = pl.program_id(0)
    nblk = nblk_ref[0]
    kx = xbuf.shape[1] // MOE_BLOCK
    ky = ybuf.shape[1] // MOE_BLOCK

    data_rows = w1_ref.shape[0] // LANES

    def gather_copy(sl, hbm_row, j, tokens=1):
        return pltpu.make_async_copy(h2_hbm.at[pl.ds(hbm_row * kx, tokens * data_rows)],
                                     xbuf.at[sl, pl.ds(j * kx, tokens * data_rows)], gsem.at[sl])

    def for_rows(n, per_group, per_row):
        ng = lax.shift_right_logical(n, ROW_GROUP_LOG2)
        lax.fori_loop(0, ng, lambda g, c: (per_group(g * ROW_GROUP), c)[1], 0)
        lax.fori_loop(ng * ROW_GROUP, n, lambda j, c: (per_row(j), c)[1], 0)

    def start_gather(blk, sl):
        s = bs0_ref[blk]

        def start_row(j):
            gather_copy(sl, lax.shift_right_logical(order_ref[s + j], 1), j).start()

        def group(j0):
            for u in range(ROW_GROUP):
                start_row(j0 + u)

        for_rows(bn_ref[blk], group, start_row)

    def wait_gather(blk, sl):
        for_rows(bn_ref[blk], lambda j0: gather_copy(sl, 0, 0, ROW_GROUP).wait(),
                 lambda j: gather_copy(sl, 0, 0).wait())

    half = MOE_BLOCK // 2

    def with_block_rows(blk, fn):
        @pl.when(bn_ref[blk] > half)
        def _():
            fn(MOE_BLOCK)

        @pl.when(bn_ref[blk] <= half)
        def _():
            fn(half)

    def with_copy_rows(blk, fn):
        quarter = MOE_BLOCK // 4
        n = bn_ref[blk]
        for c in range(1, 5):
            @pl.when((n > (c - 1) * quarter) & (n <= c * quarter))
            def _():
                fn(c * quarter)

    def output_copy(blk, sl, rows):
        return pltpu.make_async_copy(ybuf.at[sl, pl.ds(0, rows * ky)],
                                     y_hbm.at[pl.ds(bs0_ref[blk] * ky, rows * ky)], ssem.at[sl])

    def start_output(blk, sl):
        with_copy_rows(blk, lambda rows: output_copy(blk, sl, rows).start())

    def wait_output(blk, sl):
        with_copy_rows(blk, lambda rows: output_copy(blk, sl, rows).wait())

    @pl.when(e == 0)
    def _():
        xbuf[...] = jnp.zeros_like(xbuf)
        ybuf[...] = jnp.zeros_like(ybuf)
        for g0 in range(GATHER_AHEAD):
            @pl.when(g0 < nblk)
            def _():
                start_gather(g0, g0)

    def block(g, carry):
        slot = g & 1
        xslot = lax.rem(g, GATHER_AHEAD + 1)

        @pl.when(g + GATHER_AHEAD < nblk)
        def _():
            start_gather(g + GATHER_AHEAD, lax.rem(g + GATHER_AHEAD, GATHER_AHEAD + 1))

        wait_gather(g, xslot)

        def expert_rows(rows):
            x = _load_token_tiles(xbuf.at[xslot, pl.ds(0, rows * kx)], rows, w1_ref.shape[0])
            a = jnp.dot(x, w1_ref[...], preferred_element_type=F32)
            gate = jnp.dot(x, w3_ref[...], preferred_element_type=F32)
            hid = a * jax.nn.sigmoid(a) * gate
            y = jnp.dot(hid, w2_ref[...], preferred_element_type=F32)
            _store_token_tiles(ybuf.at[slot, pl.ds(0, rows * ky)], y)

            @pl.when(g >= 1)
            def _():
                wait_output(g - 1, 1 - slot)

            start_output(g, slot)

        with_block_rows(g, expert_rows)
        return carry

    @pl.when(enb_ref[e] > 0)
    def _():
        lax.fori_loop(eblk0_ref[e], eblk0_ref[e] + enb_ref[e], block, 0)

    @pl.when(e == pl.num_programs(0) - 1)
    def _():
        wait_output(nblk - 1, (nblk - 1) & 1)
        ybuf[0] = jnp.zeros_like(ybuf[0])
        pad_rows = MOE_BLOCK * ky
        pad = pltpu.make_async_copy(ybuf.at[0], y_hbm.at[pl.ds(y_hbm.shape[0] - pad_rows, pad_rows)], ssem.at[0])
        pad.start()
        pad.wait()


def _moe(h2t, w1, w3, w2, eidx, eblk0, enb, bs0, bn, nblk, order):
    ne, d, ff = w1.shape
    kx = ky = _token_pitch(d)
    wmap = lambda e, eidx, *_: (eidx[e], 0, 0)
    grid_spec = pltpu.PrefetchScalarGridSpec(
        num_scalar_prefetch=7,
        grid=(ne,),
        in_specs=[
            pl.BlockSpec(memory_space=pl.ANY),
            pl.BlockSpec((None, d, ff), wmap),
            pl.BlockSpec((None, d, ff), wmap),
            pl.BlockSpec((None, ff, d), wmap),
        ],
        out_specs=pl.BlockSpec(memory_space=pl.ANY),
        scratch_shapes=[
            pltpu.VMEM((GATHER_AHEAD + 1, MOE_BLOCK * kx, LANES), F32),
            pltpu.VMEM((2, MOE_BLOCK * ky, LANES), F32),
            pltpu.SemaphoreType.DMA((GATHER_AHEAD + 1,)),
            pltpu.SemaphoreType.DMA((2,)),
        ],
    )
    return pl.pallas_call(
        _moe_kernel,
        grid_spec=grid_spec,
        out_shape=jax.ShapeDtypeStruct(((order.shape[0] + MOE_BLOCK) * ky, LANES), F32),
        compiler_params=_cparams("arbitrary"),
        name="moe",
    )(eidx, eblk0, enb, bs0, bn, nblk, order, h2t, w1, w3, w2)


def _combine_kernel(pos_ref, x1_ref, ys_hbm, wts_ref, gt_ref, o_ref, ybuf, sem):
    tm = x1_ref.shape[1]
    ky = ybuf.shape[2] // tm
    nt = pl.num_programs(1)
    n = pl.program_id(0) * nt + pl.program_id(1)
    slot = n & 1

    data_rows = x1_ref.shape[2] // LANES

    def row_copy(sl, k, pos, j, tokens=1):
        return pltpu.make_async_copy(ys_hbm.at[pl.ds(pos * ky, tokens * data_rows)],
                                     ybuf.at[sl, k, pl.ds(j * ky, tokens * data_rows)], sem.at[sl])

    def start_tile(tile, sl):
        base = tile * (tm * TOP_K)

        def group(g, carry):
            for u in range(ROW_GROUP):
                row_copy(sl, u % TOP_K, pos_ref[base + g * ROW_GROUP + u],
                         g * (ROW_GROUP // TOP_K) + u // TOP_K).start()
            return carry

        lax.fori_loop(0, tm * TOP_K // ROW_GROUP, group, 0)

    @pl.when(n == 0)
    def _():
        start_tile(0, 0)

    @pl.when(n + 1 < pl.num_programs(0) * nt)
    def _():
        start_tile(n + 1, 1 - slot)

    for k in range(TOP_K):
        row_copy(slot, k, 0, 0, tokens=tm).wait()
    w = wts_ref[0]
    d = x1_ref.shape[2]
    ff = (w[:, 0:1] * _load_token_tiles(ybuf.at[slot, 0], tm, d)
          + w[:, 1:2] * _load_token_tiles(ybuf.at[slot, 1], tm, d))
    o_ref[0] = x1_ref[0] + gt_ref[0] * ff


def _combine(x1, ys, pos, wts, gt2, tm):
    b, t, d = x1.shape
    ky = _token_pitch(d)
    row = lambda bi, i, pos: (bi, i, 0)
    vec = lambda bi, i, pos: (bi, 0, 0)
    grid_spec = pltpu.PrefetchScalarGridSpec(
        num_scalar_prefetch=1,
        grid=(b, t // tm),
        in_specs=[
            pl.BlockSpec((1, tm, d), row),
            pl.BlockSpec(memory_space=pl.ANY),
            pl.BlockSpec((1, tm, LANES), row),
            pl.BlockSpec((1, 1, d), vec),
        ],
        out_specs=pl.BlockSpec((1, tm, d), row),
        scratch_shapes=[
            pltpu.VMEM((2, TOP_K, tm * ky, LANES), F32),
            pltpu.SemaphoreType.DMA((2,)),
        ],
    )
    return pl.pallas_call(
        _combine_kernel,
        grid_spec=grid_spec,
        out_shape=jax.ShapeDtypeStruct((b, t, d), F32),
        compiler_params=_cparams("arbitrary", "arbitrary"),
        name="combine",
    )(pos, x1, ys, wts, gt2)


def _rope_tables(t):
    rows = t // GRID_W
    n_freq = HEAD_DIM // 4
    inv = ROPE_THETA ** (-jnp.arange(n_freq, dtype=F32) / n_freq)
    ar = jnp.arange(rows, dtype=F32)[:, None] * inv
    ac = jnp.arange(GRID_W, dtype=F32)[:, None] * inv

    def expand(fr, fc, sign):
        by_row = jnp.broadcast_to(fr[:, None, :], (rows, GRID_W, n_freq))
        by_col = jnp.broadcast_to(fc[None, :, :], (rows, GRID_W, n_freq))
        return jnp.concatenate([sign * by_row, by_row, sign * by_col, by_col], axis=2).reshape(t, HEAD_DIM)

    return expand(jnp.cos(ar), jnp.cos(ac), 1.0), expand(jnp.sin(ar), jnp.sin(ac), -1.0)


def _gate_weights(w2, bias):
    r = GLA_GATE_RANK
    wh = w2.reshape(2, r, GLA_HEADS, GLA_DK).transpose(2, 0, 1, 3)
    w = jnp.zeros((GLA_HEADS, 2 * r, 2 * GLA_DK), F32)
    w = w.at[:, 0:r, 0:GLA_DK].set(wh[:, 0]).at[:, r:2 * r, GLA_DK:].set(wh[:, 1])
    hi = w.astype(BF16)
    mid = (w - hi.astype(F32)).astype(BF16)
    pad = jnp.zeros((GLA_HEADS, LANES - 6 * r, 2 * GLA_DK), BF16)
    w2s = jnp.concatenate([hi, hi, mid, pad], axis=1)
    bias2 = bias.reshape(2, GLA_HEADS, GLA_DK).transpose(1, 0, 2).reshape(GLA_HEADS, 1, 2 * GLA_DK)
    return w2s, bias2


def _block_plan(eid_flat, nb):
    order = jnp.argsort(eid_flat).astype(jnp.int32)
    eids = jnp.arange(N_EXPERTS, dtype=jnp.int32)
    of_assign = (eid_flat[:, None] == eids[None, :]).astype(jnp.int32)
    counts = jnp.sum(of_assign, axis=0)
    starts = jnp.cumsum(counts) - counts
    nblk_e = (counts + MOE_BLOCK - 1) // MOE_BLOCK
    bends = jnp.cumsum(nblk_e)
    bstarts = bends - nblk_e
    nblk = bends[-1]
    prev_used = lax.cummax(jnp.where(counts > 0, eids, -1))
    eidx = jnp.where(prev_used >= 0, prev_used, jnp.argmax(counts > 0)).astype(jnp.int32)
    bi = jnp.arange(nb, dtype=jnp.int32)
    bic = jnp.minimum(bi, jnp.maximum(nblk - 1, 0))
    bexp = jnp.minimum(jnp.sum(bends[None, :] <= bic[:, None], axis=1), N_EXPERTS - 1)
    of_block = (bexp[:, None] == eids[None, :]).astype(jnp.int32)

    def lookup(per_expert):
        return jnp.sum(of_block * per_expert[None, :], axis=1)

    r0 = (bic - lookup(bstarts)) * MOE_BLOCK
    bs0 = (lookup(starts) + r0).astype(jnp.int32)
    bn = jnp.where(bi < nblk, jnp.minimum(lookup(counts) - r0, MOE_BLOCK), 0).astype(jnp.int32)
    pos = jnp.argsort(order).astype(jnp.int32)
    plan = (eidx, bstarts.astype(jnp.int32), nblk_e.astype(jnp.int32), bs0, bn,
            nblk.reshape(1).astype(jnp.int32), order)
    return plan, pos


def kernel(x, c, ctx, c_ctx, w_ada, b_ada, norm1_g, w_in, q_norm_g, k_norm_g, gla_gate_w2, gla_gate_b, gla_norm_g, w_out, norm2_g, router_grp_w, router_grp_b, router_exp_w, router_exp_b, moe_w1, moe_w3, moe_w2):
    b, t, d = x.shape
    tc = ctx.shape[1]
    depth = w_ada.shape[0]
    assert depth == 1, "single-layer stack: the context stream only feeds keys/values and GLA states"
    layer = 0

    c8 = jnp.zeros((8, d), F32).at[0:b].set(c).at[b].set(c_ctx)
    mod = _ada(c8, w_ada[layer], b_ada[layer])
    sh1, sc1, gt1, sh2, sc2, gt2 = [mod[0:b, i * d:(i + 1) * d].reshape(b, 1, d) for i in range(6)]
    sh1c, sc1c = [jnp.broadcast_to(mod[b, i * d:(i + 1) * d].reshape(1, 1, d), (b, 1, d)) for i in range(2)]

    w_main, w_tail = _w_in_parts(jnp.swapaxes(w_in[layer], 0, 1))

    cos_t, sin_t = _rope_tables(t)
    g1 = norm1_g[layer].reshape(1, d)
    qg = q_norm_g[layer].reshape(1, HEAD_DIM)
    kg = k_norm_g[layer].reshape(1, HEAD_DIM)
    q, k, v, gq, gk, gv, og, lr = _in_proj(x, g1, sc1, sh1, w_main, w_tail, qg, kg, cos_t, sin_t, 512)
    ones_t = jnp.ones((tc, HEAD_DIM), F32)
    _, kc, vc, gqc, gkc, gvc, _, lrc = _in_proj(ctx, g1, sc1c, sh1c, w_main, w_tail, qg, kg,
                                                ones_t, jnp.zeros_like(ones_t), tc, context=True)

    attn = _attn(q, jnp.concatenate([kc, k], axis=1), jnp.concatenate([vc, v], axis=1), 256)
    w2s, bias2 = _gate_weights(gla_gate_w2[layer], gla_gate_b[layer])
    gla = _gla(gq, gk, gv, lr, og, gqc, gkc, gvc, lrc, w2s, bias2, gla_norm_g[layer].reshape(1, GLA_DV))

    wr = jnp.concatenate([router_grp_w[layer], router_exp_w[layer],
                          jnp.zeros((d, LANES - N_GROUPS - N_EXPERTS), F32)], axis=1)
    br = jnp.concatenate([router_grp_b[layer], router_exp_b[layer],
                          jnp.zeros((LANES - N_GROUPS - N_EXPERTS,), F32)]).reshape(1, LANES)
    wr_hi = wr.astype(BF16)
    wr_parts = jnp.concatenate([wr_hi, (wr - wr_hi.astype(F32)).astype(BF16)], axis=1)
    x1, h2t, ids, wts = _out_proj(attn, gla, w_out[layer], x, gt1, norm2_g[layer].reshape(1, d),
                                  sc2, sh2, wr_parts, br, 512)

    m = b * t
    n_assign = m * TOP_K
    nb = -(-(n_assign + N_EXPERTS * (MOE_BLOCK - 1)) // MOE_BLOCK)
    eid_flat = ids[:, :, 0:TOP_K].reshape(n_assign)
    plan, pos = _block_plan(eid_flat, nb)
    ys = _moe(h2t.reshape(-1, LANES), moe_w1[layer], moe_w3[layer], moe_w2[layer], *plan)
    return _combine(x1, ys, pos, wts, gt2, 256)
```

```python
import functools

import jax
import jax.numpy as jnp
from jax import lax
from jax.experimental import pallas as pl
from jax.experimental.pallas import tpu as pltpu

EPS = 1e-6
GRID_W = 64
ROPE_THETA = 10000.0

ATTN_HEADS = 8
ATTN_KV_HEADS = 2
HEAD_DIM = 128
GQA_GROUP = ATTN_HEADS // ATTN_KV_HEADS
ATTN_SCORES_AHEAD = 1

GLA_HEADS = 4
GLA_DK = 128
GLA_DV = 256
GLA_GATE_RANK = 16
GLA_GATE_NORMALIZER = 16.0
GLA_CHUNK = 64
GLA_CHUNK_LOG2 = 6
GLA_PREP_TILE = 256
GLA_OUT_GROUP = 16

N_GROUPS = 8
EXPERTS_PER_GROUP = 8
N_EXPERTS = N_GROUPS * EXPERTS_PER_GROUP
TOP_K = 2
MOE_BLOCK = 256
ROW_GROUP_LOG2 = 3
ROW_GROUP = 1 << ROW_GROUP_LOG2
GATHER_AHEAD = 2

LOG2_E = 1.4426950408889634
LANES = 128
VMEM_LIMIT = 56 * 1024 * 1024

ATTN_Q_W = ATTN_HEADS * HEAD_DIM
ATTN_KV_W = ATTN_KV_HEADS * HEAD_DIM
GLA_K_W = GLA_HEADS * GLA_DK
GLA_V_W = GLA_HEADS * GLA_DV

BF16 = jnp.bfloat16
F32 = jnp.float32


def _cparams(*sem):
    return pltpu.CompilerParams(dimension_semantics=sem, vmem_limit_bytes=VMEM_LIMIT)


def _ada_kernel(c_ref, w_ref, b_ref, o_ref):
    c = c_ref[...]
    s = c * jax.nn.sigmoid(c)
    o_ref[...] = jnp.dot(s.astype(BF16), w_ref[...].astype(BF16), preferred_element_type=F32) + b_ref[...]


def _ada(c8, w, b):
    d, n = w.shape
    tn = 1024
    return pl.pallas_call(
        _ada_kernel,
        grid=(n // tn,),
        in_specs=[
            pl.BlockSpec((8, d), lambda j: (0, 0)),
            pl.BlockSpec((d, tn), lambda j: (0, j)),
            pl.BlockSpec((1, tn), lambda j: (0, j)),
        ],
        out_specs=pl.BlockSpec((8, tn), lambda j: (0, j)),
        out_shape=jax.ShapeDtypeStruct((8, n), F32),
        compiler_params=_cparams("arbitrary"),
        name="ada",
    )(c8, w, b.reshape(1, n))


def _w_main_kernel(wt_ref, o_ref):
    o_ref[...] = wt_ref[...].T.astype(BF16)


def _w_tail_kernel(wt_hbm, o_ref, tail, sem):
    r = 2 * GLA_GATE_RANK
    n_tail = tail.shape[0]
    copy = pltpu.make_async_copy(wt_hbm.at[pl.ds(wt_hbm.shape[0] - n_tail, n_tail)], tail, sem)
    copy.start()
    copy.wait()
    o_ref[:, :GLA_V_W] = tail[r:, :].T.astype(BF16)
    lowrank = jnp.concatenate([tail[:r, :], jnp.zeros((LANES - r, tail.shape[1]), F32)], axis=0)
    o_ref[:, GLA_V_W:] = lowrank.T.astype(BF16)


def _w_in_parts(wt):
    n, d = wt.shape
    n_main = n - GLA_V_W - 2 * GLA_GATE_RANK
    tn = 512
    main = pl.pallas_call(
        _w_main_kernel,
        grid=(n_main // tn,),
        in_specs=[pl.BlockSpec((tn, d), lambda j: (j, 0))],
        out_specs=pl.BlockSpec((d, tn), lambda j: (0, j)),
        out_shape=jax.ShapeDtypeStruct((d, n_main), BF16),
        compiler_params=_cparams("arbitrary"),
        name="w_in_main",
    )(wt)
    tail = pl.pallas_call(
        _w_tail_kernel,
        in_specs=[pl.BlockSpec(memory_space=pl.ANY)],
        out_shape=jax.ShapeDtypeStruct((d, GLA_V_W + LANES), BF16),
        scratch_shapes=[pltpu.VMEM((n - n_main, d), F32), pltpu.SemaphoreType.DMA],
        compiler_params=pltpu.CompilerParams(vmem_limit_bytes=VMEM_LIMIT),
        name="w_in_tail",
    )(wt)
    return main, tail


def _swap32(y):
    lane = lax.broadcasted_iota(jnp.int32, y.shape, 1)
    return jnp.where((lane & 63) < 32, pltpu.roll(y, 96, 1), pltpu.roll(y, 32, 1))


def _head_norm_rope(a, g, cos, sin):
    y = a * lax.rsqrt(jnp.mean(a * a, axis=-1, keepdims=True) + EPS) * g
    return y * cos + _swap32(y) * sin


def _in_proj_kernel(x_ref, g1_ref, sc_ref, sh_ref, w_ref, wt_ref, qg_ref, kg_ref, cos_ref, sin_ref,
                    q_ref, k_ref, v_ref, gq_ref, gk_ref, gv_ref, og_ref, lr_ref, *, context):
    x = x_ref[0]
    xn = x * lax.rsqrt(jnp.mean(x * x, axis=-1, keepdims=True) + EPS)
    h = (xn * g1_ref[...] * (1.0 + sc_ref[0]) + sh_ref[0]).astype(BF16)
    cos = cos_ref[...]
    sin = sin_ref[...]

    def proj(off, width, w=w_ref):
        return jnp.dot(h, w[:, off:off + width], preferred_element_type=F32)

    qg = qg_ref[...] * (HEAD_DIM ** -0.5 * LOG2_E)
    off = 0
    for out_ref, gain, heads in ((q_ref, qg, ATTN_HEADS), (k_ref, kg_ref[...], ATTN_KV_HEADS)):
        if context and out_ref is q_ref:
            q_ref[...] = jnp.zeros_like(q_ref)
            off += heads * HEAD_DIM
            continue
        for pair in range(heads // 2):
            a2 = proj(off, 2 * HEAD_DIM)
            for u in range(2):
                lo = (2 * pair + u) * HEAD_DIM
                a = a2[:, u * HEAD_DIM:(u + 1) * HEAD_DIM]
                out_ref[0, :, lo:lo + HEAD_DIM] = _head_norm_rope(a, gain, cos, sin).astype(BF16)
            off += 2 * HEAD_DIM
    vv = proj(off, ATTN_KV_W).astype(BF16)
    for hk in range(ATTN_KV_HEADS):
        v_ref[0, :, 2 * hk * HEAD_DIM:(2 * hk + 1) * HEAD_DIM] = vv[:, hk * HEAD_DIM:(hk + 1) * HEAD_DIM]
        v_ref[0, :, (2 * hk + 1) * HEAD_DIM:(2 * hk + 2) * HEAD_DIM] = jnp.ones((vv.shape[0], HEAD_DIM), BF16)
    off += ATTN_KV_W
    gq_ref[0] = proj(off, GLA_K_W)
    off += GLA_K_W
    gk_ref[0] = proj(off, GLA_K_W)
    off += GLA_K_W
    gv_ref[0] = proj(off, GLA_V_W).astype(BF16)
    og_ref[0] = jnp.zeros_like(og_ref[0]) if context else proj(0, GLA_V_W, wt_ref)
    lr_ref[0] = proj(GLA_V_W, LANES, wt_ref)


def _in_proj(x, g1, sc, sh, w_main, w_tail, qg, kg, cos_t, sin_t, tm, context=False):
    b, t, d = x.shape
    row = lambda bi, i: (bi, i, 0)
    vec = lambda bi, i: (bi, 0, 0)
    const = lambda bi, i: (0, 0)
    tab = lambda bi, i: (i, 0)
    widths = [(ATTN_Q_W, BF16), (ATTN_KV_W, BF16), (2 * ATTN_KV_W, BF16), (GLA_K_W, F32), (GLA_K_W, F32),
              (GLA_V_W, BF16), (GLA_V_W, F32), (LANES, F32)]
    return pl.pallas_call(
        functools.partial(_in_proj_kernel, context=context),
        grid=(b, t // tm),
        in_specs=[
            pl.BlockSpec((1, tm, d), row),
            pl.BlockSpec((1, d), const),
            pl.BlockSpec((1, 1, d), vec),
            pl.BlockSpec((1, 1, d), vec),
            pl.BlockSpec(w_main.shape, const, pipeline_mode=pl.Buffered(1)),
            pl.BlockSpec(w_tail.shape, const, pipeline_mode=pl.Buffered(1)),
            pl.BlockSpec((1, HEAD_DIM), const),
            pl.BlockSpec((1, HEAD_DIM), const),
            pl.BlockSpec((tm, HEAD_DIM), tab),
            pl.BlockSpec((tm, HEAD_DIM), tab),
        ],
        out_specs=[pl.BlockSpec((1, tm, wd), row) for wd, _ in widths],
        out_shape=[jax.ShapeDtypeStruct((b, t, wd), dt) for wd, dt in widths],
        compiler_params=_cparams("arbitrary", "arbitrary"),
        name="in_proj",
    )(x, g1, sc, sh, w_main, w_tail, qg, kg, cos_t, sin_t)


def _attn_kernel(q_ref, k_ref, v_ref, o_ref):
    def scores(h):
        hk = h // GQA_GROUP
        q = q_ref[0, :, h * HEAD_DIM:(h + 1) * HEAD_DIM]
        k = k_ref[0, :, hk * HEAD_DIM:(hk + 1) * HEAD_DIM]
        return lax.dot_general(q, k, (((1,), (1,)), ((), ())), preferred_element_type=F32)

    ahead = ATTN_SCORES_AHEAD
    pending = [scores(h) for h in range(ahead)]
    for h in range(ATTN_HEADS):
        hk = h // GQA_GROUP
        s = pending.pop(0)
        if h + ahead < ATTN_HEADS:
            pending.append(scores(h + ahead))
        p = jnp.exp2(s - jnp.max(s, axis=-1, keepdims=True))
        ol = jnp.dot(p.astype(BF16), v_ref[0, :, 2 * hk * HEAD_DIM:2 * (hk + 1) * HEAD_DIM],
                     preferred_element_type=F32)
        o_ref[0, :, h * HEAD_DIM:(h + 1) * HEAD_DIM] = (
            ol[:, :HEAD_DIM] / ol[:, HEAD_DIM:HEAD_DIM + 1]).astype(BF16)


def _attn(q, k_all, v_all, tq):
    b, t, _ = q.shape
    tk = k_all.shape[1]
    return pl.pallas_call(
        _attn_kernel,
        grid=(b, t // tq),
        in_specs=[
            pl.BlockSpec((1, tq, ATTN_Q_W), lambda bi, i: (bi, i, 0)),
            pl.BlockSpec((1, tk, ATTN_KV_W), lambda bi, i: (bi, 0, 0)),
            pl.BlockSpec((1, tk, 2 * ATTN_KV_W), lambda bi, i: (bi, 0, 0)),
        ],
        out_specs=pl.BlockSpec((1, tq, ATTN_Q_W), lambda bi, i: (bi, i, 0)),
        out_shape=jax.ShapeDtypeStruct((b, t, ATTN_Q_W), BF16),
        compiler_params=_cparams("arbitrary", "arbitrary"),
        name="attn",
    )(q, k_all, v_all)


def _log_sigmoid(x):
    return jnp.minimum(x, 0.0) - jnp.log(1.0 + jnp.exp(-jnp.abs(x)))


def _split3(x):
    hi = x.astype(BF16)
    r = x - hi.astype(F32)
    mid = r.astype(BF16)
    lo = (r - mid.astype(F32)).astype(BF16)
    return hi, mid, lo


def _dot_exact_lhs(m, x):
    return sum(jnp.dot(m, part, preferred_element_type=F32) for part in _split3(x))


def _prefix_operator(n):
    ri = lax.broadcasted_iota(jnp.int32, (n, n), 0)
    ci = lax.broadcasted_iota(jnp.int32, (n, n), 1)
    same = lax.shift_right_logical(ri, GLA_CHUNK_LOG2) == lax.shift_right_logical(ci, GLA_CHUNK_LOG2)
    return (same & (ci <= ri)).astype(BF16)


def _gate_prep(q, k, lr, w2s, bias2, prefix):
    c = GLA_CHUNK
    hi = lr.astype(BF16).astype(F32)
    mid = (lr - hi).astype(BF16).astype(F32)
    lhs = (hi + pltpu.roll(mid, 2 * GLA_GATE_RANK, 1) + pltpu.roll(hi, 4 * GLA_GATE_RANK, 1)).astype(BF16)
    logits = jnp.dot(lhs, w2s, preferred_element_type=F32) + bias2
    g = _log_sigmoid(logits) * (1.0 / GLA_GATE_NORMALIZER)
    pre = _dot_exact_lhs(prefix, g)
    tot = jnp.concatenate([jnp.broadcast_to(pre[lo + c - 1:lo + c, :], (c, pre.shape[1]))
                           for lo in range(0, pre.shape[0], c)], axis=0)
    dk = GLA_DK
    bcs = (pre[:, :dk], tot[:, dk:] - pre[:, dk:] + g[:, dk:])
    out = []
    for d, bc in enumerate(bcs):
        b_end = tot[:, d * dk:(d + 1) * dk]
        qe = (q * (GLA_DK ** -0.5) * jnp.exp(bc)).astype(BF16)
        ke = (k * jnp.exp(-bc)).astype(BF16)
        kend = (k * jnp.exp(b_end - bc)).astype(BF16)
        out.append((qe, ke, kend, b_end))
    return out


def _state_step(s_ref, v, kend, dec):
    upd = lax.dot_general(v, kend, (((0,), (0,)), ((), ())), preferred_element_type=F32)
    s_ref[...] = s_ref[...] * dec + upd


def _gla_kernel(q_ref, k_ref, v_ref, lr_ref, og_ref, qc_ref, kc_ref, vc_ref, lrc_ref,
                w2_ref, b_ref, ng_ref, o_ref,
                s_ref, qe_ref, ke_ref, kend_ref, dec_ref, kendc_ref, decc_ref, sbf_ref):
    c = GLA_CHUNK
    t = q_ref.shape[1]
    tc = qc_ref.shape[1]
    nc = t // c
    tile = GLA_PREP_TILE
    cpt = tile // c
    dirs = (0, 1)

    def store_dec(ref, d, base, b_end):
        for ch in range(b_end.shape[0] // c):
            ref[d, pl.ds(base + ch, 1), :] = jnp.exp(b_end[ch * c:ch * c + 1, :])

    prefix = _prefix_operator(tile)

    for i in range(tc // tile):
        rows = slice(i * tile, (i + 1) * tile)
        prep_c = _gate_prep(qc_ref[0, rows, :], kc_ref[0, rows, :], lrc_ref[0, rows, :],
                            w2_ref[...], b_ref[...], prefix)
        for d, (_, _, kend, b_end) in enumerate(prep_c):
            kendc_ref[d, rows, :] = kend
            store_dec(decc_ref, d, i * cpt, b_end)
    s_ref[...] = jnp.zeros_like(s_ref)
    for j in range(tc // c):
        for d in dirs:
            ch = j if d == 0 else tc // c - 1 - j
            _state_step(s_ref.at[d], vc_ref[0, ch * c:(ch + 1) * c, :], kendc_ref[d, ch * c:(ch + 1) * c, :],
                        decc_ref[d, ch:ch + 1, :])

    def prep(i, carry):
        lo = pl.multiple_of(i * tile, tile)
        rows = pl.ds(lo, tile)
        prep_l = _gate_prep(q_ref[0, rows, :], k_ref[0, rows, :], lr_ref[0, rows, :],
                            w2_ref[...], b_ref[...], prefix)
        for d, (qe, ke, kend, b_end) in enumerate(prep_l):
            qe_ref[d, rows, :] = qe
            ke_ref[d, rows, :] = ke
            kend_ref[d, rows, :] = kend
            store_dec(dec_ref, d, i * cpt, b_end)
        return carry

    lax.fori_loop(0, t // tile, prep, 0, unroll=4)

    def scan(j, carry):
        for d in dirs:
            ch = j if d == 0 else nc - 1 - j
            rows = pl.ds(pl.multiple_of(ch * c, c), c)
            sbf_ref[d, ch] = s_ref[d].astype(BF16)
            _state_step(s_ref.at[d], v_ref[0, rows, :], kend_ref[d, rows, :], dec_ref[d, pl.ds(ch, 1), :])
        return carry

    lax.fori_loop(0, nc, scan, 0, unroll=16)

    ri = lax.broadcasted_iota(jnp.int32, (c, c), 0)
    ci = lax.broadcasted_iota(jnp.int32, (c, c), 1)
    masks = (ci <= ri, ci >= ri)

    group = GLA_OUT_GROUP

    def out(i, carry):
        chunks = [i * group + u for u in range(group)]
        rows = [pl.ds(pl.multiple_of(ch * c, c), c) for ch in chunks]
        qes = [[qe_ref[d, r, :] for d in dirs] for r in rows]
        scores = [[lax.dot_general(qes[u][d], ke_ref[d, rows[u], :], (((1,), (1,)), ((), ())),
                                   preferred_element_type=F32) for d in dirs] for u in range(group)]
        inter = [[lax.dot_general(qes[u][d], sbf_ref[d, chunks[u]], (((1,), (1,)), ((), ())),
                                  preferred_element_type=F32) for d in dirs] for u in range(group)]
        for u in range(group):
            v = v_ref[0, rows[u], :]
            o = inter[u][0] + inter[u][1]
            for d in dirs:
                a = jnp.where(masks[d], scores[u][d], 0.0).astype(BF16)
                o = o + jnp.dot(a, v, preferred_element_type=F32)
            on = o * lax.rsqrt(jnp.mean(o * o, axis=-1, keepdims=True) + EPS) * ng_ref[...]
            og = og_ref[0, rows[u], :]
            o_ref[0, rows[u], :] = (on * (og * jax.nn.sigmoid(og))).astype(BF16)
        return carry

    lax.fori_loop(0, nc // group, out, 0)


def _gla(gq, gk, gv, lr, og, gqc, gkc, gvc, lrc, w2, bias, ng):
    b, t, _ = gq.shape
    tc = gqc.shape[1]
    hk = lambda bi, h: (bi, 0, h)
    h0 = lambda bi, h: (bi, 0, 0)
    return pl.pallas_call(
        _gla_kernel,
        grid=(b, GLA_HEADS),
        in_specs=[
            pl.BlockSpec((1, t, GLA_DK), hk),
            pl.BlockSpec((1, t, GLA_DK), hk),
            pl.BlockSpec((1, t, GLA_DV), hk),
            pl.BlockSpec((1, t, LANES), h0),
            pl.BlockSpec((1, t, GLA_DV), hk),
            pl.BlockSpec((1, tc, GLA_DK), hk),
            pl.BlockSpec((1, tc, GLA_DK), hk),
            pl.BlockSpec((1, tc, GLA_DV), hk),
            pl.BlockSpec((1, tc, LANES), h0),
            pl.BlockSpec((None, LANES, 2 * GLA_DK), lambda bi, h: (h, 0, 0)),
            pl.BlockSpec((None, 1, 2 * GLA_DK), lambda bi, h: (h, 0, 0)),
            pl.BlockSpec((1, GLA_DV), lambda bi, h: (0, 0)),
        ],
        out_specs=pl.BlockSpec((1, t, GLA_DV), hk),
        out_shape=jax.ShapeDtypeStruct((b, t, GLA_V_W), BF16),
        scratch_shapes=[
            pltpu.VMEM((2, GLA_DV, GLA_DK), F32),
            pltpu.VMEM((2, t, GLA_DK), BF16),
            pltpu.VMEM((2, t, GLA_DK), BF16),
            pltpu.VMEM((2, t, GLA_DK), BF16),
            pltpu.VMEM((2, t // GLA_CHUNK, GLA_DK), F32),
            pltpu.VMEM((2, tc, GLA_DK), BF16),
            pltpu.VMEM((2, tc // GLA_CHUNK, GLA_DK), F32),
            pltpu.VMEM((2, t // GLA_CHUNK, GLA_DV, GLA_DK), BF16),
        ],
        compiler_params=_cparams("arbitrary", "arbitrary"),
        name="gla",
    )(gq, gk, gv, lr, og, gqc, gkc, gvc, lrc, w2, bias, ng)


def _token_pitch(d):
    rows = d // LANES
    return rows + 4 if rows % 8 == 0 else rows


def _store_token_tiles(ref, x):
    n = x.shape[0]
    k = x.shape[1] // LANES
    pitch = ref.shape[0] // n
    for s in range(pitch):
        slab = x[:, s * LANES:(s + 1) * LANES] if s < k else jnp.zeros((n, LANES), x.dtype)
        ref[pl.ds(s, n, stride=pitch), :] = slab


def _load_token_tiles(ref, n, d):
    pitch = ref.shape[0] // n
    return jnp.concatenate([ref[pl.ds(s, n, stride=pitch), :] for s in range(d // LANES)], axis=1)


def _out_proj_kernel(attn_ref, gla_ref, wo_ref, x_ref, gt_ref, g2_ref, sc_ref, sh_ref, wr_ref, br_ref,
                     x1_ref, h2_ref, ids_ref, wts_ref):
    y = jnp.dot(attn_ref[0], wo_ref[0:ATTN_Q_W, :], preferred_element_type=F32)
    y = y + jnp.dot(gla_ref[0], wo_ref[ATTN_Q_W:, :], preferred_element_type=F32)
    x1 = x_ref[0] + gt_ref[0] * y
    x1_ref[0] = x1
    xn = x1 * lax.rsqrt(jnp.mean(x1 * x1, axis=-1, keepdims=True) + EPS)
    h2 = xn * g2_ref[...] * (1.0 + sc_ref[0]) + sh_ref[0]
    hi = h2.astype(BF16)
    hi_f = hi.astype(F32)

    _store_token_tiles(h2_ref.at[0], h2)

    mid = (h2 - hi_f).astype(BF16)
    both = jnp.dot(hi, wr_ref[...], preferred_element_type=F32)
    logits = (both[:, :LANES] + both[:, LANES:]
              + jnp.dot(mid, wr_ref[:, :LANES], preferred_element_type=F32)) + br_ref[...]
    lane = lax.broadcasted_iota(jnp.int32, logits.shape, 1)
    lane_f = lane.astype(F32)
    neg = jnp.float32(-jnp.inf)

    def first_argmax(vals):
        m = jnp.max(vals, axis=-1, keepdims=True)
        idx = jnp.min(jnp.where(vals == m, lane_f, float(LANES)), axis=-1, keepdims=True)
        return m, idx

    lg = jnp.where(lane < N_GROUPS, logits, neg)
    mg, grp = first_argmax(lg)
    pg_sel = 1.0 / jnp.sum(jnp.exp(lg - mg), axis=-1, keepdims=True)
    lo = N_GROUPS + grp * EXPERTS_PER_GROUP
    in_grp = (lane_f >= lo) & (lane_f < lo + EXPERTS_PER_GROUP)
    le = jnp.where(in_grp, logits, neg)
    v1, i1 = first_argmax(le)
    v2, i2 = first_argmax(jnp.where(lane_f == i1, neg, le))
    e2 = jnp.exp(v2 - v1)
    w1 = pg_sel / (1.0 + e2)
    w2 = pg_sel * e2 / (1.0 + e2)
    ids = jnp.where(lane == 0, i1 - N_GROUPS, jnp.where(lane == 1, i2 - N_GROUPS, 0.0))
    ids_ref[0] = ids.astype(jnp.int32)
    wts_ref[0] = jnp.where(lane == 0, w1, jnp.where(lane == 1, w2, 0.0))


def _out_proj(attn, gla, wo, x, gt1, g2, sc2, sh2, wr, br, tm):
    b, t, d = x.shape
    row = lambda bi, i: (bi, i, 0)
    vec = lambda bi, i: (bi, 0, 0)
    const = lambda bi, i: (0, 0)
    return pl.pallas_call(
        _out_proj_kernel,
        grid=(b, t // tm),
        in_specs=[
            pl.BlockSpec((1, tm, ATTN_Q_W), row),
            pl.BlockSpec((1, tm, GLA_V_W), row),
            pl.BlockSpec(wo.shape, const, pipeline_mode=pl.Buffered(1)),
            pl.BlockSpec((1, tm, d), row),
            pl.BlockSpec((1, 1, d), vec),
            pl.BlockSpec((1, d), const),
            pl.BlockSpec((1, 1, d), vec),
            pl.BlockSpec((1, 1, d), vec),
            pl.BlockSpec((d, 2 * LANES), const),
            pl.BlockSpec((1, LANES), const),
        ],
        out_specs=[
            pl.BlockSpec((1, tm, d), row),
            pl.BlockSpec((1, tm * _token_pitch(d), LANES), row),
            pl.BlockSpec((1, tm, LANES), row),
            pl.BlockSpec((1, tm, LANES), row),
        ],
        out_shape=[
            jax.ShapeDtypeStruct((b, t, d), F32),
            jax.ShapeDtypeStruct((b, t * _token_pitch(d), LANES), F32),
            jax.ShapeDtypeStruct((b, t, LANES), jnp.int32),
            jax.ShapeDtypeStruct((b, t, LANES), F32),
        ],
        compiler_params=_cparams("arbitrary", "arbitrary"),
        name="out_proj",
    )(attn, gla, wo, x, gt1, g2, sc2, sh2, wr, br)


def _moe_kernel(eidx_ref, eblk0_ref, enb_ref, bs0_ref, bn_ref, nblk_ref, order_ref,
                h2_hbm, w1_ref, w3_ref, w2_ref, y_hbm,
                xbuf, ybuf, gsem, ssem):
    del eidx_ref
    e = pl.program_id(0)
    nblk = nblk_ref[0]
    kx = xbuf.shape[1] // MOE_BLOCK
    ky = ybuf.shape[1] // MOE_BLOCK

    data_rows = w1_ref.shape[0] // LANES

    def gather_copy(sl, hbm_row, j, tokens=1):
        return pltpu.make_async_copy(h2_hbm.at[pl.ds(hbm_row * kx, tokens * data_rows)],
                                     xbuf.at[sl, pl.ds(j * kx, tokens * data_rows)], gsem.at[sl])

    def for_rows(n, per_group, per_row):
        ng = lax.shift_right_logical(n, ROW_GROUP_LOG2)
        lax.fori_loop(0, ng, lambda g, c: (per_group(g * ROW_GROUP), c)[1], 0)
        lax.fori_loop(ng * ROW_GROUP, n, lambda j, c: (per_row(j), c)[1], 0)

    def start_gather(blk, sl):
        s = bs0_ref[blk]

        def start_row(j):
            gather_copy(sl, lax.shift_right_logical(order_ref[s + j], 1), j).start()

        def group(j0):
            for u in range(ROW_GROUP):
                start_row(j0 + u)

        for_rows(bn_ref[blk], group, start_row)

    def wait_gather(blk, sl):
        for_rows(bn_ref[blk], lambda j0: gather_copy(sl, 0, 0, ROW_GROUP).wait(),
                 lambda j: gather_copy(sl, 0, 0).wait())

    half = MOE_BLOCK // 2

    def with_block_rows(blk, fn):
        @pl.when(bn_ref[blk] > half)
        def _():
            fn(MOE_BLOCK)

        @pl.when(bn_ref[blk] <= half)
        def _():
            fn(half)

    def with_copy_rows(blk, fn):
        quarter = MOE_BLOCK // 4
        n = bn_ref[blk]
        for c in range(1, 5):
            @pl.when((n > (c - 1) * quarter) & (n <= c * quarter))
            def _():
                fn(c * quarter)

    def output_copy(blk, sl, rows):
        return pltpu.make_async_copy(ybuf.at[sl, pl.ds(0, rows * ky)],
                                     y_hbm.at[pl.ds(bs0_ref[blk] * ky, rows * ky)], ssem.at[sl])

    def start_output(blk, sl):
        with_copy_rows(blk, lambda rows: output_copy(blk, sl, rows).start())

    def wait_output(blk, sl):
        with_copy_rows(blk, lambda rows: output_copy(blk, sl, rows).wait())

    @pl.when(e == 0)
    def _():
        xbuf[...] = jnp.zeros_like(xbuf)
        ybuf[...] = jnp.zeros_like(ybuf)
        for g0 in range(GATHER_AHEAD):
            @pl.when(g0 < nblk)
            def _():
                start_gather(g0, g0)

    def block(g, carry):
        slot = g & 1
        xslot = lax.rem(g, GATHER_AHEAD + 1)

        @pl.when(g + GATHER_AHEAD < nblk)
        def _():
            start_gather(g + GATHER_AHEAD, lax.rem(g + GATHER_AHEAD, GATHER_AHEAD + 1))

        wait_gather(g, xslot)

        def expert_rows(rows):
            x = _load_token_tiles(xbuf.at[xslot, pl.ds(0, rows * kx)], rows, w1_ref.shape[0])
            a = jnp.dot(x, w1_ref[...], preferred_element_type=F32)
            gate = jnp.dot(x, w3_ref[...], preferred_element_type=F32)
            hid = a * jax.nn.sigmoid(a) * gate
            y = jnp.dot(hid, w2_ref[...], preferred_element_type=F32)
            _store_token_tiles(ybuf.at[slot, pl.ds(0, rows * ky)], y)

            @pl.when(g >= 1)
            def _():
                wait_output(g - 1, 1 - slot)

            start_output(g, slot)

        with_block_rows(g, expert_rows)
        return carry

    @pl.when(enb_ref[e] > 0)
    def _():
        lax.fori_loop(eblk0_ref[e], eblk0_ref[e] + enb_ref[e], block, 0)

    @pl.when(e == pl.num_programs(0) - 1)
    def _():
        wait_output(nblk - 1, (nblk - 1) & 1)
        ybuf[0] = jnp.zeros_like(ybuf[0])
        pad_rows = MOE_BLOCK * ky
        pad = pltpu.make_async_copy(ybuf.at[0], y_hbm.at[pl.ds(y_hbm.shape[0] - pad_rows, pad_rows)], ssem.at[0])
        pad.start()
        pad.wait()


def _moe(h2t, w1, w3, w2, eidx, eblk0, enb, bs0, bn, nblk, order):
    ne, d, ff = w1.shape
    kx = ky = _token_pitch(d)
    wmap = lambda e, eidx, *_: (eidx[e], 0, 0)
    grid_spec = pltpu.PrefetchScalarGridSpec(
        num_scalar_prefetch=7,
        grid=(ne,),
        in_specs=[
            pl.BlockSpec(memory_space=pl.ANY),
            pl.BlockSpec((None, d, ff), wmap),
            pl.BlockSpec((None, d, ff), wmap),
            pl.BlockSpec((None, ff, d), wmap),
        ],
        out_specs=pl.BlockSpec(memory_space=pl.ANY),
        scratch_shapes=[
            pltpu.VMEM((GATHER_AHEAD + 1, MOE_BLOCK * kx, LANES), F32),
            pltpu.VMEM((2, MOE_BLOCK * ky, LANES), F32),
            pltpu.SemaphoreType.DMA((GATHER_AHEAD + 1,)),
            pltpu.SemaphoreType.DMA((2,)),
        ],
    )
    return pl.pallas_call(
        _moe_kernel,
        grid_spec=grid_spec,
        out_shape=jax.ShapeDtypeStruct(((order.shape[0] + MOE_BLOCK) * ky, LANES), F32),
        compiler_params=_cparams("arbitrary"),
        name="moe",
    )(eidx, eblk0, enb, bs0, bn, nblk, order, h2t, w1, w3, w2)


def _combine_kernel(pos_ref, x1_ref, ys_hbm, wts_ref, gt_ref, o_ref, ybuf, sem):
    tm = x1_ref.shape[1]
    ky = ybuf.shape[2] // tm
    nt = pl.num_programs(1)
    n = pl.program_id(0) * nt + pl.program_id(1)
    slot = n & 1

    data_rows = x1_ref.shape[2] // LANES

    def row_copy(sl, k, pos, j, tokens=1):
        return pltpu.make_async_copy(ys_hbm.at[pl.ds(pos * ky, tokens * data_rows)],
                                     ybuf.at[sl, k, pl.ds(j * ky, tokens * data_rows)], sem.at[sl])

    def start_tile(tile, sl):
        base = tile * (tm * TOP_K)

        def group(g, carry):
            for u in range(ROW_GROUP):
                row_copy(sl, u % TOP_K, pos_ref[base + g * ROW_GROUP + u],
                         g * (ROW_GROUP // TOP_K) + u // TOP_K).start()
            return carry

        lax.fori_loop(0, tm * TOP_K // ROW_GROUP, group, 0)

    @pl.when(n == 0)
    def _():
        start_tile(0, 0)

    @pl.when(n + 1 < pl.num_programs(0) * nt)
    def _():
        start_tile(n + 1, 1 - slot)

    for k in range(TOP_K):
        row_copy(slot, k, 0, 0, tokens=tm).wait()
    w = wts_ref[0]
    d = x1_ref.shape[2]
    ff = (w[:, 0:1] * _load_token_tiles(ybuf.at[slot, 0], tm, d)
          + w[:, 1:2] * _load_token_tiles(ybuf.at[slot, 1], tm, d))
    o_ref[0] = x1_ref[0] + gt_ref[0] * ff


def _combine(x1, ys, pos, wts, gt2, tm):
    b, t, d = x1.shape
    ky = _token_pitch(d)
    row = lambda bi, i, pos: (bi, i, 0)
    vec = lambda bi, i, pos: (bi, 0, 0)
    grid_spec = pltpu.PrefetchScalarGridSpec(
        num_scalar_prefetch=1,
        grid=(b, t // tm),
        in_specs=[
            pl.BlockSpec((1, tm, d), row),
            pl.BlockSpec(memory_space=pl.ANY),
            pl.BlockSpec((1, tm, LANES), row),
            pl.BlockSpec((1, 1, d), vec),
        ],
        out_specs=pl.BlockSpec((1, tm, d), row),
        scratch_shapes=[
            pltpu.VMEM((2, TOP_K, tm * ky, LANES), F32),
            pltpu.SemaphoreType.DMA((2,)),
        ],
    )
    return pl.pallas_call(
        _combine_kernel,
        grid_spec=grid_spec,
        out_shape=jax.ShapeDtypeStruct((b, t, d), F32),
        compiler_params=_cparams("arbitrary", "arbitrary"),
        name="combine",
    )(pos, x1, ys, wts, gt2)


def _rope_tables(t):
    rows = t // GRID_W
    n_freq = HEAD_DIM // 4
    inv = ROPE_THETA ** (-jnp.arange(n_freq, dtype=F32) / n_freq)
    ar = jnp.arange(rows, dtype=F32)[:, None] * inv
    ac = jnp.arange(GRID_W, dtype=F32)[:, None] * inv

    def expand(fr, fc, sign):
        by_row = jnp.broadcast_to(fr[:, None, :], (rows, GRID_W, n_freq))
        by_col = jnp.broadcast_to(fc[None, :, :], (rows, GRID_W, n_freq))
        return jnp.concatenate([sign * by_row, by_row, sign * by_col, by_col], axis=2).reshape(t, HEAD_DIM)

    return expand(jnp.cos(ar), jnp.cos(ac), 1.0), expand(jnp.sin(ar), jnp.sin(ac), -1.0)


def _gate_weights(w2, bias):
    r = GLA_GATE_RANK
    wh = w2.reshape(2, r, GLA_HEADS, GLA_DK).transpose(2, 0, 1, 3)
    w = jnp.zeros((GLA_HEADS, 2 * r, 2 * GLA_DK), F32)
    w = w.at[:, 0:r, 0:GLA_DK].set(wh[:, 0]).at[:, r:2 * r, GLA_DK:].set(wh[:, 1])
    hi = w.astype(BF16)
    mid = (w - hi.astype(F32)).astype(BF16)
    pad = jnp.zeros((GLA_HEADS, LANES - 6 * r, 2 * GLA_DK), BF16)
    w2s = jnp.concatenate([hi, hi, mid, pad], axis=1)
    bias2 = bias.reshape(2, GLA_HEADS, GLA_DK).transpose(1, 0, 2).reshape(GLA_HEADS, 1, 2 * GLA_DK)
    return w2s, bias2


def _block_plan(eid_flat, nb):
    order = jnp.argsort(eid_flat).astype(jnp.int32)
    eids = jnp.arange(N_EXPERTS, dtype=jnp.int32)
    of_assign = (eid_flat[:, None] == eids[None, :]).astype(jnp.int32)
    counts = jnp.sum(of_assign, axis=0)
    starts = jnp.cumsum(counts) - counts
    nblk_e = (counts + MOE_BLOCK - 1) // MOE_BLOCK
    bends = jnp.cumsum(nblk_e)
    bstarts = bends - nblk_e
    nblk = bends[-1]
    prev_used = lax.cummax(jnp.where(counts > 0, eids, -1))
    eidx = jnp.where(prev_used >= 0, prev_used, jnp.argmax(counts > 0)).astype(jnp.int32)
    bi = jnp.arange(nb, dtype=jnp.int32)
    bic = jnp.minimum(bi, jnp.maximum(nblk - 1, 0))
    bexp = jnp.minimum(jnp.sum(bends[None, :] <= bic[:, None], axis=1), N_EXPERTS - 1)
    of_block = (bexp[:, None] == eids[None, :]).astype(jnp.int32)

    def lookup(per_expert):
        return jnp.sum(of_block * per_expert[None, :], axis=1)

    r0 = (bic - lookup(bstarts)) * MOE_BLOCK
    bs0 = (lookup(starts) + r0).astype(jnp.int32)
    bn = jnp.where(bi < nblk, jnp.minimum(lookup(counts) - r0, MOE_BLOCK), 0).astype(jnp.int32)
    pos = jnp.argsort(order).astype(jnp.int32)
    plan = (eidx, bstarts.astype(jnp.int32), nblk_e.astype(jnp.int32), bs0, bn,
            nblk.reshape(1).astype(jnp.int32), order)
    return plan, pos


def kernel(x, c, ctx, c_ctx, w_ada, b_ada, norm1_g, w_in, q_norm_g, k_norm_g, gla_gate_w2, gla_gate_b, gla_norm_g, w_out, norm2_g, router_grp_w, router_grp_b, router_exp_w, router_exp_b, moe_w1, moe_w3, moe_w2):
    b, t, d = x.shape
    tc = ctx.shape[1]
    depth = w_ada.shape[0]
    assert depth == 1, "single-layer stack: the context stream only feeds keys/values and GLA states"
    layer = 0

    c8 = jnp.zeros((8, d), F32).at[0:b].set(c).at[b].set(c_ctx)
    mod = _ada(c8, w_ada[layer], b_ada[layer])
    sh1, sc1, gt1, sh2, sc2, gt2 = [mod[0:b, i * d:(i + 1) * d].reshape(b, 1, d) for i in range(6)]
    sh1c, sc1c = [jnp.broadcast_to(mod[b, i * d:(i + 1) * d].reshape(1, 1, d), (b, 1, d)) for i in range(2)]

    w_main, w_tail = _w_in_parts(jnp.swapaxes(w_in[layer], 0, 1))

    cos_t, sin_t = _rope_tables(t)
    g1 = norm1_g[layer].reshape(1, d)
    qg = q_norm_g[layer].reshape(1, HEAD_DIM)
    kg = k_norm_g[layer].reshape(1, HEAD_DIM)
    q, k, v, gq, gk, gv, og, lr = _in_proj(x, g1, sc1, sh1, w_main, w_tail, qg, kg, cos_t, sin_t, 512)
    ones_t = jnp.ones((tc, HEAD_DIM), F32)
    _, kc, vc, gqc, gkc, gvc, _, lrc = _in_proj(ctx, g1, sc1c, sh1c, w_main, w_tail, qg, kg,
                                                ones_t, jnp.zeros_like(ones_t), tc, context=True)

    attn = _attn(q, jnp.concatenate([kc, k], axis=1), jnp.concatenate([vc, v], axis=1), 256)
    w2s, bias2 = _gate_weights(gla_gate_w2[layer], gla_gate_b[layer])
    gla = _gla(gq, gk, gv, lr, og, gqc, gkc, gvc, lrc, w2s, bias2, gla_norm_g[layer].reshape(1, GLA_DV))

    wr = jnp.concatenate([router_grp_w[layer], router_exp_w[layer],
                          jnp.zeros((d, LANES - N_GROUPS - N_EXPERTS), F32)], axis=1)
    br = jnp.concatenate([router_grp_b[layer], router_exp_b[layer],
                          jnp.zeros((LANES - N_GROUPS - N_EXPERTS,), F32)]).reshape(1, LANES)
    wr_hi = wr.astype(BF16)
    wr_parts = jnp.concatenate([wr_hi, (wr - wr_hi.astype(F32)).astype(BF16)], axis=1)
    x1, h2t, ids, wts = _out_proj(attn, gla, w_out[layer], x, gt1, norm2_g[layer].reshape(1, d),
                                  sc2, sh2, wr_parts, br, 512)

    m = b * t
    n_assign = m * TOP_K
    nb = -(-(n_assign + N_EXPERTS * (MOE_BLOCK - 1)) // MOE_BLOCK)
    eid_flat = ids[:, :, 0:TOP_K].reshape(n_assign)
    plan, pos = _block_plan(eid_flat, nb)
    ys = _moe(h2t.reshape(-1, LANES), moe_w1[layer], moe_w3[layer], moe_w2[layer], *plan)
    return _combine(x1, ys, pos, wts, gt2, 512)
```

```python
import functools

import jax
import jax.numpy as jnp
from jax import lax
from jax.experimental import pallas as pl
from jax.experimental.pallas import tpu as pltpu

EPS = 1e-6
GRID_W = 64
ROPE_THETA = 10000.0

ATTN_HEADS = 8
ATTN_KV_HEADS = 2
HEAD_DIM = 128
GQA_GROUP = ATTN_HEADS // ATTN_KV_HEADS
ATTN_SCORES_AHEAD = 1

GLA_HEADS = 4
GLA_DK = 128
GLA_DV = 256
GLA_GATE_RANK = 16
GLA_GATE_NORMALIZER = 16.0
GLA_CHUNK = 64
GLA_CHUNK_LOG2 = 6
GLA_PREP_TILE = 256
GLA_OUT_GROUP = 16

N_GROUPS = 8
EXPERTS_PER_GROUP = 8
N_EXPERTS = N_GROUPS * EXPERTS_PER_GROUP
TOP_K = 2
MOE_BLOCK = 256
ROW_GROUP_LOG2 = 4
ROW_GROUP = 1 << ROW_GROUP_LOG2
GATHER_AHEAD = 2

PROJ_ROWS = 512
ATTN_Q_ROWS = 512
COMBINE_ROWS = 256

LOG2_E = 1.4426950408889634
LANES = 128
VMEM_LIMIT = 56 * 1024 * 1024

ATTN_Q_W = ATTN_HEADS * HEAD_DIM
ATTN_KV_W = ATTN_KV_HEADS * HEAD_DIM
GLA_K_W = GLA_HEADS * GLA_DK
GLA_V_W = GLA_HEADS * GLA_DV

BF16 = jnp.bfloat16
F32 = jnp.float32


def _cparams(*sem):
    return pltpu.CompilerParams(dimension_semantics=sem, vmem_limit_bytes=VMEM_LIMIT)


def _ada_kernel(c_ref, w_ref, b_ref, o_ref):
    c = c_ref[...]
    s = c * jax.nn.sigmoid(c)
    o_ref[...] = jnp.dot(s.astype(BF16), w_ref[...].astype(BF16), preferred_element_type=F32) + b_ref[...]


def _ada(c8, w, b):
    d, n = w.shape
    tn = 1024
    return pl.pallas_call(
        _ada_kernel,
        grid=(n // tn,),
        in_specs=[
            pl.BlockSpec((8, d), lambda j: (0, 0)),
            pl.BlockSpec((d, tn), lambda j: (0, j)),
            pl.BlockSpec((1, tn), lambda j: (0, j)),
        ],
        out_specs=pl.BlockSpec((8, tn), lambda j: (0, j)),
        out_shape=jax.ShapeDtypeStruct((8, n), F32),
        compiler_params=_cparams("arbitrary"),
        name="ada",
    )(c8, w, b.reshape(1, n))


def _w_main_kernel(wt_ref, o_ref):
    o_ref[...] = wt_ref[...].T.astype(BF16)


def _w_tail_kernel(wt_hbm, o_ref, tail, sem):
    r = 2 * GLA_GATE_RANK
    n_tail = tail.shape[0]
    copy = pltpu.make_async_copy(wt_hbm.at[pl.ds(wt_hbm.shape[0] - n_tail, n_tail)], tail, sem)
    copy.start()
    copy.wait()
    o_ref[:, :GLA_V_W] = tail[r:, :].T.astype(BF16)
    lowrank = jnp.concatenate([tail[:r, :], jnp.zeros((LANES - r, tail.shape[1]), F32)], axis=0)
    o_ref[:, GLA_V_W:] = lowrank.T.astype(BF16)


def _w_in_parts(wt):
    n, d = wt.shape
    n_main = n - GLA_V_W - 2 * GLA_GATE_RANK
    tn = 512
    main = pl.pallas_call(
        _w_main_kernel,
        grid=(n_main // tn,),
        in_specs=[pl.BlockSpec((tn, d), lambda j: (j, 0))],
        out_specs=pl.BlockSpec((d, tn), lambda j: (0, j)),
        out_shape=jax.ShapeDtypeStruct((d, n_main), BF16),
        compiler_params=_cparams("arbitrary"),
        name="w_in_main",
    )(wt)
    tail = pl.pallas_call(
        _w_tail_kernel,
        in_specs=[pl.BlockSpec(memory_space=pl.ANY)],
        out_shape=jax.ShapeDtypeStruct((d, GLA_V_W + LANES), BF16),
        scratch_shapes=[pltpu.VMEM((n - n_main, d), F32), pltpu.SemaphoreType.DMA],
        compiler_params=pltpu.CompilerParams(vmem_limit_bytes=VMEM_LIMIT),
        name="w_in_tail",
    )(wt)
    return main, tail


def _swap32(y):
    lane = lax.broadcasted_iota(jnp.int32, y.shape, 1)
    return jnp.where((lane & 63) < 32, pltpu.roll(y, 96, 1), pltpu.roll(y, 32, 1))


def _head_norm_rope(a, g, cos, sin):
    y = a * lax.rsqrt(jnp.mean(a * a, axis=-1, keepdims=True) + EPS) * g
    return y * cos + _swap32(y) * sin


def _in_proj_kernel(x_ref, g1_ref, sc_ref, sh_ref, w_ref, wt_ref, qg_ref, kg_ref, cos_ref, sin_ref,
                    q_ref, k_ref, v_ref, gq_ref, gk_ref, gv_ref, og_ref, lr_ref, *, context):
    x = x_ref[0]
    xn = x * lax.rsqrt(jnp.mean(x * x, axis=-1, keepdims=True) + EPS)
    h = (xn * g1_ref[...] * (1.0 + sc_ref[0]) + sh_ref[0]).astype(BF16)
    cos = cos_ref[...]
    sin = sin_ref[...]

    def proj(off, width, w=w_ref):
        return jnp.dot(h, w[:, off:off + width], preferred_element_type=F32)

    qg = qg_ref[...] * (HEAD_DIM ** -0.5 * LOG2_E)
    off = 0
    for out_ref, gain, heads in ((q_ref, qg, ATTN_HEADS), (k_ref, kg_ref[...], ATTN_KV_HEADS)):
        if context and out_ref is q_ref:
            q_ref[...] = jnp.zeros_like(q_ref)
            off += heads * HEAD_DIM
            continue
        for pair in range(heads // 2):
            a2 = proj(off, 2 * HEAD_DIM)
            for u in range(2):
                lo = (2 * pair + u) * HEAD_DIM
                a = a2[:, u * HEAD_DIM:(u + 1) * HEAD_DIM]
                out_ref[0, :, lo:lo + HEAD_DIM] = _head_norm_rope(a, gain, cos, sin).astype(BF16)
            off += 2 * HEAD_DIM
    vv = proj(off, ATTN_KV_W).astype(BF16)
    for hk in range(ATTN_KV_HEADS):
        v_ref[0, :, 2 * hk * HEAD_DIM:(2 * hk + 1) * HEAD_DIM] = vv[:, hk * HEAD_DIM:(hk + 1) * HEAD_DIM]
        v_ref[0, :, (2 * hk + 1) * HEAD_DIM:(2 * hk + 2) * HEAD_DIM] = jnp.ones((vv.shape[0], HEAD_DIM), BF16)
    off += ATTN_KV_W
    gq_ref[0] = proj(off, GLA_K_W)
    off += GLA_K_W
    gk_ref[0] = proj(off, GLA_K_W)
    off += GLA_K_W
    gv_ref[0] = proj(off, GLA_V_W).astype(BF16)
    og_ref[0] = jnp.zeros_like(og_ref[0]) if context else proj(0, GLA_V_W, wt_ref)
    lr_ref[0] = proj(GLA_V_W, LANES, wt_ref)


def _in_proj(x, g1, sc, sh, w_main, w_tail, qg, kg, cos_t, sin_t, tm, context=False):
    b, t, d = x.shape
    row = lambda bi, i: (bi, i, 0)
    vec = lambda bi, i: (bi, 0, 0)
    const = lambda bi, i: (0, 0)
    tab = lambda bi, i: (i, 0)
    widths = [(ATTN_Q_W, BF16), (ATTN_KV_W, BF16), (2 * ATTN_KV_W, BF16), (GLA_K_W, F32), (GLA_K_W, F32),
              (GLA_V_W, BF16), (GLA_V_W, F32), (LANES, F32)]
    return pl.pallas_call(
        functools.partial(_in_proj_kernel, context=context),
        grid=(b, t // tm),
        in_specs=[
            pl.BlockSpec((1, tm, d), row),
            pl.BlockSpec((1, d), const),
            pl.BlockSpec((1, 1, d), vec),
            pl.BlockSpec((1, 1, d), vec),
            pl.BlockSpec(w_main.shape, const, pipeline_mode=pl.Buffered(1)),
            pl.BlockSpec(w_tail.shape, const, pipeline_mode=pl.Buffered(1)),
            pl.BlockSpec((1, HEAD_DIM), const),
            pl.BlockSpec((1, HEAD_DIM), const),
            pl.BlockSpec((tm, HEAD_DIM), tab),
            pl.BlockSpec((tm, HEAD_DIM), tab),
        ],
        out_specs=[pl.BlockSpec((1, tm, wd), row) for wd, _ in widths],
        out_shape=[jax.ShapeDtypeStruct((b, t, wd), dt) for wd, dt in widths],
        compiler_params=_cparams("arbitrary", "arbitrary"),
        name="in_proj",
    )(x, g1, sc, sh, w_main, w_tail, qg, kg, cos_t, sin_t)


def _attn_kernel(q_ref, k_ref, v_ref, o_ref):
    def scores(h):
        hk = h // GQA_GROUP
        q = q_ref[0, :, h * HEAD_DIM:(h + 1) * HEAD_DIM]
        k = k_ref[0, :, hk * HEAD_DIM:(hk + 1) * HEAD_DIM]
        return lax.dot_general(q, k, (((1,), (1,)), ((), ())), preferred_element_type=F32)

    ahead = ATTN_SCORES_AHEAD
    pending = [scores(h) for h in range(ahead)]
    for h in range(ATTN_HEADS):
        hk = h // GQA_GROUP
        s = pending.pop(0)
        if h + ahead < ATTN_HEADS:
            pending.append(scores(h + ahead))
        p = jnp.exp2(s - jnp.max(s, axis=-1, keepdims=True))
        ol = jnp.dot(p.astype(BF16), v_ref[0, :, 2 * hk * HEAD_DIM:2 * (hk + 1) * HEAD_DIM],
                     preferred_element_type=F32)
        o_ref[0, :, h * HEAD_DIM:(h + 1) * HEAD_DIM] = (
            ol[:, :HEAD_DIM] / ol[:, HEAD_DIM:HEAD_DIM + 1]).astype(BF16)


def _attn(q, k_all, v_all, tq):
    b, t, _ = q.shape
    tk = k_all.shape[1]
    return pl.pallas_call(
        _attn_kernel,
        grid=(b, t // tq),
        in_specs=[
            pl.BlockSpec((1, tq, ATTN_Q_W), lambda bi, i: (bi, i, 0)),
            pl.BlockSpec((1, tk, ATTN_KV_W), lambda bi, i: (bi, 0, 0)),
            pl.BlockSpec((1, tk, 2 * ATTN_KV_W), lambda bi, i: (bi, 0, 0)),
        ],
        out_specs=pl.BlockSpec((1, tq, ATTN_Q_W), lambda bi, i: (bi, i, 0)),
        out_shape=jax.ShapeDtypeStruct((b, t, ATTN_Q_W), BF16),
        compiler_params=_cparams("arbitrary", "arbitrary"),
        name="attn",
    )(q, k_all, v_all)


def _log_sigmoid(x):
    return jnp.minimum(x, 0.0) - jnp.log(1.0 + jnp.exp(-jnp.abs(x)))


def _split3(x):
    hi = x.astype(BF16)
    r = x - hi.astype(F32)
    mid = r.astype(BF16)
    lo = (r - mid.astype(F32)).astype(BF16)
    return hi, mid, lo


def _dot_exact_lhs(m, x):
    return sum(jnp.dot(m, part, preferred_element_type=F32) for part in _split3(x))


def _prefix_operator(n):
    ri = lax.broadcasted_iota(jnp.int32, (n, n), 0)
    ci = lax.broadcasted_iota(jnp.int32, (n, n), 1)
    same = lax.shift_right_logical(ri, GLA_CHUNK_LOG2) == lax.shift_right_logical(ci, GLA_CHUNK_LOG2)
    return (same & (ci <= ri)).astype(BF16)


def _gate_prep(q, k, lr, w2s, bias2, prefix):
    c = GLA_CHUNK
    hi = lr.astype(BF16).astype(F32)
    mid = (lr - hi).astype(BF16).astype(F32)
    lhs = (hi + pltpu.roll(mid, 2 * GLA_GATE_RANK, 1) + pltpu.roll(hi, 4 * GLA_GATE_RANK, 1)).astype(BF16)
    logits = jnp.dot(lhs, w2s, preferred_element_type=F32) + bias2
    g = _log_sigmoid(logits) * (1.0 / GLA_GATE_NORMALIZER)
    pre = _dot_exact_lhs(prefix, g)
    tot = jnp.concatenate([jnp.broadcast_to(pre[lo + c - 1:lo + c, :], (c, pre.shape[1]))
                           for lo in range(0, pre.shape[0], c)], axis=0)
    dk = GLA_DK
    bcs = (pre[:, :dk], tot[:, dk:] - pre[:, dk:] + g[:, dk:])
    out = []
    for d, bc in enumerate(bcs):
        b_end = tot[:, d * dk:(d + 1) * dk]
        qe = (q * (GLA_DK ** -0.5) * jnp.exp(bc)).astype(BF16)
        ke = (k * jnp.exp(-bc)).astype(BF16)
        kend = (k * jnp.exp(b_end - bc)).astype(BF16)
        out.append((qe, ke, kend, b_end))
    return out


def _state_step(s_ref, v, kend, dec):
    upd = lax.dot_general(v, kend, (((0,), (0,)), ((), ())), preferred_element_type=F32)
    s_ref[...] = s_ref[...] * dec + upd


def _gla_kernel(q_ref, k_ref, v_ref, lr_ref, og_ref, qc_ref, kc_ref, vc_ref, lrc_ref,
                w2_ref, b_ref, ng_ref, o_ref,
                s_ref, qe_ref, ke_ref, kend_ref, dec_ref, kendc_ref, decc_ref, sbf_ref):
    c = GLA_CHUNK
    t = q_ref.shape[1]
    tc = qc_ref.shape[1]
    nc = t // c
    tile = GLA_PREP_TILE
    cpt = tile // c
    dirs = (0, 1)

    def store_dec(ref, d, base, b_end):
        for ch in range(b_end.shape[0] // c):
            ref[d, pl.ds(base + ch, 1), :] = jnp.exp(b_end[ch * c:ch * c + 1, :])

    prefix = _prefix_operator(tile)

    for i in range(tc // tile):
        rows = slice(i * tile, (i + 1) * tile)
        prep_c = _gate_prep(qc_ref[0, rows, :], kc_ref[0, rows, :], lrc_ref[0, rows, :],
                            w2_ref[...], b_ref[...], prefix)
        for d, (_, _, kend, b_end) in enumerate(prep_c):
            kendc_ref[d, rows, :] = kend
            store_dec(decc_ref, d, i * cpt, b_end)
    s_ref[...] = jnp.zeros_like(s_ref)
    for j in range(tc // c):
        for d in dirs:
            ch = j if d == 0 else tc // c - 1 - j
            _state_step(s_ref.at[d], vc_ref[0, ch * c:(ch + 1) * c, :], kendc_ref[d, ch * c:(ch + 1) * c, :],
                        decc_ref[d, ch:ch + 1, :])

    def prep(i, carry):
        lo = pl.multiple_of(i * tile, tile)
        rows = pl.ds(lo, tile)
        prep_l = _gate_prep(q_ref[0, rows, :], k_ref[0, rows, :], lr_ref[0, rows, :],
                            w2_ref[...], b_ref[...], prefix)
        for d, (qe, ke, kend, b_end) in enumerate(prep_l):
            qe_ref[d, rows, :] = qe
            ke_ref[d, rows, :] = ke
            kend_ref[d, rows, :] = kend
            store_dec(dec_ref, d, i * cpt, b_end)
        return carry

    lax.fori_loop(0, t // tile, prep, 0, unroll=4)

    def scan(j, carry):
        for d in dirs:
            ch = j if d == 0 else nc - 1 - j
            rows = pl.ds(pl.multiple_of(ch * c, c), c)
            sbf_ref[d, ch] = s_ref[d].astype(BF16)
            _state_step(s_ref.at[d], v_ref[0, rows, :], kend_ref[d, rows, :], dec_ref[d, pl.ds(ch, 1), :])
        return carry

    lax.fori_loop(0, nc, scan, 0, unroll=16)

    ri = lax.broadcasted_iota(jnp.int32, (c, c), 0)
    ci = lax.broadcasted_iota(jnp.int32, (c, c), 1)
    masks = (ci <= ri, ci >= ri)

    group = GLA_OUT_GROUP

    def out(i, carry):
        chunks = [i * group + u for u in range(group)]
        rows = [pl.ds(pl.multiple_of(ch * c, c), c) for ch in chunks]
        qes = [[qe_ref[d, r, :] for d in dirs] for r in rows]
        scores = [[lax.dot_general(qes[u][d], ke_ref[d, rows[u], :], (((1,), (1,)), ((), ())),
                                   preferred_element_type=F32) for d in dirs] for u in range(group)]
        inter = [[lax.dot_general(qes[u][d], sbf_ref[d, chunks[u]], (((1,), (1,)), ((), ())),
                                  preferred_element_type=F32) for d in dirs] for u in range(group)]
        for u in range(group):
            v = v_ref[0, rows[u], :]
            o = inter[u][0] + inter[u][1]
            for d in dirs:
                a = jnp.where(masks[d], scores[u][d], 0.0).astype(BF16)
                o = o + jnp.dot(a, v, preferred_element_type=F32)
            on = o * lax.rsqrt(jnp.mean(o * o, axis=-1, keepdims=True) + EPS) * ng_ref[...]
            og = og_ref[0, rows[u], :]
            o_ref[0, rows[u], :] = (on * (og * jax.nn.sigmoid(og))).astype(BF16)
        return carry

    lax.fori_loop(0, nc // group, out, 0)


def _gla(gq, gk, gv, lr, og, gqc, gkc, gvc, lrc, w2, bias, ng):
    b, t, _ = gq.shape
    tc = gqc.shape[1]
    hk = lambda bi, h: (bi, 0, h)
    h0 = lambda bi, h: (bi, 0, 0)
    return pl.pallas_call(
        _gla_kernel,
        grid=(b, GLA_HEADS),
        in_specs=[
            pl.BlockSpec((1, t, GLA_DK), hk),
            pl.BlockSpec((1, t, GLA_DK), hk),
            pl.BlockSpec((1, t, GLA_DV), hk),
            pl.BlockSpec((1, t, LANES), h0),
            pl.BlockSpec((1, t, GLA_DV), hk),
            pl.BlockSpec((1, tc, GLA_DK), hk),
            pl.BlockSpec((1, tc, GLA_DK), hk),
            pl.BlockSpec((1, tc, GLA_DV), hk),
            pl.BlockSpec((1, tc, LANES), h0),
            pl.BlockSpec((None, LANES, 2 * GLA_DK), lambda bi, h: (h, 0, 0)),
            pl.BlockSpec((None, 1, 2 * GLA_DK), lambda bi, h: (h, 0, 0)),
            pl.BlockSpec((1, GLA_DV), lambda bi, h: (0, 0)),
        ],
        out_specs=pl.BlockSpec((1, t, GLA_DV), hk),
        out_shape=jax.ShapeDtypeStruct((b, t, GLA_V_W), BF16),
        scratch_shapes=[
            pltpu.VMEM((2, GLA_DV, GLA_DK), F32),
            pltpu.VMEM((2, t, GLA_DK), BF16),
            pltpu.VMEM((2, t, GLA_DK), BF16),
            pltpu.VMEM((2, t, GLA_DK), BF16),
            pltpu.VMEM((2, t // GLA_CHUNK, GLA_DK), F32),
            pltpu.VMEM((2, tc, GLA_DK), BF16),
            pltpu.VMEM((2, tc // GLA_CHUNK, GLA_DK), F32),
            pltpu.VMEM((2, t // GLA_CHUNK, GLA_DV, GLA_DK), BF16),
        ],
        compiler_params=_cparams("arbitrary", "arbitrary"),
        name="gla",
    )(gq, gk, gv, lr, og, gqc, gkc, gvc, lrc, w2, bias, ng)


def _token_pitch(d):
    rows = d // LANES
    return rows + 4 if rows % 8 == 0 else rows


def _store_token_tiles(ref, x):
    n = x.shape[0]
    k = x.shape[1] // LANES
    pitch = ref.shape[0] // n
    for s in range(pitch):
        slab = x[:, s * LANES:(s + 1) * LANES] if s < k else jnp.zeros((n, LANES), x.dtype)
        ref[pl.ds(s, n, stride=pitch), :] = slab


def _load_token_tiles(ref, n, d):
    pitch = ref.shape[0] // n
    return jnp.concatenate([ref[pl.ds(s, n, stride=pitch), :] for s in range(d // LANES)], axis=1)


def _out_proj_kernel(attn_ref, gla_ref, wo_ref, x_ref, gt_ref, g2_ref, sc_ref, sh_ref, wr_ref, br_ref,
                     x1_ref, h2_ref, ids_ref, wts_ref):
    y = jnp.dot(attn_ref[0], wo_ref[0:ATTN_Q_W, :], preferred_element_type=F32)
    y = y + jnp.dot(gla_ref[0], wo_ref[ATTN_Q_W:, :], preferred_element_type=F32)
    x1 = x_ref[0] + gt_ref[0] * y
    x1_ref[0] = x1
    xn = x1 * lax.rsqrt(jnp.mean(x1 * x1, axis=-1, keepdims=True) + EPS)
    h2 = xn * g2_ref[...] * (1.0 + sc_ref[0]) + sh_ref[0]
    hi = h2.astype(BF16)
    hi_f = hi.astype(F32)

    _store_token_tiles(h2_ref.at[0], h2)

    mid = (h2 - hi_f).astype(BF16)
    both = jnp.dot(hi, wr_ref[...], preferred_element_type=F32)
    logits = (both[:, :LANES] + both[:, LANES:]
              + jnp.dot(mid, wr_ref[:, :LANES], preferred_element_type=F32)) + br_ref[...]
    lane = lax.broadcasted_iota(jnp.int32, logits.shape, 1)
    lane_f = lane.astype(F32)
    neg = jnp.float32(-jnp.inf)

    def first_argmax(vals):
        m = jnp.max(vals, axis=-1, keepdims=True)
        idx = jnp.min(jnp.where(vals == m, lane_f, float(LANES)), axis=-1, keepdims=True)
        return m, idx

    lg = jnp.where(lane < N_GROUPS, logits, neg)
    mg, grp = first_argmax(lg)
    pg_sel = 1.0 / jnp.sum(jnp.exp(lg - mg), axis=-1, keepdims=True)
    lo = N_GROUPS + grp * EXPERTS_PER_GROUP
    in_grp = (lane_f >= lo) & (lane_f < lo + EXPERTS_PER_GROUP)
    le = jnp.where(in_grp, logits, neg)
    v1, i1 = first_argmax(le)
    v2, i2 = first_argmax(jnp.where(lane_f == i1, neg, le))
    e2 = jnp.exp(v2 - v1)
    w1 = pg_sel / (1.0 + e2)
    w2 = pg_sel * e2 / (1.0 + e2)
    ids = jnp.where(lane == 0, i1 - N_GROUPS, jnp.where(lane == 1, i2 - N_GROUPS, 0.0))
    ids_ref[0] = ids.astype(jnp.int32)
    wts_ref[0] = jnp.where(lane == 0, w1, jnp.where(lane == 1, w2, 0.0))


def _out_proj(attn, gla, wo, x, gt1, g2, sc2, sh2, wr, br, tm):
    b, t, d = x.shape
    row = lambda bi, i: (bi, i, 0)
    vec = lambda bi, i: (bi, 0, 0)
    const = lambda bi, i: (0, 0)
    return pl.pallas_call(
        _out_proj_kernel,
        grid=(b, t // tm),
        in_specs=[
            pl.BlockSpec((1, tm, ATTN_Q_W), row),
            pl.BlockSpec((1, tm, GLA_V_W), row),
            pl.BlockSpec(wo.shape, const, pipeline_mode=pl.Buffered(1)),
            pl.BlockSpec((1, tm, d), row),
            pl.BlockSpec((1, 1, d), vec),
            pl.BlockSpec((1, d), const),
            pl.BlockSpec((1, 1, d), vec),
            pl.BlockSpec((1, 1, d), vec),
            pl.BlockSpec((d, 2 * LANES), const),
            pl.BlockSpec((1, LANES), const),
        ],
        out_specs=[
            pl.BlockSpec((1, tm, d), row),
            pl.BlockSpec((1, tm * _token_pitch(d), LANES), row),
            pl.BlockSpec((1, tm, LANES), row),
            pl.BlockSpec((1, tm, LANES), row),
        ],
        out_shape=[
            jax.ShapeDtypeStruct((b, t, d), F32),
            jax.ShapeDtypeStruct((b, t * _token_pitch(d), LANES), F32),
            jax.ShapeDtypeStruct((b, t, LANES), jnp.int32),
            jax.ShapeDtypeStruct((b, t, LANES), F32),
        ],
        compiler_params=_cparams("arbitrary", "arbitrary"),
        name="out_proj",
    )(attn, gla, wo, x, gt1, g2, sc2, sh2, wr, br)


def _moe_kernel(eidx_ref, eblk0_ref, enb_ref, bs0_ref, bn_ref, nblk_ref, order_ref,
                h2_hbm, w1_ref, w3_ref, w2_ref, y_hbm,
                xbuf, ybuf, gsem, ssem):
    del eidx_ref
    e = pl.program_id(0)
    nblk = nblk_ref[0]
    kx = xbuf.shape[1] // MOE_BLOCK
    ky = ybuf.shape[1] // MOE_BLOCK

    data_rows = w1_ref.shape[0] // LANES

    def gather_copy(sl, hbm_row, j, tokens=1):
        return pltpu.make_async_copy(h2_hbm.at[pl.ds(hbm_row * kx, tokens * data_rows)],
                                     xbuf.at[sl, pl.ds(j * kx, tokens * data_rows)], gsem.at[sl])

    def for_rows(n, per_group, per_row):
        ng = lax.shift_right_logical(n, ROW_GROUP_LOG2)
        lax.fori_loop(0, ng, lambda g, c: (per_group(g * ROW_GROUP), c)[1], 0)
        lax.fori_loop(ng * ROW_GROUP, n, lambda j, c: (per_row(j), c)[1], 0)

    def start_gather(blk, sl):
        s = bs0_ref[blk]

        def start_row(j):
            gather_copy(sl, lax.shift_right_logical(order_ref[s + j], 1), j).start()

        def group(j0):
            for u in range(ROW_GROUP):
                start_row(j0 + u)

        for_rows(bn_ref[blk], group, start_row)

    def wait_gather(blk, sl):
        for_rows(bn_ref[blk], lambda j0: gather_copy(sl, 0, 0, ROW_GROUP).wait(),
                 lambda j: gather_copy(sl, 0, 0).wait())

    half = MOE_BLOCK // 2

    def with_block_rows(blk, fn):
        @pl.when(bn_ref[blk] > half)
        def _():
            fn(MOE_BLOCK)

        @pl.when(bn_ref[blk] <= half)
        def _():
            fn(half)

    def with_copy_rows(blk, fn):
        quarter = MOE_BLOCK // 4
        n = bn_ref[blk]
        for c in range(1, 5):
            @pl.when((n > (c - 1) * quarter) & (n <= c * quarter))
            def _():
                fn(c * quarter)

    def output_copy(blk, sl, rows):
        return pltpu.make_async_copy(ybuf.at[sl, pl.ds(0, rows * ky)],
                                     y_hbm.at[pl.ds(bs0_ref[blk] * ky, rows * ky)], ssem.at[sl])

    def start_output(blk, sl):
        with_copy_rows(blk, lambda rows: output_copy(blk, sl, rows).start())

    def wait_output(blk, sl):
        with_copy_rows(blk, lambda rows: output_copy(blk, sl, rows).wait())

    @pl.when(e == 0)
    def _():
        xbuf[...] = jnp.zeros_like(xbuf)
        ybuf[...] = jnp.zeros_like(ybuf)
        for g0 in range(GATHER_AHEAD):
            @pl.when(g0 < nblk)
            def _():
                start_gather(g0, g0)

    def block(g, carry):
        slot = g & 1
        xslot = lax.rem(g, GATHER_AHEAD + 1)

        @pl.when(g + GATHER_AHEAD < nblk)
        def _():
            start_gather(g + GATHER_AHEAD, lax.rem(g + GATHER_AHEAD, GATHER_AHEAD + 1))

        wait_gather(g, xslot)

        def expert_rows(rows):
            x = _load_token_tiles(xbuf.at[xslot, pl.ds(0, rows * kx)], rows, w1_ref.shape[0])
            a = jnp.dot(x, w1_ref[...], preferred_element_type=F32)
            gate = jnp.dot(x, w3_ref[...], preferred_element_type=F32)
            hid = a * jax.nn.sigmoid(a) * gate
            y = jnp.dot(hid, w2_ref[...], preferred_element_type=F32)
            _store_token_tiles(ybuf.at[slot, pl.ds(0, rows * ky)], y)

            @pl.when(g >= 1)
            def _():
                wait_output(g - 1, 1 - slot)

            start_output(g, slot)

        with_block_rows(g, expert_rows)
        return carry

    @pl.when(enb_ref[e] > 0)
    def _():
        lax.fori_loop(eblk0_ref[e], eblk0_ref[e] + enb_ref[e], block, 0)

    @pl.when(e == pl.num_programs(0) - 1)
    def _():
        wait_output(nblk - 1, (nblk - 1) & 1)
        ybuf[0] = jnp.zeros_like(ybuf[0])
        pad_rows = MOE_BLOCK * ky
        pad = pltpu.make_async_copy(ybuf.at[0], y_hbm.at[pl.ds(y_hbm.shape[0] - pad_rows, pad_rows)], ssem.at[0])
        pad.start()
        pad.wait()


def _moe(h2t, w1, w3, w2, eidx, eblk0, enb, bs0, bn, nblk, order):
    ne, d, ff = w1.shape
    kx = ky = _token_pitch(d)
    wmap = lambda e, eidx, *_: (eidx[e], 0, 0)
    grid_spec = pltpu.PrefetchScalarGridSpec(
        num_scalar_prefetch=7,
        grid=(ne,),
        in_specs=[
            pl.BlockSpec(memory_space=pl.ANY),
            pl.BlockSpec((None, d, ff), wmap),
            pl.BlockSpec((None, d, ff), wmap),
            pl.BlockSpec((None, ff, d), wmap),
        ],
        out_specs=pl.BlockSpec(memory_space=pl.ANY),
        scratch_shapes=[
            pltpu.VMEM((GATHER_AHEAD + 1, MOE_BLOCK * kx, LANES), F32),
            pltpu.VMEM((2, MOE_BLOCK * ky, LANES), F32),
            pltpu.SemaphoreType.DMA((GATHER_AHEAD + 1,)),
            pltpu.SemaphoreType.DMA((2,)),
        ],
    )
    return pl.pallas_call(
        _moe_kernel,
        grid_spec=grid_spec,
        out_shape=jax.ShapeDtypeStruct(((order.shape[0] + MOE_BLOCK) * ky, LANES), F32),
        compiler_params=_cparams("arbitrary"),
        name="moe",
    )(eidx, eblk0, enb, bs0, bn, nblk, order, h2t, w1, w3, w2)


def _combine_kernel(pos_ref, x1_ref, ys_hbm, wts_ref, gt_ref, o_ref, ybuf, sem):
    tm = x1_ref.shape[1]
    ky = ybuf.shape[2] // tm
    nt = pl.num_programs(1)
    n = pl.program_id(0) * nt + pl.program_id(1)
    slot = n & 1

    data_rows = x1_ref.shape[2] // LANES

    def row_copy(sl, k, pos, j, tokens=1):
        return pltpu.make_async_copy(ys_hbm.at[pl.ds(pos * ky, tokens * data_rows)],
                                     ybuf.at[sl, k, pl.ds(j * ky, tokens * data_rows)], sem.at[sl])

    def start_tile(tile, sl):
        base = tile * (tm * TOP_K)

        def group(g, carry):
            for u in range(ROW_GROUP):
                row_copy(sl, u % TOP_K, pos_ref[base + g * ROW_GROUP + u],
                         g * (ROW_GROUP // TOP_K) + u // TOP_K).start()
            return carry

        lax.fori_loop(0, tm * TOP_K // ROW_GROUP, group, 0)

    @pl.when(n == 0)
    def _():
        start_tile(0, 0)

    @pl.when(n + 1 < pl.num_programs(0) * nt)
    def _():
        start_tile(n + 1, 1 - slot)

    for k in range(TOP_K):
        row_copy(slot, k, 0, 0, tokens=tm).wait()
    w = wts_ref[0]
    d = x1_ref.shape[2]
    ff = (w[:, 0:1] * _load_token_tiles(ybuf.at[slot, 0], tm, d)
          + w[:, 1:2] * _load_token_tiles(ybuf.at[slot, 1], tm, d))
    o_ref[0] = x1_ref[0] + gt_ref[0] * ff


def _combine(x1, ys, pos, wts, gt2, tm):
    b, t, d = x1.shape
    ky = _token_pitch(d)
    row = lambda bi, i, pos: (bi, i, 0)
    vec = lambda bi, i, pos: (bi, 0, 0)
    grid_spec = pltpu.PrefetchScalarGridSpec(
        num_scalar_prefetch=1,
        grid=(b, t // tm),
        in_specs=[
            pl.BlockSpec((1, tm, d), row),
            pl.BlockSpec(memory_space=pl.ANY),
            pl.BlockSpec((1, tm, LANES), row),
            pl.BlockSpec((1, 1, d), vec),
        ],
        out_specs=pl.BlockSpec((1, tm, d), row),
        scratch_shapes=[
            pltpu.VMEM((2, TOP_K, tm * ky, LANES), F32),
            pltpu.SemaphoreType.DMA((2,)),
        ],
    )
    return pl.pallas_call(
        _combine_kernel,
        grid_spec=grid_spec,
        out_shape=jax.ShapeDtypeStruct((b, t, d), F32),
        compiler_params=_cparams("arbitrary", "arbitrary"),
        name="combine",
    )(pos, x1, ys, wts, gt2)


def _rope_tables(t):
    rows = t // GRID_W
    n_freq = HEAD_DIM // 4
    inv = ROPE_THETA ** (-jnp.arange(n_freq, dtype=F32) / n_freq)
    ar = jnp.arange(rows, dtype=F32)[:, None] * inv
    ac = jnp.arange(GRID_W, dtype=F32)[:, None] * inv

    def expand(fr, fc, sign):
        by_row = jnp.broadcast_to(fr[:, None, :], (rows, GRID_W, n_freq))
        by_col = jnp.broadcast_to(fc[None, :, :], (rows, GRID_W, n_freq))
        return jnp.concatenate([sign * by_row, by_row, sign * by_col, by_col], axis=2).reshape(t, HEAD_DIM)

    return expand(jnp.cos(ar), jnp.cos(ac), 1.0), expand(jnp.sin(ar), jnp.sin(ac), -1.0)


def _gate_weights(w2, bias):
    r = GLA_GATE_RANK
    wh = w2.reshape(2, r, GLA_HEADS, GLA_DK).transpose(2, 0, 1, 3)
    w = jnp.zeros((GLA_HEADS, 2 * r, 2 * GLA_DK), F32)
    w = w.at[:, 0:r, 0:GLA_DK].set(wh[:, 0]).at[:, r:2 * r, GLA_DK:].set(wh[:, 1])
    hi = w.astype(BF16)
    mid = (w - hi.astype(F32)).astype(BF16)
    pad = jnp.zeros((GLA_HEADS, LANES - 6 * r, 2 * GLA_DK), BF16)
    w2s = jnp.concatenate([hi, hi, mid, pad], axis=1)
    bias2 = bias.reshape(2, GLA_HEADS, GLA_DK).transpose(1, 0, 2).reshape(GLA_HEADS, 1, 2 * GLA_DK)
    return w2s, bias2


def _block_plan(eid_flat, nb):
    order = jnp.argsort(eid_flat).astype(jnp.int32)
    eids = jnp.arange(N_EXPERTS, dtype=jnp.int32)
    of_assign = (eid_flat[:, None] == eids[None, :]).astype(jnp.int32)
    counts = jnp.sum(of_assign, axis=0)
    starts = jnp.cumsum(counts) - counts
    nblk_e = (counts + MOE_BLOCK - 1) // MOE_BLOCK
    bends = jnp.cumsum(nblk_e)
    bstarts = bends - nblk_e
    nblk = bends[-1]
    prev_used = lax.cummax(jnp.where(counts > 0, eids, -1))
    eidx = jnp.where(prev_used >= 0, prev_used, jnp.argmax(counts > 0)).astype(jnp.int32)
    bi = jnp.arange(nb, dtype=jnp.int32)
    bic = jnp.minimum(bi, jnp.maximum(nblk - 1, 0))
    bexp = jnp.minimum(jnp.sum(bends[None, :] <= bic[:, None], axis=1), N_EXPERTS - 1)
    of_block = (bexp[:, None] == eids[None, :]).astype(jnp.int32)

    def lookup(per_expert):
        return jnp.sum(of_block * per_expert[None, :], axis=1)

    r0 = (bic - lookup(bstarts)) * MOE_BLOCK
    bs0 = (lookup(starts) + r0).astype(jnp.int32)
    bn = jnp.where(bi < nblk, jnp.minimum(lookup(counts) - r0, MOE_BLOCK), 0).astype(jnp.int32)
    pos = jnp.argsort(order).astype(jnp.int32)
    plan = (eidx, bstarts.astype(jnp.int32), nblk_e.astype(jnp.int32), bs0, bn,
            nblk.reshape(1).astype(jnp.int32), order)
    return plan, pos


def kernel(x, c, ctx, c_ctx, w_ada, b_ada, norm1_g, w_in, q_norm_g, k_norm_g, gla_gate_w2, gla_gate_b, gla_norm_g, w_out, norm2_g, router_grp_w, router_grp_b, router_exp_w, router_exp_b, moe_w1, moe_w3, moe_w2):
    b, t, d = x.shape
    tc = ctx.shape[1]
    depth = w_ada.shape[0]
    assert depth == 1, "single-layer stack: the context stream only feeds keys/values and GLA states"
    layer = 0

    c8 = jnp.zeros((8, d), F32).at[0:b].set(c).at[b].set(c_ctx)
    mod = _ada(c8, w_ada[layer], b_ada[layer])
    sh1, sc1, gt1, sh2, sc2, gt2 = [mod[0:b, i * d:(i + 1) * d].reshape(b, 1, d) for i in range(6)]
    sh1c, sc1c = [jnp.broadcast_to(mod[b, i * d:(i + 1) * d].reshape(1, 1, d), (b, 1, d)) for i in range(2)]

    w_main, w_tail = _w_in_parts(jnp.swapaxes(w_in[layer], 0, 1))

    cos_t, sin_t = _rope_tables(t)
    g1 = norm1_g[layer].reshape(1, d)
    qg = q_norm_g[layer].reshape(1, HEAD_DIM)
    kg = k_norm_g[layer].reshape(1, HEAD_DIM)
    q, k, v, gq, gk, gv, og, lr = _in_proj(x, g1, sc1, sh1, w_main, w_tail, qg, kg, cos_t, sin_t, PROJ_ROWS)
    ones_t = jnp.ones((tc, HEAD_DIM), F32)
    _, kc, vc, gqc, gkc, gvc, _, lrc = _in_proj(ctx, g1, sc1c, sh1c, w_main, w_tail, qg, kg,
                                                ones_t, jnp.zeros_like(ones_t), tc, context=True)

    attn = _attn(q, jnp.concatenate([kc, k], axis=1), jnp.concatenate([vc, v], axis=1), ATTN_Q_ROWS)
    w2s, bias2 = _gate_weights(gla_gate_w2[layer], gla_gate_b[layer])
    gla = _gla(gq, gk, gv, lr, og, gqc, gkc, gvc, lrc, w2s, bias2, gla_norm_g[layer].reshape(1, GLA_DV))

    wr = jnp.concatenate([router_grp_w[layer], router_exp_w[layer],
                          jnp.zeros((d, LANES - N_GROUPS - N_EXPERTS), F32)], axis=1)
    br = jnp.concatenate([router_grp_b[layer], router_exp_b[layer],
                          jnp.zeros((LANES - N_GROUPS - N_EXPERTS,), F32)]).reshape(1, LANES)
    wr_hi = wr.astype(BF16)
    wr_parts = jnp.concatenate([wr_hi, (wr - wr_hi.astype(F32)).astype(BF16)], axis=1)
    x1, h2t, ids, wts = _out_proj(attn, gla, w_out[layer], x, gt1, norm2_g[layer].reshape(1, d),
                                  sc2, sh2, wr_parts, br, PROJ_ROWS)

    m = b * t
    n_assign = m * TOP_K
    nb = -(-(n_assign + N_EXPERTS * (MOE_BLOCK - 1)) // MOE_BLOCK)
    eid_flat = ids[:, :, 0:TOP_K].reshape(n_assign)
    plan, pos = _block_plan(eid_flat, nb)
    ys = _moe(h2t.reshape(-1, LANES), moe_w1[layer], moe_w3[layer], moe_w2[layer], *plan)
    return _combine(x1, ys, pos, wts, gt2, COMBINE_ROWS)
```
